```python
import jax, jax.numpy as jnp
from jax import lax
import numpy as np

D_MODEL = 1024
BATCH = 4
SEQ = 4096
DEPTH = 1
DEC_BATCH = 8
DEC_SEQ = 32
PAST_LEN = 2048

CHUNK = 64
N_META = 16
Q_BLOCK = 128
SB_HEADS = 8
SB_HEAD_DIM = 64
SB_WIDTH = SB_HEADS * SB_HEAD_DIM
HG_HEADS = 4
HG_HEAD_DIM = 128
HG_WIDTH = HG_HEADS * HG_HEAD_DIM
D_FF = 2816
EPS = 1e-6
IN_SPLITS = [SB_WIDTH, SB_WIDTH, SB_WIDTH, HG_WIDTH, HG_WIDTH, HG_WIDTH, HG_WIDTH, D_MODEL, D_MODEL]
N_IN = sum(IN_SPLITS)

kernel_name = 'streaming_stickbreak_hgrn2_macaron'


def rmsnorm(x, g):
    xf = x.astype(jnp.float32)
    y = xf * lax.rsqrt(jnp.mean(xf * xf, axis=-1, keepdims=True) + EPS)
    return (y * g.astype(jnp.float32)).astype(x.dtype)


def half_ffn(x, g, w_gate, w_up, w_down):
    h = rmsnorm(x, g)
    return x + 0.5 * ((jax.nn.silu(h @ w_gate) * (h @ w_up)) @ w_down)


def sb_block(q, k, v, q_idx, k_idx):
    z = jnp.einsum('bqhd,bkhd->bhqk', q.astype(jnp.float32), k.astype(jnp.float32)) * (SB_HEAD_DIM ** -0.5)
    strict = k_idx[None, :] < q_idx[:, None]
    log_stay = jnp.where(strict, jax.nn.log_sigmoid(-z), 0.0)
    after = lax.cumsum(log_stay, axis=3, reverse=True) - log_stay
    a = jnp.where(strict, jnp.exp(jax.nn.log_sigmoid(z) + after), 0.0)
    return jnp.einsum('bhqk,bkhd->bqhd', a, v.astype(jnp.float32))


def sb_prompt(q, k, v):
    B, L, H, D = q.shape
    Lp = -(-L // Q_BLOCK) * Q_BLOCK
    pad = ((0, 0), (0, Lp - L), (0, 0), (0, 0))
    qp, kp, vp = jnp.pad(q, pad), jnp.pad(k, pad), jnp.pad(v, pad)
    n = Lp // Q_BLOCK
    qb = jnp.moveaxis(qp.reshape(B, n, Q_BLOCK, H, D), 1, 0)
    starts = jnp.arange(n, dtype=jnp.int32) * Q_BLOCK
    k_idx = jnp.arange(Lp, dtype=jnp.int32)
    out = lax.map(lambda a: sb_block(a[0], kp, vp, a[1] + jnp.arange(Q_BLOCK, dtype=jnp.int32), k_idx), (qb, starts))
    return jnp.moveaxis(out, 0, 1).reshape(B, Lp, H, D)[:, :L]


def hgrn_chunk(S, q, k, iv, logf):
    T = q.shape[1]
    b = jnp.cumsum(logf, axis=1)
    causal = jnp.arange(T)[:, None] >= jnp.arange(T)[None, :]
    diff = b[:, :, None] - b[:, None, :]
    decay = jnp.exp(jnp.where(causal[None, :, :, None, None], diff, -jnp.inf))
    scores = jnp.einsum('bthk,bshk,btshk->bhts', q, k, decay)
    o = jnp.einsum('bhts,bshv->bthv', scores, iv) + jnp.einsum('bthk,bhkv->bthv', q * jnp.exp(b), S)
    b_last = b[:, -1]
    S_new = jnp.exp(b_last)[..., None] * S + jnp.einsum('bshk,bshv->bhkv', k * jnp.exp(b_last[:, None] - b), iv)
    return o, S_new


def hgrn_prompt(q, k, iv, logf):
    B, L, H, K = q.shape
    V = iv.shape[-1]
    S0 = jnp.zeros((B, H, K, V), jnp.float32)
    o_meta, S = hgrn_chunk(S0, q[:, :N_META], k[:, :N_META], iv[:, :N_META], logf[:, :N_META])
    n = (L - N_META) // CHUNK
    blk = lambda a: jnp.moveaxis(a[:, N_META:].reshape(B, n, CHUNK, H, a.shape[-1]), 1, 0)

    def step(S, xs):
        o, S2 = hgrn_chunk(S, *xs)
        return S2, o

    S, outs = lax.scan(step, S, (blk(q), blk(k), blk(iv), blk(logf)))
    o_rest = jnp.moveaxis(outs, 0, 1).reshape(B, n * CHUNK, H, V)
    return jnp.concatenate([o_meta, o_rest], axis=1), S


def hgrn_lower_bounds(logits):
    p = jax.nn.softmax(logits.astype(jnp.float32), axis=0)
    return jnp.cumsum(p, axis=0) - p[0]


def setup_inputs(seed: int = 0) -> dict:
    key = jax.random.key(seed)
    ks = jax.random.split(key, 32)
    nrm = lambda k, s, sc: jax.random.normal(k, s, jnp.float32) * sc
    gain = lambda k, s: 1.0 + 0.02 * jax.random.normal(k, s, jnp.float32)
    L_cache = N_META + PAST_LEN
    return {
        'x_prompt': nrm(ks[0], (BATCH, SEQ, D_MODEL), 1.0),
        'x_sample': nrm(ks[1], (DEC_BATCH, DEC_SEQ, D_MODEL), 1.0),
        'cache_sb_k': nrm(ks[2], (DEPTH, DEC_BATCH, L_cache, SB_HEADS, SB_HEAD_DIM), 1.0),
        'cache_sb_v': nrm(ks[3], (DEPTH, DEC_BATCH, L_cache, SB_HEADS, SB_HEAD_DIM), 1.0),
        'state_hgrn': nrm(ks[4], (DEPTH, DEC_BATCH, HG_HEADS, HG_HEAD_DIM, HG_HEAD_DIM), 0.5),
        'meta_tokens': nrm(ks[5], (N_META, D_MODEL), 1.0),
        'ffn1_norm': gain(ks[6], (DEPTH, D_MODEL)),
        'ffn1_w_gate': nrm(ks[7], (DEPTH, D_MODEL, D_FF), D_MODEL ** -0.5),
        'ffn1_w_up': nrm(ks[8], (DEPTH, D_MODEL, D_FF), D_MODEL ** -0.5),
        'ffn1_w_down': nrm(ks[9], (DEPTH, D_FF, D_MODEL), D_FF ** -0.5),
        'mix_norm': gain(ks[10], (DEPTH, D_MODEL)),
        'w_in': nrm(ks[11], (DEPTH, D_MODEL, N_IN), D_MODEL ** -0.5),
        'b_gate': nrm(ks[12], (DEPTH, 2 * D_MODEL), 0.02),
        'hg_lb_logits': nrm(ks[13], (DEPTH + 1, HG_WIDTH), 0.1),
        'hg_out_norm': gain(ks[14], (DEPTH, HG_WIDTH)),
        'w_branch_a': nrm(ks[15], (DEPTH, SB_WIDTH, D_MODEL), SB_WIDTH ** -0.5),
        'w_branch_b': nrm(ks[16], (DEPTH, HG_WIDTH, D_MODEL), HG_WIDTH ** -0.5),
        'w_out': nrm(ks[17], (DEPTH, D_MODEL, D_MODEL), D_MODEL ** -0.5),
        'ffn2_norm': gain(ks[18], (DEPTH, D_MODEL)),
        'ffn2_w_gate': nrm(ks[19], (DEPTH, D_MODEL, D_FF), D_MODEL ** -0.5),
        'ffn2_w_up': nrm(ks[20], (DEPTH, D_MODEL, D_FF), D_MODEL ** -0.5),
        'ffn2_w_down': nrm(ks[21], (DEPTH, D_FF, D_MODEL), D_FF ** -0.5),
        'final_norm': gain(ks[22], (D_MODEL,)),
    }


def reference(x_prompt, x_sample, cache_sb_k, cache_sb_v, state_hgrn, meta_tokens,
              ffn1_norm, ffn1_w_gate, ffn1_w_up, ffn1_w_down, mix_norm, w_in, b_gate,
              hg_lb_logits, hg_out_norm, w_branch_a, w_branch_b, w_out,
              ffn2_norm, ffn2_w_gate, ffn2_w_up, ffn2_w_down, final_norm):
    lower = hgrn_lower_bounds(hg_lb_logits)
    split_at = np.cumsum(IN_SPLITS)[:-1].tolist()

    def pre_mix(x, l):
        x = half_ffn(x, ffn1_norm[l], ffn1_w_gate[l], ffn1_w_up[l], ffn1_w_down[l])
        h = rmsnorm(x, mix_norm[l])
        B, L, _ = h.shape
        qa, ka, va, zf, ih, qh, g, ga, gb = jnp.split(h @ w_in[l], split_at, axis=-1)
        heads_a = lambda a: a.reshape(B, L, SB_HEADS, SB_HEAD_DIM)
        heads_b = lambda a: a.astype(jnp.float32).reshape(B, L, HG_HEADS, HG_HEAD_DIM)
        lb = lower[l + 1]
        f = lb + (1.0 - lb) * jax.nn.sigmoid(zf.astype(jnp.float32))
        hg = (heads_b(qh), heads_b(1.0 - f), heads_b(ih), heads_b(jnp.log(f)))
        gates = jax.nn.sigmoid((jnp.concatenate([ga, gb], -1) + b_gate[l]).astype(jnp.float32))
        return x, (heads_a(qa), heads_a(ka), heads_a(va)), hg, g, gates

    def post_mix(x, l, oa, ob, g, gates):
        B, L, _ = x.shape
        on = ob * lax.rsqrt(jnp.mean(ob * ob, axis=-1, keepdims=True) + EPS) * hg_out_norm[l].astype(jnp.float32).reshape(HG_HEADS, HG_HEAD_DIM)
        ob = (on * jax.nn.silu(g.astype(jnp.float32).reshape(B, L, HG_HEADS, HG_HEAD_DIM))).reshape(B, L, HG_WIDTH).astype(x.dtype)
        oa = oa.reshape(B, L, SB_WIDTH).astype(x.dtype)
        gates = gates.astype(x.dtype)
        merged = gates[..., :D_MODEL] * (oa @ w_branch_a[l]) + gates[..., D_MODEL:] * (ob @ w_branch_b[l])
        x = x + merged @ w_out[l]
        return half_ffn(x, ffn2_norm[l], ffn2_w_gate[l], ffn2_w_up[l], ffn2_w_down[l])

    Bp = x_prompt.shape[0]
    xp = jnp.concatenate([jnp.broadcast_to(meta_tokens.astype(x_prompt.dtype)[None], (Bp, N_META, D_MODEL)), x_prompt], axis=1)
    pk, pv, ps = [], [], []
    for l in range(DEPTH):
        xp, (qa, ka, va), hg, g, gates = pre_mix(xp, l)
        oa = sb_prompt(qa, ka, va)
        ob, S = hgrn_prompt(*hg)
        xp = post_mix(xp, l, oa, ob, g, gates)
        pk.append(ka)
        pv.append(va)
        ps.append(S)
    y_prompt = rmsnorm(xp, final_norm)[:, N_META:]

    xs = x_sample
    T = xs.shape[1]
    Lc = cache_sb_k.shape[2]
    q_idx = Lc + jnp.arange(T, dtype=jnp.int32)
    k_idx = jnp.arange(Lc + T, dtype=jnp.int32)
    sk, sv, ss = [], [], []
    for l in range(DEPTH):
        xs, (qa, ka, va), hg, g, gates = pre_mix(xs, l)
        kk = jnp.concatenate([cache_sb_k[l], ka.astype(cache_sb_k.dtype)], axis=1)
        vv = jnp.concatenate([cache_sb_v[l], va.astype(cache_sb_v.dtype)], axis=1)
        oa = sb_block(qa, kk, vv, q_idx, k_idx)
        ob, S = hgrn_chunk(state_hgrn[l].astype(jnp.float32), *hg)
        xs = post_mix(xs, l, oa, ob, g, gates)
        sk.append(ka)
        sv.append(va)
        ss.append(S)
    y_sample = rmsnorm(xs, final_norm)

    return (y_prompt, y_sample, jnp.stack(pk), jnp.stack(pv), jnp.stack(ps), jnp.stack(sk), jnp.stack(sv), jnp.stack(ss))
```

```python
import functools

import jax
import jax.numpy as jnp
from jax import lax
from jax.experimental import pallas as pl
from jax.experimental.pallas import tpu as pltpu

F32 = jnp.float32
BF16 = jnp.bfloat16
EPS = 1e-6
N_META = 16
SB_HEADS = 8
SB_HEAD_DIM = 64
SB_WIDTH = SB_HEADS * SB_HEAD_DIM
HG_HEADS = 4
HG_HEAD_DIM = 128
HG_WIDTH = HG_HEADS * HG_HEAD_DIM
LANES = 128
FF_CHUNK = 256
KEY_BLOCK = 128
HG_CHUNK = 64
HG_SAFE_LOG_DECAY = -60.0
VMEM_LIMIT_BYTES = 56 * 1024 * 1024

_NT = (((1,), (1,)), ((), ()))
_TN = (((0,), (0,)), ((), ()))


def _const_spec(shape):
    nd = len(shape)
    return pl.BlockSpec(shape, lambda *_: (0,) * nd, pipeline_mode=pl.Buffered(1))


def _rows_spec(tm, width):
    return pl.BlockSpec((tm, width), lambda i: (i, 0))


def _params(sem):
    return pltpu.CompilerParams(dimension_semantics=sem, vmem_limit_bytes=VMEM_LIMIT_BYTES)


def _rms(x, g):
    return x * lax.rsqrt(jnp.mean(x * x, axis=-1, keepdims=True) + EPS) * g


def _half_ffn(x, g, wg_ref, wu_ref, wd_ref, acc_ref):
    h = _rms(x, g).astype(BF16)
    for c in range(wg_ref.shape[0]):
        gate = jnp.dot(h, wg_ref[c], preferred_element_type=F32)
        up = jnp.dot(h, wu_ref[c], preferred_element_type=F32)
        act = (gate * jax.nn.sigmoid(gate) * up).astype(BF16)
        contrib = jnp.dot(act, wd_ref[c], preferred_element_type=F32)
        if c == 0:
            acc_ref[...] = contrib
        else:
            acc_ref[...] += contrib
    return x + 0.5 * acc_ref[...]


def _ffn_kernel(x_ref, n_ref, wg_ref, wu_ref, wd_ref, o_ref, acc_ref):
    o_ref[...] = _half_ffn(x_ref[...], n_ref[...], wg_ref, wu_ref, wd_ref, acc_ref)


def _ffn_call(x, norm, wg, wu, wd, tm):
    n, d = x.shape
    return pl.pallas_call(
        _ffn_kernel,
        grid=(n // tm,),
        in_specs=[_rows_spec(tm, d), _const_spec(norm.shape), _const_spec(wg.shape),
                  _const_spec(wu.shape), _const_spec(wd.shape)],
        out_specs=_rows_spec(tm, d),
        out_shape=jax.ShapeDtypeStruct((n, d), F32),
        scratch_shapes=[pltpu.VMEM((tm, d), F32)],
        compiler_params=_params(("parallel",)),
        name="ffn",
    )(x, norm, wg, wu, wd)


def _mix_kernel(x_ref, n_ref, win_ref, bg_ref, lbl_ref,
                q_ref, k_ref, kb_ref, v_ref, vb_ref,
                hq_ref, hk_ref, hi_ref, hlf_ref, sg_ref, gates_ref):
    h = _rms(x_ref[...], n_ref[...]).astype(BF16)

    def proj(lo, width):
        return jnp.dot(h, win_ref[:, lo:lo + width], preferred_element_type=F32)

    w = SB_WIDTH
    q_ref[...] = (proj(0, w) * (SB_HEAD_DIM ** -0.5)).astype(BF16)
    k = proj(w, w)
    k_ref[...] = k
    kb_ref[...] = k.astype(BF16)
    v = proj(2 * w, w)
    v_ref[...] = v
    vb_ref[...] = v.astype(BF16)

    logits = lbl_ref[...]
    l0, l1 = logits[0:1], logits[1:2]
    m = jnp.maximum(l0, l1)
    e0, e1 = jnp.exp(l0 - m), jnp.exp(l1 - m)
    p0, p1 = e0 / (e0 + e1), e1 / (e0 + e1)
    lb = (p0 + p1) - p0
    f = lb + (1.0 - lb) * jax.nn.sigmoid(proj(3 * w, w))
    hk_ref[...] = 1.0 - f
    hlf_ref[...] = jnp.log(f)
    hi_ref[...] = proj(4 * w, w)
    hq_ref[...] = proj(5 * w, w)
    g = proj(6 * w, w)
    sg_ref[...] = g * jax.nn.sigmoid(g)
    d2 = gates_ref.shape[1]
    gates_ref[...] = jax.nn.sigmoid(proj(7 * w, d2) + bg_ref[...])


def _mix_call(x, norm, w_in, b_gate, lb_logits, tm):
    n, d = x.shape
    w = SB_WIDTH
    widths = [(w, BF16), (w, F32), (w, BF16), (w, F32), (w, BF16),
              (w, F32), (w, F32), (w, F32), (w, F32), (w, F32), (2 * d, F32)]
    return pl.pallas_call(
        _mix_kernel,
        grid=(n // tm,),
        in_specs=[_rows_spec(tm, d), _const_spec(norm.shape), _const_spec(w_in.shape),
                  _const_spec(b_gate.shape), _const_spec(lb_logits.shape)],
        out_specs=[_rows_spec(tm, wd) for wd, _ in widths],
        out_shape=[jax.ShapeDtypeStruct((n, wd), dt) for wd, dt in widths],
        compiler_params=_params(("parallel",)),
        name="mix",
    )(x, norm, w_in, b_gate, lb_logits)


def _sb_kernel(q_ref, kn_ref, vn_ref, ko_ref, vo_ref, o_ref, qm_ref, acc_ref, r_ref, *, tq, n_old):
    tk = KEY_BLOCK
    i = pl.program_id(1)
    n_head, n_full = n_old % tk, n_old // tk
    row = lax.broadcasted_iota(jnp.int32, (tq, tk), 0)
    col = lax.broadcasted_iota(jnp.int32, (tq, tk), 1)
    urow = lax.broadcasted_iota(jnp.int32, (tk, 2 * tk), 0)
    ucol = lax.broadcasted_iota(jnp.int32, (tk, 2 * tk), 1)
    uext = jnp.where(jnp.logical_or(urow >= ucol, ucol >= tk), 1.0, 0.0).astype(BF16)
    lane = lax.broadcasted_iota(jnp.int32, (tq, LANES), 1)

    for h in range(SB_HEADS):
        p = h // 2
        qp = q_ref[0, :, p * LANES:(p + 1) * LANES]
        keep = (lane < SB_HEAD_DIM) if h % 2 == 0 else (lane >= SB_HEAD_DIM)
        qm_ref[h] = jnp.where(keep, qp, jnp.zeros_like(qp))
        r_ref[h] = jnp.zeros((tq, tk), F32)
        acc_ref[h] = jnp.zeros((tq, LANES), F32)

    def sweep(k_ref, v_ref, start, mask):
        for h in range(SB_HEADS):
            p = h // 2
            kb = k_ref[0, pl.ds(start, tk), p * LANES:(p + 1) * LANES].astype(BF16)
            vb = v_ref[0, pl.ds(start, tk), p * LANES:(p + 1) * LANES].astype(BF16)
            z = lax.dot_general(qm_ref[h], kb, _NT, preferred_element_type=F32)
            ls = -(jnp.maximum(z, 0.0) + jnp.log1p(jnp.exp(-jnp.abs(z))))
            if mask is not None:
                ls = jnp.where(mask, ls, 0.0)
            hi = ls.astype(BF16)
            lo = (ls - hi.astype(F32)).astype(BF16)
            ct = (jnp.dot(hi, uext, preferred_element_type=F32)
                  + jnp.dot(lo, uext, preferred_element_type=F32))
            a = jnp.exp(z + ct[:, :tk] + r_ref[h])
            if mask is not None:
                a = jnp.where(mask, a, 0.0)
            acc_ref[h] += jnp.dot(a.astype(BF16), vb, preferred_element_type=F32)
            r_ref[h] += ct[:, tk:]

    q0 = i * tq
    j0 = q0 // tk
    sweep(kn_ref, vn_ref, pl.multiple_of(j0 * tk, tk), col + j0 * tk < row + q0)

    def new_body(t, carry):
        sweep(kn_ref, vn_ref, pl.multiple_of((j0 - 1 - t) * tk, tk), None)
        return carry

    lax.fori_loop(0, j0, new_body, 0)

    if n_full:
        def old_body(t, carry):
            sweep(ko_ref, vo_ref, pl.multiple_of(n_head + (n_full - 1 - t) * tk, 8), None)
            return carry

        lax.fori_loop(0, n_full, old_body, 0)
    if n_head:
        sweep(ko_ref, vo_ref, 0, col < n_head)

    for p in range(SB_HEADS // 2):
        o_ref[0, :, p * LANES:(p + 1) * LANES] = jnp.where(
            lane < SB_HEAD_DIM, acc_ref[2 * p], acc_ref[2 * p + 1]).astype(o_ref.dtype)


def _sb_call(q, k_new, v_new, k_old, v_old, n_old, tq):
    s, lq, w = q.shape
    ln = k_new.shape[1]
    lo = k_old.shape[1]
    assert lq % tq == 0 and ln % KEY_BLOCK == 0 and lo >= max(n_old, KEY_BLOCK)
    assert tq <= KEY_BLOCK and KEY_BLOCK % tq == 0
    old_map = (lambda b, i: (b, 0, 0)) if k_old.shape[0] == s else (lambda b, i: (0, 0, 0))
    return pl.pallas_call(
        functools.partial(_sb_kernel, tq=tq, n_old=n_old),
        grid=(s, lq // tq),
        in_specs=[pl.BlockSpec((1, tq, w), lambda b, i: (b, i, 0)),
                  pl.BlockSpec((1, ln, w), lambda b, i: (b, 0, 0)),
                  pl.BlockSpec((1, ln, w), lambda b, i: (b, 0, 0)),
                  pl.BlockSpec((1, lo, w), old_map),
                  pl.BlockSpec((1, lo, w), old_map)],
        out_specs=pl.BlockSpec((1, tq, w), lambda b, i: (b, i, 0)),
        out_shape=jax.ShapeDtypeStruct((s, lq, w), BF16),
        scratch_shapes=[pltpu.VMEM((SB_HEADS, tq, LANES), BF16),
                        pltpu.VMEM((SB_HEADS, tq, LANES), F32),
                        pltpu.VMEM((SB_HEADS, tq, KEY_BLOCK), F32)],
        compiler_params=_params(("parallel", "parallel")),
        name="sb",
    )(q, k_new, v_new, k_old, v_old)


def _pad_rows(x, rows):
    if x.shape[0] == rows:
        return x
    return jnp.concatenate([x, jnp.zeros((rows - x.shape[0], x.shape[1]), x.dtype)], axis=0)


def _hg_kernel(q_ref, k_ref, i_ref, lf_ref, s0_ref, o_ref, sout_ref, st_ref, *, chunk):
    c = pl.program_id(1)
    hd = HG_HEAD_DIM

    @pl.when(c == 0)
    def _():
        st_ref[...] = s0_ref[0]

    trow = lax.broadcasted_iota(jnp.int32, (chunk, chunk), 0)
    tcol = lax.broadcasted_iota(jnp.int32, (chunk, chunk), 1)
    causal = trow >= tcol
    tri = jnp.where(causal, 1.0, 0.0).astype(BF16)
    lf = lf_ref[0]
    hi = lf.astype(BF16)
    r1 = lf - hi.astype(F32)
    mid = r1.astype(BF16)
    lo = (r1 - mid.astype(F32)).astype(BF16)
    b = (jnp.dot(tri, hi, preferred_element_type=F32)
         + jnp.dot(tri, mid, preferred_element_type=F32)
         + jnp.dot(tri, lo, preferred_element_type=F32))
    b_last = b[chunk - 1:chunk, :]
    safe = jnp.min(b_last) >= HG_SAFE_LOG_DECAY

    @pl.when(safe)
    def _():
        q = q_ref[0]
        k = k_ref[0]
        qd = (q * jnp.exp(b)).astype(BF16)
        kd = (k * jnp.exp(-b)).astype(BF16)
        kl = (k * jnp.exp(b_last - b)).astype(BF16)
        iv = i_ref[0].astype(BF16)
        dl = jnp.exp(b_last)
        for h in range(HG_HEADS):
            sl = slice(h * hd, (h + 1) * hd)
            sc = lax.dot_general(qd[:, sl], kd[:, sl], _NT, preferred_element_type=F32)
            sc = jnp.where(causal, sc, 0.0).astype(BF16)
            st = st_ref[h]
            o_ref[0, :, sl] = (jnp.dot(sc, iv[:, sl], preferred_element_type=F32)
                               + lax.dot_general(qd[:, sl], st.astype(BF16), _NT,
                                                 preferred_element_type=F32))
            st_ref[h] = st * dl[:, sl] + lax.dot_general(iv[:, sl], kl[:, sl], _TN,
                                                         preferred_element_type=F32)

    @pl.when(jnp.logical_not(safe))
    def _():
        lane_t = lax.broadcasted_iota(jnp.int32, (hd, hd), 1)
        row_t = lax.broadcasted_iota(jnp.int32, (hd, hd), 0)
        for h in range(HG_HEADS):
            sl = slice(h * hd, (h + 1) * hd)
            f_t = _pad_rows(jnp.exp(lf[:, sl]), hd).T
            k_t = _pad_rows(k_ref[0, :, sl], hd).T
            q_t = _pad_rows(q_ref[0, :, sl], hd).T
            i_p = _pad_rows(i_ref[0, :, sl], hd)

            def body(t, carry, f_t=f_t, k_t=k_t, q_t=q_t, i_p=i_p):
                s, o = carry
                sel = lane_t == t
                fc = jnp.sum(jnp.where(sel, f_t, 0.0), axis=1, keepdims=True)
                kc = jnp.sum(jnp.where(sel, k_t, 0.0), axis=1, keepdims=True)
                qc = jnp.sum(jnp.where(sel, q_t, 0.0), axis=1, keepdims=True)
                i_row = jnp.sum(jnp.where(row_t == t, i_p, 0.0), axis=0, keepdims=True)
                s = s * fc + kc * i_row
                o = jnp.where(row_t == t, jnp.sum(qc * s, axis=0, keepdims=True), o)
                return s, o

            s, o = lax.fori_loop(0, chunk, body, (st_ref[h].T, jnp.zeros((hd, hd), F32)))
            st_ref[h] = s.T
            o_ref[0, :, sl] = o[:chunk]

    @pl.when(c == pl.num_programs(1) - 1)
    def _():
        sout_ref[0] = st_ref[...]


def _hg_call(q, k, iv, lf, s0t, chunk):
    s, l, w = q.shape
    assert l % chunk == 0
    seq_spec = pl.BlockSpec((1, chunk, w), lambda b, c: (b, c, 0))
    st_shape = (1, HG_HEADS, HG_HEAD_DIM, HG_HEAD_DIM)
    s0_map = (lambda b, c: (b, 0, 0, 0)) if s0t.shape[0] == s else (lambda b, c: (0, 0, 0, 0))
    return pl.pallas_call(
        functools.partial(_hg_kernel, chunk=chunk),
        grid=(s, l // chunk),
        in_specs=[seq_spec, seq_spec, seq_spec, seq_spec, pl.BlockSpec(st_shape, s0_map)],
        out_specs=[seq_spec, pl.BlockSpec(st_shape, lambda b, c: (b, 0, 0, 0))],
        out_shape=[jax.ShapeDtypeStruct((s, l, w), F32),
                   jax.ShapeDtypeStruct((s,) + st_shape[1:], F32)],
        scratch_shapes=[pltpu.VMEM(st_shape[1:], F32)],
        compiler_params=_params(("parallel", "arbitrary")),
        name="hgrn",
    )(q, k, iv, lf, s0t)


def _post_kernel(x_ref, oa_ref, ob_ref, sg_ref, gates_ref, hn_ref, wa_ref, wb_ref, wo_ref,
                 n_ref, wg_ref, wu_ref, wd_ref, fn_ref, y_ref, acc_ref):
    hd = HG_HEAD_DIM
    d = x_ref.shape[1]
    parts = []
    for h in range(HG_HEADS):
        sl = slice(h * hd, (h + 1) * hd)
        parts.append(_rms(ob_ref[:, sl], hn_ref[:, sl]) * sg_ref[:, sl])
    ob = jnp.concatenate(parts, axis=-1).astype(BF16)
    ma = jnp.dot(oa_ref[...], wa_ref[...], preferred_element_type=F32)
    mb = jnp.dot(ob, wb_ref[...], preferred_element_type=F32)
    merged = gates_ref[:, :d] * ma + gates_ref[:, d:] * mb
    x = x_ref[...] + jnp.dot(merged.astype(BF16), wo_ref[...], preferred_element_type=F32)
    x = _half_ffn(x, n_ref[...], wg_ref, wu_ref, wd_ref, acc_ref)
    y_ref[...] = _rms(x, fn_ref[...])


def _post_call(x, oa, ob, sg, gates, hn, wa, wb, wo, norm, wg, wu, wd, fnorm, tm):
    n, d = x.shape
    consts = [hn, wa, wb, wo, norm, wg, wu, wd, fnorm]
    return pl.pallas_call(
        _post_kernel,
        grid=(n // tm,),
        in_specs=[_rows_spec(tm, d), _rows_spec(tm, oa.shape[1]), _rows_spec(tm, ob.shape[1]),
                  _rows_spec(tm, sg.shape[1]), _rows_spec(tm, gates.shape[1])]
                 + [_const_spec(a.shape) for a in consts],
        out_specs=_rows_spec(tm, d),
        out_shape=jax.ShapeDtypeStruct((n, d), F32),
        scratch_shapes=[pltpu.VMEM((tm, d), F32)],
        compiler_params=_params(("parallel",)),
        name="post",
    )(x, oa, ob, sg, gates, *consts)


def _ff_chunks(w_gate, w_up, w_down):
    d, dff = w_gate.shape
    assert dff % FF_CHUNK == 0
    nc = dff // FF_CHUNK
    cols = lambda w: w.astype(BF16).reshape(d, nc, FF_CHUNK).transpose(1, 0, 2)
    return cols(w_gate), cols(w_up), w_down.astype(BF16).reshape(nc, FF_CHUNK, d)


def _row_tile(n, target):
    tm = min(n, target)
    assert n % tm == 0 and tm % 8 == 0
    return tm


def kernel(x_prompt, x_sample, cache_sb_k, cache_sb_v, state_hgrn, meta_tokens, ffn1_norm, ffn1_w_gate, ffn1_w_up, ffn1_w_down, mix_norm, w_in, b_gate, hg_lb_logits, hg_out_norm, w_branch_a, w_branch_b, w_out, ffn2_norm, ffn2_w_gate, ffn2_w_up, ffn2_w_down, final_norm):
    depth = w_in.shape[0]
    assert depth == 1, "single-layer step"
    nb, seq, d = x_prompt.shape
    ns, t_new, _ = x_sample.shape
    n_meta = meta_tokens.shape[0]
    l_cache = cache_sb_k.shape[2]
    w = SB_WIDTH
    row = lambda a: a.reshape(1, -1)

    f1 = _ff_chunks(ffn1_w_gate[0], ffn1_w_up[0], ffn1_w_down[0])
    f2 = _ff_chunks(ffn2_w_gate[0], ffn2_w_up[0], ffn2_w_down[0])
    w_in_b = w_in[0].astype(BF16)
    wa, wb, wo = w_branch_a[0].astype(BF16), w_branch_b[0].astype(BF16), w_out[0].astype(BF16)

    def pre(rows, tm):
        x1 = _ffn_call(rows, row(ffn1_norm[0]), *f1, tm)
        return x1, _mix_call(x1, row(mix_norm[0]), w_in_b, row(b_gate[0]), hg_lb_logits, tm)

    def post(x1, oa, ob, sg, gates, tm):
        return _post_call(x1, oa, ob, sg, gates, row(hg_out_norm[0]), wa, wb, wo,
                          row(ffn2_norm[0]), *f2, row(final_norm), tm)

    n_frames = nb * seq
    x1_p, (q_p, k_p, kb_p, v_p, vb_p, hq_p, hk_p, hi_p, hlf_p, sg_p, gates_p) = pre(
        x_prompt.reshape(n_frames, d), _row_tile(n_frames, 512))
    n_run = ns * t_new
    small = jnp.concatenate([x_sample.reshape(n_run, d), meta_tokens.astype(F32)], axis=0)
    x1_s, (q_s, k_s, kb_s, v_s, vb_s, hq_s, hk_s, hi_s, hlf_s, sg_s, gates_s) = pre(
        small, small.shape[0])
    run = lambda a: a[:n_run]
    meta = lambda a: a[n_run:]

    seq3 = lambda a, s: a.reshape(s, -1, a.shape[-1])
    pad_to = lambda a, rows: jnp.pad(a, ((0, 0), (0, rows - a.shape[1]), (0, 0)))
    oa_p = _sb_call(seq3(q_p, nb), seq3(kb_p, nb), seq3(vb_p, nb),
                    pad_to(meta(kb_s)[None], KEY_BLOCK), pad_to(meta(vb_s)[None], KEY_BLOCK),
                    n_meta, KEY_BLOCK)
    oa_s = _sb_call(seq3(run(q_s), ns), pad_to(seq3(run(kb_s), ns), KEY_BLOCK),
                    pad_to(seq3(run(vb_s), ns), KEY_BLOCK),
                    cache_sb_k[0].reshape(ns, l_cache, w), cache_sb_v[0].reshape(ns, l_cache, w),
                    l_cache, t_new)

    hg = lambda sel, s, arrs: [seq3(sel(a), s) for a in arrs]
    zero_state = jnp.zeros((1, HG_HEADS, HG_HEAD_DIM, HG_HEAD_DIM), F32)
    _, st_meta = _hg_call(*hg(meta, 1, (hq_s, hk_s, hi_s, hlf_s)), zero_state, n_meta)
    ob_p, st_p = _hg_call(*hg(lambda a: a, nb, (hq_p, hk_p, hi_p, hlf_p)), st_meta, HG_CHUNK)
    ob_s, st_s = _hg_call(*hg(run, ns, (hq_s, hk_s, hi_s, hlf_s)),
                          jnp.swapaxes(state_hgrn[0].astype(F32), -1, -2), t_new)

    y_p = post(x1_p, oa_p.reshape(n_frames, w), ob_p.reshape(n_frames, HG_WIDTH), sg_p, gates_p,
               _row_tile(n_frames, 512))
    y_s = post(run(x1_s), oa_s.reshape(n_run, w), ob_s.reshape(n_run, HG_WIDTH), run(sg_s),
               run(gates_s), n_run)

    def with_meta(frames, meta_rows):
        m = jnp.broadcast_to(meta_rows[None], (nb, n_meta, w))
        full = jnp.concatenate([m, frames.reshape(nb, seq, w)], axis=1)
        return full.reshape(1, nb, n_meta + seq, SB_HEADS, SB_HEAD_DIM)

    heads = lambda a: a.reshape(1, ns, t_new, SB_HEADS, SB_HEAD_DIM)
    return (y_p.reshape(nb, seq, d), y_s.reshape(ns, t_new, d),
            with_meta(k_p, meta(k_s)), with_meta(v_p, meta(v_s)),
            jnp.swapaxes(st_p, -1, -2)[None],
            heads(run(k_s)), heads(run(v_s)),
            jnp.swapaxes(st_s, -1, -2)[None])
```

```python
import functools

import jax
import jax.numpy as jnp
from jax import lax
from jax.experimental import pallas as pl
from jax.experimental.pallas import tpu as pltpu

F32 = jnp.float32
BF16 = jnp.bfloat16
EPS = 1e-6
N_META = 16
SB_HEADS = 8
SB_HEAD_DIM = 64
SB_WIDTH = SB_HEADS * SB_HEAD_DIM
HG_HEADS = 4
HG_HEAD_DIM = 128
HG_WIDTH = HG_HEADS * HG_HEAD_DIM
LANES = 128
FF_CHUNK = 256
KEY_BLOCK = 128
WIDE_BLOCK = 256
HG_CHUNK = 64
HG_SAFE_LOG_DECAY = -60.0
VMEM_LIMIT_BYTES = 56 * 1024 * 1024

_NT = (((1,), (1,)), ((), ()))
_TN = (((0,), (0,)), ((), ()))


def _const_spec(shape):
    nd = len(shape)
    return pl.BlockSpec(shape, lambda *_: (0,) * nd, pipeline_mode=pl.Buffered(1))


def _rows_spec(tm, width):
    return pl.BlockSpec((tm, width), lambda i: (i, 0))


def _params(sem):
    return pltpu.CompilerParams(dimension_semantics=sem, vmem_limit_bytes=VMEM_LIMIT_BYTES)


def _rms(x, g):
    return x * lax.rsqrt(jnp.mean(x * x, axis=-1, keepdims=True) + EPS) * g


def _half_ffn(x, g, wg_ref, wu_ref, wd_ref, acc_ref):
    h = _rms(x, g).astype(BF16)
    for c in range(wg_ref.shape[0]):
        gate = jnp.dot(h, wg_ref[c], preferred_element_type=F32)
        up = jnp.dot(h, wu_ref[c], preferred_element_type=F32)
        act = (gate * jax.nn.sigmoid(gate) * up).astype(BF16)
        contrib = jnp.dot(act, wd_ref[c], preferred_element_type=F32)
        if c == 0:
            acc_ref[...] = contrib
        else:
            acc_ref[...] += contrib
    return x + 0.5 * acc_ref[...]


def _ffn_kernel(x_ref, n_ref, wg_ref, wu_ref, wd_ref, o_ref, acc_ref):
    o_ref[...] = _half_ffn(x_ref[...], n_ref[...], wg_ref, wu_ref, wd_ref, acc_ref)


def _ffn_call(x, norm, wg, wu, wd, tm):
    n, d = x.shape
    return pl.pallas_call(
        _ffn_kernel,
        grid=(n // tm,),
        in_specs=[_rows_spec(tm, d), _const_spec(norm.shape), _const_spec(wg.shape),
                  _const_spec(wu.shape), _const_spec(wd.shape)],
        out_specs=_rows_spec(tm, d),
        out_shape=jax.ShapeDtypeStruct((n, d), F32),
        scratch_shapes=[pltpu.VMEM((tm, d), F32)],
        compiler_params=_params(("parallel",)),
        name="ffn",
    )(x, norm, wg, wu, wd)


def _mix_kernel(x_ref, n_ref, win_ref, bg_ref, lbl_ref,
                q_ref, k_ref, kb_ref, v_ref, vb_ref,
                hq_ref, hk_ref, hi_ref, hlf_ref, sg_ref, gates_ref):
    h = _rms(x_ref[...], n_ref[...]).astype(BF16)

    def proj(lo, width):
        return jnp.dot(h, win_ref[:, lo:lo + width], preferred_element_type=F32)

    w = SB_WIDTH
    q_ref[...] = (proj(0, w) * (SB_HEAD_DIM ** -0.5)).astype(BF16)
    k = proj(w, w)
    k_ref[...] = k
    kb_ref[...] = k.astype(BF16)
    v = proj(2 * w, w)
    v_ref[...] = v
    vb_ref[...] = v.astype(BF16)

    logits = lbl_ref[...]
    l0, l1 = logits[0:1], logits[1:2]
    m = jnp.maximum(l0, l1)
    e0, e1 = jnp.exp(l0 - m), jnp.exp(l1 - m)
    p0, p1 = e0 / (e0 + e1), e1 / (e0 + e1)
    lb = (p0 + p1) - p0
    f = lb + (1.0 - lb) * jax.nn.sigmoid(proj(3 * w, w))
    hk_ref[...] = 1.0 - f
    hlf_ref[...] = jnp.log(f)
    hi_ref[...] = proj(4 * w, w)
    hq_ref[...] = proj(5 * w, w)
    g = proj(6 * w, w)
    sg_ref[...] = g * jax.nn.sigmoid(g)
    d2 = gates_ref.shape[1]
    gates_ref[...] = jax.nn.sigmoid(proj(7 * w, d2) + bg_ref[...])


def _mix_call(x, norm, w_in, b_gate, lb_logits, tm):
    n, d = x.shape
    w = SB_WIDTH
    widths = [(w, BF16), (w, F32), (w, BF16), (w, F32), (w, BF16),
              (w, F32), (w, F32), (w, F32), (w, F32), (w, F32), (2 * d, F32)]
    return pl.pallas_call(
        _mix_kernel,
        grid=(n // tm,),
        in_specs=[_rows_spec(tm, d), _const_spec(norm.shape), _const_spec(w_in.shape),
                  _const_spec(b_gate.shape), _const_spec(lb_logits.shape)],
        out_specs=[_rows_spec(tm, wd) for wd, _ in widths],
        out_shape=[jax.ShapeDtypeStruct((n, wd), dt) for wd, dt in widths],
        compiler_params=_params(("parallel",)),
        name="mix",
    )(x, norm, w_in, b_gate, lb_logits)


def _sb_kernel(q_ref, kn_ref, vn_ref, ko_ref, vo_ref, o_ref, qm_ref, acc_ref, r_ref, *,
               tq, n_old, wn):
    kb = KEY_BLOCK
    i = pl.program_id(1)
    n_head, n_full = n_old % kb, n_old // kb
    lane = lax.broadcasted_iota(jnp.int32, (tq, LANES), 1)

    def suffix_ones(width):
        r = lax.broadcasted_iota(jnp.int32, (width, width), 0)
        c = lax.broadcasted_iota(jnp.int32, (width, width), 1)
        return jnp.where(r >= c, 1.0, 0.0).astype(BF16)

    ones = {width: suffix_ones(width) for width in {kb, wn, WIDE_BLOCK}}

    for h in range(SB_HEADS):
        p = h // 2
        qp = q_ref[0, :, p * LANES:(p + 1) * LANES]
        keep = (lane < SB_HEAD_DIM) if h % 2 == 0 else (lane >= SB_HEAD_DIM)
        qm_ref[h] = jnp.where(keep, qp, jnp.zeros_like(qp))
        r_ref[h] = jnp.zeros((tq, LANES), F32)
        acc_ref[h] = jnp.zeros((tq, LANES), F32)

    def sweep(k_ref, v_ref, start, width, mask):
        k_all = k_ref[0, pl.ds(start, width), :].astype(BF16)
        v_all = v_ref[0, pl.ds(start, width), :].astype(BF16)
        heads = range(SB_HEADS)
        pair = lambda a, h: a[:, (h // 2) * LANES:(h // 2 + 1) * LANES]
        zs = [lax.dot_general(qm_ref[h], pair(k_all, h), _NT, preferred_element_type=F32)
              for h in heads]
        his, los = [], []
        for h in heads:
            z = zs[h]
            ls = jnp.minimum(-z, 0.0) - jnp.log(1.0 + jnp.exp(-jnp.abs(z)))
            if mask is not None:
                ls = jnp.where(mask, ls, 0.0)
            hi = ls.astype(BF16)
            his.append(hi)
            los.append((ls - hi.astype(F32)).astype(BF16))
        cums = [jnp.dot(his[h], ones[width], preferred_element_type=F32)
                + jnp.dot(los[h], ones[width], preferred_element_type=F32) for h in heads]
        ws = []
        for h in heads:
            r = r_ref[h]
            cr = cums[h] + jnp.concatenate([r] * (width // LANES), axis=1)
            a = jnp.exp(zs[h] + cr)
            if mask is not None:
                a = jnp.where(mask, a, 0.0)
            ws.append(a.astype(BF16))
            r_ref[h] = jnp.broadcast_to(cr[:, 0:1], (tq, LANES))
        for h in heads:
            acc_ref[h] += jnp.dot(ws[h], pair(v_all, h), preferred_element_type=F32)

    def key_before(width, k0, q_first):
        row = lax.broadcasted_iota(jnp.int32, (tq, width), 0)
        col = lax.broadcasted_iota(jnp.int32, (tq, width), 1)
        return col + k0 < row + q_first

    q0 = i * tq
    j0 = q0 // wn
    sweep(kn_ref, vn_ref, pl.multiple_of(j0 * wn, wn), wn, key_before(wn, j0 * wn, q0))

    def new_body(t, carry):
        sweep(kn_ref, vn_ref, pl.multiple_of((j0 - 1 - t) * wn, wn), wn, None)
        return carry

    lax.fori_loop(0, j0, new_body, 0)

    n_wide = (n_full * kb) // WIDE_BLOCK
    rest = n_full * kb - n_wide * WIDE_BLOCK
    if n_wide:
        def old_body(t, carry):
            start = n_head + rest + (n_wide - 1 - t) * WIDE_BLOCK
            sweep(ko_ref, vo_ref, pl.multiple_of(start, 8), WIDE_BLOCK, None)
            return carry

        lax.fori_loop(0, n_wide, old_body, 0)
    if rest:
        sweep(ko_ref, vo_ref, n_head, rest, None)
    if n_head:
        sweep(ko_ref, vo_ref, 0, kb, lax.broadcasted_iota(jnp.int32, (tq, kb), 1) < n_head)

    for p in range(SB_HEADS // 2):
        o_ref[0, :, p * LANES:(p + 1) * LANES] = jnp.where(
            lane < SB_HEAD_DIM, acc_ref[2 * p], acc_ref[2 * p + 1]).astype(o_ref.dtype)


def _sb_call(q, k_new, v_new, k_old, v_old, n_old, tq):
    s, lq, w = q.shape
    ln = k_new.shape[1]
    lo = k_old.shape[1]
    assert lq % tq == 0 and ln % KEY_BLOCK == 0 and lo >= max(n_old, KEY_BLOCK)
    assert tq <= KEY_BLOCK and KEY_BLOCK % tq == 0
    wn = WIDE_BLOCK if ln % WIDE_BLOCK == 0 else KEY_BLOCK
    old_map = (lambda b, i: (b, 0, 0)) if k_old.shape[0] == s else (lambda b, i: (0, 0, 0))
    return pl.pallas_call(
        functools.partial(_sb_kernel, tq=tq, n_old=n_old, wn=wn),
        grid=(s, lq // tq),
        in_specs=[pl.BlockSpec((1, tq, w), lambda b, i: (b, i, 0)),
                  pl.BlockSpec((1, ln, w), lambda b, i: (b, 0, 0)),
                  pl.BlockSpec((1, ln, w), lambda b, i: (b, 0, 0)),
                  pl.BlockSpec((1, lo, w), old_map),
                  pl.BlockSpec((1, lo, w), old_map)],
        out_specs=pl.BlockSpec((1, tq, w), lambda b, i: (b, i, 0)),
        out_shape=jax.ShapeDtypeStruct((s, lq, w), BF16),
        scratch_shapes=[pltpu.VMEM((SB_HEADS, tq, LANES), BF16),
                        pltpu.VMEM((SB_HEADS, tq, LANES), F32),
                        pltpu.VMEM((SB_HEADS, tq, LANES), F32)],
        compiler_params=_params(("parallel", "parallel")),
        name="sb",
    )(q, k_new, v_new, k_old, v_old)


def _pad_rows(x, rows):
    if x.shape[0] == rows:
        return x
    return jnp.concatenate([x, jnp.zeros((rows - x.shape[0], x.shape[1]), x.dtype)], axis=0)


def _hg_kernel(q_ref, k_ref, i_ref, lf_ref, s0_ref, o_ref, sout_ref, st_ref, *, chunk):
    c = pl.program_id(1)
    hd = HG_HEAD_DIM

    @pl.when(c == 0)
    def _():
        st_ref[...] = s0_ref[0]

    trow = lax.broadcasted_iota(jnp.int32, (chunk, chunk), 0)
    tcol = lax.broadcasted_iota(jnp.int32, (chunk, chunk), 1)
    causal = trow >= tcol
    tri = jnp.where(causal, 1.0, 0.0).astype(BF16)
    lf = lf_ref[0]
    hi = lf.astype(BF16)
    r1 = lf - hi.astype(F32)
    mid = r1.astype(BF16)
    lo = (r1 - mid.astype(F32)).astype(BF16)
    b = (jnp.dot(tri, hi, preferred_element_type=F32)
         + jnp.dot(tri, mid, preferred_element_type=F32)
         + jnp.dot(tri, lo, preferred_element_type=F32))
    b_last = b[chunk - 1:chunk, :]
    safe = jnp.min(b_last) >= HG_SAFE_LOG_DECAY

    @pl.when(safe)
    def _():
        q = q_ref[0]
        k = k_ref[0]
        qd = (q * jnp.exp(b)).astype(BF16)
        kd = (k * jnp.exp(-b)).astype(BF16)
        kl = (k * jnp.exp(b_last - b)).astype(BF16)
        iv = i_ref[0].astype(BF16)
        dl = jnp.exp(b_last)
        for h in range(HG_HEADS):
            sl = slice(h * hd, (h + 1) * hd)
            sc = lax.dot_general(qd[:, sl], kd[:, sl], _NT, preferred_element_type=F32)
            sc = jnp.where(causal, sc, 0.0).astype(BF16)
            st = st_ref[h]
            o_ref[0, :, sl] = (jnp.dot(sc, iv[:, sl], preferred_element_type=F32)
                               + lax.dot_general(qd[:, sl], st.astype(BF16), _NT,
                                                 preferred_element_type=F32))
            st_ref[h] = st * dl[:, sl] + lax.dot_general(iv[:, sl], kl[:, sl], _TN,
                                                         preferred_element_type=F32)

    @pl.when(jnp.logical_not(safe))
    def _():
        lane_t = lax.broadcasted_iota(jnp.int32, (hd, hd), 1)
        row_t = lax.broadcasted_iota(jnp.int32, (hd, hd), 0)
        for h in range(HG_HEADS):
            sl = slice(h * hd, (h + 1) * hd)
            f_t = _pad_rows(jnp.exp(lf[:, sl]), hd).T
            k_t = _pad_rows(k_ref[0, :, sl], hd).T
            q_t = _pad_rows(q_ref[0, :, sl], hd).T
            i_p = _pad_rows(i_ref[0, :, sl], hd)

            def body(t, carry, f_t=f_t, k_t=k_t, q_t=q_t, i_p=i_p):
                s, o = carry
                sel = lane_t == t
                fc = jnp.sum(jnp.where(sel, f_t, 0.0), axis=1, keepdims=True)
                kc = jnp.sum(jnp.where(sel, k_t, 0.0), axis=1, keepdims=True)
                qc = jnp.sum(jnp.where(sel, q_t, 0.0), axis=1, keepdims=True)
                i_row = jnp.sum(jnp.where(row_t == t, i_p, 0.0), axis=0, keepdims=True)
                s = s * fc + kc * i_row
                o = jnp.where(row_t == t, jnp.sum(qc * s, axis=0, keepdims=True), o)
                return s, o

            s, o = lax.fori_loop(0, chunk, body, (st_ref[h].T, jnp.zeros((hd, hd), F32)))
            st_ref[h] = s.T
            o_ref[0, :, sl] = o[:chunk]

    @pl.when(c == pl.num_programs(1) - 1)
    def _():
        sout_ref[0] = st_ref[...]


def _hg_call(q, k, iv, lf, s0t, chunk):
    s, l, w = q.shape
    assert l % chunk == 0
    seq_spec = pl.BlockSpec((1, chunk, w), lambda b, c: (b, c, 0))
    st_shape = (1, HG_HEADS, HG_HEAD_DIM, HG_HEAD_DIM)
    s0_map = (lambda b, c: (b, 0, 0, 0)) if s0t.shape[0] == s else (lambda b, c: (0, 0, 0, 0))
    return pl.pallas_call(
        functools.partial(_hg_kernel, chunk=chunk),
        grid=(s, l // chunk),
        in_specs=[seq_spec, seq_spec, seq_spec, seq_spec, pl.BlockSpec(st_shape, s0_map)],
        out_specs=[seq_spec, pl.BlockSpec(st_shape, lambda b, c: (b, 0, 0, 0))],
        out_shape=[jax.ShapeDtypeStruct((s, l, w), F32),
                   jax.ShapeDtypeStruct((s,) + st_shape[1:], F32)],
        scratch_shapes=[pltpu.VMEM(st_shape[1:], F32)],
        compiler_params=_params(("parallel", "arbitrary")),
        name="hgrn",
    )(q, k, iv, lf, s0t)


def _post_kernel(x_ref, oa_ref, ob_ref, sg_ref, gates_ref, hn_ref, wa_ref, wb_ref, wo_ref,
                 n_ref, wg_ref, wu_ref, wd_ref, fn_ref, y_ref, acc_ref):
    hd = HG_HEAD_DIM
    d = x_ref.shape[1]
    parts = []
    for h in range(HG_HEADS):
        sl = slice(h * hd, (h + 1) * hd)
        parts.append(_rms(ob_ref[:, sl], hn_ref[:, sl]) * sg_ref[:, sl])
    ob = jnp.concatenate(parts, axis=-1).astype(BF16)
    ma = jnp.dot(oa_ref[...], wa_ref[...], preferred_element_type=F32)
    mb = jnp.dot(ob, wb_ref[...], preferred_element_type=F32)
    merged = gates_ref[:, :d] * ma + gates_ref[:, d:] * mb
    x = x_ref[...] + jnp.dot(merged.astype(BF16), wo_ref[...], preferred_element_type=F32)
    x = _half_ffn(x, n_ref[...], wg_ref, wu_ref, wd_ref, acc_ref)
    y_ref[...] = _rms(x, fn_ref[...])


def _post_call(x, oa, ob, sg, gates, hn, wa, wb, wo, norm, wg, wu, wd, fnorm, tm):
    n, d = x.shape
    consts = [hn, wa, wb, wo, norm, wg, wu, wd, fnorm]
    return pl.pallas_call(
        _post_kernel,
        grid=(n // tm,),
        in_specs=[_rows_spec(tm, d), _rows_spec(tm, oa.shape[1]), _rows_spec(tm, ob.shape[1]),
                  _rows_spec(tm, sg.shape[1]), _rows_spec(tm, gates.shape[1])]
                 + [_const_spec(a.shape) for a in consts],
        out_specs=_rows_spec(tm, d),
        out_shape=jax.ShapeDtypeStruct((n, d), F32),
        scratch_shapes=[pltpu.VMEM((tm, d), F32)],
        compiler_params=_params(("parallel",)),
        name="post",
    )(x, oa, ob, sg, gates, *consts)


def _ff_chunks(w_gate, w_up, w_down):
    d, dff = w_gate.shape
    assert dff % FF_CHUNK == 0
    nc = dff // FF_CHUNK
    cols = lambda w: w.astype(BF16).reshape(d, nc, FF_CHUNK).transpose(1, 0, 2)
    return cols(w_gate), cols(w_up), w_down.astype(BF16).reshape(nc, FF_CHUNK, d)


def _row_tile(n, target):
    tm = min(n, target)
    assert n % tm == 0 and tm % 8 == 0
    return tm


def kernel(x_prompt, x_sample, cache_sb_k, cache_sb_v, state_hgrn, meta_tokens, ffn1_norm, ffn1_w_gate, ffn1_w_up, ffn1_w_down, mix_norm, w_in, b_gate, hg_lb_logits, hg_out_norm, w_branch_a, w_branch_b, w_out, ffn2_norm, ffn2_w_gate, ffn2_w_up, ffn2_w_down, final_norm):
    depth = w_in.shape[0]
    assert depth == 1, "single-layer step"
    nb, seq, d = x_prompt.shape
    ns, t_new, _ = x_sample.shape
    n_meta = meta_tokens.shape[0]
    l_cache = cache_sb_k.shape[2]
    w = SB_WIDTH
    row = lambda a: a.reshape(1, -1)

    f1 = _ff_chunks(ffn1_w_gate[0], ffn1_w_up[0], ffn1_w_down[0])
    f2 = _ff_chunks(ffn2_w_gate[0], ffn2_w_up[0], ffn2_w_down[0])
    w_in_b = w_in[0].astype(BF16)
    wa, wb, wo = w_branch_a[0].astype(BF16), w_branch_b[0].astype(BF16), w_out[0].astype(BF16)

    def pre(rows, tm):
        x1 = _ffn_call(rows, row(ffn1_norm[0]), *f1, tm)
        return x1, _mix_call(x1, row(mix_norm[0]), w_in_b, row(b_gate[0]), hg_lb_logits, tm)

    def post(x1, oa, ob, sg, gates, tm):
        return _post_call(x1, oa, ob, sg, gates, row(hg_out_norm[0]), wa, wb, wo,
                          row(ffn2_norm[0]), *f2, row(final_norm), tm)

    n_frames = nb * seq
    x1_p, (q_p, k_p, kb_p, v_p, vb_p, hq_p, hk_p, hi_p, hlf_p, sg_p, gates_p) = pre(
        x_prompt.reshape(n_frames, d), _row_tile(n_frames, 512))
    n_run = ns * t_new
    small = jnp.concatenate([x_sample.reshape(n_run, d), meta_tokens.astype(F32)], axis=0)
    x1_s, (q_s, k_s, kb_s, v_s, vb_s, hq_s, hk_s, hi_s, hlf_s, sg_s, gates_s) = pre(
        small, small.shape[0])
    run = lambda a: a[:n_run]
    meta = lambda a: a[n_run:]

    seq3 = lambda a, s: a.reshape(s, -1, a.shape[-1])
    pad_to = lambda a, rows: jnp.pad(a, ((0, 0), (0, rows - a.shape[1]), (0, 0)))
    oa_p = _sb_call(seq3(q_p, nb), seq3(kb_p, nb), seq3(vb_p, nb),
                    pad_to(meta(kb_s)[None], KEY_BLOCK), pad_to(meta(vb_s)[None], KEY_BLOCK),
                    n_meta, KEY_BLOCK)
    oa_s = _sb_call(seq3(run(q_s), ns), pad_to(seq3(run(kb_s), ns), KEY_BLOCK),
                    pad_to(seq3(run(vb_s), ns), KEY_BLOCK),
                    cache_sb_k[0].reshape(ns, l_cache, w), cache_sb_v[0].reshape(ns, l_cache, w),
                    l_cache, t_new)

    hg = lambda sel, s, arrs: [seq3(sel(a), s) for a in arrs]
    zero_state = jnp.zeros((1, HG_HEADS, HG_HEAD_DIM, HG_HEAD_DIM), F32)
    _, st_meta = _hg_call(*hg(meta, 1, (hq_s, hk_s, hi_s, hlf_s)), zero_state, n_meta)
    ob_p, st_p = _hg_call(*hg(lambda a: a, nb, (hq_p, hk_p, hi_p, hlf_p)), st_meta, HG_CHUNK)
    ob_s, st_s = _hg_call(*hg(run, ns, (hq_s, hk_s, hi_s, hlf_s)),
                          jnp.swapaxes(state_hgrn[0].astype(F32), -1, -2), t_new)

    y_p = post(x1_p, oa_p.reshape(n_frames, w), ob_p.reshape(n_frames, HG_WIDTH), sg_p, gates_p,
               _row_tile(n_frames, 512))
    y_s = post(run(x1_s), oa_s.reshape(n_run, w), ob_s.reshape(n_run, HG_WIDTH), run(sg_s),
               run(gates_s), n_run)

    def with_meta(frames, meta_rows):
        m = jnp.broadcast_to(meta_rows[None], (nb, n_meta, w))
        full = jnp.concatenate([m, frames.reshape(nb, seq, w)], axis=1)
        return full.reshape(1, nb, n_meta + seq, SB_HEADS, SB_HEAD_DIM)

    heads = lambda a: a.reshape(1, ns, t_new, SB_HEADS, SB_HEAD_DIM)
    return (y_p.reshape(nb, seq, d), y_s.reshape(ns, t_new, d),
            with_meta(k_p, meta(k_s)), with_meta(v_p, meta(v_s)),
            jnp.swapaxes(st_p, -1, -2)[None],
            heads(run(k_s)), heads(run(v_s)),
            jnp.swapaxes(st_s, -1, -2)[None])
```

```python
import functools

import jax
import jax.numpy as jnp
from jax import lax
from jax.experimental import pallas as pl
from jax.experimental.pallas import tpu as pltpu

F32 = jnp.float32
BF16 = jnp.bfloat16
EPS = 1e-6
N_META = 16
SB_HEADS = 8
SB_HEAD_DIM = 64
SB_WIDTH = SB_HEADS * SB_HEAD_DIM
HG_HEADS = 4
HG_HEAD_DIM = 128
HG_WIDTH = HG_HEADS * HG_HEAD_DIM
LANES = 128
FF_CHUNK = 256
KEY_BLOCK = 128
WIDE_BLOCK = 256
F32_EXP_UNDERFLOW = -105.0
HG_CHUNK = 64
HG_SAFE_LOG_DECAY = -60.0
VMEM_LIMIT_BYTES = 56 * 1024 * 1024

_NT = (((1,), (1,)), ((), ()))
_TN = (((0,), (0,)), ((), ()))


def _const_spec(shape):
    nd = len(shape)
    return pl.BlockSpec(shape, lambda *_: (0,) * nd, pipeline_mode=pl.Buffered(1))


def _rows_spec(tm, width):
    return pl.BlockSpec((tm, width), lambda i: (i, 0))


def _params(sem):
    return pltpu.CompilerParams(dimension_semantics=sem, vmem_limit_bytes=VMEM_LIMIT_BYTES)


def _rms(x, g):
    return x * lax.rsqrt(jnp.mean(x * x, axis=-1, keepdims=True) + EPS) * g


def _half_ffn(x, g, wg_ref, wu_ref, wd_ref, acc_ref):
    h = _rms(x, g).astype(BF16)
    dff = wg_ref.shape[1]
    assert dff % FF_CHUNK == 0
    for c in range(dff // FF_CHUNK):
        cols = slice(c * FF_CHUNK, (c + 1) * FF_CHUNK)
        gate = jnp.dot(h, wg_ref[:, cols], preferred_element_type=F32)
        up = jnp.dot(h, wu_ref[:, cols], preferred_element_type=F32)
        act = (gate * jax.nn.sigmoid(gate) * up).astype(BF16)
        contrib = jnp.dot(act, wd_ref[cols, :], preferred_element_type=F32)
        if c == 0:
            acc_ref[...] = contrib
        else:
            acc_ref[...] += contrib
    return x + 0.5 * acc_ref[...]


def _ffn_kernel(x_ref, n_ref, wg_ref, wu_ref, wd_ref, o_ref, acc_ref):
    o_ref[...] = _half_ffn(x_ref[...], n_ref[...], wg_ref, wu_ref, wd_ref, acc_ref)


def _ffn_call(x, norm, wg, wu, wd, tm):
    n, d = x.shape
    return pl.pallas_call(
        _ffn_kernel,
        grid=(n // tm,),
        in_specs=[_rows_spec(tm, d), _const_spec(norm.shape), _const_spec(wg.shape),
                  _const_spec(wu.shape), _const_spec(wd.shape)],
        out_specs=_rows_spec(tm, d),
        out_shape=jax.ShapeDtypeStruct((n, d), F32),
        scratch_shapes=[pltpu.VMEM((tm, d), F32)],
        compiler_params=_params(("parallel",)),
        name="ffn",
    )(x, norm, wg, wu, wd)


def _mix_kernel(x_ref, n_ref, win_ref, bg_ref, lbl_ref,
                q_ref, k_ref, kb_ref, v_ref, vb_ref,
                hq_ref, hk_ref, hi_ref, hlf_ref, sg_ref, gates_ref):
    h = _rms(x_ref[...], n_ref[...]).astype(BF16)

    def proj(lo, width):
        return jnp.dot(h, win_ref[:, lo:lo + width], preferred_element_type=F32)

    w = SB_WIDTH
    q_ref[...] = (proj(0, w) * (SB_HEAD_DIM ** -0.5)).astype(BF16)
    k = proj(w, w)
    k_ref[...] = k
    kb_ref[...] = k.astype(BF16)
    v = proj(2 * w, w)
    v_ref[...] = v
    vb_ref[...] = v.astype(BF16)

    logits = lbl_ref[...]
    l0, l1 = logits[0:1], logits[1:2]
    m = jnp.maximum(l0, l1)
    e0, e1 = jnp.exp(l0 - m), jnp.exp(l1 - m)
    p0, p1 = e0 / (e0 + e1), e1 / (e0 + e1)
    lb = (p0 + p1) - p0
    f = lb + (1.0 - lb) * jax.nn.sigmoid(proj(3 * w, w))
    hk_ref[...] = 1.0 - f
    hlf_ref[...] = jnp.log(f)
    hi_ref[...] = proj(4 * w, w)
    hq_ref[...] = proj(5 * w, w)
    g = proj(6 * w, w)
    sg_ref[...] = g * jax.nn.sigmoid(g)
    d2 = gates_ref.shape[1]
    gates_ref[...] = jax.nn.sigmoid(proj(7 * w, d2) + bg_ref[...])


def _mix_call(x, norm, w_in, b_gate, lb_logits, tm):
    n, d = x.shape
    w = SB_WIDTH
    widths = [(w, BF16), (w, F32), (w, BF16), (w, F32), (w, BF16),
              (w, F32), (w, F32), (w, F32), (w, F32), (w, F32), (2 * d, F32)]
    return pl.pallas_call(
        _mix_kernel,
        grid=(n // tm,),
        in_specs=[_rows_spec(tm, d), _const_spec(norm.shape), _const_spec(w_in.shape),
                  _const_spec(b_gate.shape), _const_spec(lb_logits.shape)],
        out_specs=[_rows_spec(tm, wd) for wd, _ in widths],
        out_shape=[jax.ShapeDtypeStruct((n, wd), dt) for wd, dt in widths],
        compiler_params=_params(("parallel",)),
        name="mix",
    )(x, norm, w_in, b_gate, lb_logits)


def _sb_kernel(q_ref, kn_ref, vn_ref, ko_ref, vo_ref, o_ref, qm_ref, acc_ref, r_ref, *,
               tq, n_old, wn):
    kb = KEY_BLOCK
    i = pl.program_id(1)
    n_head, n_full = n_old % kb, n_old // kb
    lane = lax.broadcasted_iota(jnp.int32, (tq, LANES), 1)

    def suffix_ones(width):
        r = lax.broadcasted_iota(jnp.int32, (width, width), 0)
        c = lax.broadcasted_iota(jnp.int32, (width, width), 1)
        return jnp.where(r >= c, 1.0, 0.0).astype(BF16)

    ones = {width: suffix_ones(width) for width in {kb, wn, WIDE_BLOCK}}

    for h in range(SB_HEADS):
        p = h // 2
        qp = q_ref[0, :, p * LANES:(p + 1) * LANES]
        keep = (lane < SB_HEAD_DIM) if h % 2 == 0 else (lane >= SB_HEAD_DIM)
        qm_ref[h] = jnp.where(keep, qp, jnp.zeros_like(qp))
        r_ref[h] = jnp.zeros((tq, LANES), F32)
        acc_ref[h] = jnp.zeros((tq, LANES), F32)

    def sweep(k_ref, v_ref, start, width, mask):
        k_all = k_ref[0, pl.ds(start, width), :].astype(BF16)
        v_all = v_ref[0, pl.ds(start, width), :].astype(BF16)
        heads = range(SB_HEADS)
        pair = lambda a, h: a[:, (h // 2) * LANES:(h // 2 + 1) * LANES]
        zs = [lax.dot_general(qm_ref[h], pair(k_all, h), _NT, preferred_element_type=F32)
              for h in heads]
        his, los = [], []
        for h in heads:
            z = zs[h]
            ls = jnp.minimum(-z, 0.0) - jnp.log(1.0 + jnp.exp(-jnp.abs(z)))
            if mask is not None:
                ls = jnp.where(mask, ls, 0.0)
            hi = ls.astype(BF16)
            his.append(hi)
            los.append((ls - hi.astype(F32)).astype(BF16))
        cums = [jnp.dot(his[h], ones[width], preferred_element_type=F32)
                + jnp.dot(los[h], ones[width], preferred_element_type=F32) for h in heads]
        ws = []
        for h in heads:
            r = r_ref[h]
            cr = cums[h] + jnp.concatenate([r] * (width // LANES), axis=1)
            a = jnp.exp(zs[h] + cr)
            if mask is not None:
                a = jnp.where(mask, a, 0.0)
            ws.append(a.astype(BF16))
            r_ref[h] = jnp.broadcast_to(cr[:, 0:1], (tq, LANES))
        for h in heads:
            acc_ref[h] += jnp.dot(ws[h], pair(v_all, h), preferred_element_type=F32)

    def key_before(width, k0, q_first):
        row = lax.broadcasted_iota(jnp.int32, (tq, width), 0)
        col = lax.broadcasted_iota(jnp.int32, (tq, width), 1)
        return col + k0 < row + q_first

    def live():
        m = r_ref[0]
        for h in range(1, SB_HEADS):
            m = jnp.maximum(m, r_ref[h])
        return (jnp.max(m) >= F32_EXP_UNDERFLOW).astype(jnp.int32)

    def sweep_while(n_blocks, alive, block):
        def body(carry):
            t, _ = carry
            block(t)
            return t + 1, live()

        _, alive = lax.while_loop(lambda c: jnp.logical_and(c[0] < n_blocks, c[1] > 0),
                                  body, (jnp.int32(0), alive))
        return alive

    q0 = i * tq
    j0 = q0 // wn
    sweep(kn_ref, vn_ref, pl.multiple_of(j0 * wn, wn), wn, key_before(wn, j0 * wn, q0))
    alive = sweep_while(j0, live(), lambda t: sweep(
        kn_ref, vn_ref, pl.multiple_of((j0 - 1 - t) * wn, wn), wn, None))

    n_wide = (n_full * kb) // WIDE_BLOCK
    rest = n_full * kb - n_wide * WIDE_BLOCK
    if n_wide:
        alive = sweep_while(n_wide, alive, lambda t: sweep(
            ko_ref, vo_ref, pl.multiple_of(n_head + rest + (n_wide - 1 - t) * WIDE_BLOCK, 8),
            WIDE_BLOCK, None))
    if rest:
        alive = sweep_while(1, alive, lambda t: sweep(ko_ref, vo_ref, n_head, rest, None))
    if n_head:
        sweep_while(1, alive, lambda t: sweep(
            ko_ref, vo_ref, 0, kb, lax.broadcasted_iota(jnp.int32, (tq, kb), 1) < n_head))

    for p in range(SB_HEADS // 2):
        o_ref[0, :, p * LANES:(p + 1) * LANES] = jnp.where(
            lane < SB_HEAD_DIM, acc_ref[2 * p], acc_ref[2 * p + 1]).astype(o_ref.dtype)


def _sb_call(q, k_new, v_new, k_old, v_old, n_old, tq):
    s, lq, w = q.shape
    ln = k_new.shape[1]
    lo = k_old.shape[1]
    assert lq % tq == 0 and ln % KEY_BLOCK == 0 and lo >= max(n_old, KEY_BLOCK)
    assert tq <= KEY_BLOCK and KEY_BLOCK % tq == 0
    wn = WIDE_BLOCK if ln % WIDE_BLOCK == 0 else KEY_BLOCK
    old_map = (lambda b, i: (b, 0, 0)) if k_old.shape[0] == s else (lambda b, i: (0, 0, 0))
    return pl.pallas_call(
        functools.partial(_sb_kernel, tq=tq, n_old=n_old, wn=wn),
        grid=(s, lq // tq),
        in_specs=[pl.BlockSpec((1, tq, w), lambda b, i: (b, i, 0)),
                  pl.BlockSpec((1, ln, w), lambda b, i: (b, 0, 0)),
                  pl.BlockSpec((1, ln, w), lambda b, i: (b, 0, 0)),
                  pl.BlockSpec((1, lo, w), old_map),
                  pl.BlockSpec((1, lo, w), old_map)],
        out_specs=pl.BlockSpec((1, tq, w), lambda b, i: (b, i, 0)),
        out_shape=jax.ShapeDtypeStruct((s, lq, w), BF16),
        scratch_shapes=[pltpu.VMEM((SB_HEADS, tq, LANES), BF16),
                        pltpu.VMEM((SB_HEADS, tq, LANES), F32),
                        pltpu.VMEM((SB_HEADS, tq, LANES), F32)],
        compiler_params=_params(("parallel", "parallel")),
        name="sb",
    )(q, k_new, v_new, k_old, v_old)


def _pad_rows(x, rows):
    if x.shape[0] == rows:
        return x
    return jnp.concatenate([x, jnp.zeros((rows - x.shape[0], x.shape[1]), x.dtype)], axis=0)


def _hg_kernel(q_ref, k_ref, i_ref, lf_ref, s0_ref, o_ref, sout_ref, st_ref, *, chunk):
    c = pl.program_id(0)
    hd = HG_HEAD_DIM
    ns = q_ref.shape[0]
    streams = range(ns)
    heads = [(b, h, slice(h * hd, (h + 1) * hd)) for b in streams for h in range(HG_HEADS)]

    @pl.when(c == 0)
    def _():
        for b in streams:
            st_ref[b] = s0_ref[b if s0_ref.shape[0] == ns else 0]

    trow = lax.broadcasted_iota(jnp.int32, (chunk, chunk), 0)
    tcol = lax.broadcasted_iota(jnp.int32, (chunk, chunk), 1)
    causal = trow >= tcol
    tri = jnp.where(causal, 1.0, 0.0).astype(BF16)

    terms = []
    for b in streams:
        lf = lf_ref[b]
        hi = lf.astype(BF16)
        r1 = lf - hi.astype(F32)
        mid = r1.astype(BF16)
        terms.append((hi, mid, (r1 - mid.astype(F32)).astype(BF16)))
    cum = [sum(jnp.dot(tri, t, preferred_element_type=F32) for t in terms[b]) for b in streams]
    last = [cb[chunk - 1:chunk, :] for cb in cum]
    weakest = last[0]
    for lb in last[1:]:
        weakest = jnp.minimum(weakest, lb)
    safe = jnp.min(weakest) >= HG_SAFE_LOG_DECAY

    @pl.when(safe)
    def _():
        qd, kd, kl, iv, dl = [], [], [], [], []
        for b in streams:
            k = k_ref[b]
            qd.append((q_ref[b] * jnp.exp(cum[b])).astype(BF16))
            kd.append((k * jnp.exp(-cum[b])).astype(BF16))
            kl.append((k * jnp.exp(last[b] - cum[b])).astype(BF16))
            iv.append(i_ref[b].astype(BF16))
            dl.append(jnp.exp(last[b]))
        sc = [lax.dot_general(qd[b][:, sl], kd[b][:, sl], _NT, preferred_element_type=F32)
              for b, h, sl in heads]
        carried = [lax.dot_general(qd[b][:, sl], st_ref[b, h].astype(BF16), _NT,
                                   preferred_element_type=F32) for b, h, sl in heads]
        grown = [lax.dot_general(iv[b][:, sl], kl[b][:, sl], _TN, preferred_element_type=F32)
                 for b, h, sl in heads]
        for n, (b, h, sl) in enumerate(heads):
            within = jnp.dot(jnp.where(causal, sc[n], 0.0).astype(BF16), iv[b][:, sl],
                             preferred_element_type=F32)
            o_ref[b, :, sl] = within + carried[n]
            st_ref[b, h] = st_ref[b, h] * dl[b][:, sl] + grown[n]

    @pl.when(jnp.logical_not(safe))
    def _():
        lane_t = lax.broadcasted_iota(jnp.int32, (hd, hd), 1)
        row_t = lax.broadcasted_iota(jnp.int32, (hd, hd), 0)

        def stream(b, carry):
            for h in range(HG_HEADS):
                sl = slice(h * hd, (h + 1) * hd)
                f_t = _pad_rows(jnp.exp(lf_ref[b, :, sl]), hd).T
                k_t = _pad_rows(k_ref[b, :, sl], hd).T
                q_t = _pad_rows(q_ref[b, :, sl], hd).T
                i_p = _pad_rows(i_ref[b, :, sl], hd)

                def token(t, sc, f_t=f_t, k_t=k_t, q_t=q_t, i_p=i_p):
                    s, o = sc
                    sel = lane_t == t
                    fc = jnp.sum(jnp.where(sel, f_t, 0.0), axis=1, keepdims=True)
                    kc = jnp.sum(jnp.where(sel, k_t, 0.0), axis=1, keepdims=True)
                    qc = jnp.sum(jnp.where(sel, q_t, 0.0), axis=1, keepdims=True)
                    i_row = jnp.sum(jnp.where(row_t == t, i_p, 0.0), axis=0, keepdims=True)
                    s = s * fc + kc * i_row
                    o = jnp.where(row_t == t, jnp.sum(qc * s, axis=0, keepdims=True), o)
                    return s, o

                s, o = lax.fori_loop(0, chunk, token,
                                     (st_ref[b, h].T, jnp.zeros((hd, hd), F32)))
                st_ref[b, h] = s.T
                o_ref[b, :, sl] = o[:chunk]
            return carry

        lax.fori_loop(0, ns, stream, 0)

    @pl.when(c == pl.num_programs(0) - 1)
    def _():
        sout_ref[...] = st_ref[...]


def _hg_call(q, k, iv, lf, s0t, chunk):
    s, l, w = q.shape
    assert l % chunk == 0
    seq_spec = pl.BlockSpec((s, chunk, w), lambda c: (0, c, 0))
    st_shape = (s, HG_HEADS, HG_HEAD_DIM, HG_HEAD_DIM)
    return pl.pallas_call(
        functools.partial(_hg_kernel, chunk=chunk),
        grid=(l // chunk,),
        in_specs=[seq_spec, seq_spec, seq_spec, seq_spec,
                  pl.BlockSpec(s0t.shape, lambda c: (0, 0, 0, 0))],
        out_specs=[seq_spec, pl.BlockSpec(st_shape, lambda c: (0, 0, 0, 0))],
        out_shape=[jax.ShapeDtypeStruct((s, l, w), F32), jax.ShapeDtypeStruct(st_shape, F32)],
        scratch_shapes=[pltpu.VMEM(st_shape, F32)],
        compiler_params=_params(("arbitrary",)),
        name="hgrn",
    )(q, k, iv, lf, s0t)


def _post_kernel(x_ref, oa_ref, ob_ref, sg_ref, gates_ref, hn_ref, wa_ref, wb_ref, wo_ref,
                 n_ref, wg_ref, wu_ref, wd_ref, fn_ref, y_ref, acc_ref):
    hd = HG_HEAD_DIM
    d = x_ref.shape[1]
    parts = []
    for h in range(HG_HEADS):
        sl = slice(h * hd, (h + 1) * hd)
        parts.append(_rms(ob_ref[:, sl], hn_ref[:, sl]) * sg_ref[:, sl])
    ob = jnp.concatenate(parts, axis=-1).astype(BF16)
    ma = jnp.dot(oa_ref[...], wa_ref[...], preferred_element_type=F32)
    mb = jnp.dot(ob, wb_ref[...], preferred_element_type=F32)
    merged = gates_ref[:, :d] * ma + gates_ref[:, d:] * mb
    x = x_ref[...] + jnp.dot(merged.astype(BF16), wo_ref[...], preferred_element_type=F32)
    x = _half_ffn(x, n_ref[...], wg_ref, wu_ref, wd_ref, acc_ref)
    y_ref[...] = _rms(x, fn_ref[...])


def _post_call(x, oa, ob, sg, gates, hn, wa, wb, wo, norm, wg, wu, wd, fnorm, tm):
    n, d = x.shape
    consts = [hn, wa, wb, wo, norm, wg, wu, wd, fnorm]
    return pl.pallas_call(
        _post_kernel,
        grid=(n // tm,),
        in_specs=[_rows_spec(tm, d), _rows_spec(tm, oa.shape[1]), _rows_spec(tm, ob.shape[1]),
                  _rows_spec(tm, sg.shape[1]), _rows_spec(tm, gates.shape[1])]
                 + [_const_spec(a.shape) for a in consts],
        out_specs=_rows_spec(tm, d),
        out_shape=jax.ShapeDtypeStruct((n, d), F32),
        scratch_shapes=[pltpu.VMEM((tm, d), F32)],
        compiler_params=_params(("parallel",)),
        name="post",
    )(x, oa, ob, sg, gates, *consts)


def _row_tile(n, target):
    tm = min(n, target)
    assert n % tm == 0 and tm % 8 == 0
    return tm


def kernel(x_prompt, x_sample, cache_sb_k, cache_sb_v, state_hgrn, meta_tokens, ffn1_norm, ffn1_w_gate, ffn1_w_up, ffn1_w_down, mix_norm, w_in, b_gate, hg_lb_logits, hg_out_norm, w_branch_a, w_branch_b, w_out, ffn2_norm, ffn2_w_gate, ffn2_w_up, ffn2_w_down, final_norm):
    depth = w_in.shape[0]
    assert depth == 1, "single-layer step"
    nb, seq, d = x_prompt.shape
    ns, t_new, _ = x_sample.shape
    n_meta = meta_tokens.shape[0]
    l_cache = cache_sb_k.shape[2]
    w = SB_WIDTH
    row = lambda a: a.reshape(1, -1)

    f1 = (ffn1_w_gate[0].astype(BF16), ffn1_w_up[0].astype(BF16), ffn1_w_down[0].astype(BF16))
    f2 = (ffn2_w_gate[0].astype(BF16), ffn2_w_up[0].astype(BF16), ffn2_w_down[0].astype(BF16))
    w_in_b = w_in[0].astype(BF16)
    wa, wb, wo = w_branch_a[0].astype(BF16), w_branch_b[0].astype(BF16), w_out[0].astype(BF16)

    def pre(rows, tm):
        x1 = _ffn_call(rows, row(ffn1_norm[0]), *f1, tm)
        return x1, _mix_call(x1, row(mix_norm[0]), w_in_b, row(b_gate[0]), hg_lb_logits, tm)

    def post(x1, oa, ob, sg, gates, tm):
        return _post_call(x1, oa, ob, sg, gates, row(hg_out_norm[0]), wa, wb, wo,
                          row(ffn2_norm[0]), *f2, row(final_norm), tm)

    n_frames = nb * seq
    x1_p, (q_p, k_p, kb_p, v_p, vb_p, hq_p, hk_p, hi_p, hlf_p, sg_p, gates_p) = pre(
        x_prompt.reshape(n_frames, d), _row_tile(n_frames, 512))
    n_run = ns * t_new
    small = jnp.concatenate([x_sample.reshape(n_run, d), meta_tokens.astype(F32)], axis=0)
    x1_s, (q_s, k_s, kb_s, v_s, vb_s, hq_s, hk_s, hi_s, hlf_s, sg_s, gates_s) = pre(
        small, small.shape[0])
    run = lambda a: a[:n_run]
    meta = lambda a: a[n_run:]

    seq3 = lambda a, s: a.reshape(s, -1, a.shape[-1])
    pad_to = lambda a, rows: jnp.pad(a, ((0, 0), (0, rows - a.shape[1]), (0, 0)))
    oa_p = _sb_call(seq3(q_p, nb), seq3(kb_p, nb), seq3(vb_p, nb),
                    pad_to(meta(kb_s)[None], KEY_BLOCK), pad_to(meta(vb_s)[None], KEY_BLOCK),
                    n_meta, KEY_BLOCK)
    oa_s = _sb_call(seq3(run(q_s), ns), pad_to(seq3(run(kb_s), ns), KEY_BLOCK),
                    pad_to(seq3(run(vb_s), ns), KEY_BLOCK),
                    cache_sb_k[0].reshape(ns, l_cache, w), cache_sb_v[0].reshape(ns, l_cache, w),
                    l_cache, t_new)

    hg = lambda sel, s, arrs: [seq3(sel(a), s) for a in arrs]
    zero_state = jnp.zeros((1, HG_HEADS, HG_HEAD_DIM, HG_HEAD_DIM), F32)
    _, st_meta = _hg_call(*hg(meta, 1, (hq_s, hk_s, hi_s, hlf_s)), zero_state, n_meta)
    ob_p, st_p = _hg_call(*hg(lambda a: a, nb, (hq_p, hk_p, hi_p, hlf_p)), st_meta, HG_CHUNK)
    ob_s, st_s = _hg_call(*hg(run, ns, (hq_s, hk_s, hi_s, hlf_s)),
                          jnp.swapaxes(state_hgrn[0].astype(F32), -1, -2), t_new)

    y_p = post(x1_p, oa_p.reshape(n_frames, w), ob_p.reshape(n_frames, HG_WIDTH), sg_p, gates_p,
               _row_tile(n_frames, 512))
    y_s = post(run(x1_s), oa_s.reshape(n_run, w), ob_s.reshape(n_run, HG_WIDTH), run(sg_s),
               run(gates_s), n_run)

    def with_meta(frames, meta_rows):
        m = jnp.broadcast_to(meta_rows[None], (nb, n_meta, w))
        full = jnp.concatenate([m, frames.reshape(nb, seq, w)], axis=1)
        return full.reshape(1, nb, n_meta + seq, SB_HEADS, SB_HEAD_DIM)

    heads = lambda a: a.reshape(1, ns, t_new, SB_HEADS, SB_HEAD_DIM)
    return (y_p.reshape(nb, seq, d), y_s.reshape(ns, t_new, d),
            with_meta(k_p, meta(k_s)), with_meta(v_p, meta(v_s)),
            jnp.swapaxes(st_p, -1, -2)[None],
            heads(run(k_s)), heads(run(v_s)),
            jnp.swapaxes(st_s, -1, -2)[None])
```

```python
import functools

import jax
import jax.numpy as jnp
from jax import lax
from jax.experimental import pallas as pl
from jax.experimental.pallas import tpu as pltpu

F32 = jnp.float32
BF16 = jnp.bfloat16
EPS = 1e-6
N_META = 16
SB_HEADS = 8
SB_HEAD_DIM = 64
SB_WIDTH = SB_HEADS * SB_HEAD_DIM
HG_HEADS = 4
HG_HEAD_DIM = 128
HG_WIDTH = HG_HEADS * HG_HEAD_DIM
LANES = 128
FF_CHUNK = 256
KEY_BLOCK = 128
WIDE_BLOCK = 256
F32_EXP_UNDERFLOW = -105.0
HG_CHUNK = 64
HG_SAFE_LOG_DECAY = -60.0
VMEM_LIMIT_BYTES = 56 * 1024 * 1024

_NN = (((1,), (0,)), ((), ()))
_NT = (((1,), (1,)), ((), ()))
_TN = (((0,), (0,)), ((), ()))


def _const_spec(shape):
    nd = len(shape)
    return pl.BlockSpec(shape, lambda *_: (0,) * nd, pipeline_mode=pl.Buffered(1))


def _rows_spec(tm, width):
    return pl.BlockSpec((tm, width), lambda i: (i, 0))


def _params(sem):
    return pltpu.CompilerParams(dimension_semantics=sem, vmem_limit_bytes=VMEM_LIMIT_BYTES)


def _rms(x, g):
    return x * lax.rsqrt(jnp.mean(x * x, axis=-1, keepdims=True) + EPS) * g


def _half_ffn(x, g, wg_ref, wu_ref, wd_ref, acc_ref):
    h = _rms(x, g).astype(BF16)
    dff = wg_ref.shape[1]
    assert dff % FF_CHUNK == 0
    for c in range(dff // FF_CHUNK):
        cols = slice(c * FF_CHUNK, (c + 1) * FF_CHUNK)
        gate = jnp.dot(h, wg_ref[:, cols], preferred_element_type=F32)
        up = jnp.dot(h, wu_ref[:, cols], preferred_element_type=F32)
        act = (gate * jax.nn.sigmoid(gate) * up).astype(BF16)
        contrib = jnp.dot(act, wd_ref[cols, :], preferred_element_type=F32)
        if c == 0:
            acc_ref[...] = contrib
        else:
            acc_ref[...] += contrib
    return x + 0.5 * acc_ref[...]


def _ffn_kernel(x_ref, n_ref, wg_ref, wu_ref, wd_ref, o_ref, acc_ref):
    o_ref[...] = _half_ffn(x_ref[...], n_ref[...], wg_ref, wu_ref, wd_ref, acc_ref)


def _ffn_call(x, norm, wg, wu, wd, tm):
    n, d = x.shape
    return pl.pallas_call(
        _ffn_kernel,
        grid=(n // tm,),
        in_specs=[_rows_spec(tm, d), _const_spec(norm.shape), _const_spec(wg.shape),
                  _const_spec(wu.shape), _const_spec(wd.shape)],
        out_specs=_rows_spec(tm, d),
        out_shape=jax.ShapeDtypeStruct((n, d), F32),
        scratch_shapes=[pltpu.VMEM((tm, d), F32)],
        compiler_params=_params(("parallel",)),
        name="ffn",
    )(x, norm, wg, wu, wd)


def _mix_body(x_ref, n_ref, win_ref, bg_ref, lbl_ref,
              q_ref, kb_ref, vb_ref, hq_ref, hk_ref, hi_ref, hlf_ref, sg_ref, gates_ref):
    h = _rms(x_ref[...], n_ref[...]).astype(BF16)

    def proj(lo, width):
        return jnp.dot(h, win_ref[:, lo:lo + width], preferred_element_type=F32)

    w = SB_WIDTH
    q_ref[...] = (proj(0, w) * (SB_HEAD_DIM ** -0.5)).astype(BF16)
    k = proj(w, w)
    kb_ref[...] = k.astype(BF16)
    v = proj(2 * w, w)
    vb_ref[...] = v.astype(BF16)

    logits = lbl_ref[...]
    l0, l1 = logits[0:1], logits[1:2]
    m = jnp.maximum(l0, l1)
    e0, e1 = jnp.exp(l0 - m), jnp.exp(l1 - m)
    p0, p1 = e0 / (e0 + e1), e1 / (e0 + e1)
    lb = (p0 + p1) - p0
    f = lb + (1.0 - lb) * jax.nn.sigmoid(proj(3 * w, w))
    hk_ref[...] = 1.0 - f
    hlf_ref[...] = jnp.log(f)
    hi_ref[...] = proj(4 * w, w)
    hq_ref[...] = proj(5 * w, w)
    g = proj(6 * w, w)
    sg_ref[...] = g * jax.nn.sigmoid(g)
    d2 = gates_ref.shape[1]
    gates_ref[...] = jax.nn.sigmoid(proj(7 * w, d2) + bg_ref[...])
    return k, v


def _mix_small_kernel(*refs, n_run, n_meta):
    body_refs, (k_ref, v_ref, kt_ref, vt_ref) = refs[:14], refs[14:]
    k, v = _mix_body(*body_refs)
    k_ref[...] = k
    v_ref[...] = v
    kt_ref[...] = _pad_rows(k[n_run:n_run + n_meta], LANES).T
    vt_ref[...] = _pad_rows(v[n_run:n_run + n_meta], LANES).T


def _mix_frames_kernel(*refs, tm, n_meta, n_tiles):
    x_ref, consts, mk_ref, mv_ref = refs[0], refs[1:5], refs[5], refs[6]
    row_outs, (kt_ref, vt_ref, ck_ref, cv_ref) = refs[7:16], refs[16:]
    t = pl.program_id(1)
    shape = (SB_HEADS, SB_HEAD_DIM, tm)

    @pl.when(t == 0)
    def _():
        ck_ref[...] = mk_ref[...]
        cv_ref[...] = mv_ref[...]

    @pl.when(t < n_tiles)
    def _():
        k, v = _mix_body(x_ref.at[0], *consts, *[r.at[0] for r in row_outs])
        for rows, out_ref, carry_ref in ((k, kt_ref, ck_ref), (v, vt_ref, cv_ref)):
            cols = rows.T
            late = jnp.concatenate([carry_ref[:, :n_meta], cols[:, :tm - n_meta]], axis=1)
            out_ref[0, 0] = late.reshape(shape)
            carry_ref[:, :n_meta] = cols[:, tm - n_meta:]

    @pl.when(t == n_tiles)
    def _():
        for out_ref, carry_ref in ((kt_ref, ck_ref), (vt_ref, cv_ref)):
            tail = jnp.concatenate(
                [carry_ref[:, :n_meta], jnp.zeros((SB_WIDTH, tm - n_meta), F32)], axis=1)
            out_ref[0, 0] = tail.reshape(shape)


_MIX_WIDTHS = [(SB_WIDTH, BF16)] * 3 + [(SB_WIDTH, F32)] * 5


def _mix_small_call(x, norm, w_in, b_gate, lb_logits, n_run, n_meta):
    n, d = x.shape
    widths = _MIX_WIDTHS + [(2 * d, F32), (SB_WIDTH, F32), (SB_WIDTH, F32)]
    meta_t = jax.ShapeDtypeStruct((SB_WIDTH, LANES), F32)
    return pl.pallas_call(
        functools.partial(_mix_small_kernel, n_run=n_run, n_meta=n_meta),
        grid=(1,),
        in_specs=[_rows_spec(n, d), _const_spec(norm.shape), _const_spec(w_in.shape),
                  _const_spec(b_gate.shape), _const_spec(lb_logits.shape)],
        out_specs=[_rows_spec(n, wd) for wd, _ in widths]
                  + [pl.BlockSpec(meta_t.shape, lambda i: (0, 0))] * 2,
        out_shape=[jax.ShapeDtypeStruct((n, wd), dt) for wd, dt in widths] + [meta_t] * 2,
        compiler_params=_params(("arbitrary",)),
        name="mix_small",
    )(x, norm, w_in, b_gate, lb_logits)


def _mix_frames_call(x, norm, w_in, b_gate, lb_logits, meta_kt, meta_vt, tm, n_meta):
    nb, seq, d = x.shape
    assert seq % tm == 0 and n_meta < tm
    n_tiles = seq // tm
    widths = _MIX_WIDTHS + [(2 * d, F32)]
    rows_map = lambda b, t: (b, jnp.minimum(t, n_tiles - 1), 0)
    const2 = lambda shape: pl.BlockSpec(shape, lambda b, t: (0, 0), pipeline_mode=pl.Buffered(1))
    kv_shape = jax.ShapeDtypeStruct((1, nb, SB_HEADS, SB_HEAD_DIM, n_meta + seq), F32)
    kv_spec = pl.BlockSpec((1, 1, SB_HEADS, SB_HEAD_DIM, tm), lambda b, t: (0, b, 0, 0, t))
    carry = pltpu.VMEM((SB_WIDTH, LANES), F32)
    return pl.pallas_call(
        functools.partial(_mix_frames_kernel, tm=tm, n_meta=n_meta, n_tiles=n_tiles),
        grid=(nb, n_tiles + 1),
        in_specs=[pl.BlockSpec((1, tm, d), rows_map)]
                 + [const2(a.shape) for a in (norm, w_in, b_gate, lb_logits, meta_kt, meta_vt)],
        out_specs=[pl.BlockSpec((1, tm, wd), rows_map) for wd, _ in widths] + [kv_spec] * 2,
        out_shape=[jax.ShapeDtypeStruct((nb, seq, wd), dt) for wd, dt in widths] + [kv_shape] * 2,
        scratch_shapes=[carry, carry],
        compiler_params=_params(("arbitrary", "arbitrary")),
        name="mix_frames",
    )(x, norm, w_in, b_gate, lb_logits, meta_kt, meta_vt)


def _suffix_ones(width):
    r = lax.broadcasted_iota(jnp.int32, (width, width), 0)
    c = lax.broadcasted_iota(jnp.int32, (width, width), 1)
    return jnp.where(r >= c, 1.0, 0.0).astype(BF16)


def _key_before(tq, width, k0, q_first):
    row = lax.broadcasted_iota(jnp.int32, (tq, width), 0)
    col = lax.broadcasted_iota(jnp.int32, (tq, width), 1)
    return col + k0 < row + q_first


def _sb_sweep(q_ops, k_ops, v_ops, ones, mask, r_ref, acc_ref, token_minor=False):
    heads = range(SB_HEADS)
    k_dims, v_dims = (_NN, _NT) if token_minor else (_NT, _NN)
    tq, width = q_ops[0].shape[0], ones.shape[0]
    zs = [lax.dot_general(q_ops[h], k_ops[h], k_dims, preferred_element_type=F32) for h in heads]
    his, los = [], []
    for h in heads:
        z = zs[h]
        ls = jnp.minimum(-z, 0.0) - jnp.log(1.0 + jnp.exp(-jnp.abs(z)))
        if mask is not None:
            ls = jnp.where(mask, ls, 0.0)
        hi = ls.astype(BF16)
        his.append(hi)
        los.append((ls - hi.astype(F32)).astype(BF16))
    cums = [jnp.dot(his[h], ones, preferred_element_type=F32)
            + jnp.dot(los[h], ones, preferred_element_type=F32) for h in heads]
    ws = []
    for h in heads:
        r = r_ref[h]
        cr = cums[h] + jnp.concatenate([r] * (width // LANES), axis=1)
        a = jnp.exp(zs[h] + cr)
        if mask is not None:
            a = jnp.where(mask, a, 0.0)
        ws.append(a.astype(BF16))
        r_ref[h] = jnp.broadcast_to(cr[:, 0:1], (tq, LANES))
    for h in heads:
        acc_ref[h] += lax.dot_general(ws[h], v_ops[h], v_dims, preferred_element_type=F32)


def _sb_live(r_ref):
    m = r_ref[0]
    for h in range(1, SB_HEADS):
        m = jnp.maximum(m, r_ref[h])
    return (jnp.max(m) >= F32_EXP_UNDERFLOW).astype(jnp.int32)


def _sweep_while(n_blocks, alive, r_ref, block, first=0):
    def body(carry):
        t, _ = carry
        block(t)
        return t + 1, _sb_live(r_ref)

    _, alive = lax.while_loop(lambda c: jnp.logical_and(c[0] < n_blocks, c[1] > 0),
                              body, (jnp.int32(first), alive))
    return alive


def _sb_kernel(q_ref, kn_ref, vn_ref, ko_ref, vo_ref, o_ref, qm_ref, acc_ref, r_ref, *,
               tq, n_old, wn):
    kb = KEY_BLOCK
    i = pl.program_id(1)
    n_head, n_full = n_old % kb, n_old // kb
    lane = lax.broadcasted_iota(jnp.int32, (tq, LANES), 1)
    ones = {width: _suffix_ones(width) for width in {kb, wn, WIDE_BLOCK}}

    for h in range(SB_HEADS):
        p = h // 2
        qp = q_ref[0, :, p * LANES:(p + 1) * LANES]
        keep = (lane < SB_HEAD_DIM) if h % 2 == 0 else (lane >= SB_HEAD_DIM)
        qm_ref[h] = jnp.where(keep, qp, jnp.zeros_like(qp))
        r_ref[h] = jnp.zeros((tq, LANES), F32)
        acc_ref[h] = jnp.zeros((tq, LANES), F32)

    def sweep(k_ref, v_ref, start, width, mask):
        k_all = k_ref[0, pl.ds(start, width), :].astype(BF16)
        v_all = v_ref[0, pl.ds(start, width), :].astype(BF16)
        pair = lambda a: [a[:, (h // 2) * LANES:(h // 2 + 1) * LANES] for h in range(SB_HEADS)]
        _sb_sweep([qm_ref[h] for h in range(SB_HEADS)], pair(k_all), pair(v_all),
                  ones[width], mask, r_ref, acc_ref)

    q0 = i * tq
    j0 = q0 // wn
    sweep(kn_ref, vn_ref, pl.multiple_of(j0 * wn, wn), wn, _key_before(tq, wn, j0 * wn, q0))
    alive = _sweep_while(j0, _sb_live(r_ref), r_ref, lambda t: sweep(
        kn_ref, vn_ref, pl.multiple_of((j0 - 1 - t) * wn, wn), wn, None))

    n_wide = (n_full * kb) // WIDE_BLOCK
    rest = n_full * kb - n_wide * WIDE_BLOCK
    if n_wide:
        alive = _sweep_while(n_wide, alive, r_ref, lambda t: sweep(
            ko_ref, vo_ref, pl.multiple_of(n_head + rest + (n_wide - 1 - t) * WIDE_BLOCK, 8),
            WIDE_BLOCK, None))
    if rest:
        alive = _sweep_while(1, alive, r_ref, lambda t: sweep(ko_ref, vo_ref, n_head, rest, None))
    if n_head:
        _sweep_while(1, alive, r_ref, lambda t: sweep(
            ko_ref, vo_ref, 0, kb, lax.broadcasted_iota(jnp.int32, (tq, kb), 1) < n_head))

    for p in range(SB_HEADS // 2):
        o_ref[0, :, p * LANES:(p + 1) * LANES] = jnp.where(
            lane < SB_HEAD_DIM, acc_ref[2 * p], acc_ref[2 * p + 1]).astype(o_ref.dtype)


def _sbd_kernel(q_ref, kn_ref, vn_ref, ck_ref, cv_ref, o_ref, acc_ref, r_ref, *, tq, n_old):
    kb, wb, hd = KEY_BLOCK, WIDE_BLOCK, SB_HEAD_DIM
    heads = range(SB_HEADS)
    n_tail, n_full = n_old % kb, n_old // kb
    n_wide = (n_full * kb) // wb
    rest = n_full * kb - n_wide * wb
    per_head = lambda a: [a[:, h * hd:(h + 1) * hd] for h in heads]

    q_ops = per_head(q_ref[0])
    for h in heads:
        r_ref[h] = jnp.zeros((tq, LANES), F32)
        acc_ref[h] = jnp.zeros((tq, hd), F32)

    assert kn_ref.shape[1] == kb
    _sb_sweep(q_ops, per_head(kn_ref[0]), per_head(vn_ref[0]), _suffix_ones(kb),
              _key_before(tq, kb, 0, 0), r_ref, acc_ref)
    alive = _sb_live(r_ref)

    def cached(start, width, valid):
        def load(ref, h):
            cols = ref[0, 0, h, :, pl.ds(start, valid)].astype(BF16)
            if valid < width:
                cols = jnp.concatenate([cols, jnp.zeros((hd, width - valid), BF16)], axis=1)
            return cols

        mask = None if valid == width else (
            lax.broadcasted_iota(jnp.int32, (tq, width), 1) < valid)
        _sb_sweep(q_ops, [load(ck_ref, h) for h in heads], [load(cv_ref, h) for h in heads],
                  _suffix_ones(width), mask, r_ref, acc_ref, token_minor=True)

    if n_tail:
        alive = _sweep_while(1, alive, r_ref, lambda t: cached(n_full * kb, kb, n_tail))
    if n_wide:
        alive = _sweep_while(n_wide, alive, r_ref, lambda t: cached(
            pl.multiple_of(rest + (n_wide - 1 - t) * wb, kb), wb, wb))
    if rest:
        _sweep_while(1, alive, r_ref, lambda t: cached(0, rest, rest))

    o_ref[0] = jnp.concatenate([acc_ref[h] for h in heads], axis=1).astype(o_ref.dtype)


def _sb_call(q, k_new, v_new, k_old, v_old, n_old, tq):
    s, lq, w = q.shape
    ln = k_new.shape[1]
    lo = k_old.shape[1]
    assert lq % tq == 0 and ln % KEY_BLOCK == 0 and lo >= max(n_old, KEY_BLOCK)
    assert tq <= KEY_BLOCK and KEY_BLOCK % tq == 0
    wn = WIDE_BLOCK if ln % WIDE_BLOCK == 0 else KEY_BLOCK
    old_map = (lambda b, i: (b, 0, 0)) if k_old.shape[0] == s else (lambda b, i: (0, 0, 0))
    return pl.pallas_call(
        functools.partial(_sb_kernel, tq=tq, n_old=n_old, wn=wn),
        grid=(s, lq // tq),
        in_specs=[pl.BlockSpec((1, tq, w), lambda b, i: (b, i, 0)),
                  pl.BlockSpec((1, ln, w), lambda b, i: (b, 0, 0)),
                  pl.BlockSpec((1, ln, w), lambda b, i: (b, 0, 0)),
                  pl.BlockSpec((1, lo, w), old_map),
                  pl.BlockSpec((1, lo, w), old_map)],
        out_specs=pl.BlockSpec((1, tq, w), lambda b, i: (b, i, 0)),
        out_shape=jax.ShapeDtypeStruct((s, lq, w), BF16),
        scratch_shapes=[pltpu.VMEM((SB_HEADS, tq, LANES), BF16),
                        pltpu.VMEM((SB_HEADS, tq, LANES), F32),
                        pltpu.VMEM((SB_HEADS, tq, LANES), F32)],
        compiler_params=_params(("parallel", "parallel")),
        name="sb",
    )(q, k_new, v_new, k_old, v_old)


def _sbd_call(q, k_new, v_new, cache_k, cache_v):
    s, tq, w = q.shape
    n_old = cache_k.shape[-1]
    assert tq <= KEY_BLOCK and k_new.shape[1] == KEY_BLOCK
    new_spec = pl.BlockSpec((1, KEY_BLOCK, w), lambda b: (b, 0, 0))
    cache_spec = pl.BlockSpec((1, 1) + cache_k.shape[2:], lambda b: (0, b, 0, 0, 0))
    return pl.pallas_call(
        functools.partial(_sbd_kernel, tq=tq, n_old=n_old),
        grid=(s,),
        in_specs=[pl.BlockSpec((1, tq, w), lambda b: (b, 0, 0)), new_spec, new_spec,
                  cache_spec, cache_spec],
        out_specs=pl.BlockSpec((1, tq, w), lambda b: (b, 0, 0)),
        out_shape=jax.ShapeDtypeStruct((s, tq, w), BF16),
        scratch_shapes=[pltpu.VMEM((SB_HEADS, tq, SB_HEAD_DIM), F32),
                        pltpu.VMEM((SB_HEADS, tq, LANES), F32)],
        compiler_params=_params(("parallel",)),
        name="sb_decode",
    )(q, k_new, v_new, cache_k, cache_v)


def _pad_rows(x, rows):
    if x.shape[0] == rows:
        return x
    return jnp.concatenate([x, jnp.zeros((rows - x.shape[0], x.shape[1]), x.dtype)], axis=0)


def _hg_kernel(q_ref, k_ref, i_ref, lf_ref, s0_ref, o_ref, sout_ref, st_ref, *, chunk):
    c = pl.program_id(0)
    hd = HG_HEAD_DIM
    ns = q_ref.shape[0]
    streams = range(ns)
    heads = [(b, h, slice(h * hd, (h + 1) * hd)) for b in streams for h in range(HG_HEADS)]

    @pl.when(c == 0)
    def _():
        for b in streams:
            st_ref[b] = s0_ref[b if s0_ref.shape[0] == ns else 0]

    trow = lax.broadcasted_iota(jnp.int32, (chunk, chunk), 0)
    tcol = lax.broadcasted_iota(jnp.int32, (chunk, chunk), 1)
    causal = trow >= tcol
    tri = jnp.where(causal, 1.0, 0.0).astype(BF16)

    terms = []
    for b in streams:
        lf = lf_ref[b]
        hi = lf.astype(BF16)
        r1 = lf - hi.astype(F32)
        mid = r1.astype(BF16)
        terms.append((hi, mid, (r1 - mid.astype(F32)).astype(BF16)))
    cum = [sum(jnp.dot(tri, t, preferred_element_type=F32) for t in terms[b]) for b in streams]
    last = [cb[chunk - 1:chunk, :] for cb in cum]
    weakest = last[0]
    for lb in last[1:]:
        weakest = jnp.minimum(weakest, lb)
    safe = jnp.min(weakest) >= HG_SAFE_LOG_DECAY

    @pl.when(safe)
    def _():
        qd, kd, kl, iv, dl = [], [], [], [], []
        for b in streams:
            k = k_ref[b]
            qd.append((q_ref[b] * jnp.exp(cum[b])).astype(BF16))
            kd.append((k * jnp.exp(-cum[b])).astype(BF16))
            kl.append((k * jnp.exp(last[b] - cum[b])).astype(BF16))
            iv.append(i_ref[b].astype(BF16))
            dl.append(jnp.exp(last[b]))
        sc = [lax.dot_general(qd[b][:, sl], kd[b][:, sl], _NT, preferred_element_type=F32)
              for b, h, sl in heads]
        carried = [lax.dot_general(qd[b][:, sl], st_ref[b, h].astype(BF16), _NT,
                                   preferred_element_type=F32) for b, h, sl in heads]
        grown = [lax.dot_general(iv[b][:, sl], kl[b][:, sl], _TN, preferred_element_type=F32)
                 for b, h, sl in heads]
        for n, (b, h, sl) in enumerate(heads):
            within = jnp.dot(jnp.where(causal, sc[n], 0.0).astype(BF16), iv[b][:, sl],
                             preferred_element_type=F32)
            o_ref[b, :, sl] = within + carried[n]
            st_ref[b, h] = st_ref[b, h] * dl[b][:, sl] + grown[n]

    @pl.when(jnp.logical_not(safe))
    def _():
        lane_t = lax.broadcasted_iota(jnp.int32, (hd, hd), 1)
        row_t = lax.broadcasted_iota(jnp.int32, (hd, hd), 0)

        def stream(b, carry):
            for h in range(HG_HEADS):
                sl = slice(h * hd, (h + 1) * hd)
                f_t = _pad_rows(jnp.exp(lf_ref[b, :, sl]), hd).T
                k_t = _pad_rows(k_ref[b, :, sl], hd).T
                q_t = _pad_rows(q_ref[b, :, sl], hd).T
                i_p = _pad_rows(i_ref[b, :, sl], hd)

                def token(t, sc, f_t=f_t, k_t=k_t, q_t=q_t, i_p=i_p):
                    s, o = sc
                    sel = lane_t == t
                    fc = jnp.sum(jnp.where(sel, f_t, 0.0), axis=1, keepdims=True)
                    kc = jnp.sum(jnp.where(sel, k_t, 0.0), axis=1, keepdims=True)
                    qc = jnp.sum(jnp.where(sel, q_t, 0.0), axis=1, keepdims=True)
                    i_row = jnp.sum(jnp.where(row_t == t, i_p, 0.0), axis=0, keepdims=True)
                    s = s * fc + kc * i_row
                    o = jnp.where(row_t == t, jnp.sum(qc * s, axis=0, keepdims=True), o)
                    return s, o

                s, o = lax.fori_loop(0, chunk, token,
                                     (st_ref[b, h].T, jnp.zeros((hd, hd), F32)))
                st_ref[b, h] = s.T
                o_ref[b, :, sl] = o[:chunk]
            return carry

        lax.fori_loop(0, ns, stream, 0)

    @pl.when(c == pl.num_programs(0) - 1)
    def _():
        sout_ref[...] = st_ref[...]


def _hg_call(q, k, iv, lf, s0t, chunk):
    s, l, w = q.shape
    assert l % chunk == 0
    seq_spec = pl.BlockSpec((s, chunk, w), lambda c: (0, c, 0))
    st_shape = (s, HG_HEADS, HG_HEAD_DIM, HG_HEAD_DIM)
    return pl.pallas_call(
        functools.partial(_hg_kernel, chunk=chunk),
        grid=(l // chunk,),
        in_specs=[seq_spec, seq_spec, seq_spec, seq_spec,
                  pl.BlockSpec(s0t.shape, lambda c: (0, 0, 0, 0))],
        out_specs=[seq_spec, pl.BlockSpec(st_shape, lambda c: (0, 0, 0, 0))],
        out_shape=[jax.ShapeDtypeStruct((s, l, w), F32), jax.ShapeDtypeStruct(st_shape, F32)],
        scratch_shapes=[pltpu.VMEM(st_shape, F32)],
        compiler_params=_params(("arbitrary",)),
        name="hgrn",
    )(q, k, iv, lf, s0t)


def _post_kernel(x_ref, oa_ref, ob_ref, sg_ref, gates_ref, hn_ref, wa_ref, wb_ref, wo_ref,
                 n_ref, wg_ref, wu_ref, wd_ref, fn_ref, y_ref, acc_ref):
    hd = HG_HEAD_DIM
    d = x_ref.shape[1]
    parts = []
    for h in range(HG_HEADS):
        sl = slice(h * hd, (h + 1) * hd)
        parts.append(_rms(ob_ref[:, sl], hn_ref[:, sl]) * sg_ref[:, sl])
    ob = jnp.concatenate(parts, axis=-1).astype(BF16)
    ma = jnp.dot(oa_ref[...], wa_ref[...], preferred_element_type=F32)
    mb = jnp.dot(ob, wb_ref[...], preferred_element_type=F32)
    merged = gates_ref[:, :d] * ma + gates_ref[:, d:] * mb
    x = x_ref[...] + jnp.dot(merged.astype(BF16), wo_ref[...], preferred_element_type=F32)
    x = _half_ffn(x, n_ref[...], wg_ref, wu_ref, wd_ref, acc_ref)
    y_ref[...] = _rms(x, fn_ref[...])


def _post_call(x, oa, ob, sg, gates, hn, wa, wb, wo, norm, wg, wu, wd, fnorm, tm):
    n, d = x.shape
    consts = [hn, wa, wb, wo, norm, wg, wu, wd, fnorm]
    return pl.pallas_call(
        _post_kernel,
        grid=(n // tm,),
        in_specs=[_rows_spec(tm, d), _rows_spec(tm, oa.shape[1]), _rows_spec(tm, ob.shape[1]),
                  _rows_spec(tm, sg.shape[1]), _rows_spec(tm, gates.shape[1])]
                 + [_const_spec(a.shape) for a in consts],
        out_specs=_rows_spec(tm, d),
        out_shape=jax.ShapeDtypeStruct((n, d), F32),
        scratch_shapes=[pltpu.VMEM((tm, d), F32)],
        compiler_params=_params(("parallel",)),
        name="post",
    )(x, oa, ob, sg, gates, *consts)


def _row_tile(n, target):
    tm = min(n, target)
    assert n % tm == 0 and tm % 8 == 0
    return tm


def kernel(x_prompt, x_sample, cache_sb_k, cache_sb_v, state_hgrn, meta_tokens, ffn1_norm, ffn1_w_gate, ffn1_w_up, ffn1_w_down, mix_norm, w_in, b_gate, hg_lb_logits, hg_out_norm, w_branch_a, w_branch_b, w_out, ffn2_norm, ffn2_w_gate, ffn2_w_up, ffn2_w_down, final_norm):
    depth = w_in.shape[0]
    assert depth == 1, "single-layer step"
    nb, seq, d = x_prompt.shape
    ns, t_new, _ = x_sample.shape
    n_meta = meta_tokens.shape[0]
    w = SB_WIDTH
    row = lambda a: a.reshape(1, -1)

    f1 = (ffn1_w_gate[0].astype(BF16), ffn1_w_up[0].astype(BF16), ffn1_w_down[0].astype(BF16))
    f2 = (ffn2_w_gate[0].astype(BF16), ffn2_w_up[0].astype(BF16), ffn2_w_down[0].astype(BF16))
    w_in_b = w_in[0].astype(BF16)
    wa, wb, wo = w_branch_a[0].astype(BF16), w_branch_b[0].astype(BF16), w_out[0].astype(BF16)

    def post(x1, oa, ob, sg, gates, tm):
        return _post_call(x1, oa, ob, sg, gates, row(hg_out_norm[0]), wa, wb, wo,
                          row(ffn2_norm[0]), *f2, row(final_norm), tm)

    mix_consts = (row(mix_norm[0]), w_in_b, row(b_gate[0]), hg_lb_logits)

    n_frames = nb * seq
    tm_p = _row_tile(seq, 512)
    n_run = ns * t_new
    small = jnp.concatenate([x_sample.reshape(n_run, d), meta_tokens.astype(F32)], axis=0)
    x1_s = _ffn_call(small, row(ffn1_norm[0]), *f1, small.shape[0])
    (q_s, kb_s, vb_s, hq_s, hk_s, hi_s, hlf_s, sg_s, gates_s, k_s, v_s, meta_kt, meta_vt) = (
        _mix_small_call(x1_s, *mix_consts, n_run, n_meta))
    x1_p = _ffn_call(x_prompt.reshape(n_frames, d), row(ffn1_norm[0]), *f1, tm_p)
    (q_p, kb_p, vb_p, hq_p, hk_p, hi_p, hlf_p, sg_p, gates_p, kt_full, vt_full) = (
        _mix_frames_call(x1_p.reshape(nb, seq, d), *mix_consts, meta_kt, meta_vt, tm_p, n_meta))
    run = lambda a: a[:n_run]
    meta = lambda a: a[n_run:]
    token_minor = lambda a: jnp.transpose(a, (0, 1, 3, 4, 2))
    token_major = lambda a: jnp.transpose(a, (0, 1, 4, 2, 3))

    seq3 = lambda a, s: a.reshape(s, -1, a.shape[-1])
    pad_to = lambda a, rows: jnp.pad(a, ((0, 0), (0, rows - a.shape[1]), (0, 0)))
    oa_p = _sb_call(q_p, kb_p, vb_p,
                    pad_to(meta(kb_s)[None], KEY_BLOCK), pad_to(meta(vb_s)[None], KEY_BLOCK),
                    n_meta, KEY_BLOCK)
    oa_s = _sbd_call(seq3(run(q_s), ns), pad_to(seq3(run(kb_s), ns), KEY_BLOCK),
                     pad_to(seq3(run(vb_s), ns), KEY_BLOCK),
                     token_minor(cache_sb_k), token_minor(cache_sb_v))

    hg = lambda sel, s, arrs: [seq3(sel(a), s) for a in arrs]
    zero_state = jnp.zeros((1, HG_HEADS, HG_HEAD_DIM, HG_HEAD_DIM), F32)
    _, st_meta = _hg_call(*hg(meta, 1, (hq_s, hk_s, hi_s, hlf_s)), zero_state, n_meta)
    ob_p, st_p = _hg_call(*hg(lambda a: a, nb, (hq_p, hk_p, hi_p, hlf_p)), st_meta, HG_CHUNK)
    ob_s, st_s = _hg_call(*hg(run, ns, (hq_s, hk_s, hi_s, hlf_s)),
                          jnp.swapaxes(state_hgrn[0].astype(F32), -1, -2), t_new)

    rows = lambda a: a.reshape(-1, a.shape[-1])
    y_p = post(x1_p, rows(oa_p), rows(ob_p), rows(sg_p), rows(gates_p), tm_p)
    y_s = post(run(x1_s), rows(oa_s), rows(ob_s), run(sg_s), run(gates_s), n_run)

    heads = lambda a: a.reshape(1, ns, t_new, SB_HEADS, SB_HEAD_DIM)
    return (y_p.reshape(nb, seq, d), y_s.reshape(ns, t_new, d),
            token_major(kt_full), token_major(vt_full),
            jnp.swapaxes(st_p, -1, -2)[None],
            heads(run(k_s)), heads(run(v_s)),
            jnp.swapaxes(st_s, -1, -2)[None])
```

```python
import functools

import jax
import jax.numpy as jnp
from jax import lax
from jax.experimental import pallas as pl
from jax.experimental.pallas import tpu as pltpu

F32 = jnp.float32
BF16 = jnp.bfloat16
EPS = 1e-6
N_META = 16
SB_HEADS = 8
SB_HEAD_DIM = 64
SB_WIDTH = SB_HEADS * SB_HEAD_DIM
HG_HEADS = 4
HG_HEAD_DIM = 128
HG_WIDTH = HG_HEADS * HG_HEAD_DIM
LANES = 128
FF_CHUNK = 256
KEY_BLOCK = 128
WIDE_BLOCK = 256
F32_EXP2_UNDERFLOW = -152.0
LOG2_E = 1.4426950408889634
HG_CHUNK = 64
HG_SAFE_LOG_DECAY = -60.0
VMEM_LIMIT_BYTES = 56 * 1024 * 1024

_NN = (((1,), (0,)), ((), ()))
_NT = (((1,), (1,)), ((), ()))
_TN = (((0,), (0,)), ((), ()))


def _const_spec(shape):
    nd = len(shape)
    return pl.BlockSpec(shape, lambda *_: (0,) * nd, pipeline_mode=pl.Buffered(1))


def _rows_spec(tm, width):
    return pl.BlockSpec((tm, width), lambda i: (i, 0))


def _params(sem):
    return pltpu.CompilerParams(dimension_semantics=sem, vmem_limit_bytes=VMEM_LIMIT_BYTES)


def _rms(x, g):
    return x * lax.rsqrt(jnp.mean(x * x, axis=-1, keepdims=True) + EPS) * g


def _half_ffn(x, g, wg_ref, wu_ref, wd_ref, acc_ref):
    h = _rms(x, g).astype(BF16)
    dff = wg_ref.shape[1]
    assert dff % FF_CHUNK == 0
    for c in range(dff // FF_CHUNK):
        cols = slice(c * FF_CHUNK, (c + 1) * FF_CHUNK)
        gate = jnp.dot(h, wg_ref[:, cols], preferred_element_type=F32)
        up = jnp.dot(h, wu_ref[:, cols], preferred_element_type=F32)
        act = (gate * jax.nn.sigmoid(gate) * up).astype(BF16)
        contrib = jnp.dot(act, wd_ref[cols, :], preferred_element_type=F32)
        if c == 0:
            acc_ref[...] = contrib
        else:
            acc_ref[...] += contrib
    return x + 0.5 * acc_ref[...]


def _ffn_kernel(x_ref, n_ref, wg_ref, wu_ref, wd_ref, o_ref, acc_ref):
    o_ref[...] = _half_ffn(x_ref[...], n_ref[...], wg_ref, wu_ref, wd_ref, acc_ref)


def _ffn_call(x, norm, wg, wu, wd, tm):
    n, d = x.shape
    return pl.pallas_call(
        _ffn_kernel,
        grid=(n // tm,),
        in_specs=[_rows_spec(tm, d), _const_spec(norm.shape), _const_spec(wg.shape),
                  _const_spec(wu.shape), _const_spec(wd.shape)],
        out_specs=_rows_spec(tm, d),
        out_shape=jax.ShapeDtypeStruct((n, d), F32),
        scratch_shapes=[pltpu.VMEM((tm, d), F32)],
        compiler_params=_params(("parallel",)),
        name="ffn",
    )(x, norm, wg, wu, wd)


def _mix_body(x_ref, n_ref, win_ref, bg_ref, lbl_ref,
              q_ref, kb_ref, vb_ref, hq_ref, hk_ref, hi_ref, hlf_ref, sg_ref, gates_ref):
    h = _rms(x_ref[...], n_ref[...]).astype(BF16)

    def proj(lo, width):
        return jnp.dot(h, win_ref[:, lo:lo + width], preferred_element_type=F32)

    w = SB_WIDTH
    q_ref[...] = (proj(0, w) * (SB_HEAD_DIM ** -0.5 * LOG2_E)).astype(BF16)
    k = proj(w, w)
    kb_ref[...] = k.astype(BF16)
    v = proj(2 * w, w)
    vb_ref[...] = v.astype(BF16)

    logits = lbl_ref[...]
    l0, l1 = logits[0:1], logits[1:2]
    m = jnp.maximum(l0, l1)
    e0, e1 = jnp.exp(l0 - m), jnp.exp(l1 - m)
    p0, p1 = e0 / (e0 + e1), e1 / (e0 + e1)
    lb = (p0 + p1) - p0
    f = lb + (1.0 - lb) * jax.nn.sigmoid(proj(3 * w, w))
    hk_ref[...] = (1.0 - f).astype(hk_ref.dtype)
    hlf_ref[...] = jnp.log(f)
    hi_ref[...] = proj(4 * w, w).astype(hi_ref.dtype)
    hq_ref[...] = proj(5 * w, w).astype(hq_ref.dtype)
    g = proj(6 * w, w)
    sg_ref[...] = (g * jax.nn.sigmoid(g)).astype(sg_ref.dtype)
    d2 = gates_ref.shape[1]
    gates_ref[...] = jax.nn.sigmoid(proj(7 * w, d2) + bg_ref[...]).astype(gates_ref.dtype)
    return k, v


def _mix_small_kernel(*refs, n_run, n_meta):
    body_refs, (k_ref, v_ref, kt_ref, vt_ref) = refs[:14], refs[14:]
    k, v = _mix_body(*body_refs)
    k_ref[...] = k
    v_ref[...] = v
    kt_ref[...] = _pad_rows(k[n_run:n_run + n_meta], LANES).T
    vt_ref[...] = _pad_rows(v[n_run:n_run + n_meta], LANES).T


def _mix_frames_kernel(*refs, tm, n_meta, n_tiles):
    x_ref, consts, mk_ref, mv_ref = refs[0], refs[1:5], refs[5], refs[6]
    row_outs, (kt_ref, vt_ref, ck_ref, cv_ref) = refs[7:16], refs[16:]
    t = pl.program_id(1)
    shape = (SB_HEADS, SB_HEAD_DIM, tm)

    @pl.when(t == 0)
    def _():
        ck_ref[...] = mk_ref[...]
        cv_ref[...] = mv_ref[...]

    @pl.when(t < n_tiles)
    def _():
        k, v = _mix_body(x_ref.at[0], *consts, *[r.at[0] for r in row_outs])
        for rows, out_ref, carry_ref in ((k, kt_ref, ck_ref), (v, vt_ref, cv_ref)):
            cols = rows.T
            late = jnp.concatenate([carry_ref[:, :n_meta], cols[:, :tm - n_meta]], axis=1)
            out_ref[0, 0] = late.reshape(shape)
            carry_ref[:, :n_meta] = cols[:, tm - n_meta:]

    @pl.when(t == n_tiles)
    def _():
        for out_ref, carry_ref in ((kt_ref, ck_ref), (vt_ref, cv_ref)):
            tail = jnp.concatenate(
                [carry_ref[:, :n_meta], jnp.zeros((SB_WIDTH, tm - n_meta), F32)], axis=1)
            out_ref[0, 0] = tail.reshape(shape)


_MIX_WIDTHS = [(SB_WIDTH, BF16)] * 6 + [(SB_WIDTH, F32), (SB_WIDTH, BF16)]


def _mix_small_call(x, norm, w_in, b_gate, lb_logits, n_run, n_meta):
    n, d = x.shape
    widths = _MIX_WIDTHS + [(2 * d, BF16), (SB_WIDTH, F32), (SB_WIDTH, F32)]
    meta_t = jax.ShapeDtypeStruct((SB_WIDTH, LANES), F32)
    return pl.pallas_call(
        functools.partial(_mix_small_kernel, n_run=n_run, n_meta=n_meta),
        grid=(1,),
        in_specs=[_rows_spec(n, d), _const_spec(norm.shape), _const_spec(w_in.shape),
                  _const_spec(b_gate.shape), _const_spec(lb_logits.shape)],
        out_specs=[_rows_spec(n, wd) for wd, _ in widths]
                  + [pl.BlockSpec(meta_t.shape, lambda i: (0, 0))] * 2,
        out_shape=[jax.ShapeDtypeStruct((n, wd), dt) for wd, dt in widths] + [meta_t] * 2,
        compiler_params=_params(("arbitrary",)),
        name="mix_small",
    )(x, norm, w_in, b_gate, lb_logits)


def _mix_frames_call(x, norm, w_in, b_gate, lb_logits, meta_kt, meta_vt, tm, n_meta):
    nb, seq, d = x.shape
    assert seq % tm == 0 and n_meta < tm
    n_tiles = seq // tm
    widths = _MIX_WIDTHS + [(2 * d, BF16)]
    rows_map = lambda b, t: (b, jnp.minimum(t, n_tiles - 1), 0)
    const2 = lambda shape: pl.BlockSpec(shape, lambda b, t: (0, 0), pipeline_mode=pl.Buffered(1))
    kv_shape = jax.ShapeDtypeStruct((1, nb, SB_HEADS, SB_HEAD_DIM, n_meta + seq), F32)
    kv_spec = pl.BlockSpec((1, 1, SB_HEADS, SB_HEAD_DIM, tm), lambda b, t: (0, b, 0, 0, t))
    carry = pltpu.VMEM((SB_WIDTH, LANES), F32)
    return pl.pallas_call(
        functools.partial(_mix_frames_kernel, tm=tm, n_meta=n_meta, n_tiles=n_tiles),
        grid=(nb, n_tiles + 1),
        in_specs=[pl.BlockSpec((1, tm, d), rows_map)]
                 + [const2(a.shape) for a in (norm, w_in, b_gate, lb_logits, meta_kt, meta_vt)],
        out_specs=[pl.BlockSpec((1, tm, wd), rows_map) for wd, _ in widths] + [kv_spec] * 2,
        out_shape=[jax.ShapeDtypeStruct((nb, seq, wd), dt) for wd, dt in widths] + [kv_shape] * 2,
        scratch_shapes=[carry, carry],
        compiler_params=_params(("arbitrary", "arbitrary")),
        name="mix_frames",
    )(x, norm, w_in, b_gate, lb_logits, meta_kt, meta_vt)


def _suffix_ones(width):
    r = lax.broadcasted_iota(jnp.int32, (width, width), 0)
    c = lax.broadcasted_iota(jnp.int32, (width, width), 1)
    return jnp.where(r >= c, 1.0, 0.0).astype(BF16)


def _key_before(tq, width, k0, q_first):
    row = lax.broadcasted_iota(jnp.int32, (tq, width), 0)
    col = lax.broadcasted_iota(jnp.int32, (tq, width), 1)
    return col + k0 < row + q_first


def _sb_sweep(q_ops, blocks, r_ref, acc_ref, token_minor=False):
    heads = range(SB_HEADS)
    k_dims, v_dims = (_NN, _NT) if token_minor else (_NT, _NN)
    tq = q_ops[0].shape[0]
    zs = [[lax.dot_general(q_ops[h], k_ops[h], k_dims, preferred_element_type=F32)
           for h in heads] for k_ops, _, _, _ in blocks]
    stays = []
    for z_blk, (_, _, _, mask) in zip(zs, blocks):
        stays.append([])
        for z in z_blk:
            nz = -z
            ls = jnp.minimum(nz, 0.0) - jnp.log2(1.0 + jnp.exp2(jnp.minimum(z, nz)))
            if mask is not None:
                ls = jnp.where(mask, ls, 0.0)
            stays[-1].append(ls.astype(BF16))
    cums = [[jnp.dot(st, ones, preferred_element_type=F32) for st in st_blk]
            for st_blk, (_, _, ones, _) in zip(stays, blocks)]
    ws = []
    for z_blk, cum_blk, (_, _, ones, mask) in zip(zs, cums, blocks):
        ws.append([])
        for h in heads:
            r = r_ref[h]
            cr = cum_blk[h] + jnp.concatenate([r] * (ones.shape[1] // LANES), axis=1)
            a = jnp.exp2(z_blk[h] + cr)
            if mask is not None:
                a = jnp.where(mask, a, 0.0)
            ws[-1].append(a.astype(BF16))
            r_ref[h] = jnp.broadcast_to(cr[:, 0:1], (tq, LANES))
    for w_blk, (_, v_ops, _, _) in zip(ws, blocks):
        for h in heads:
            acc_ref[h] += lax.dot_general(w_blk[h], v_ops[h], v_dims, preferred_element_type=F32)


def _sb_live(r_ref):
    m = r_ref[0]
    for h in range(1, SB_HEADS):
        m = jnp.maximum(m, r_ref[h])
    return (jnp.max(m) >= F32_EXP2_UNDERFLOW).astype(jnp.int32)


def _sweep_while(n_blocks, alive, r_ref, block, first=0):
    def body(carry):
        t, _ = carry
        block(t)
        return t + 1, _sb_live(r_ref)

    _, alive = lax.while_loop(lambda c: jnp.logical_and(c[0] < n_blocks, c[1] > 0),
                              body, (jnp.int32(first), alive))
    return alive


def _sb_kernel(q_ref, kn_ref, vn_ref, ko_ref, vo_ref, o_ref, qm_ref, acc_ref, r_ref, *,
               tq, n_old):
    kb, wb = KEY_BLOCK, WIDE_BLOCK
    i = pl.program_id(1)
    lane = lax.broadcasted_iota(jnp.int32, (tq, LANES), 1)
    ones = {width: _suffix_ones(width) for width in (kb, wb)}

    for h in range(SB_HEADS):
        p = h // 2
        qp = q_ref[0, :, p * LANES:(p + 1) * LANES]
        keep = (lane < SB_HEAD_DIM) if h % 2 == 0 else (lane >= SB_HEAD_DIM)
        qm_ref[h] = jnp.where(keep, qp, jnp.zeros_like(qp))
        r_ref[h] = jnp.zeros((tq, LANES), F32)
        acc_ref[h] = jnp.zeros((tq, LANES), F32)

    def block(k_ref, v_ref, start, width, mask):
        k_all = k_ref[0, pl.ds(start, width), :].astype(BF16)
        v_all = v_ref[0, pl.ds(start, width), :].astype(BF16)
        pair = lambda a: [a[:, (h // 2) * LANES:(h // 2 + 1) * LANES] for h in range(SB_HEADS)]
        return pair(k_all), pair(v_all), ones[width], mask

    def sweep(*blocks):
        _sb_sweep([qm_ref[h] for h in range(SB_HEADS)], blocks, r_ref, acc_ref)

    q0 = i * tq
    below = q0 // kb
    d0 = pl.multiple_of(below * kb, kb)
    own = lambda: block(kn_ref, vn_ref, d0, kb, _key_before(tq, kb, d0, q0))
    paired = below >= 2

    @pl.when(paired)
    def _():
        sweep(own(), block(kn_ref, vn_ref, pl.multiple_of(d0 - wb, kb), wb, None))

    @pl.when(jnp.logical_not(paired))
    def _():
        sweep(own())

    top = jnp.where(paired, d0 - wb, d0)
    left = jnp.where(paired, below - 2, below)
    alive = _sweep_while(left // 2, _sb_live(r_ref), r_ref, lambda t: sweep(block(
        kn_ref, vn_ref, pl.multiple_of(top - (t + 1) * wb, kb), wb, None)))
    alive = _sweep_while(left % 2, alive, r_ref,
                         lambda t: sweep(block(kn_ref, vn_ref, 0, kb, None)))
    _sweep_while(1, alive, r_ref, lambda t: sweep(block(
        ko_ref, vo_ref, 0, kb, lax.broadcasted_iota(jnp.int32, (tq, kb), 1) < n_old)))

    for p in range(SB_HEADS // 2):
        o_ref[0, :, p * LANES:(p + 1) * LANES] = jnp.where(
            lane < SB_HEAD_DIM, acc_ref[2 * p], acc_ref[2 * p + 1]).astype(o_ref.dtype)


def _sbd_kernel(q_ref, kn_ref, vn_ref, ck_ref, cv_ref, o_ref, acc_ref, r_ref, *, tq, n_old):
    kb, wb, hd = KEY_BLOCK, WIDE_BLOCK, SB_HEAD_DIM
    heads = range(SB_HEADS)
    n_tail, n_full = n_old % kb, n_old // kb
    n_wide = (n_full * kb) // wb
    rest = n_full * kb - n_wide * wb
    per_head = lambda a: [a[:, h * hd:(h + 1) * hd] for h in heads]

    q_ops = per_head(q_ref[0])
    for h in heads:
        r_ref[h] = jnp.zeros((tq, LANES), F32)
        acc_ref[h] = jnp.zeros((tq, hd), F32)

    assert kn_ref.shape[1] == kb
    _sb_sweep(q_ops, [(per_head(kn_ref[0]), per_head(vn_ref[0]), _suffix_ones(kb),
                       _key_before(tq, kb, 0, 0))], r_ref, acc_ref)
    alive = _sb_live(r_ref)

    def cached(start, width, valid):
        def load(ref, h):
            cols = ref[0, 0, h, :, pl.ds(start, valid)].astype(BF16)
            if valid < width:
                cols = jnp.concatenate([cols, jnp.zeros((hd, width - valid), BF16)], axis=1)
            return cols

        mask = None if valid == width else (
            lax.broadcasted_iota(jnp.int32, (tq, width), 1) < valid)
        _sb_sweep(q_ops, [([load(ck_ref, h) for h in heads], [load(cv_ref, h) for h in heads],
                           _suffix_ones(width), mask)], r_ref, acc_ref, token_minor=True)

    if n_tail:
        alive = _sweep_while(1, alive, r_ref, lambda t: cached(n_full * kb, kb, n_tail))
    if n_wide:
        alive = _sweep_while(n_wide, alive, r_ref, lambda t: cached(
            pl.multiple_of(rest + (n_wide - 1 - t) * wb, kb), wb, wb))
    if rest:
        _sweep_while(1, alive, r_ref, lambda t: cached(0, rest, rest))

    o_ref[0] = jnp.concatenate([acc_ref[h] for h in heads], axis=1).astype(o_ref.dtype)


def _sb_call(q, k_new, v_new, k_old, v_old, n_old, tq):
    s, lq, w = q.shape
    ln = k_new.shape[1]
    lo = k_old.shape[1]
    assert lq % tq == 0 and ln % KEY_BLOCK == 0 and n_old <= KEY_BLOCK <= lo
    assert tq <= KEY_BLOCK and KEY_BLOCK % tq == 0
    old_map = (lambda b, i: (b, 0, 0)) if k_old.shape[0] == s else (lambda b, i: (0, 0, 0))
    return pl.pallas_call(
        functools.partial(_sb_kernel, tq=tq, n_old=n_old),
        grid=(s, lq // tq),
        in_specs=[pl.BlockSpec((1, tq, w), lambda b, i: (b, i, 0)),
                  pl.BlockSpec((1, ln, w), lambda b, i: (b, 0, 0)),
                  pl.BlockSpec((1, ln, w), lambda b, i: (b, 0, 0)),
                  pl.BlockSpec((1, lo, w), old_map),
                  pl.BlockSpec((1, lo, w), old_map)],
        out_specs=pl.BlockSpec((1, tq, w), lambda b, i: (b, i, 0)),
        out_shape=jax.ShapeDtypeStruct((s, lq, w), BF16),
        scratch_shapes=[pltpu.VMEM((SB_HEADS, tq, LANES), BF16),
                        pltpu.VMEM((SB_HEADS, tq, LANES), F32),
                        pltpu.VMEM((SB_HEADS, tq, LANES), F32)],
        compiler_params=_params(("parallel", "parallel")),
        name="sb",
    )(q, k_new, v_new, k_old, v_old)


def _sbd_call(q, k_new, v_new, cache_k, cache_v):
    s, tq, w = q.shape
    n_old = cache_k.shape[-1]
    assert tq <= KEY_BLOCK and k_new.shape[1] == KEY_BLOCK
    new_spec = pl.BlockSpec((1, KEY_BLOCK, w), lambda b: (b, 0, 0))
    cache_spec = pl.BlockSpec((1, 1) + cache_k.shape[2:], lambda b: (0, b, 0, 0, 0))
    return pl.pallas_call(
        functools.partial(_sbd_kernel, tq=tq, n_old=n_old),
        grid=(s,),
        in_specs=[pl.BlockSpec((1, tq, w), lambda b: (b, 0, 0)), new_spec, new_spec,
                  cache_spec, cache_spec],
        out_specs=pl.BlockSpec((1, tq, w), lambda b: (b, 0, 0)),
        out_shape=jax.ShapeDtypeStruct((s, tq, w), BF16),
        scratch_shapes=[pltpu.VMEM((SB_HEADS, tq, SB_HEAD_DIM), F32),
                        pltpu.VMEM((SB_HEADS, tq, LANES), F32)],
        compiler_params=_params(("parallel",)),
        name="sb_decode",
    )(q, k_new, v_new, cache_k, cache_v)


def _pad_rows(x, rows):
    if x.shape[0] == rows:
        return x
    return jnp.concatenate([x, jnp.zeros((rows - x.shape[0], x.shape[1]), x.dtype)], axis=0)


def _hg_kernel(q_ref, k_ref, i_ref, lf_ref, s0_ref, o_ref, sout_ref, st_ref, *, chunk):
    c = pl.program_id(0)
    hd = HG_HEAD_DIM
    ns = q_ref.shape[0]
    streams = range(ns)
    heads = [(b, h, slice(h * hd, (h + 1) * hd)) for b in streams for h in range(HG_HEADS)]

    @pl.when(c == 0)
    def _():
        for b in streams:
            st_ref[b] = s0_ref[b if s0_ref.shape[0] == ns else 0]

    trow = lax.broadcasted_iota(jnp.int32, (chunk, chunk), 0)
    tcol = lax.broadcasted_iota(jnp.int32, (chunk, chunk), 1)
    causal = trow >= tcol
    tri = jnp.where(causal, 1.0, 0.0).astype(BF16)

    terms = []
    for b in streams:
        lf = lf_ref[b]
        hi = lf.astype(BF16)
        r1 = lf - hi.astype(F32)
        mid = r1.astype(BF16)
        terms.append((hi, mid, (r1 - mid.astype(F32)).astype(BF16)))
    cum = [sum(jnp.dot(tri, t, preferred_element_type=F32) for t in terms[b]) for b in streams]
    last = [cb[chunk - 1:chunk, :] for cb in cum]
    weakest = last[0]
    for lb in last[1:]:
        weakest = jnp.minimum(weakest, lb)
    safe = jnp.min(weakest) >= HG_SAFE_LOG_DECAY

    @pl.when(safe)
    def _():
        qd, kd, kl, iv, dl = [], [], [], [], []
        for b in streams:
            k = k_ref[b]
            qd.append((q_ref[b] * jnp.exp(cum[b])).astype(BF16))
            kd.append((k * jnp.exp(-cum[b])).astype(BF16))
            kl.append((k * jnp.exp(last[b] - cum[b])).astype(BF16))
            iv.append(i_ref[b].astype(BF16))
            dl.append(jnp.exp(last[b]))
        sc = [lax.dot_general(qd[b][:, sl], kd[b][:, sl], _NT, preferred_element_type=F32)
              for b, h, sl in heads]
        carried = [lax.dot_general(qd[b][:, sl], st_ref[b, h].astype(BF16), _NT,
                                   preferred_element_type=F32) for b, h, sl in heads]
        grown = [lax.dot_general(iv[b][:, sl], kl[b][:, sl], _TN, preferred_element_type=F32)
                 for b, h, sl in heads]
        for n, (b, h, sl) in enumerate(heads):
            within = jnp.dot(jnp.where(causal, sc[n], 0.0).astype(BF16), iv[b][:, sl],
                             preferred_element_type=F32)
            o_ref[b, :, sl] = (within + carried[n]).astype(o_ref.dtype)
            st_ref[b, h] = st_ref[b, h] * dl[b][:, sl] + grown[n]

    @pl.when(jnp.logical_not(safe))
    def _():
        lane_t = lax.broadcasted_iota(jnp.int32, (hd, hd), 1)
        row_t = lax.broadcasted_iota(jnp.int32, (hd, hd), 0)

        def stream(b, carry):
            for h in range(HG_HEADS):
                sl = slice(h * hd, (h + 1) * hd)
                f_t = _pad_rows(jnp.exp(lf_ref[b, :, sl]), hd).T
                k_t = _pad_rows(k_ref[b, :, sl].astype(F32), hd).T
                q_t = _pad_rows(q_ref[b, :, sl].astype(F32), hd).T
                i_p = _pad_rows(i_ref[b, :, sl].astype(F32), hd)

                def token(t, sc, f_t=f_t, k_t=k_t, q_t=q_t, i_p=i_p):
                    s, o = sc
                    sel = lane_t == t
                    fc = jnp.sum(jnp.where(sel, f_t, 0.0), axis=1, keepdims=True)
                    kc = jnp.sum(jnp.where(sel, k_t, 0.0), axis=1, keepdims=True)
                    qc = jnp.sum(jnp.where(sel, q_t, 0.0), axis=1, keepdims=True)
                    i_row = jnp.sum(jnp.where(row_t == t, i_p, 0.0), axis=0, keepdims=True)
                    s = s * fc + kc * i_row
                    o = jnp.where(row_t == t, jnp.sum(qc * s, axis=0, keepdims=True), o)
                    return s, o

                s, o = lax.fori_loop(0, chunk, token,
                                     (st_ref[b, h].T, jnp.zeros((hd, hd), F32)))
                st_ref[b, h] = s.T
                o_ref[b, :, sl] = o[:chunk].astype(o_ref.dtype)
            return carry

        lax.fori_loop(0, ns, stream, 0)

    @pl.when(c == pl.num_programs(0) - 1)
    def _():
        sout_ref[...] = st_ref[...]


def _hg_call(q, k, iv, lf, s0t, chunk):
    s, l, w = q.shape
    assert l % chunk == 0
    seq_spec = pl.BlockSpec((s, chunk, w), lambda c: (0, c, 0))
    st_shape = (s, HG_HEADS, HG_HEAD_DIM, HG_HEAD_DIM)
    return pl.pallas_call(
        functools.partial(_hg_kernel, chunk=chunk),
        grid=(l // chunk,),
        in_specs=[seq_spec, seq_spec, seq_spec, seq_spec,
                  pl.BlockSpec(s0t.shape, lambda c: (0, 0, 0, 0))],
        out_specs=[seq_spec, pl.BlockSpec(st_shape, lambda c: (0, 0, 0, 0))],
        out_shape=[jax.ShapeDtypeStruct((s, l, w), BF16), jax.ShapeDtypeStruct(st_shape, F32)],
        scratch_shapes=[pltpu.VMEM(st_shape, F32)],
        compiler_params=_params(("arbitrary",)),
        name="hgrn",
    )(q, k, iv, lf, s0t)


def _post_kernel(x_ref, oa_ref, ob_ref, sg_ref, gates_ref, hn_ref, wa_ref, wb_ref, wo_ref,
                 n_ref, wg_ref, wu_ref, wd_ref, fn_ref, y_ref, acc_ref):
    hd = HG_HEAD_DIM
    d = x_ref.shape[1]
    parts = []
    for h in range(HG_HEADS):
        sl = slice(h * hd, (h + 1) * hd)
        parts.append(_rms(ob_ref[:, sl].astype(F32), hn_ref[:, sl]) * sg_ref[:, sl].astype(F32))
    ob = jnp.concatenate(parts, axis=-1).astype(BF16)
    ma = jnp.dot(oa_ref[...], wa_ref[...], preferred_element_type=F32)
    mb = jnp.dot(ob, wb_ref[...], preferred_element_type=F32)
    merged = gates_ref[:, :d].astype(F32) * ma + gates_ref[:, d:].astype(F32) * mb
    x = x_ref[...] + jnp.dot(merged.astype(BF16), wo_ref[...], preferred_element_type=F32)
    x = _half_ffn(x, n_ref[...], wg_ref, wu_ref, wd_ref, acc_ref)
    y_ref[...] = _rms(x, fn_ref[...])


def _post_call(x, oa, ob, sg, gates, hn, wa, wb, wo, norm, wg, wu, wd, fnorm, tm):
    n, d = x.shape
    consts = [hn, wa, wb, wo, norm, wg, wu, wd, fnorm]
    return pl.pallas_call(
        _post_kernel,
        grid=(n // tm,),
        in_specs=[_rows_spec(tm, d), _rows_spec(tm, oa.shape[1]), _rows_spec(tm, ob.shape[1]),
                  _rows_spec(tm, sg.shape[1]), _rows_spec(tm, gates.shape[1])]
                 + [_const_spec(a.shape) for a in consts],
        out_specs=_rows_spec(tm, d),
        out_shape=jax.ShapeDtypeStruct((n, d), F32),
        scratch_shapes=[pltpu.VMEM((tm, d), F32)],
        compiler_params=_params(("parallel",)),
        name="post",
    )(x, oa, ob, sg, gates, *consts)


def _row_tile(n, target):
    tm = min(n, target)
    assert n % tm == 0 and tm % 8 == 0
    return tm


def kernel(x_prompt, x_sample, cache_sb_k, cache_sb_v, state_hgrn, meta_tokens, ffn1_norm, ffn1_w_gate, ffn1_w_up, ffn1_w_down, mix_norm, w_in, b_gate, hg_lb_logits, hg_out_norm, w_branch_a, w_branch_b, w_out, ffn2_norm, ffn2_w_gate, ffn2_w_up, ffn2_w_down, final_norm):
    depth = w_in.shape[0]
    assert depth == 1, "single-layer step"
    nb, seq, d = x_prompt.shape
    ns, t_new, _ = x_sample.shape
    n_meta = meta_tokens.shape[0]
    w = SB_WIDTH
    row = lambda a: a.reshape(1, -1)

    f1 = (ffn1_w_gate[0].astype(BF16), ffn1_w_up[0].astype(BF16), ffn1_w_down[0].astype(BF16))
    f2 = (ffn2_w_gate[0].astype(BF16), ffn2_w_up[0].astype(BF16), ffn2_w_down[0].astype(BF16))
    w_in_b = w_in[0].astype(BF16)
    wa, wb, wo = w_branch_a[0].astype(BF16), w_branch_b[0].astype(BF16), w_out[0].astype(BF16)

    def post(x1, oa, ob, sg, gates, tm):
        return _post_call(x1, oa, ob, sg, gates, row(hg_out_norm[0]), wa, wb, wo,
                          row(ffn2_norm[0]), *f2, row(final_norm), tm)

    mix_consts = (row(mix_norm[0]), w_in_b, row(b_gate[0]), hg_lb_logits)

    n_frames = nb * seq
    tm_p = _row_tile(seq, 512)
    n_run = ns * t_new
    small = jnp.concatenate([x_sample.reshape(n_run, d), meta_tokens.astype(F32)], axis=0)
    x1_s = _ffn_call(small, row(ffn1_norm[0]), *f1, small.shape[0])
    (q_s, kb_s, vb_s, hq_s, hk_s, hi_s, hlf_s, sg_s, gates_s, k_s, v_s, meta_kt, meta_vt) = (
        _mix_small_call(x1_s, *mix_consts, n_run, n_meta))
    x1_p = _ffn_call(x_prompt.reshape(n_frames, d), row(ffn1_norm[0]), *f1, tm_p)
    (q_p, kb_p, vb_p, hq_p, hk_p, hi_p, hlf_p, sg_p, gates_p, kt_full, vt_full) = (
        _mix_frames_call(x1_p.reshape(nb, seq, d), *mix_consts, meta_kt, meta_vt, tm_p, n_meta))
    run = lambda a: a[:n_run]
    meta = lambda a: a[n_run:]
    token_minor = lambda a: jnp.transpose(a, (0, 1, 3, 4, 2))
    token_major = lambda a: jnp.transpose(a, (0, 1, 4, 2, 3))

    seq3 = lambda a, s: a.reshape(s, -1, a.shape[-1])
    pad_to = lambda a, rows: jnp.pad(a, ((0, 0), (0, rows - a.shape[1]), (0, 0)))
    oa_p = _sb_call(q_p, kb_p, vb_p,
                    pad_to(meta(kb_s)[None], KEY_BLOCK), pad_to(meta(vb_s)[None], KEY_BLOCK),
                    n_meta, KEY_BLOCK)
    oa_s = _sbd_call(seq3(run(q_s), ns), pad_to(seq3(run(kb_s), ns), KEY_BLOCK),
                     pad_to(seq3(run(vb_s), ns), KEY_BLOCK),
                     token_minor(cache_sb_k), token_minor(cache_sb_v))

    hg = lambda sel, s, arrs: [seq3(sel(a), s) for a in arrs]
    zero_state = jnp.zeros((1, HG_HEADS, HG_HEAD_DIM, HG_HEAD_DIM), F32)
    _, st_meta = _hg_call(*hg(meta, 1, (hq_s, hk_s, hi_s, hlf_s)), zero_state, n_meta)
    ob_p, st_p = _hg_call(*hg(lambda a: a, nb, (hq_p, hk_p, hi_p, hlf_p)), st_meta, HG_CHUNK)
    ob_s, st_s = _hg_call(*hg(run, ns, (hq_s, hk_s, hi_s, hlf_s)),
                          jnp.swapaxes(state_hgrn[0].astype(F32), -1, -2), t_new)

    rows = lambda a: a.reshape(-1, a.shape[-1])
    y_p = post(x1_p, rows(oa_p), rows(ob_p), rows(sg_p), rows(gates_p), tm_p)
    y_s = post(run(x1_s), rows(oa_s), rows(ob_s), run(sg_s), run(gates_s), n_run)

    heads = lambda a: a.reshape(1, ns, t_new, SB_HEADS, SB_HEAD_DIM)
    return (y_p.reshape(nb, seq, d), y_s.reshape(ns, t_new, d),
            token_major(kt_full), token_major(vt_full),
            jnp.swapaxes(st_p, -1, -2)[None],
            heads(run(k_s)), heads(run(v_s)),
            jnp.swapaxes(st_s, -1, -2)[None])
```

```python
import functools

import jax
import jax.numpy as jnp
from jax import lax
from jax.experimental import pallas as pl
from jax.experimental.pallas import tpu as pltpu

F32 = jnp.float32
BF16 = jnp.bfloat16
EPS = 1e-6
SB_HEADS = 8
SB_HEAD_DIM = 64
SB_WIDTH = SB_HEADS * SB_HEAD_DIM
HG_HEADS = 4
HG_HEAD_DIM = 128
HG_WIDTH = HG_HEADS * HG_HEAD_DIM
LANES = 128
FF_CHUNK = 256
ROW_PART = 256
KEY_BLOCK = 128
WIDE_BLOCK = 256
F32_EXP2_UNDERFLOW = -152.0
LOG2_E = 1.4426950408889634
HG_CHUNK = 64
HG_CHUNKS_PER_STEP = 4
HG_SAFE_LOG_DECAY = -60.0
VMEM_LIMIT_BYTES = 56 * 1024 * 1024

_NN = (((1,), (0,)), ((), ()))
_NT = (((1,), (1,)), ((), ()))
_TN = (((0,), (0,)), ((), ()))


def _const_spec(shape):
    nd = len(shape)
    return pl.BlockSpec(shape, lambda *_: (0,) * nd, pipeline_mode=pl.Buffered(1))


def _rows_spec(tm, width):
    return pl.BlockSpec((tm, width), lambda i: (i, 0))


def _params(sem):
    return pltpu.CompilerParams(dimension_semantics=sem, vmem_limit_bytes=VMEM_LIMIT_BYTES)


def _row_parts(rows):
    if rows % ROW_PART or rows == ROW_PART:
        return [slice(0, rows)]
    return [slice(r, r + ROW_PART) for r in range(0, rows, ROW_PART)]


def _rms(x, g):
    return x * lax.rsqrt(jnp.mean(x * x, axis=-1, keepdims=True) + EPS) * g


def _half_ffn(x, g, wg_ref, wu_ref, wd_ref, acc_ref):
    h = _rms(x, g).astype(BF16)
    dff = wg_ref.shape[1]
    assert dff % FF_CHUNK == 0
    for c in range(dff // FF_CHUNK):
        cols = slice(c * FF_CHUNK, (c + 1) * FF_CHUNK)
        gate = jnp.dot(h, wg_ref[:, cols], preferred_element_type=F32)
        up = jnp.dot(h, wu_ref[:, cols], preferred_element_type=F32)
        act = (gate * jax.nn.sigmoid(gate) * up).astype(BF16)
        contrib = jnp.dot(act, wd_ref[cols, :], preferred_element_type=F32)
        if c == 0:
            acc_ref[...] = contrib
        else:
            acc_ref[...] += contrib
    return x + 0.5 * acc_ref[...]


def _ffn_kernel(x_ref, n_ref, wg_ref, wu_ref, wd_ref, o_ref, acc_ref):
    o_ref[...] = _half_ffn(x_ref[...], n_ref[...], wg_ref, wu_ref, wd_ref, acc_ref)


def _ffn_call(x, norm, wg, wu, wd, tm):
    n, d = x.shape
    return pl.pallas_call(
        _ffn_kernel,
        grid=(n // tm,),
        in_specs=[_rows_spec(tm, d), _const_spec(norm.shape), _const_spec(wg.shape),
                  _const_spec(wu.shape), _const_spec(wd.shape)],
        out_specs=_rows_spec(tm, d),
        out_shape=jax.ShapeDtypeStruct((n, d), F32),
        scratch_shapes=[pltpu.VMEM((tm, d), F32)],
        compiler_params=_params(("parallel",)),
        name="ffn",
    )(x, norm, wg, wu, wd)


def _mix_body(x_ref, n_ref, win_ref, bg_ref, lbl_ref,
              q_ref, kb_ref, vb_ref, hq_ref, hk_ref, hi_ref, hlf_ref, sg_ref, gates_ref):
    w = SB_WIDTH
    d2 = gates_ref.shape[1]
    logits = lbl_ref[...]
    l0, l1 = logits[0:1], logits[1:2]
    m = jnp.maximum(l0, l1)
    e0, e1 = jnp.exp(l0 - m), jnp.exp(l1 - m)
    p0, p1 = e0 / (e0 + e1), e1 / (e0 + e1)
    lb = (p0 + p1) - p0

    ks, vs = [], []
    for r in _row_parts(x_ref.shape[0]):
        h = _rms(x_ref[r, :], n_ref[...]).astype(BF16)

        def proj(lo, width, h=h):
            return jnp.dot(h, win_ref[:, lo:lo + width], preferred_element_type=F32)

        q_ref[r, :] = (proj(0, w) * (SB_HEAD_DIM ** -0.5 * LOG2_E)).astype(BF16)
        k = proj(w, w)
        kb_ref[r, :] = k.astype(BF16)
        v = proj(2 * w, w)
        vb_ref[r, :] = v.astype(BF16)
        f = lb + (1.0 - lb) * jax.nn.sigmoid(proj(3 * w, w))
        hk_ref[r, :] = (1.0 - f).astype(hk_ref.dtype)
        hlf_ref[r, :] = jnp.log(f)
        hi_ref[r, :] = proj(4 * w, w).astype(hi_ref.dtype)
        hq_ref[r, :] = proj(5 * w, w).astype(hq_ref.dtype)
        g = proj(6 * w, w)
        sg_ref[r, :] = (g * jax.nn.sigmoid(g)).astype(sg_ref.dtype)
        gates_ref[r, :] = jax.nn.sigmoid(proj(7 * w, d2) + bg_ref[...]).astype(gates_ref.dtype)
        ks.append(k)
        vs.append(v)
    return jnp.concatenate(ks, axis=0), jnp.concatenate(vs, axis=0)


def _mix_small_kernel(*refs, n_run, n_meta):
    body_refs, (k_ref, v_ref, kt_ref, vt_ref) = refs[:14], refs[14:]
    k, v = _mix_body(*body_refs)
    k_ref[...] = k
    v_ref[...] = v
    kt_ref[...] = _pad_rows(k[n_run:n_run + n_meta], LANES).T
    vt_ref[...] = _pad_rows(v[n_run:n_run + n_meta], LANES).T


def _mix_frames_kernel(*refs, tm, n_meta, n_tiles):
    x_ref, consts, mk_ref, mv_ref = refs[0], refs[1:5], refs[5], refs[6]
    row_outs, (kt_ref, vt_ref, ck_ref, cv_ref) = refs[7:16], refs[16:]
    t = pl.program_id(1)
    shape = (SB_HEADS, SB_HEAD_DIM, tm)

    @pl.when(t == 0)
    def _():
        ck_ref[...] = mk_ref[...]
        cv_ref[...] = mv_ref[...]

    @pl.when(t < n_tiles)
    def _():
        k, v = _mix_body(x_ref.at[0], *consts, *[r.at[0] for r in row_outs])
        for rows, out_ref, carry_ref in ((k, kt_ref, ck_ref), (v, vt_ref, cv_ref)):
            cols = rows.T
            late = jnp.concatenate([carry_ref[:, :n_meta], cols[:, :tm - n_meta]], axis=1)
            out_ref[0, 0] = late.reshape(shape)
            carry_ref[:, :n_meta] = cols[:, tm - n_meta:]

    @pl.when(t == n_tiles)
    def _():
        for out_ref, carry_ref in ((kt_ref, ck_ref), (vt_ref, cv_ref)):
            tail = jnp.concatenate(
                [carry_ref[:, :n_meta], jnp.zeros((SB_WIDTH, tm - n_meta), F32)], axis=1)
            out_ref[0, 0] = tail.reshape(shape)


_MIX_WIDTHS = [(SB_WIDTH, BF16)] * 6 + [(SB_WIDTH, F32), (SB_WIDTH, BF16)]


def _mix_small_call(x, norm, w_in, b_gate, lb_logits, n_run, n_meta):
    n, d = x.shape
    widths = _MIX_WIDTHS + [(2 * d, BF16), (SB_WIDTH, F32), (SB_WIDTH, F32)]
    meta_t = jax.ShapeDtypeStruct((SB_WIDTH, LANES), F32)
    return pl.pallas_call(
        functools.partial(_mix_small_kernel, n_run=n_run, n_meta=n_meta),
        grid=(1,),
        in_specs=[_rows_spec(n, d), _const_spec(norm.shape), _const_spec(w_in.shape),
                  _const_spec(b_gate.shape), _const_spec(lb_logits.shape)],
        out_specs=[_rows_spec(n, wd) for wd, _ in widths]
                  + [pl.BlockSpec(meta_t.shape, lambda i: (0, 0))] * 2,
        out_shape=[jax.ShapeDtypeStruct((n, wd), dt) for wd, dt in widths] + [meta_t] * 2,
        compiler_params=_params(("arbitrary",)),
        name="mix_small",
    )(x, norm, w_in, b_gate, lb_logits)


def _mix_frames_call(x, norm, w_in, b_gate, lb_logits, meta_kt, meta_vt, tm, n_meta):
    nb, seq, d = x.shape
    assert seq % tm == 0 and n_meta < tm
    n_tiles = seq // tm
    widths = _MIX_WIDTHS + [(2 * d, BF16)]
    rows_map = lambda b, t: (b, jnp.minimum(t, n_tiles - 1), 0)
    const2 = lambda shape: pl.BlockSpec(shape, lambda b, t: (0, 0), pipeline_mode=pl.Buffered(1))
    kv_shape = jax.ShapeDtypeStruct((1, nb, SB_HEADS, SB_HEAD_DIM, n_meta + seq), F32)
    kv_spec = pl.BlockSpec((1, 1, SB_HEADS, SB_HEAD_DIM, tm), lambda b, t: (0, b, 0, 0, t))
    carry = pltpu.VMEM((SB_WIDTH, LANES), F32)
    return pl.pallas_call(
        functools.partial(_mix_frames_kernel, tm=tm, n_meta=n_meta, n_tiles=n_tiles),
        grid=(nb, n_tiles + 1),
        in_specs=[pl.BlockSpec((1, tm, d), rows_map)]
                 + [const2(a.shape) for a in (norm, w_in, b_gate, lb_logits, meta_kt, meta_vt)],
        out_specs=[pl.BlockSpec((1, tm, wd), rows_map) for wd, _ in widths] + [kv_spec] * 2,
        out_shape=[jax.ShapeDtypeStruct((nb, seq, wd), dt) for wd, dt in widths] + [kv_shape] * 2,
        scratch_shapes=[carry, carry],
        compiler_params=_params(("arbitrary", "arbitrary")),
        name="mix_frames",
    )(x, norm, w_in, b_gate, lb_logits, meta_kt, meta_vt)


def _suffix_ones(width):
    r = lax.broadcasted_iota(jnp.int32, (width, width), 0)
    c = lax.broadcasted_iota(jnp.int32, (width, width), 1)
    return jnp.where(r >= c, 1.0, 0.0).astype(BF16)


def _key_before(tq, width, k0, q_first):
    row = lax.broadcasted_iota(jnp.int32, (tq, width), 0)
    col = lax.broadcasted_iota(jnp.int32, (tq, width), 1)
    return col + k0 < row + q_first


def _sb_sweep(q_ops, blocks, r_ref, acc_ref, token_minor=False):
    heads = range(SB_HEADS)
    k_dims, v_dims = (_NN, _NT) if token_minor else (_NT, _NN)
    tq = q_ops[0].shape[0]
    zs = [[lax.dot_general(q_ops[h], k_ops[h], k_dims, preferred_element_type=F32)
           for h in heads] for k_ops, _, _, _ in blocks]
    stays = []
    for z_blk, (_, _, _, mask) in zip(zs, blocks):
        stays.append([])
        for z in z_blk:
            nz = -z
            ls = jnp.minimum(nz, 0.0) - jnp.log2(1.0 + jnp.exp2(jnp.minimum(z, nz)))
            if mask is not None:
                ls = jnp.where(mask, ls, 0.0)
            stays[-1].append(ls.astype(BF16))
    cums = [[jnp.dot(st, ones, preferred_element_type=F32) for st in st_blk]
            for st_blk, (_, _, ones, _) in zip(stays, blocks)]
    ws = []
    for z_blk, cum_blk, (_, _, ones, mask) in zip(zs, cums, blocks):
        ws.append([])
        for h in heads:
            r = r_ref[h]
            cr = cum_blk[h] + jnp.concatenate([r] * (ones.shape[1] // LANES), axis=1)
            a = jnp.exp2(z_blk[h] + cr)
            if mask is not None:
                a = jnp.where(mask, a, 0.0)
            ws[-1].append(a.astype(BF16))
            r_ref[h] = jnp.broadcast_to(cr[:, 0:1], (tq, LANES))
    for w_blk, (_, v_ops, _, _) in zip(ws, blocks):
        for h in heads:
            acc_ref[h] += lax.dot_general(w_blk[h], v_ops[h], v_dims, preferred_element_type=F32)


def _sb_live(r_ref):
    m = r_ref[0]
    for h in range(1, SB_HEADS):
        m = jnp.maximum(m, r_ref[h])
    return (jnp.max(m) >= F32_EXP2_UNDERFLOW).astype(jnp.int32)


def _sweep_while(n_blocks, alive, r_ref, block, first=0):
    def body(carry):
        t, _ = carry
        block(t)
        return t + 1, _sb_live(r_ref)

    _, alive = lax.while_loop(lambda c: jnp.logical_and(c[0] < n_blocks, c[1] > 0),
                              body, (jnp.int32(first), alive))
    return alive


def _sb_kernel(q_ref, kn_ref, vn_ref, ko_ref, vo_ref, o_ref, qm_ref, acc_ref, r_ref, *,
               tq, n_old):
    kb, wb = KEY_BLOCK, WIDE_BLOCK
    i = pl.program_id(1)
    lane = lax.broadcasted_iota(jnp.int32, (tq, LANES), 1)
    ones = {width: _suffix_ones(width) for width in (kb, wb)}

    for h in range(SB_HEADS):
        p = h // 2
        qp = q_ref[0, :, p * LANES:(p + 1) * LANES]
        keep = (lane < SB_HEAD_DIM) if h % 2 == 0 else (lane >= SB_HEAD_DIM)
        qm_ref[h] = jnp.where(keep, qp, jnp.zeros_like(qp))
        r_ref[h] = jnp.zeros((tq, LANES), F32)
        acc_ref[h] = jnp.zeros((tq, LANES), F32)

    def block(k_ref, v_ref, start, width, mask):
        k_all = k_ref[0, pl.ds(start, width), :].astype(BF16)
        v_all = v_ref[0, pl.ds(start, width), :].astype(BF16)
        pair = lambda a: [a[:, (h // 2) * LANES:(h // 2 + 1) * LANES] for h in range(SB_HEADS)]
        return pair(k_all), pair(v_all), ones[width], mask

    def sweep(*blocks):
        _sb_sweep([qm_ref[h] for h in range(SB_HEADS)], blocks, r_ref, acc_ref)

    q0 = i * tq
    below = q0 // kb
    d0 = pl.multiple_of(below * kb, kb)
    own = lambda: block(kn_ref, vn_ref, d0, kb, _key_before(tq, kb, d0, q0))
    paired = below >= 2

    @pl.when(paired)
    def _():
        sweep(own(), block(kn_ref, vn_ref, pl.multiple_of(d0 - wb, kb), wb, None))

    @pl.when(jnp.logical_not(paired))
    def _():
        sweep(own())

    top = jnp.where(paired, d0 - wb, d0)
    left = jnp.where(paired, below - 2, below)
    alive = _sweep_while(left // 2, _sb_live(r_ref), r_ref, lambda t: sweep(block(
        kn_ref, vn_ref, pl.multiple_of(top - (t + 1) * wb, kb), wb, None)))
    alive = _sweep_while(left % 2, alive, r_ref,
                         lambda t: sweep(block(kn_ref, vn_ref, 0, kb, None)))
    _sweep_while(1, alive, r_ref, lambda t: sweep(block(
        ko_ref, vo_ref, 0, kb, lax.broadcasted_iota(jnp.int32, (tq, kb), 1) < n_old)))

    for p in range(SB_HEADS // 2):
        o_ref[0, :, p * LANES:(p + 1) * LANES] = jnp.where(
            lane < SB_HEAD_DIM, acc_ref[2 * p], acc_ref[2 * p + 1]).astype(o_ref.dtype)


def _sbd_kernel(q_ref, kn_ref, vn_ref, ck_ref, cv_ref, o_ref, acc_ref, r_ref, *, tq, n_old):
    kb, wb, hd = KEY_BLOCK, WIDE_BLOCK, SB_HEAD_DIM
    heads = range(SB_HEADS)
    n_tail, n_full = n_old % kb, n_old // kb
    n_wide = (n_full * kb) // wb
    rest = n_full * kb - n_wide * wb
    per_head = lambda a: [a[:, h * hd:(h + 1) * hd] for h in heads]

    q_ops = per_head(q_ref[0])
    for h in heads:
        r_ref[h] = jnp.zeros((tq, LANES), F32)
        acc_ref[h] = jnp.zeros((tq, hd), F32)

    assert kn_ref.shape[1] == kb
    _sb_sweep(q_ops, [(per_head(kn_ref[0]), per_head(vn_ref[0]), _suffix_ones(kb),
                       _key_before(tq, kb, 0, 0))], r_ref, acc_ref)
    alive = _sb_live(r_ref)

    def cached(start, width, valid):
        def load(ref, h):
            cols = ref[0, 0, h, :, pl.ds(start, valid)].astype(BF16)
            if valid < width:
                cols = jnp.concatenate([cols, jnp.zeros((hd, width - valid), BF16)], axis=1)
            return cols

        mask = None if valid == width else (
            lax.broadcasted_iota(jnp.int32, (tq, width), 1) < valid)
        _sb_sweep(q_ops, [([load(ck_ref, h) for h in heads], [load(cv_ref, h) for h in heads],
                           _suffix_ones(width), mask)], r_ref, acc_ref, token_minor=True)

    if n_tail:
        alive = _sweep_while(1, alive, r_ref, lambda t: cached(n_full * kb, kb, n_tail))
    if n_wide:
        alive = _sweep_while(n_wide, alive, r_ref, lambda t: cached(
            pl.multiple_of(rest + (n_wide - 1 - t) * wb, kb), wb, wb))
    if rest:
        _sweep_while(1, alive, r_ref, lambda t: cached(0, rest, rest))

    o_ref[0] = jnp.concatenate([acc_ref[h] for h in heads], axis=1).astype(o_ref.dtype)


def _sb_call(q, k_new, v_new, k_old, v_old, n_old, tq):
    s, lq, w = q.shape
    ln = k_new.shape[1]
    lo = k_old.shape[1]
    assert lq % tq == 0 and ln % KEY_BLOCK == 0 and n_old <= KEY_BLOCK <= lo
    assert tq <= KEY_BLOCK and KEY_BLOCK % tq == 0
    old_map = (lambda b, i: (b, 0, 0)) if k_old.shape[0] == s else (lambda b, i: (0, 0, 0))
    return pl.pallas_call(
        functools.partial(_sb_kernel, tq=tq, n_old=n_old),
        grid=(s, lq // tq),
        in_specs=[pl.BlockSpec((1, tq, w), lambda b, i: (b, i, 0)),
                  pl.BlockSpec((1, ln, w), lambda b, i: (b, 0, 0)),
                  pl.BlockSpec((1, ln, w), lambda b, i: (b, 0, 0)),
                  pl.BlockSpec((1, lo, w), old_map),
                  pl.BlockSpec((1, lo, w), old_map)],
        out_specs=pl.BlockSpec((1, tq, w), lambda b, i: (b, i, 0)),
        out_shape=jax.ShapeDtypeStruct((s, lq, w), BF16),
        scratch_shapes=[pltpu.VMEM((SB_HEADS, tq, LANES), BF16),
                        pltpu.VMEM((SB_HEADS, tq, LANES), F32),
                        pltpu.VMEM((SB_HEADS, tq, LANES), F32)],
        compiler_params=_params(("parallel", "parallel")),
        name="sb",
    )(q, k_new, v_new, k_old, v_old)


def _sbd_call(q, k_new, v_new, cache_k, cache_v):
    s, tq, w = q.shape
    n_old = cache_k.shape[-1]
    assert tq <= KEY_BLOCK and k_new.shape[1] == KEY_BLOCK
    new_spec = pl.BlockSpec((1, KEY_BLOCK, w), lambda b: (b, 0, 0))
    cache_spec = pl.BlockSpec((1, 1) + cache_k.shape[2:], lambda b: (0, b, 0, 0, 0))
    return pl.pallas_call(
        functools.partial(_sbd_kernel, tq=tq, n_old=n_old),
        grid=(s,),
        in_specs=[pl.BlockSpec((1, tq, w), lambda b: (b, 0, 0)), new_spec, new_spec,
                  cache_spec, cache_spec],
        out_specs=pl.BlockSpec((1, tq, w), lambda b: (b, 0, 0)),
        out_shape=jax.ShapeDtypeStruct((s, tq, w), BF16),
        scratch_shapes=[pltpu.VMEM((SB_HEADS, tq, SB_HEAD_DIM), F32),
                        pltpu.VMEM((SB_HEADS, tq, LANES), F32)],
        compiler_params=_params(("parallel",)),
        name="sb_decode",
    )(q, k_new, v_new, cache_k, cache_v)


def _pad_rows(x, rows):
    if x.shape[0] == rows:
        return x
    return jnp.concatenate([x, jnp.zeros((rows - x.shape[0], x.shape[1]), x.dtype)], axis=0)


def _hg_kernel(q_ref, k_ref, i_ref, lf_ref, s0_ref, o_ref, sout_ref, st_ref, *, chunk):
    c = pl.program_id(0)
    hd = HG_HEAD_DIM
    ns, n_chunks = q_ref.shape[0], q_ref.shape[1] // chunk
    streams = range(ns)
    units = [(g, b) for g in range(n_chunks) for b in streams]
    rows = lambda g: slice(g * chunk, (g + 1) * chunk)
    head_cols = [slice(h * hd, (h + 1) * hd) for h in range(HG_HEADS)]

    @pl.when(c == 0)
    def _():
        for b in streams:
            st_ref[b] = s0_ref[b if s0_ref.shape[0] == ns else 0]

    trow = lax.broadcasted_iota(jnp.int32, (chunk, chunk), 0)
    tcol = lax.broadcasted_iota(jnp.int32, (chunk, chunk), 1)
    causal = trow >= tcol
    tri = jnp.where(causal, 1.0, 0.0).astype(BF16)

    terms = {}
    for g, b in units:
        lf = lf_ref[b, rows(g), :]
        hi = lf.astype(BF16)
        r1 = lf - hi.astype(F32)
        mid = r1.astype(BF16)
        terms[g, b] = (hi, mid, (r1 - mid.astype(F32)).astype(BF16))
    cum = {u: sum(jnp.dot(tri, t, preferred_element_type=F32) for t in terms[u]) for u in units}
    last = {u: cum[u][chunk - 1:chunk, :] for u in units}
    weakest = functools.reduce(jnp.minimum, last.values())
    safe = jnp.min(weakest) >= HG_SAFE_LOG_DECAY

    @pl.when(safe)
    def _():
        qd, kd, kl, iv, dl = {}, {}, {}, {}, {}
        for g, b in units:
            k = k_ref[b, rows(g), :]
            qd[g, b] = (q_ref[b, rows(g), :] * jnp.exp(cum[g, b])).astype(BF16)
            kd[g, b] = (k * jnp.exp(-cum[g, b])).astype(BF16)
            kl[g, b] = (k * jnp.exp(last[g, b] - cum[g, b])).astype(BF16)
            iv[g, b] = i_ref[b, rows(g), :].astype(BF16)
            dl[g, b] = jnp.exp(last[g, b])
        sc = {(u, h): lax.dot_general(qd[u][:, sl], kd[u][:, sl], _NT, preferred_element_type=F32)
              for u in units for h, sl in enumerate(head_cols)}
        grown = {(u, h): lax.dot_general(iv[u][:, sl], kl[u][:, sl], _TN,
                                         preferred_element_type=F32)
                 for u in units for h, sl in enumerate(head_cols)}
        within = {(u, h): jnp.dot(jnp.where(causal, sc[u, h], 0.0).astype(BF16), iv[u][:, sl],
                                  preferred_element_type=F32)
                  for u in units for h, sl in enumerate(head_cols)}
        for g in range(n_chunks):
            carried = {(b, h): lax.dot_general(qd[g, b][:, sl], st_ref[b, h].astype(BF16), _NT,
                                               preferred_element_type=F32)
                       for b in streams for h, sl in enumerate(head_cols)}
            for b in streams:
                for h, sl in enumerate(head_cols):
                    o_ref[b, rows(g), sl] = (within[(g, b), h] + carried[b, h]).astype(o_ref.dtype)
                    st_ref[b, h] = st_ref[b, h] * dl[g, b][:, sl] + grown[(g, b), h]

    @pl.when(jnp.logical_not(safe))
    def _():
        lane_t = lax.broadcasted_iota(jnp.int32, (hd, hd), 1)
        row_t = lax.broadcasted_iota(jnp.int32, (hd, hd), 0)

        def one_chunk(n, carry):
            b, r0 = n // n_chunks, pl.multiple_of((n % n_chunks) * chunk, chunk)
            for h, sl in enumerate(head_cols):
                load = lambda ref: ref[b, pl.ds(r0, chunk), sl].astype(F32)
                f_t = _pad_rows(jnp.exp(load(lf_ref)), hd).T
                k_t = _pad_rows(load(k_ref), hd).T
                q_t = _pad_rows(load(q_ref), hd).T
                i_p = _pad_rows(load(i_ref), hd)

                def token(t, sc, f_t=f_t, k_t=k_t, q_t=q_t, i_p=i_p):
                    s, o = sc
                    sel = lane_t == t
                    fc = jnp.sum(jnp.where(sel, f_t, 0.0), axis=1, keepdims=True)
                    kc = jnp.sum(jnp.where(sel, k_t, 0.0), axis=1, keepdims=True)
                    qc = jnp.sum(jnp.where(sel, q_t, 0.0), axis=1, keepdims=True)
                    i_row = jnp.sum(jnp.where(row_t == t, i_p, 0.0), axis=0, keepdims=True)
                    s = s * fc + kc * i_row
                    o = jnp.where(row_t == t, jnp.sum(qc * s, axis=0, keepdims=True), o)
                    return s, o

                s, o = lax.fori_loop(0, chunk, token,
                                     (st_ref[b, h].T, jnp.zeros((hd, hd), F32)))
                st_ref[b, h] = s.T
                o_ref[b, pl.ds(r0, chunk), sl] = o[:chunk].astype(o_ref.dtype)
            return carry

        lax.fori_loop(0, ns * n_chunks, one_chunk, 0)

    @pl.when(c == pl.num_programs(0) - 1)
    def _():
        sout_ref[...] = st_ref[...]


def _hg_call(q, k, iv, lf, s0t, chunk, chunks_per_step=1):
    s, l, w = q.shape
    rows = chunk * chunks_per_step
    assert l % rows == 0
    seq_spec = pl.BlockSpec((s, rows, w), lambda c: (0, c, 0))
    st_shape = (s, HG_HEADS, HG_HEAD_DIM, HG_HEAD_DIM)
    return pl.pallas_call(
        functools.partial(_hg_kernel, chunk=chunk),
        grid=(l // rows,),
        in_specs=[seq_spec, seq_spec, seq_spec, seq_spec,
                  pl.BlockSpec(s0t.shape, lambda c: (0, 0, 0, 0))],
        out_specs=[seq_spec, pl.BlockSpec(st_shape, lambda c: (0, 0, 0, 0))],
        out_shape=[jax.ShapeDtypeStruct((s, l, w), BF16), jax.ShapeDtypeStruct(st_shape, F32)],
        scratch_shapes=[pltpu.VMEM(st_shape, F32)],
        compiler_params=_params(("arbitrary",)),
        name="hgrn",
    )(q, k, iv, lf, s0t)


def _post_kernel(x_ref, oa_ref, ob_ref, sg_ref, gates_ref, hn_ref, wa_ref, wb_ref, wo_ref,
                 n_ref, wg_ref, wu_ref, wd_ref, fn_ref, y_ref, acc_ref):
    hd = HG_HEAD_DIM
    d = x_ref.shape[1]
    heads = []
    for h in range(HG_HEADS):
        sl = slice(h * hd, (h + 1) * hd)
        heads.append(_rms(ob_ref[:, sl].astype(F32), hn_ref[:, sl]) * sg_ref[:, sl].astype(F32))
    ob = jnp.concatenate(heads, axis=-1).astype(BF16)
    ma = jnp.dot(oa_ref[...], wa_ref[...], preferred_element_type=F32)
    mb = jnp.dot(ob, wb_ref[...], preferred_element_type=F32)
    merged = gates_ref[:, :d].astype(F32) * ma + gates_ref[:, d:].astype(F32) * mb
    x = x_ref[...] + jnp.dot(merged.astype(BF16), wo_ref[...], preferred_element_type=F32)
    x = _half_ffn(x, n_ref[...], wg_ref, wu_ref, wd_ref, acc_ref)
    y_ref[...] = _rms(x, fn_ref[...])


def _post_call(x, oa, ob, sg, gates, hn, wa, wb, wo, norm, wg, wu, wd, fnorm, tm):
    n, d = x.shape
    consts = [hn, wa, wb, wo, norm, wg, wu, wd, fnorm]
    return pl.pallas_call(
        _post_kernel,
        grid=(n // tm,),
        in_specs=[_rows_spec(tm, d), _rows_spec(tm, oa.shape[1]), _rows_spec(tm, ob.shape[1]),
                  _rows_spec(tm, sg.shape[1]), _rows_spec(tm, gates.shape[1])]
                 + [_const_spec(a.shape) for a in consts],
        out_specs=_rows_spec(tm, d),
        out_shape=jax.ShapeDtypeStruct((n, d), F32),
        scratch_shapes=[pltpu.VMEM((tm, d), F32)],
        compiler_params=_params(("parallel",)),
        name="post",
    )(x, oa, ob, sg, gates, *consts)


def _row_tile(n, target):
    tm = min(n, target)
    assert n % tm == 0 and tm % 8 == 0
    return tm


def kernel(x_prompt, x_sample, cache_sb_k, cache_sb_v, state_hgrn, meta_tokens, ffn1_norm, ffn1_w_gate, ffn1_w_up, ffn1_w_down, mix_norm, w_in, b_gate, hg_lb_logits, hg_out_norm, w_branch_a, w_branch_b, w_out, ffn2_norm, ffn2_w_gate, ffn2_w_up, ffn2_w_down, final_norm):
    depth = w_in.shape[0]
    assert depth == 1, "single-layer step"
    nb, seq, d = x_prompt.shape
    ns, t_new, _ = x_sample.shape
    n_meta = meta_tokens.shape[0]
    w = SB_WIDTH
    row = lambda a: a.reshape(1, -1)

    f1 = (ffn1_w_gate[0].astype(BF16), ffn1_w_up[0].astype(BF16), ffn1_w_down[0].astype(BF16))
    f2 = (ffn2_w_gate[0].astype(BF16), ffn2_w_up[0].astype(BF16), ffn2_w_down[0].astype(BF16))
    w_in_b = w_in[0].astype(BF16)
    wa, wb, wo = w_branch_a[0].astype(BF16), w_branch_b[0].astype(BF16), w_out[0].astype(BF16)

    def post(x1, oa, ob, sg, gates, tm):
        return _post_call(x1, oa, ob, sg, gates, row(hg_out_norm[0]), wa, wb, wo,
                          row(ffn2_norm[0]), *f2, row(final_norm), tm)

    mix_consts = (row(mix_norm[0]), w_in_b, row(b_gate[0]), hg_lb_logits)

    n_frames = nb * seq
    tm_p = _row_tile(seq, 512)
    n_run = ns * t_new
    small = jnp.concatenate([x_sample.reshape(n_run, d), meta_tokens.astype(F32)], axis=0)
    x1_s = _ffn_call(small, row(ffn1_norm[0]), *f1, small.shape[0])
    (q_s, kb_s, vb_s, hq_s, hk_s, hi_s, hlf_s, sg_s, gates_s, k_s, v_s, meta_kt, meta_vt) = (
        _mix_small_call(x1_s, *mix_consts, n_run, n_meta))
    x1_p = _ffn_call(x_prompt.reshape(n_frames, d), row(ffn1_norm[0]), *f1, tm_p)
    (q_p, kb_p, vb_p, hq_p, hk_p, hi_p, hlf_p, sg_p, gates_p, kt_full, vt_full) = (
        _mix_frames_call(x1_p.reshape(nb, seq, d), *mix_consts, meta_kt, meta_vt, tm_p, n_meta))
    run = lambda a: a[:n_run]
    meta = lambda a: a[n_run:]
    token_minor = lambda a: jnp.transpose(a, (0, 1, 3, 4, 2))
    token_major = lambda a: jnp.transpose(a, (0, 1, 4, 2, 3))

    seq3 = lambda a, s: a.reshape(s, -1, a.shape[-1])
    pad_to = lambda a, rows: jnp.pad(a, ((0, 0), (0, rows - a.shape[1]), (0, 0)))
    oa_p = _sb_call(q_p, kb_p, vb_p,
                    pad_to(meta(kb_s)[None], KEY_BLOCK), pad_to(meta(vb_s)[None], KEY_BLOCK),
                    n_meta, KEY_BLOCK)
    oa_s = _sbd_call(seq3(run(q_s), ns), pad_to(seq3(run(kb_s), ns), KEY_BLOCK),
                     pad_to(seq3(run(vb_s), ns), KEY_BLOCK),
                     token_minor(cache_sb_k), token_minor(cache_sb_v))

    hg = lambda sel, s, arrs: [seq3(sel(a), s) for a in arrs]
    zero_state = jnp.zeros((1, HG_HEADS, HG_HEAD_DIM, HG_HEAD_DIM), F32)
    _, st_meta = _hg_call(*hg(meta, 1, (hq_s, hk_s, hi_s, hlf_s)), zero_state, n_meta)
    ob_p, st_p = _hg_call(*hg(lambda a: a, nb, (hq_p, hk_p, hi_p, hlf_p)), st_meta, HG_CHUNK,
                          HG_CHUNKS_PER_STEP)
    ob_s, st_s = _hg_call(*hg(run, ns, (hq_s, hk_s, hi_s, hlf_s)),
                          jnp.swapaxes(state_hgrn[0].astype(F32), -1, -2), t_new)

    rows = lambda a: a.reshape(-1, a.shape[-1])
    y_p = post(x1_p, rows(oa_p), rows(ob_p), rows(sg_p), rows(gates_p), tm_p)
    y_s = post(run(x1_s), rows(oa_s), rows(ob_s), run(sg_s), run(gates_s), n_run)

    heads = lambda a: a.reshape(1, ns, t_new, SB_HEADS, SB_HEAD_DIM)
    return (y_p.reshape(nb, seq, d), y_s.reshape(ns, t_new, d),
            token_major(kt_full), token_major(vt_full),
            jnp.swapaxes(st_p, -1, -2)[None],
            heads(run(k_s)), heads(run(v_s)),
            jnp.swapaxes(st_s, -1, -2)[None])
```

```python
import functools

import jax
import jax.numpy as jnp
from jax import lax
from jax.experimental import pallas as pl
from jax.experimental.pallas import tpu as pltpu

F32 = jnp.float32
BF16 = jnp.bfloat16
EPS = 1e-6
SB_HEADS = 8
SB_HEAD_DIM = 64
SB_WIDTH = SB_HEADS * SB_HEAD_DIM
HG_HEADS = 4
HG_HEAD_DIM = 128
HG_WIDTH = HG_HEADS * HG_HEAD_DIM
LANES = 128
FF_CHUNK = 256
ROW_PART = 256
KEY_BLOCK = 128
WIDE_BLOCK = 256
SB_BLOCKS_PER_STEP = 4
F32_EXP2_UNDERFLOW = -152.0
LOG2_E = 1.4426950408889634
HG_CHUNK = 64
HG_CHUNKS_PER_STEP = 4
HG_SAFE_LOG_DECAY = -60.0
VMEM_LIMIT_BYTES = 56 * 1024 * 1024

_NN = (((1,), (0,)), ((), ()))
_NT = (((1,), (1,)), ((), ()))
_TN = (((0,), (0,)), ((), ()))


def _const_spec(shape):
    nd = len(shape)
    return pl.BlockSpec(shape, lambda *_: (0,) * nd, pipeline_mode=pl.Buffered(1))


def _rows_spec(tm, width):
    return pl.BlockSpec((tm, width), lambda i: (i, 0))


def _params(sem):
    return pltpu.CompilerParams(dimension_semantics=sem, vmem_limit_bytes=VMEM_LIMIT_BYTES)


def _row_parts(rows):
    if rows % ROW_PART or rows == ROW_PART:
        return [slice(0, rows)]
    return [slice(r, r + ROW_PART) for r in range(0, rows, ROW_PART)]


def _rms(x, g):
    return x * lax.rsqrt(jnp.mean(x * x, axis=-1, keepdims=True) + EPS) * g


def _ffn_chunk(h, wg, wu, wd):
    gate = jnp.dot(h, wg, preferred_element_type=F32)
    up = jnp.dot(h, wu, preferred_element_type=F32)
    act = (gate * jax.nn.sigmoid(gate) * up).astype(BF16)
    return jnp.dot(act, wd, preferred_element_type=F32)


def _half_ffn(x, g, wg_ref, wu_ref, wd_ref, acc_ref):
    h = _rms(x, g).astype(BF16)
    dff = wg_ref.shape[1]
    assert dff % FF_CHUNK == 0
    for c in range(dff // FF_CHUNK):
        cols = slice(c * FF_CHUNK, (c + 1) * FF_CHUNK)
        contrib = _ffn_chunk(h, wg_ref[:, cols], wu_ref[:, cols], wd_ref[cols, :])
        if c == 0:
            acc_ref[...] = contrib
        else:
            acc_ref[...] += contrib
    return x + 0.5 * acc_ref[...]


def _cast_ffn_chunk(h, w32_refs, w16_refs):
    wg, wu, wd = (r[...].astype(BF16) for r in w32_refs)
    for ref, wv in zip(w16_refs, (wg, wu, wd)):
        ref[...] = wv
    return _ffn_chunk(h, wg, wu, wd)


def _ff_chunk_specs(wg, wd, chunk_of_step):
    col = pl.BlockSpec((wg.shape[0], FF_CHUNK), lambda c: (0, chunk_of_step(c)))
    row = pl.BlockSpec((FF_CHUNK, wd.shape[1]), lambda c: (chunk_of_step(c), 0))
    return [col, col, row]


def _ffn_kernel(x_ref, n_ref, wg_ref, wu_ref, wd_ref, o_ref, acc_ref):
    o_ref[...] = _half_ffn(x_ref[...], n_ref[...], wg_ref, wu_ref, wd_ref, acc_ref)


def _ffn_cast_kernel(x_ref, n_ref, wg_ref, wu_ref, wd_ref, o_ref, wgb_ref, wub_ref, wdb_ref,
                     h_ref, acc_ref):
    c = pl.program_id(0)

    @pl.when(c == 0)
    def _():
        h_ref[...] = _rms(x_ref[...], n_ref[...]).astype(BF16)
        acc_ref[...] = jnp.zeros_like(acc_ref)

    acc_ref[...] += _cast_ffn_chunk(h_ref[...], (wg_ref, wu_ref, wd_ref),
                                    (wgb_ref, wub_ref, wdb_ref))

    @pl.when(c == pl.num_programs(0) - 1)
    def _():
        o_ref[...] = x_ref[...] + 0.5 * acc_ref[...]


def _ffn_cast_call(x, norm, wg, wu, wd):
    n, d = x.shape
    dff = wg.shape[1]
    assert dff % FF_CHUNK == 0
    chunk_specs = _ff_chunk_specs(wg, wd, lambda c: c)
    whole = pl.BlockSpec((n, d), lambda c: (0, 0))
    return pl.pallas_call(
        _ffn_cast_kernel,
        grid=(dff // FF_CHUNK,),
        in_specs=[whole, _const_spec(norm.shape)] + chunk_specs,
        out_specs=[whole] + chunk_specs,
        out_shape=[jax.ShapeDtypeStruct((n, d), F32)]
                  + [jax.ShapeDtypeStruct(a.shape, BF16) for a in (wg, wu, wd)],
        scratch_shapes=[pltpu.VMEM((n, d), BF16), pltpu.VMEM((n, d), F32)],
        compiler_params=_params(("arbitrary",)),
        name="ffn_cast",
    )(x, norm, wg, wu, wd)


def _ffn_call(x, norm, wg, wu, wd, tm):
    n, d = x.shape
    return pl.pallas_call(
        _ffn_kernel,
        grid=(n // tm,),
        in_specs=[_rows_spec(tm, d), _const_spec(norm.shape), _const_spec(wg.shape),
                  _const_spec(wu.shape), _const_spec(wd.shape)],
        out_specs=_rows_spec(tm, d),
        out_shape=jax.ShapeDtypeStruct((n, d), F32),
        scratch_shapes=[pltpu.VMEM((tm, d), F32)],
        compiler_params=_params(("parallel",)),
        name="ffn",
    )(x, norm, wg, wu, wd)


def _forget_lower_bound(lbl_ref):
    logits = lbl_ref[...]
    l0, l1 = logits[0:1], logits[1:2]
    m = jnp.maximum(l0, l1)
    e0, e1 = jnp.exp(l0 - m), jnp.exp(l1 - m)
    p0, p1 = e0 / (e0 + e1), e1 / (e0 + e1)
    return (p0 + p1) - p0


_N_MIX_GROUPS = 11


def _mix_group(j, p, r, lb, bg_ref, outs):
    q_ref, kb_ref, vb_ref, hq_ref, hk_ref, hi_ref, hlf_ref, sg_ref, gates_ref = outs
    if j == 0:
        q_ref[r, :] = (p * (SB_HEAD_DIM ** -0.5 * LOG2_E)).astype(BF16)
    elif j == 1:
        kb_ref[r, :] = p.astype(BF16)
    elif j == 2:
        vb_ref[r, :] = p.astype(BF16)
    elif j == 3:
        f = lb + (1.0 - lb) * jax.nn.sigmoid(p)
        hk_ref[r, :] = (1.0 - f).astype(hk_ref.dtype)
        hlf_ref[r, :] = jnp.log(f)
    elif j == 4:
        hi_ref[r, :] = p.astype(hi_ref.dtype)
    elif j == 5:
        hq_ref[r, :] = p.astype(hq_ref.dtype)
    elif j == 6:
        sg_ref[r, :] = (p * jax.nn.sigmoid(p)).astype(sg_ref.dtype)
    else:
        cols = slice((j - 7) * SB_WIDTH, (j - 6) * SB_WIDTH)
        gates_ref[r, cols] = jax.nn.sigmoid(p + bg_ref[:, cols]).astype(gates_ref.dtype)


def _mix_body(x_ref, n_ref, win_ref, bg_ref, lbl_ref, *outs):
    w = SB_WIDTH
    lb = _forget_lower_bound(lbl_ref)
    kv = ([], [])
    for r in _row_parts(x_ref.shape[0]):
        h = _rms(x_ref[r, :], n_ref[...]).astype(BF16)
        for j in range(_N_MIX_GROUPS):
            p = jnp.dot(h, win_ref[:, j * w:(j + 1) * w], preferred_element_type=F32)
            _mix_group(j, p, r, lb, bg_ref, outs)
            if j in (1, 2):
                kv[j - 1].append(p)
    return jnp.concatenate(kv[0], axis=0), jnp.concatenate(kv[1], axis=0)


def _mix_small_kernel(*refs, n_run, n_meta):
    x_ref, n_ref, win_ref, bg_ref, lbl_ref = refs[:5]
    outs, (k_ref, v_ref, kt_ref, vt_ref, winb_ref, h_ref) = refs[5:14], refs[14:]
    c = pl.program_id(0)

    @pl.when(c == 0)
    def _():
        h_ref[...] = _rms(x_ref[...], n_ref[...]).astype(BF16)

    wb = win_ref[...].astype(BF16)
    winb_ref[...] = wb
    p = jnp.dot(h_ref[...], wb, preferred_element_type=F32)
    lb = _forget_lower_bound(lbl_ref)
    for j in range(_N_MIX_GROUPS):
        @pl.when(c == j)
        def _(j=j):
            _mix_group(j, p, slice(None), lb, bg_ref, outs)
            for group, rows_ref, cols_ref in ((1, k_ref, kt_ref), (2, v_ref, vt_ref)):
                if j == group:
                    rows_ref[...] = p
                    cols_ref[...] = _pad_rows(p[n_run:n_run + n_meta], LANES).T


def _mix_frames_kernel(*refs, tm, n_meta, n_tiles):
    x_ref, consts, mk_ref, mv_ref = refs[0], refs[1:5], refs[5], refs[6]
    row_outs, (kt_ref, vt_ref, ck_ref, cv_ref) = refs[7:16], refs[16:]
    t = pl.program_id(1)
    shape = (SB_HEADS, SB_HEAD_DIM, tm)

    @pl.when(t == 0)
    def _():
        ck_ref[...] = mk_ref[...]
        cv_ref[...] = mv_ref[...]

    @pl.when(t < n_tiles)
    def _():
        k, v = _mix_body(x_ref.at[0], *consts, *[r.at[0] for r in row_outs])
        for rows, out_ref, carry_ref in ((k, kt_ref, ck_ref), (v, vt_ref, cv_ref)):
            cols = rows.T
            late = jnp.concatenate([carry_ref[:, :n_meta], cols[:, :tm - n_meta]], axis=1)
            out_ref[0, 0] = late.reshape(shape)
            carry_ref[:, :n_meta] = cols[:, tm - n_meta:]

    @pl.when(t == n_tiles)
    def _():
        for out_ref, carry_ref in ((kt_ref, ck_ref), (vt_ref, cv_ref)):
            tail = jnp.concatenate(
                [carry_ref[:, :n_meta], jnp.zeros((SB_WIDTH, tm - n_meta), F32)], axis=1)
            out_ref[0, 0] = tail.reshape(shape)


_MIX_WIDTHS = [(SB_WIDTH, BF16)] * 6 + [(SB_WIDTH, F32), (SB_WIDTH, BF16)]


def _mix_small_call(x, norm, w_in, b_gate, lb_logits, n_run, n_meta):
    n, d = x.shape
    assert w_in.shape[1] == _N_MIX_GROUPS * SB_WIDTH
    widths = _MIX_WIDTHS + [(2 * d, BF16), (SB_WIDTH, F32), (SB_WIDTH, F32)]
    meta_t = jax.ShapeDtypeStruct((SB_WIDTH, LANES), F32)
    whole = lambda shape: pl.BlockSpec(shape, lambda c: (0, 0))
    group = pl.BlockSpec((d, SB_WIDTH), lambda c: (0, c))
    return pl.pallas_call(
        functools.partial(_mix_small_kernel, n_run=n_run, n_meta=n_meta),
        grid=(_N_MIX_GROUPS,),
        in_specs=[whole((n, d)), _const_spec(norm.shape), group,
                  _const_spec(b_gate.shape), _const_spec(lb_logits.shape)],
        out_specs=[whole((n, wd)) for wd, _ in widths] + [whole(meta_t.shape)] * 2 + [group],
        out_shape=[jax.ShapeDtypeStruct((n, wd), dt) for wd, dt in widths] + [meta_t] * 2
                  + [jax.ShapeDtypeStruct(w_in.shape, BF16)],
        scratch_shapes=[pltpu.VMEM((n, d), BF16)],
        compiler_params=_params(("arbitrary",)),
        name="mix_small",
    )(x, norm, w_in, b_gate, lb_logits)


def _mix_frames_call(x, norm, w_in, b_gate, lb_logits, meta_kt, meta_vt, tm, n_meta):
    nb, seq, d = x.shape
    assert seq % tm == 0 and n_meta < tm
    n_tiles = seq // tm
    widths = _MIX_WIDTHS + [(2 * d, BF16)]
    rows_map = lambda b, t: (b, jnp.minimum(t, n_tiles - 1), 0)
    const2 = lambda shape: pl.BlockSpec(shape, lambda b, t: (0, 0), pipeline_mode=pl.Buffered(1))
    kv_shape = jax.ShapeDtypeStruct((1, nb, SB_HEADS, SB_HEAD_DIM, n_meta + seq), F32)
    kv_spec = pl.BlockSpec((1, 1, SB_HEADS, SB_HEAD_DIM, tm), lambda b, t: (0, b, 0, 0, t))
    carry = pltpu.VMEM((SB_WIDTH, LANES), F32)
    return pl.pallas_call(
        functools.partial(_mix_frames_kernel, tm=tm, n_meta=n_meta, n_tiles=n_tiles),
        grid=(nb, n_tiles + 1),
        in_specs=[pl.BlockSpec((1, tm, d), rows_map)]
                 + [const2(a.shape) for a in (norm, w_in, b_gate, lb_logits, meta_kt, meta_vt)],
        out_specs=[pl.BlockSpec((1, tm, wd), rows_map) for wd, _ in widths] + [kv_spec] * 2,
        out_shape=[jax.ShapeDtypeStruct((nb, seq, wd), dt) for wd, dt in widths] + [kv_shape] * 2,
        scratch_shapes=[carry, carry],
        compiler_params=_params(("arbitrary", "arbitrary")),
        name="mix_frames",
    )(x, norm, w_in, b_gate, lb_logits, meta_kt, meta_vt)


def _suffix_ones(width):
    r = lax.broadcasted_iota(jnp.int32, (width, width), 0)
    c = lax.broadcasted_iota(jnp.int32, (width, width), 1)
    return jnp.where(r >= c, 1.0, 0.0).astype(BF16)


def _key_before(tq, width, k0, q_first):
    row = lax.broadcasted_iota(jnp.int32, (tq, width), 0)
    col = lax.broadcasted_iota(jnp.int32, (tq, width), 1)
    return col + k0 < row + q_first


def _sb_sweep(q_ops, blocks, r_ref, acc_ref, token_minor=False):
    heads = range(SB_HEADS)
    k_dims, v_dims = (_NN, _NT) if token_minor else (_NT, _NN)
    tq = q_ops[0].shape[0]
    zs = [[lax.dot_general(q_ops[h], k_ops[h], k_dims, preferred_element_type=F32)
           for h in heads] for k_ops, _, _, _ in blocks]
    stays = []
    for z_blk, (_, _, _, mask) in zip(zs, blocks):
        stays.append([])
        for z in z_blk:
            nz = -z
            ls = jnp.minimum(nz, 0.0) - jnp.log2(1.0 + jnp.exp2(jnp.minimum(z, nz)))
            if mask is not None:
                ls = jnp.where(mask, ls, 0.0)
            stays[-1].append(ls.astype(BF16))
    cums = [[jnp.dot(st, ones, preferred_element_type=F32) for st in st_blk]
            for st_blk, (_, _, ones, _) in zip(stays, blocks)]
    ws = []
    for z_blk, cum_blk, (_, _, ones, mask) in zip(zs, cums, blocks):
        ws.append([])
        for h in heads:
            r = r_ref[h]
            cr = cum_blk[h] + jnp.concatenate([r] * (ones.shape[1] // LANES), axis=1)
            a = jnp.exp2(z_blk[h] + cr)
            if mask is not None:
                a = jnp.where(mask, a, 0.0)
            ws[-1].append(a.astype(BF16))
            r_ref[h] = jnp.broadcast_to(cr[:, 0:1], (tq, LANES))
    for w_blk, (_, v_ops, _, _) in zip(ws, blocks):
        for h in heads:
            acc_ref[h] += lax.dot_general(w_blk[h], v_ops[h], v_dims, preferred_element_type=F32)


def _sb_live(r_ref):
    m = r_ref[0]
    for h in range(1, SB_HEADS):
        m = jnp.maximum(m, r_ref[h])
    return (jnp.max(m) >= F32_EXP2_UNDERFLOW).astype(jnp.int32)


def _sweep_while(n_blocks, alive, r_ref, block, first=0):
    def body(carry):
        t, _ = carry
        block(t)
        return t + 1, _sb_live(r_ref)

    _, alive = lax.while_loop(lambda c: jnp.logical_and(c[0] < n_blocks, c[1] > 0),
                              body, (jnp.int32(first), alive))
    return alive


def _sb_kernel(q_ref, kn_ref, vn_ref, ko_ref, vo_ref, o_ref, qm_ref, acc_ref, r_ref, *,
               tq, n_old):
    kb, wb = KEY_BLOCK, WIDE_BLOCK
    blocks_per_step = q_ref.shape[1] // tq
    lane = lax.broadcasted_iota(jnp.int32, (tq, LANES), 1)
    ones = {width: _suffix_ones(width) for width in (kb, wb)}

    def block(k_ref, v_ref, start, width, mask):
        k_all = k_ref[0, pl.ds(start, width), :].astype(BF16)
        v_all = v_ref[0, pl.ds(start, width), :].astype(BF16)
        pair = lambda a: [a[:, (h // 2) * LANES:(h // 2 + 1) * LANES] for h in range(SB_HEADS)]
        return pair(k_all), pair(v_all), ones[width], mask

    def sweep(*blocks):
        _sb_sweep([qm_ref[h] for h in range(SB_HEADS)], blocks, r_ref, acc_ref)

    def query_block(j, carry):
        i = pl.program_id(1) * blocks_per_step + j
        rows = pl.ds(pl.multiple_of(j * tq, tq), tq)

        for h in range(SB_HEADS):
            p = h // 2
            qp = q_ref[0, rows, p * LANES:(p + 1) * LANES]
            keep = (lane < SB_HEAD_DIM) if h % 2 == 0 else (lane >= SB_HEAD_DIM)
            qm_ref[h] = jnp.where(keep, qp, jnp.zeros_like(qp))
            r_ref[h] = jnp.zeros((tq, LANES), F32)
            acc_ref[h] = jnp.zeros((tq, LANES), F32)

        q0 = i * tq
        below = q0 // kb
        d0 = pl.multiple_of(below * kb, kb)
        own = lambda: block(kn_ref, vn_ref, d0, kb, _key_before(tq, kb, d0, q0))
        paired = below >= 2

        @pl.when(paired)
        def _():
            sweep(own(), block(kn_ref, vn_ref, pl.multiple_of(d0 - wb, kb), wb, None))

        @pl.when(jnp.logical_not(paired))
        def _():
            sweep(own())

        top = jnp.where(paired, d0 - wb, d0)
        left = jnp.where(paired, below - 2, below)
        alive = _sweep_while(left // 2, _sb_live(r_ref), r_ref, lambda t: sweep(block(
            kn_ref, vn_ref, pl.multiple_of(top - (t + 1) * wb, kb), wb, None)))
        alive = _sweep_while(left % 2, alive, r_ref,
                             lambda t: sweep(block(kn_ref, vn_ref, 0, kb, None)))
        _sweep_while(1, alive, r_ref, lambda t: sweep(block(
            ko_ref, vo_ref, 0, kb, lax.broadcasted_iota(jnp.int32, (tq, kb), 1) < n_old)))

        for p in range(SB_HEADS // 2):
            o_ref[0, rows, p * LANES:(p + 1) * LANES] = jnp.where(
                lane < SB_HEAD_DIM, acc_ref[2 * p], acc_ref[2 * p + 1]).astype(o_ref.dtype)
        return carry

    lax.fori_loop(0, blocks_per_step, query_block, 0)


def _sbd_kernel(q_ref, kn_ref, vn_ref, ck_ref, cv_ref, o_ref, acc_ref, r_ref, *, tq, n_old):
    kb, wb, hd = KEY_BLOCK, WIDE_BLOCK, SB_HEAD_DIM
    heads = range(SB_HEADS)
    n_tail, n_full = n_old % kb, n_old // kb
    n_wide = (n_full * kb) // wb
    rest = n_full * kb - n_wide * wb
    per_head = lambda a: [a[:, h * hd:(h + 1) * hd] for h in heads]

    q_ops = per_head(q_ref[0])
    for h in heads:
        r_ref[h] = jnp.zeros((tq, LANES), F32)
        acc_ref[h] = jnp.zeros((tq, hd), F32)

    assert kn_ref.shape[1] == kb
    _sb_sweep(q_ops, [(per_head(kn_ref[0]), per_head(vn_ref[0]), _suffix_ones(kb),
                       _key_before(tq, kb, 0, 0))], r_ref, acc_ref)
    alive = _sb_live(r_ref)

    def cached(start, width, valid):
        def load(ref, h):
            cols = ref[0, 0, h, :, pl.ds(start, valid)].astype(BF16)
            if valid < width:
                cols = jnp.concatenate([cols, jnp.zeros((hd, width - valid), BF16)], axis=1)
            return cols

        mask = None if valid == width else (
            lax.broadcasted_iota(jnp.int32, (tq, width), 1) < valid)
        _sb_sweep(q_ops, [([load(ck_ref, h) for h in heads], [load(cv_ref, h) for h in heads],
                           _suffix_ones(width), mask)], r_ref, acc_ref, token_minor=True)

    if n_tail:
        alive = _sweep_while(1, alive, r_ref, lambda t: cached(n_full * kb, kb, n_tail))
    if n_wide:
        alive = _sweep_while(n_wide, alive, r_ref, lambda t: cached(
            pl.multiple_of(rest + (n_wide - 1 - t) * wb, kb), wb, wb))
    if rest:
        _sweep_while(1, alive, r_ref, lambda t: cached(0, rest, rest))

    o_ref[0] = jnp.concatenate([acc_ref[h] for h in heads], axis=1).astype(o_ref.dtype)


def _sb_call(q, k_new, v_new, k_old, v_old, n_old, tq):
    s, lq, w = q.shape
    ln = k_new.shape[1]
    lo = k_old.shape[1]
    assert lq % tq == 0 and ln % KEY_BLOCK == 0 and n_old <= KEY_BLOCK <= lo
    assert tq <= KEY_BLOCK and KEY_BLOCK % tq == 0
    step = tq * SB_BLOCKS_PER_STEP if lq % (tq * SB_BLOCKS_PER_STEP) == 0 else tq
    old_map = (lambda b, i: (b, 0, 0)) if k_old.shape[0] == s else (lambda b, i: (0, 0, 0))
    return pl.pallas_call(
        functools.partial(_sb_kernel, tq=tq, n_old=n_old),
        grid=(s, lq // step),
        in_specs=[pl.BlockSpec((1, step, w), lambda b, i: (b, i, 0)),
                  pl.BlockSpec((1, ln, w), lambda b, i: (b, 0, 0)),
                  pl.BlockSpec((1, ln, w), lambda b, i: (b, 0, 0)),
                  pl.BlockSpec((1, lo, w), old_map),
                  pl.BlockSpec((1, lo, w), old_map)],
        out_specs=pl.BlockSpec((1, step, w), lambda b, i: (b, i, 0)),
        out_shape=jax.ShapeDtypeStruct((s, lq, w), BF16),
        scratch_shapes=[pltpu.VMEM((SB_HEADS, tq, LANES), BF16),
                        pltpu.VMEM((SB_HEADS, tq, LANES), F32),
                        pltpu.VMEM((SB_HEADS, tq, LANES), F32)],
        compiler_params=_params(("parallel", "parallel")),
        name="sb",
    )(q, k_new, v_new, k_old, v_old)


def _sbd_call(q, k_new, v_new, cache_k, cache_v):
    s, tq, w = q.shape
    n_old = cache_k.shape[-1]
    assert tq <= KEY_BLOCK and k_new.shape[1] == KEY_BLOCK
    new_spec = pl.BlockSpec((1, KEY_BLOCK, w), lambda b: (b, 0, 0))
    cache_spec = pl.BlockSpec((1, 1) + cache_k.shape[2:], lambda b: (0, b, 0, 0, 0))
    return pl.pallas_call(
        functools.partial(_sbd_kernel, tq=tq, n_old=n_old),
        grid=(s,),
        in_specs=[pl.BlockSpec((1, tq, w), lambda b: (b, 0, 0)), new_spec, new_spec,
                  cache_spec, cache_spec],
        out_specs=pl.BlockSpec((1, tq, w), lambda b: (b, 0, 0)),
        out_shape=jax.ShapeDtypeStruct((s, tq, w), BF16),
        scratch_shapes=[pltpu.VMEM((SB_HEADS, tq, SB_HEAD_DIM), F32),
                        pltpu.VMEM((SB_HEADS, tq, LANES), F32)],
        compiler_params=_params(("parallel",)),
        name="sb_decode",
    )(q, k_new, v_new, cache_k, cache_v)


def _pad_rows(x, rows):
    if x.shape[0] == rows:
        return x
    return jnp.concatenate([x, jnp.zeros((rows - x.shape[0], x.shape[1]), x.dtype)], axis=0)


def _hg_kernel(q_ref, k_ref, i_ref, lf_ref, s0_ref, o_ref, sout_ref, st_ref, *, chunk):
    c = pl.program_id(0)
    hd = HG_HEAD_DIM
    ns, n_chunks = q_ref.shape[0], q_ref.shape[1] // chunk
    streams = range(ns)
    units = [(g, b) for g in range(n_chunks) for b in streams]
    rows = lambda g: slice(g * chunk, (g + 1) * chunk)
    head_cols = [slice(h * hd, (h + 1) * hd) for h in range(HG_HEADS)]

    @pl.when(c == 0)
    def _():
        for b in streams:
            st_ref[b] = s0_ref[b if s0_ref.shape[0] == ns else 0]

    trow = lax.broadcasted_iota(jnp.int32, (chunk, chunk), 0)
    tcol = lax.broadcasted_iota(jnp.int32, (chunk, chunk), 1)
    causal = trow >= tcol
    tri = jnp.where(causal, 1.0, 0.0).astype(BF16)

    terms = {}
    for g, b in units:
        lf = lf_ref[b, rows(g), :]
        hi = lf.astype(BF16)
        r1 = lf - hi.astype(F32)
        mid = r1.astype(BF16)
        terms[g, b] = (hi, mid, (r1 - mid.astype(F32)).astype(BF16))
    cum = {u: sum(jnp.dot(tri, t, preferred_element_type=F32) for t in terms[u]) for u in units}
    last = {u: cum[u][chunk - 1:chunk, :] for u in units}
    weakest = functools.reduce(jnp.minimum, last.values())
    safe = jnp.min(weakest) >= HG_SAFE_LOG_DECAY

    @pl.when(safe)
    def _():
        qd, kd, kl, iv, dl = {}, {}, {}, {}, {}
        for g, b in units:
            k = k_ref[b, rows(g), :]
            qd[g, b] = (q_ref[b, rows(g), :] * jnp.exp(cum[g, b])).astype(BF16)
            kd[g, b] = (k * jnp.exp(-cum[g, b])).astype(BF16)
            kl[g, b] = (k * jnp.exp(last[g, b] - cum[g, b])).astype(BF16)
            iv[g, b] = i_ref[b, rows(g), :].astype(BF16)
            dl[g, b] = jnp.exp(last[g, b])
        sc = {(u, h): lax.dot_general(qd[u][:, sl], kd[u][:, sl], _NT, preferred_element_type=F32)
              for u in units for h, sl in enumerate(head_cols)}
        grown = {(u, h): lax.dot_general(iv[u][:, sl], kl[u][:, sl], _TN,
                                         preferred_element_type=F32)
                 for u in units for h, sl in enumerate(head_cols)}
        within = {(u, h): jnp.dot(jnp.where(causal, sc[u, h], 0.0).astype(BF16), iv[u][:, sl],
                                  preferred_element_type=F32)
                  for u in units for h, sl in enumerate(head_cols)}
        for g in range(n_chunks):
            carried = {(b, h): lax.dot_general(qd[g, b][:, sl], st_ref[b, h].astype(BF16), _NT,
                                               preferred_element_type=F32)
                       for b in streams for h, sl in enumerate(head_cols)}
            for b in streams:
                for h, sl in enumerate(head_cols):
                    o_ref[b, rows(g), sl] = (within[(g, b), h] + carried[b, h]).astype(o_ref.dtype)
                    st_ref[b, h] = st_ref[b, h] * dl[g, b][:, sl] + grown[(g, b), h]

    @pl.when(jnp.logical_not(safe))
    def _():
        lane_t = lax.broadcasted_iota(jnp.int32, (hd, hd), 1)
        row_t = lax.broadcasted_iota(jnp.int32, (hd, hd), 0)

        def one_chunk(n, carry):
            b, r0 = n // n_chunks, pl.multiple_of((n % n_chunks) * chunk, chunk)
            for h, sl in enumerate(head_cols):
                load = lambda ref: ref[b, pl.ds(r0, chunk), sl].astype(F32)
                f_t = _pad_rows(jnp.exp(load(lf_ref)), hd).T
                k_t = _pad_rows(load(k_ref), hd).T
                q_t = _pad_rows(load(q_ref), hd).T
                i_p = _pad_rows(load(i_ref), hd)

                def token(t, sc, f_t=f_t, k_t=k_t, q_t=q_t, i_p=i_p):
                    s, o = sc
                    sel = lane_t == t
                    fc = jnp.sum(jnp.where(sel, f_t, 0.0), axis=1, keepdims=True)
                    kc = jnp.sum(jnp.where(sel, k_t, 0.0), axis=1, keepdims=True)
                    qc = jnp.sum(jnp.where(sel, q_t, 0.0), axis=1, keepdims=True)
                    i_row = jnp.sum(jnp.where(row_t == t, i_p, 0.0), axis=0, keepdims=True)
                    s = s * fc + kc * i_row
                    o = jnp.where(row_t == t, jnp.sum(qc * s, axis=0, keepdims=True), o)
                    return s, o

                s, o = lax.fori_loop(0, chunk, token,
                                     (st_ref[b, h].T, jnp.zeros((hd, hd), F32)))
                st_ref[b, h] = s.T
                o_ref[b, pl.ds(r0, chunk), sl] = o[:chunk].astype(o_ref.dtype)
            return carry

        lax.fori_loop(0, ns * n_chunks, one_chunk, 0)

    @pl.when(c == pl.num_programs(0) - 1)
    def _():
        sout_ref[...] = st_ref[...]


def _hg_call(q, k, iv, lf, s0t, chunk, chunks_per_step=1):
    s, l, w = q.shape
    rows = chunk * chunks_per_step
    assert l % rows == 0
    seq_spec = pl.BlockSpec((s, rows, w), lambda c: (0, c, 0))
    st_shape = (s, HG_HEADS, HG_HEAD_DIM, HG_HEAD_DIM)
    return pl.pallas_call(
        functools.partial(_hg_kernel, chunk=chunk),
        grid=(l // rows,),
        in_specs=[seq_spec, seq_spec, seq_spec, seq_spec,
                  pl.BlockSpec(s0t.shape, lambda c: (0, 0, 0, 0))],
        out_specs=[seq_spec, pl.BlockSpec(st_shape, lambda c: (0, 0, 0, 0))],
        out_shape=[jax.ShapeDtypeStruct((s, l, w), BF16), jax.ShapeDtypeStruct(st_shape, F32)],
        scratch_shapes=[pltpu.VMEM(st_shape, F32)],
        compiler_params=_params(("arbitrary",)),
        name="hgrn",
    )(q, k, iv, lf, s0t)


def _merge_branches(x, oa, ob, sg, gates, hn, wa, wb, wo):
    hd = HG_HEAD_DIM
    d = x.shape[1]
    heads = []
    for h in range(HG_HEADS):
        sl = slice(h * hd, (h + 1) * hd)
        heads.append(_rms(ob[:, sl].astype(F32), hn[:, sl]) * sg[:, sl].astype(F32))
    obn = jnp.concatenate(heads, axis=-1).astype(BF16)
    ma = jnp.dot(oa, wa, preferred_element_type=F32)
    mb = jnp.dot(obn, wb, preferred_element_type=F32)
    merged = gates[:, :d].astype(F32) * ma + gates[:, d:].astype(F32) * mb
    return x + jnp.dot(merged.astype(BF16), wo, preferred_element_type=F32)


def _post_kernel(x_ref, oa_ref, ob_ref, sg_ref, gates_ref, hn_ref, wa_ref, wb_ref, wo_ref,
                 n_ref, wg_ref, wu_ref, wd_ref, fn_ref, y_ref, acc_ref):
    x = _merge_branches(x_ref[...], oa_ref[...], ob_ref[...], sg_ref[...], gates_ref[...],
                        hn_ref[...], wa_ref[...], wb_ref[...], wo_ref[...])
    x = _half_ffn(x, n_ref[...], wg_ref, wu_ref, wd_ref, acc_ref)
    y_ref[...] = _rms(x, fn_ref[...])


def _post_cast_kernel(x_ref, oa_ref, ob_ref, sg_ref, gates_ref, hn_ref, wa_ref, wb_ref, wo_ref,
                      n_ref, wg_ref, wu_ref, wd_ref, fn_ref,
                      y_ref, wab_ref, wbb_ref, wob_ref, wgb_ref, wub_ref, wdb_ref,
                      x2_ref, h_ref, acc_ref):
    c = pl.program_id(0)

    @pl.when(c == 0)
    def _():
        wa, wb, wo = (r[...].astype(BF16) for r in (wa_ref, wb_ref, wo_ref))
        for ref, wv in zip((wab_ref, wbb_ref, wob_ref), (wa, wb, wo)):
            ref[...] = wv
        x2 = _merge_branches(x_ref[...], oa_ref[...], ob_ref[...], sg_ref[...], gates_ref[...],
                             hn_ref[...], wa, wb, wo)
        x2_ref[...] = x2
        h_ref[...] = _rms(x2, n_ref[...]).astype(BF16)
        acc_ref[...] = jnp.zeros_like(acc_ref)

    @pl.when(c > 0)
    def _():
        acc_ref[...] += _cast_ffn_chunk(h_ref[...], (wg_ref, wu_ref, wd_ref),
                                        (wgb_ref, wub_ref, wdb_ref))

    @pl.when(c == pl.num_programs(0) - 1)
    def _():
        y_ref[...] = _rms(x2_ref[...] + 0.5 * acc_ref[...], fn_ref[...])


def _post_cast_call(x, oa, ob, sg, gates, hn, wa, wb, wo, norm, wg, wu, wd, fnorm):
    n, d = x.shape
    dff = wg.shape[1]
    assert dff % FF_CHUNK == 0
    chunk_specs = _ff_chunk_specs(wg, wd, lambda c: jnp.maximum(c - 1, 0))
    whole = lambda a: pl.BlockSpec(a.shape, lambda c: (0,) * a.ndim)
    return pl.pallas_call(
        _post_cast_kernel,
        grid=(dff // FF_CHUNK + 1,),
        in_specs=[whole(a) for a in (x, oa, ob, sg, gates)]
                 + [_const_spec(a.shape) for a in (hn, wa, wb, wo, norm)] + chunk_specs
                 + [_const_spec(fnorm.shape)],
        out_specs=[whole(x)] + [whole(a) for a in (wa, wb, wo)] + chunk_specs,
        out_shape=[jax.ShapeDtypeStruct((n, d), F32)]
                  + [jax.ShapeDtypeStruct(a.shape, BF16) for a in (wa, wb, wo, wg, wu, wd)],
        scratch_shapes=[pltpu.VMEM((n, d), F32), pltpu.VMEM((n, d), BF16),
                        pltpu.VMEM((n, d), F32)],
        compiler_params=_params(("arbitrary",)),
        name="post_cast",
    )(x, oa, ob, sg, gates, hn, wa, wb, wo, norm, wg, wu, wd, fnorm)


def _post_call(x, oa, ob, sg, gates, hn, wa, wb, wo, norm, wg, wu, wd, fnorm, tm):
    n, d = x.shape
    consts = [hn, wa, wb, wo, norm, wg, wu, wd, fnorm]
    return pl.pallas_call(
        _post_kernel,
        grid=(n // tm,),
        in_specs=[_rows_spec(tm, d), _rows_spec(tm, oa.shape[1]), _rows_spec(tm, ob.shape[1]),
                  _rows_spec(tm, sg.shape[1]), _rows_spec(tm, gates.shape[1])]
                 + [_const_spec(a.shape) for a in consts],
        out_specs=_rows_spec(tm, d),
        out_shape=jax.ShapeDtypeStruct((n, d), F32),
        scratch_shapes=[pltpu.VMEM((tm, d), F32)],
        compiler_params=_params(("parallel",)),
        name="post",
    )(x, oa, ob, sg, gates, *consts)


def _row_tile(n, target):
    tm = min(n, target)
    assert n % tm == 0 and tm % 8 == 0
    return tm


def kernel(x_prompt, x_sample, cache_sb_k, cache_sb_v, state_hgrn, meta_tokens, ffn1_norm, ffn1_w_gate, ffn1_w_up, ffn1_w_down, mix_norm, w_in, b_gate, hg_lb_logits, hg_out_norm, w_branch_a, w_branch_b, w_out, ffn2_norm, ffn2_w_gate, ffn2_w_up, ffn2_w_down, final_norm):
    depth = w_in.shape[0]
    assert depth == 1, "single-layer step"
    nb, seq, d = x_prompt.shape
    ns, t_new, _ = x_sample.shape
    n_meta = meta_tokens.shape[0]
    row = lambda a: a.reshape(1, -1)

    mix_norms = (row(mix_norm[0]), row(b_gate[0]), hg_lb_logits)

    n_frames = nb * seq
    tm_p = _row_tile(seq, 512)
    n_run = ns * t_new
    small = jnp.concatenate([x_sample.reshape(n_run, d), meta_tokens.astype(F32)], axis=0)
    x1_s, *f1 = _ffn_cast_call(small, row(ffn1_norm[0]), ffn1_w_gate[0], ffn1_w_up[0],
                               ffn1_w_down[0])
    (q_s, kb_s, vb_s, hq_s, hk_s, hi_s, hlf_s, sg_s, gates_s, k_s, v_s, meta_kt, meta_vt,
     w_in_b) = _mix_small_call(x1_s, mix_norms[0], w_in[0], *mix_norms[1:], n_run, n_meta)
    run = lambda a: a[:n_run]
    meta = lambda a: a[n_run:]
    token_minor = lambda a: jnp.transpose(a, (0, 1, 3, 4, 2))
    token_major = lambda a: jnp.transpose(a, (0, 1, 4, 2, 3))
    seq3 = lambda a, s: a.reshape(s, -1, a.shape[-1])
    rows = lambda a: a.reshape(-1, a.shape[-1])
    pad_to = lambda a, n: jnp.pad(a, ((0, 0), (0, n - a.shape[1]), (0, 0)))
    hg = lambda sel, s, arrs: [seq3(sel(a), s) for a in arrs]

    oa_s = _sbd_call(seq3(run(q_s), ns), pad_to(seq3(run(kb_s), ns), KEY_BLOCK),
                     pad_to(seq3(run(vb_s), ns), KEY_BLOCK),
                     token_minor(cache_sb_k), token_minor(cache_sb_v))
    ob_s, st_s = _hg_call(*hg(run, ns, (hq_s, hk_s, hi_s, hlf_s)),
                          jnp.swapaxes(state_hgrn[0].astype(F32), -1, -2), t_new)
    post_norms = (row(ffn2_norm[0]), row(final_norm))
    y_s, wa, wb, wo, *f2 = _post_cast_call(
        run(x1_s), rows(oa_s), rows(ob_s), run(sg_s), run(gates_s), row(hg_out_norm[0]),
        w_branch_a[0], w_branch_b[0], w_out[0], post_norms[0],
        ffn2_w_gate[0], ffn2_w_up[0], ffn2_w_down[0], post_norms[1])

    zero_state = jnp.zeros((1, HG_HEADS, HG_HEAD_DIM, HG_HEAD_DIM), F32)
    _, st_meta = _hg_call(*hg(meta, 1, (hq_s, hk_s, hi_s, hlf_s)), zero_state, n_meta)
    x1_p = _ffn_call(x_prompt.reshape(n_frames, d), row(ffn1_norm[0]), *f1, tm_p)
    (q_p, kb_p, vb_p, hq_p, hk_p, hi_p, hlf_p, sg_p, gates_p, kt_full, vt_full) = (
        _mix_frames_call(x1_p.reshape(nb, seq, d), mix_norms[0], w_in_b, *mix_norms[1:],
                         meta_kt, meta_vt, tm_p, n_meta))
    oa_p = _sb_call(q_p, kb_p, vb_p,
                    pad_to(meta(kb_s)[None], KEY_BLOCK), pad_to(meta(vb_s)[None], KEY_BLOCK),
                    n_meta, KEY_BLOCK)
    ob_p, st_p = _hg_call(hq_p, hk_p, hi_p, hlf_p, st_meta, HG_CHUNK, HG_CHUNKS_PER_STEP)
    y_p = _post_call(x1_p, rows(oa_p), rows(ob_p), rows(sg_p), rows(gates_p),
                     row(hg_out_norm[0]), wa, wb, wo, post_norms[0], *f2, post_norms[1], tm_p)

    heads = lambda a: a.reshape(1, ns, t_new, SB_HEADS, SB_HEAD_DIM)
    return (y_p.reshape(nb, seq, d), y_s.reshape(ns, t_new, d),
            token_major(kt_full), token_major(vt_full),
            jnp.swapaxes(st_p, -1, -2)[None],
            heads(run(k_s)), heads(run(v_s)),
            jnp.swapaxes(st_s, -1, -2)[None])
```

```python
import functools

import jax
import jax.numpy as jnp
from jax import lax
from jax.experimental import pallas as pl
from jax.experimental.pallas import tpu as pltpu

F32 = jnp.float32
BF16 = jnp.bfloat16
EPS = 1e-6
SB_HEADS = 8
SB_HEAD_DIM = 64
SB_WIDTH = SB_HEADS * SB_HEAD_DIM
HG_HEADS = 4
HG_HEAD_DIM = 128
HG_WIDTH = HG_HEADS * HG_HEAD_DIM
LANES = 128
FF_CHUNK = 256
ROW_PART = 256
FFN_TILE, FFN_ROW_PART = 1024, 512
KEY_BLOCK = 128
WIDE_BLOCK = 256
SB_BLOCKS_PER_STEP = 4
F32_EXP2_UNDERFLOW = -152.0
LOG2_E = 1.4426950408889634
HG_CHUNK = 128
HG_CHUNKS_PER_STEP = 2
HG_SAFE_LOG_DECAY = -60.0
VMEM_LIMIT_BYTES = 56 * 1024 * 1024

_NN = (((1,), (0,)), ((), ()))
_NT = (((1,), (1,)), ((), ()))
_TN = (((0,), (0,)), ((), ()))


def _const_spec(shape):
    nd = len(shape)
    return pl.BlockSpec(shape, lambda *_: (0,) * nd, pipeline_mode=pl.Buffered(1))


def _rows_spec(tm, width):
    return pl.BlockSpec((tm, width), lambda i: (i, 0))


def _params(sem):
    return pltpu.CompilerParams(dimension_semantics=sem, vmem_limit_bytes=VMEM_LIMIT_BYTES)


def _row_parts(rows, part=ROW_PART):
    if rows % part or rows == part:
        return [slice(0, rows)]
    return [slice(r, r + part) for r in range(0, rows, part)]


def _rms(x, g):
    return x * lax.rsqrt(jnp.mean(x * x, axis=-1, keepdims=True) + EPS) * g


def _ffn_chunk(h, wg, wu, wd):
    gate = jnp.dot(h, wg, preferred_element_type=F32)
    up = jnp.dot(h, wu, preferred_element_type=F32)
    act = (gate * jax.nn.sigmoid(gate) * up).astype(BF16)
    return jnp.dot(act, wd, preferred_element_type=F32)


def _half_ffn(x, g, wg_ref, wu_ref, wd_ref, acc_ref):
    h = _rms(x, g).astype(BF16)
    dff = wg_ref.shape[1]
    assert dff % FF_CHUNK == 0
    for c in range(dff // FF_CHUNK):
        cols = slice(c * FF_CHUNK, (c + 1) * FF_CHUNK)
        contrib = _ffn_chunk(h, wg_ref[:, cols], wu_ref[:, cols], wd_ref[cols, :])
        if c == 0:
            acc_ref[...] = contrib
        else:
            acc_ref[...] += contrib
    return x + 0.5 * acc_ref[...]


def _cast_ffn_chunk(h, w32_refs, w16_refs):
    wg, wu, wd = (r[...].astype(BF16) for r in w32_refs)
    for ref, wv in zip(w16_refs, (wg, wu, wd)):
        ref[...] = wv
    return _ffn_chunk(h, wg, wu, wd)


def _ff_chunk_specs(wg, wd, chunk_of_step):
    col = pl.BlockSpec((wg.shape[0], FF_CHUNK), lambda c: (0, chunk_of_step(c)))
    row = pl.BlockSpec((FF_CHUNK, wd.shape[1]), lambda c: (chunk_of_step(c), 0))
    return [col, col, row]


def _ffn_kernel(x_ref, n_ref, wg_ref, wu_ref, wd_ref, o_ref, acc_ref):
    for r in _row_parts(x_ref.shape[0], FFN_ROW_PART):
        o_ref[r, :] = _half_ffn(x_ref[r, :], n_ref[...], wg_ref, wu_ref, wd_ref, acc_ref.at[r, :])


def _ffn_cast_kernel(x_ref, n_ref, wg_ref, wu_ref, wd_ref, o_ref, wgb_ref, wub_ref, wdb_ref,
                     h_ref, acc_ref):
    c = pl.program_id(0)

    @pl.when(c == 0)
    def _():
        h_ref[...] = _rms(x_ref[...], n_ref[...]).astype(BF16)
        acc_ref[...] = jnp.zeros_like(acc_ref)

    acc_ref[...] += _cast_ffn_chunk(h_ref[...], (wg_ref, wu_ref, wd_ref),
                                    (wgb_ref, wub_ref, wdb_ref))

    @pl.when(c == pl.num_programs(0) - 1)
    def _():
        o_ref[...] = x_ref[...] + 0.5 * acc_ref[...]


def _ffn_cast_call(x, norm, wg, wu, wd):
    n, d = x.shape
    dff = wg.shape[1]
    assert dff % FF_CHUNK == 0
    chunk_specs = _ff_chunk_specs(wg, wd, lambda c: c)
    whole = pl.BlockSpec((n, d), lambda c: (0, 0))
    return pl.pallas_call(
        _ffn_cast_kernel,
        grid=(dff // FF_CHUNK,),
        in_specs=[whole, _const_spec(norm.shape)] + chunk_specs,
        out_specs=[whole] + chunk_specs,
        out_shape=[jax.ShapeDtypeStruct((n, d), F32)]
                  + [jax.ShapeDtypeStruct(a.shape, BF16) for a in (wg, wu, wd)],
        scratch_shapes=[pltpu.VMEM((n, d), BF16), pltpu.VMEM((n, d), F32)],
        compiler_params=_params(("arbitrary",)),
        name="ffn_cast",
    )(x, norm, wg, wu, wd)


def _ffn_call(x, norm, wg, wu, wd, tm):
    n, d = x.shape
    return pl.pallas_call(
        _ffn_kernel,
        grid=(n // tm,),
        in_specs=[_rows_spec(tm, d), _const_spec(norm.shape), _const_spec(wg.shape),
                  _const_spec(wu.shape), _const_spec(wd.shape)],
        out_specs=_rows_spec(tm, d),
        out_shape=jax.ShapeDtypeStruct((n, d), F32),
        scratch_shapes=[pltpu.VMEM((tm, d), F32)],
        compiler_params=_params(("parallel",)),
        name="ffn",
    )(x, norm, wg, wu, wd)


def _forget_lower_bound(lbl_ref):
    logits = lbl_ref[...]
    l0, l1 = logits[0:1], logits[1:2]
    m = jnp.maximum(l0, l1)
    e0, e1 = jnp.exp(l0 - m), jnp.exp(l1 - m)
    p0, p1 = e0 / (e0 + e1), e1 / (e0 + e1)
    return (p0 + p1) - p0


_N_MIX_GROUPS = 11
_MIX_GROUP_ORDER = (7, 8, 9, 10, 3, 6, 0, 1, 2, 4, 5)


def _mix_group(j, p, r, lb, bg_ref, outs):
    q_ref, kb_ref, vb_ref, hq_ref, hk_ref, hi_ref, hlf_ref, sg_ref, gates_ref = outs
    if j == 0:
        q_ref[r, :] = (p * (SB_HEAD_DIM ** -0.5 * LOG2_E)).astype(BF16)
    elif j == 1:
        kb_ref[r, :] = p.astype(BF16)
    elif j == 2:
        vb_ref[r, :] = p.astype(BF16)
    elif j == 3:
        f = lb + (1.0 - lb) * jax.nn.sigmoid(p)
        hk_ref[r, :] = (1.0 - f).astype(hk_ref.dtype)
        hlf_ref[r, :] = jnp.log(f)
    elif j == 4:
        hi_ref[r, :] = p.astype(hi_ref.dtype)
    elif j == 5:
        hq_ref[r, :] = p.astype(hq_ref.dtype)
    elif j == 6:
        sg_ref[r, :] = (p * jax.nn.sigmoid(p)).astype(sg_ref.dtype)
    else:
        cols = slice((j - 7) * SB_WIDTH, (j - 6) * SB_WIDTH)
        gates_ref[r, cols] = jax.nn.sigmoid(p + bg_ref[:, cols]).astype(gates_ref.dtype)


def _mix_body(x_ref, n_ref, win_ref, bg_ref, lbl_ref, *outs):
    w = SB_WIDTH
    lb = _forget_lower_bound(lbl_ref)
    kv = ([], [])
    for r in _row_parts(x_ref.shape[0]):
        h = _rms(x_ref[r, :], n_ref[...]).astype(BF16)
        for j in _MIX_GROUP_ORDER:
            p = jnp.dot(h, win_ref[:, j * w:(j + 1) * w], preferred_element_type=F32)
            _mix_group(j, p, r, lb, bg_ref, outs)
            if j in (1, 2):
                kv[j - 1].append(p)
    return jnp.concatenate(kv[0], axis=0), jnp.concatenate(kv[1], axis=0)


def _mix_small_kernel(*refs, n_run, n_meta):
    x_ref, n_ref, win_ref, bg_ref, lbl_ref = refs[:5]
    outs, (k_ref, v_ref, kt_ref, vt_ref, winb_ref, h_ref) = refs[5:14], refs[14:]
    c = pl.program_id(0)

    @pl.when(c == 0)
    def _():
        h_ref[...] = _rms(x_ref[...], n_ref[...]).astype(BF16)

    wb = win_ref[...].astype(BF16)
    winb_ref[...] = wb
    p = jnp.dot(h_ref[...], wb, preferred_element_type=F32)
    lb = _forget_lower_bound(lbl_ref)
    for j in range(_N_MIX_GROUPS):
        @pl.when(c == j)
        def _(j=j):
            _mix_group(j, p, slice(None), lb, bg_ref, outs)
            for group, rows_ref, cols_ref in ((1, k_ref, kt_ref), (2, v_ref, vt_ref)):
                if j == group:
                    rows_ref[...] = p
                    cols_ref[...] = _pad_rows(p[n_run:n_run + n_meta], LANES).T


def _mix_frames_kernel(*refs, tm, n_meta, n_tiles):
    x_ref, consts, mk_ref, mv_ref = refs[0], refs[1:5], refs[5], refs[6]
    row_outs, (kt_ref, vt_ref, ck_ref, cv_ref) = refs[7:16], refs[16:]
    t = pl.program_id(1)
    shape = (SB_HEADS, SB_HEAD_DIM, tm)

    @pl.when(t == 0)
    def _():
        ck_ref[...] = mk_ref[...]
        cv_ref[...] = mv_ref[...]

    @pl.when(t < n_tiles)
    def _():
        k, v = _mix_body(x_ref.at[0], *consts, *[r.at[0] for r in row_outs])
        for rows, out_ref, carry_ref in ((k, kt_ref, ck_ref), (v, vt_ref, cv_ref)):
            cols = rows.T
            late = jnp.concatenate([carry_ref[:, :n_meta], cols[:, :tm - n_meta]], axis=1)
            out_ref[0, 0] = late.reshape(shape)
            carry_ref[:, :n_meta] = cols[:, tm - n_meta:]

    @pl.when(t == n_tiles)
    def _():
        for out_ref, carry_ref in ((kt_ref, ck_ref), (vt_ref, cv_ref)):
            tail = jnp.concatenate(
                [carry_ref[:, :n_meta], jnp.zeros((SB_WIDTH, tm - n_meta), F32)], axis=1)
            out_ref[0, 0] = tail.reshape(shape)


_MIX_WIDTHS = [(SB_WIDTH, BF16)] * 6 + [(SB_WIDTH, F32), (SB_WIDTH, BF16)]


def _mix_small_call(x, norm, w_in, b_gate, lb_logits, n_run, n_meta):
    n, d = x.shape
    assert w_in.shape[1] == _N_MIX_GROUPS * SB_WIDTH
    widths = _MIX_WIDTHS + [(2 * d, BF16), (SB_WIDTH, F32), (SB_WIDTH, F32)]
    meta_t = jax.ShapeDtypeStruct((SB_WIDTH, LANES), F32)
    whole = lambda shape: pl.BlockSpec(shape, lambda c: (0, 0))
    group = pl.BlockSpec((d, SB_WIDTH), lambda c: (0, c))
    return pl.pallas_call(
        functools.partial(_mix_small_kernel, n_run=n_run, n_meta=n_meta),
        grid=(_N_MIX_GROUPS,),
        in_specs=[whole((n, d)), _const_spec(norm.shape), group,
                  _const_spec(b_gate.shape), _const_spec(lb_logits.shape)],
        out_specs=[whole((n, wd)) for wd, _ in widths] + [whole(meta_t.shape)] * 2 + [group],
        out_shape=[jax.ShapeDtypeStruct((n, wd), dt) for wd, dt in widths] + [meta_t] * 2
                  + [jax.ShapeDtypeStruct(w_in.shape, BF16)],
        scratch_shapes=[pltpu.VMEM((n, d), BF16)],
        compiler_params=_params(("arbitrary",)),
        name="mix_small",
    )(x, norm, w_in, b_gate, lb_logits)


def _mix_frames_call(x, norm, w_in, b_gate, lb_logits, meta_kt, meta_vt, tm, n_meta):
    nb, seq, d = x.shape
    assert seq % tm == 0 and n_meta < tm
    n_tiles = seq // tm
    widths = _MIX_WIDTHS + [(2 * d, BF16)]
    rows_map = lambda b, t: (b, jnp.minimum(t, n_tiles - 1), 0)
    const2 = lambda shape: pl.BlockSpec(shape, lambda b, t: (0, 0), pipeline_mode=pl.Buffered(1))
    kv_shape = jax.ShapeDtypeStruct((1, nb, SB_HEADS, SB_HEAD_DIM, n_meta + seq), F32)
    kv_spec = pl.BlockSpec((1, 1, SB_HEADS, SB_HEAD_DIM, tm), lambda b, t: (0, b, 0, 0, t))
    carry = pltpu.VMEM((SB_WIDTH, LANES), F32)
    return pl.pallas_call(
        functools.partial(_mix_frames_kernel, tm=tm, n_meta=n_meta, n_tiles=n_tiles),
        grid=(nb, n_tiles + 1),
        in_specs=[pl.BlockSpec((1, tm, d), rows_map)]
                 + [const2(a.shape) for a in (norm, w_in, b_gate, lb_logits, meta_kt, meta_vt)],
        out_specs=[pl.BlockSpec((1, tm, wd), rows_map) for wd, _ in widths] + [kv_spec] * 2,
        out_shape=[jax.ShapeDtypeStruct((nb, seq, wd), dt) for wd, dt in widths] + [kv_shape] * 2,
        scratch_shapes=[carry, carry],
        compiler_params=_params(("arbitrary", "arbitrary")),
        name="mix_frames",
    )(x, norm, w_in, b_gate, lb_logits, meta_kt, meta_vt)


def _suffix_ones(width):
    r = lax.broadcasted_iota(jnp.int32, (width, width), 0)
    c = lax.broadcasted_iota(jnp.int32, (width, width), 1)
    return jnp.where(r >= c, 1.0, 0.0).astype(BF16)


def _key_before(tq, width, k0, q_first):
    row = lax.broadcasted_iota(jnp.int32, (tq, width), 0)
    col = lax.broadcasted_iota(jnp.int32, (tq, width), 1)
    return col + k0 < row + q_first


def _sb_sweep(q_ops, blocks, r_ref, acc_ref, token_minor=False):
    heads = range(SB_HEADS)
    k_dims, v_dims = (_NN, _NT) if token_minor else (_NT, _NN)
    tq = q_ops[0].shape[0]
    zs = [[lax.dot_general(q_ops[h], k_ops[h], k_dims, preferred_element_type=F32)
           for h in heads] for k_ops, _, _, _ in blocks]
    stays = []
    for z_blk, (_, _, _, mask) in zip(zs, blocks):
        stays.append([])
        for z in z_blk:
            nz = -z
            ls = jnp.minimum(nz, 0.0) - jnp.log2(1.0 + jnp.exp2(jnp.minimum(z, nz)))
            if mask is not None:
                ls = jnp.where(mask, ls, 0.0)
            stays[-1].append(ls.astype(BF16))
    cums = [[jnp.dot(st, ones, preferred_element_type=F32) for st in st_blk]
            for st_blk, (_, _, ones, _) in zip(stays, blocks)]
    ws = []
    for z_blk, cum_blk, (_, _, ones, mask) in zip(zs, cums, blocks):
        ws.append([])
        for h in heads:
            r = r_ref[h]
            cr = cum_blk[h] + jnp.concatenate([r] * (ones.shape[1] // LANES), axis=1)
            a = jnp.exp2(z_blk[h] + cr)
            if mask is not None:
                a = jnp.where(mask, a, 0.0)
            ws[-1].append(a.astype(BF16))
            r_ref[h] = jnp.broadcast_to(cr[:, 0:1], (tq, LANES))
    for w_blk, (_, v_ops, _, _) in zip(ws, blocks):
        for h in heads:
            acc_ref[h] += lax.dot_general(w_blk[h], v_ops[h], v_dims, preferred_element_type=F32)


def _sb_live(r_ref):
    m = r_ref[0]
    for h in range(1, SB_HEADS):
        m = jnp.maximum(m, r_ref[h])
    return (jnp.max(m) >= F32_EXP2_UNDERFLOW).astype(jnp.int32)


def _sweep_while(n_blocks, alive, r_ref, block, first=0):
    def body(carry):
        t, _ = carry
        block(t)
        return t + 1, _sb_live(r_ref)

    _, alive = lax.while_loop(lambda c: jnp.logical_and(c[0] < n_blocks, c[1] > 0),
                              body, (jnp.int32(first), alive))
    return alive


def _sb_kernel(q_ref, kn_ref, vn_ref, ko_ref, vo_ref, o_ref, qm_ref, acc_ref, r_ref, *,
               tq, n_old):
    kb, wb = KEY_BLOCK, WIDE_BLOCK
    blocks_per_step = q_ref.shape[1] // tq
    lane = lax.broadcasted_iota(jnp.int32, (tq, LANES), 1)
    ones = {width: _suffix_ones(width) for width in (kb, wb)}

    def block(k_ref, v_ref, start, width, mask):
        k_all = k_ref[0, pl.ds(start, width), :].astype(BF16)
        v_all = v_ref[0, pl.ds(start, width), :].astype(BF16)
        pair = lambda a: [a[:, (h // 2) * LANES:(h // 2 + 1) * LANES] for h in range(SB_HEADS)]
        return pair(k_all), pair(v_all), ones[width], mask

    def sweep(*blocks):
        _sb_sweep([qm_ref[h] for h in range(SB_HEADS)], blocks, r_ref, acc_ref)

    def query_block(j, carry):
        i = pl.program_id(1) * blocks_per_step + j
        rows = pl.ds(pl.multiple_of(j * tq, tq), tq)

        for h in range(SB_HEADS):
            p = h // 2
            qp = q_ref[0, rows, p * LANES:(p + 1) * LANES]
            keep = (lane < SB_HEAD_DIM) if h % 2 == 0 else (lane >= SB_HEAD_DIM)
            qm_ref[h] = jnp.where(keep, qp, jnp.zeros_like(qp))
            r_ref[h] = jnp.zeros((tq, LANES), F32)
            acc_ref[h] = jnp.zeros((tq, LANES), F32)

        q0 = i * tq
        below = q0 // kb
        d0 = pl.multiple_of(below * kb, kb)
        own = lambda: block(kn_ref, vn_ref, d0, kb, _key_before(tq, kb, d0, q0))
        paired = below >= 2

        @pl.when(paired)
        def _():
            sweep(own(), block(kn_ref, vn_ref, pl.multiple_of(d0 - wb, kb), wb, None))

        @pl.when(jnp.logical_not(paired))
        def _():
            sweep(own())

        top = jnp.where(paired, d0 - wb, d0)
        left = jnp.where(paired, below - 2, below)
        alive = _sweep_while(left // 2, _sb_live(r_ref), r_ref, lambda t: sweep(block(
            kn_ref, vn_ref, pl.multiple_of(top - (t + 1) * wb, kb), wb, None)))
        alive = _sweep_while(left % 2, alive, r_ref,
                             lambda t: sweep(block(kn_ref, vn_ref, 0, kb, None)))
        _sweep_while(1, alive, r_ref, lambda t: sweep(block(
            ko_ref, vo_ref, 0, kb, lax.broadcasted_iota(jnp.int32, (tq, kb), 1) < n_old)))

        for p in range(SB_HEADS // 2):
            o_ref[0, rows, p * LANES:(p + 1) * LANES] = jnp.where(
                lane < SB_HEAD_DIM, acc_ref[2 * p], acc_ref[2 * p + 1]).astype(o_ref.dtype)
        return carry

    lax.fori_loop(0, blocks_per_step, query_block, 0)


def _sbd_kernel(q_ref, kn_ref, vn_ref, ck_ref, cv_ref, o_ref, acc_ref, r_ref, *, tq, n_old):
    kb, wb, hd = KEY_BLOCK, WIDE_BLOCK, SB_HEAD_DIM
    heads = range(SB_HEADS)
    n_tail, n_full = n_old % kb, n_old // kb
    n_wide = (n_full * kb) // wb
    rest = n_full * kb - n_wide * wb
    per_head = lambda a: [a[:, h * hd:(h + 1) * hd] for h in heads]

    q_ops = per_head(q_ref[0])
    for h in heads:
        r_ref[h] = jnp.zeros((tq, LANES), F32)
        acc_ref[h] = jnp.zeros((tq, hd), F32)

    assert kn_ref.shape[1] == kb
    _sb_sweep(q_ops, [(per_head(kn_ref[0]), per_head(vn_ref[0]), _suffix_ones(kb),
                       _key_before(tq, kb, 0, 0))], r_ref, acc_ref)
    alive = _sb_live(r_ref)

    def cached(start, width, valid):
        def load(ref, h):
            cols = ref[0, 0, h, :, pl.ds(start, valid)].astype(BF16)
            if valid < width:
                cols = jnp.concatenate([cols, jnp.zeros((hd, width - valid), BF16)], axis=1)
            return cols

        mask = None if valid == width else (
            lax.broadcasted_iota(jnp.int32, (tq, width), 1) < valid)
        _sb_sweep(q_ops, [([load(ck_ref, h) for h in heads], [load(cv_ref, h) for h in heads],
                           _suffix_ones(width), mask)], r_ref, acc_ref, token_minor=True)

    if n_tail:
        alive = _sweep_while(1, alive, r_ref, lambda t: cached(n_full * kb, kb, n_tail))
    if n_wide:
        alive = _sweep_while(n_wide, alive, r_ref, lambda t: cached(
            pl.multiple_of(rest + (n_wide - 1 - t) * wb, kb), wb, wb))
    if rest:
        _sweep_while(1, alive, r_ref, lambda t: cached(0, rest, rest))

    o_ref[0] = jnp.concatenate([acc_ref[h] for h in heads], axis=1).astype(o_ref.dtype)


def _sb_call(q, k_new, v_new, k_old, v_old, n_old, tq):
    s, lq, w = q.shape
    ln = k_new.shape[1]
    lo = k_old.shape[1]
    assert lq % tq == 0 and ln % KEY_BLOCK == 0 and n_old <= KEY_BLOCK <= lo
    assert tq <= KEY_BLOCK and KEY_BLOCK % tq == 0
    step = tq * SB_BLOCKS_PER_STEP if lq % (tq * SB_BLOCKS_PER_STEP) == 0 else tq
    old_map = (lambda b, i: (b, 0, 0)) if k_old.shape[0] == s else (lambda b, i: (0, 0, 0))
    return pl.pallas_call(
        functools.partial(_sb_kernel, tq=tq, n_old=n_old),
        grid=(s, lq // step),
        in_specs=[pl.BlockSpec((1, step, w), lambda b, i: (b, i, 0)),
                  pl.BlockSpec((1, ln, w), lambda b, i: (b, 0, 0)),
                  pl.BlockSpec((1, ln, w), lambda b, i: (b, 0, 0)),
                  pl.BlockSpec((1, lo, w), old_map),
                  pl.BlockSpec((1, lo, w), old_map)],
        out_specs=pl.BlockSpec((1, step, w), lambda b, i: (b, i, 0)),
        out_shape=jax.ShapeDtypeStruct((s, lq, w), BF16),
        scratch_shapes=[pltpu.VMEM((SB_HEADS, tq, LANES), BF16),
                        pltpu.VMEM((SB_HEADS, tq, LANES), F32),
                        pltpu.VMEM((SB_HEADS, tq, LANES), F32)],
        compiler_params=_params(("parallel", "parallel")),
        name="sb",
    )(q, k_new, v_new, k_old, v_old)


def _sbd_call(q, k_new, v_new, cache_k, cache_v):
    s, tq, w = q.shape
    n_old = cache_k.shape[-1]
    assert tq <= KEY_BLOCK and k_new.shape[1] == KEY_BLOCK
    new_spec = pl.BlockSpec((1, KEY_BLOCK, w), lambda b: (b, 0, 0))
    cache_spec = pl.BlockSpec((1, 1) + cache_k.shape[2:], lambda b: (0, b, 0, 0, 0))
    return pl.pallas_call(
        functools.partial(_sbd_kernel, tq=tq, n_old=n_old),
        grid=(s,),
        in_specs=[pl.BlockSpec((1, tq, w), lambda b: (b, 0, 0)), new_spec, new_spec,
                  cache_spec, cache_spec],
        out_specs=pl.BlockSpec((1, tq, w), lambda b: (b, 0, 0)),
        out_shape=jax.ShapeDtypeStruct((s, tq, w), BF16),
        scratch_shapes=[pltpu.VMEM((SB_HEADS, tq, SB_HEAD_DIM), F32),
                        pltpu.VMEM((SB_HEADS, tq, LANES), F32)],
        compiler_params=_params(("parallel",)),
        name="sb_decode",
    )(q, k_new, v_new, cache_k, cache_v)


def _pad_rows(x, rows):
    if x.shape[0] == rows:
        return x
    return jnp.concatenate([x, jnp.zeros((rows - x.shape[0], x.shape[1]), x.dtype)], axis=0)


def _hg_kernel(q_ref, k_ref, i_ref, lf_ref, s0_ref, o_ref, sout_ref, st_ref, *, chunk):
    c = pl.program_id(0)
    hd = HG_HEAD_DIM
    ns, n_chunks = q_ref.shape[0], q_ref.shape[1] // chunk
    streams = range(ns)
    units = [(g, b) for g in range(n_chunks) for b in streams]
    rows = lambda g: slice(g * chunk, (g + 1) * chunk)
    head_cols = [slice(h * hd, (h + 1) * hd) for h in range(HG_HEADS)]

    @pl.when(c == 0)
    def _():
        for b in streams:
            st_ref[b] = s0_ref[b if s0_ref.shape[0] == ns else 0]

    trow = lax.broadcasted_iota(jnp.int32, (chunk, chunk), 0)
    tcol = lax.broadcasted_iota(jnp.int32, (chunk, chunk), 1)
    causal = trow >= tcol

    token = lax.broadcasted_iota(jnp.int32, (chunk, lf_ref.shape[2]), 0)
    cum = {}
    for g, b in units:
        part = lf_ref[b, rows(g), :]
        shift = 1
        while shift < chunk:
            part = part + jnp.where(token >= shift, pltpu.roll(part, shift, axis=0), 0.0)
            shift *= 2
        cum[g, b] = part
    last = {u: cum[u][chunk - 1:chunk, :] for u in units}
    weakest = functools.reduce(jnp.minimum, last.values())
    safe = jnp.min(weakest) >= HG_SAFE_LOG_DECAY

    @pl.when(safe)
    def _():
        qd, kd, kl, iv, dl = {}, {}, {}, {}, {}
        for g, b in units:
            k = k_ref[b, rows(g), :]
            qd[g, b] = (q_ref[b, rows(g), :] * jnp.exp(cum[g, b])).astype(BF16)
            kd[g, b] = (k * jnp.exp(-cum[g, b])).astype(BF16)
            kl[g, b] = (k * jnp.exp(last[g, b] - cum[g, b])).astype(BF16)
            iv[g, b] = i_ref[b, rows(g), :].astype(BF16)
            dl[g, b] = jnp.exp(last[g, b])
        sc = {(u, h): lax.dot_general(qd[u][:, sl], kd[u][:, sl], _NT, preferred_element_type=F32)
              for u in units for h, sl in enumerate(head_cols)}
        grown = {(u, h): lax.dot_general(iv[u][:, sl], kl[u][:, sl], _TN,
                                         preferred_element_type=F32)
                 for u in units for h, sl in enumerate(head_cols)}
        within = {(u, h): jnp.dot(jnp.where(causal, sc[u, h], 0.0).astype(BF16), iv[u][:, sl],
                                  preferred_element_type=F32)
                  for u in units for h, sl in enumerate(head_cols)}
        for g in range(n_chunks):
            carried = {(b, h): lax.dot_general(qd[g, b][:, sl], st_ref[b, h].astype(BF16), _NT,
                                               preferred_element_type=F32)
                       for b in streams for h, sl in enumerate(head_cols)}
            for b in streams:
                for h, sl in enumerate(head_cols):
                    o_ref[b, rows(g), sl] = (within[(g, b), h] + carried[b, h]).astype(o_ref.dtype)
                    st_ref[b, h] = st_ref[b, h] * dl[g, b][:, sl] + grown[(g, b), h]

    @pl.when(jnp.logical_not(safe))
    def _():
        lane_t = lax.broadcasted_iota(jnp.int32, (hd, hd), 1)
        row_t = lax.broadcasted_iota(jnp.int32, (hd, hd), 0)

        def one_chunk(n, carry):
            b, r0 = n // n_chunks, pl.multiple_of((n % n_chunks) * chunk, chunk)
            for h, sl in enumerate(head_cols):
                load = lambda ref: ref[b, pl.ds(r0, chunk), sl].astype(F32)
                f_t = _pad_rows(jnp.exp(load(lf_ref)), hd).T
                k_t = _pad_rows(load(k_ref), hd).T
                q_t = _pad_rows(load(q_ref), hd).T
                i_p = _pad_rows(load(i_ref), hd)

                def token(t, sc, f_t=f_t, k_t=k_t, q_t=q_t, i_p=i_p):
                    s, o = sc
                    sel = lane_t == t
                    fc = jnp.sum(jnp.where(sel, f_t, 0.0), axis=1, keepdims=True)
                    kc = jnp.sum(jnp.where(sel, k_t, 0.0), axis=1, keepdims=True)
                    qc = jnp.sum(jnp.where(sel, q_t, 0.0), axis=1, keepdims=True)
                    i_row = jnp.sum(jnp.where(row_t == t, i_p, 0.0), axis=0, keepdims=True)
                    s = s * fc + kc * i_row
                    o = jnp.where(row_t == t, jnp.sum(qc * s, axis=0, keepdims=True), o)
                    return s, o

                s, o = lax.fori_loop(0, chunk, token,
                                     (st_ref[b, h].T, jnp.zeros((hd, hd), F32)))
                st_ref[b, h] = s.T
                o_ref[b, pl.ds(r0, chunk), sl] = o[:chunk].astype(o_ref.dtype)
            return carry

        lax.fori_loop(0, ns * n_chunks, one_chunk, 0)

    @pl.when(c == pl.num_programs(0) - 1)
    def _():
        sout_ref[...] = st_ref[...]


def _hg_call(q, k, iv, lf, s0t, chunk, chunks_per_step=1):
    s, l, w = q.shape
    rows = chunk * chunks_per_step
    assert l % rows == 0
    seq_spec = pl.BlockSpec((s, rows, w), lambda c: (0, c, 0))
    st_shape = (s, HG_HEADS, HG_HEAD_DIM, HG_HEAD_DIM)
    return pl.pallas_call(
        functools.partial(_hg_kernel, chunk=chunk),
        grid=(l // rows,),
        in_specs=[seq_spec, seq_spec, seq_spec, seq_spec,
                  pl.BlockSpec(s0t.shape, lambda c: (0, 0, 0, 0))],
        out_specs=[seq_spec, pl.BlockSpec(st_shape, lambda c: (0, 0, 0, 0))],
        out_shape=[jax.ShapeDtypeStruct((s, l, w), BF16), jax.ShapeDtypeStruct(st_shape, F32)],
        scratch_shapes=[pltpu.VMEM(st_shape, F32)],
        compiler_params=_params(("arbitrary",)),
        name="hgrn",
    )(q, k, iv, lf, s0t)


def _merge_branches(x, oa, ob, sg, gates, hn, wa, wb, wo):
    hd = HG_HEAD_DIM
    d = x.shape[1]
    heads = []
    for h in range(HG_HEADS):
        sl = slice(h * hd, (h + 1) * hd)
        heads.append(_rms(ob[:, sl].astype(F32), hn[:, sl]) * sg[:, sl].astype(F32))
    obn = jnp.concatenate(heads, axis=-1).astype(BF16)
    ma = jnp.dot(oa, wa, preferred_element_type=F32)
    mb = jnp.dot(obn, wb, preferred_element_type=F32)
    merged = gates[:, :d].astype(F32) * ma + gates[:, d:].astype(F32) * mb
    return x + jnp.dot(merged.astype(BF16), wo, preferred_element_type=F32)


def _post_kernel(x_ref, oa_ref, ob_ref, sg_ref, gates_ref, hn_ref, wa_ref, wb_ref, wo_ref,
                 n_ref, wg_ref, wu_ref, wd_ref, fn_ref, y_ref, acc_ref):
    x = jnp.concatenate(
        [_merge_branches(x_ref[r, :], oa_ref[r, :], ob_ref[r, :], sg_ref[r, :], gates_ref[r, :],
                         hn_ref[...], wa_ref[...], wb_ref[...], wo_ref[...])
         for r in _row_parts(x_ref.shape[0])], axis=0)
    x = _half_ffn(x, n_ref[...], wg_ref, wu_ref, wd_ref, acc_ref)
    y_ref[...] = _rms(x, fn_ref[...])


def _post_cast_kernel(x_ref, oa_ref, ob_ref, sg_ref, gates_ref, hn_ref, wa_ref, wb_ref, wo_ref,
                      n_ref, wg_ref, wu_ref, wd_ref, fn_ref,
                      y_ref, wab_ref, wbb_ref, wob_ref, wgb_ref, wub_ref, wdb_ref,
                      x2_ref, h_ref, acc_ref):
    c = pl.program_id(0)

    @pl.when(c == 0)
    def _():
        wa, wb, wo = (r[...].astype(BF16) for r in (wa_ref, wb_ref, wo_ref))
        for ref, wv in zip((wab_ref, wbb_ref, wob_ref), (wa, wb, wo)):
            ref[...] = wv
        x2 = _merge_branches(x_ref[...], oa_ref[...], ob_ref[...], sg_ref[...], gates_ref[...],
                             hn_ref[...], wa, wb, wo)
        x2_ref[...] = x2
        h_ref[...] = _rms(x2, n_ref[...]).astype(BF16)
        acc_ref[...] = jnp.zeros_like(acc_ref)

    @pl.when(c > 0)
    def _():
        acc_ref[...] += _cast_ffn_chunk(h_ref[...], (wg_ref, wu_ref, wd_ref),
                                        (wgb_ref, wub_ref, wdb_ref))

    @pl.when(c == pl.num_programs(0) - 1)
    def _():
        y_ref[...] = _rms(x2_ref[...] + 0.5 * acc_ref[...], fn_ref[...])


def _post_cast_call(x, oa, ob, sg, gates, hn, wa, wb, wo, norm, wg, wu, wd, fnorm):
    n, d = x.shape
    dff = wg.shape[1]
    assert dff % FF_CHUNK == 0
    chunk_specs = _ff_chunk_specs(wg, wd, lambda c: jnp.maximum(c - 1, 0))
    whole = lambda a: pl.BlockSpec(a.shape, lambda c: (0,) * a.ndim)
    return pl.pallas_call(
        _post_cast_kernel,
        grid=(dff // FF_CHUNK + 1,),
        in_specs=[whole(a) for a in (x, oa, ob, sg, gates)]
                 + [_const_spec(a.shape) for a in (hn, wa, wb, wo, norm)] + chunk_specs
                 + [_const_spec(fnorm.shape)],
        out_specs=[whole(x)] + [whole(a) for a in (wa, wb, wo)] + chunk_specs,
        out_shape=[jax.ShapeDtypeStruct((n, d), F32)]
                  + [jax.ShapeDtypeStruct(a.shape, BF16) for a in (wa, wb, wo, wg, wu, wd)],
        scratch_shapes=[pltpu.VMEM((n, d), F32), pltpu.VMEM((n, d), BF16),
                        pltpu.VMEM((n, d), F32)],
        compiler_params=_params(("arbitrary",)),
        name="post_cast",
    )(x, oa, ob, sg, gates, hn, wa, wb, wo, norm, wg, wu, wd, fnorm)


def _post_call(x, oa, ob, sg, gates, hn, wa, wb, wo, norm, wg, wu, wd, fnorm, tm):
    n, d = x.shape
    consts = [hn, wa, wb, wo, norm, wg, wu, wd, fnorm]
    return pl.pallas_call(
        _post_kernel,
        grid=(n // tm,),
        in_specs=[_rows_spec(tm, d), _rows_spec(tm, oa.shape[1]), _rows_spec(tm, ob.shape[1]),
                  _rows_spec(tm, sg.shape[1]), _rows_spec(tm, gates.shape[1])]
                 + [_const_spec(a.shape) for a in consts],
        out_specs=_rows_spec(tm, d),
        out_shape=jax.ShapeDtypeStruct((n, d), F32),
        scratch_shapes=[pltpu.VMEM((tm, d), F32)],
        compiler_params=_params(("parallel",)),
        name="post",
    )(x, oa, ob, sg, gates, *consts)


def _row_tile(n, target):
    tm = min(n, target)
    assert n % tm == 0 and tm % 8 == 0
    return tm


def kernel(x_prompt, x_sample, cache_sb_k, cache_sb_v, state_hgrn, meta_tokens, ffn1_norm, ffn1_w_gate, ffn1_w_up, ffn1_w_down, mix_norm, w_in, b_gate, hg_lb_logits, hg_out_norm, w_branch_a, w_branch_b, w_out, ffn2_norm, ffn2_w_gate, ffn2_w_up, ffn2_w_down, final_norm):
    depth = w_in.shape[0]
    assert depth == 1, "single-layer step"
    nb, seq, d = x_prompt.shape
    ns, t_new, _ = x_sample.shape
    n_meta = meta_tokens.shape[0]
    row = lambda a: a.reshape(1, -1)

    mix_norms = (row(mix_norm[0]), row(b_gate[0]), hg_lb_logits)

    n_frames = nb * seq
    tm_p = _row_tile(seq, 512)
    n_run = ns * t_new
    small = jnp.concatenate([x_sample.reshape(n_run, d), meta_tokens.astype(F32)], axis=0)
    x1_s, *f1 = _ffn_cast_call(small, row(ffn1_norm[0]), ffn1_w_gate[0], ffn1_w_up[0],
                               ffn1_w_down[0])
    (q_s, kb_s, vb_s, hq_s, hk_s, hi_s, hlf_s, sg_s, gates_s, k_s, v_s, meta_kt, meta_vt,
     w_in_b) = _mix_small_call(x1_s, mix_norms[0], w_in[0], *mix_norms[1:], n_run, n_meta)
    run = lambda a: a[:n_run]
    meta = lambda a: a[n_run:]
    token_minor = lambda a: jnp.transpose(a, (0, 1, 3, 4, 2))
    token_major = lambda a: jnp.transpose(a, (0, 1, 4, 2, 3))
    seq3 = lambda a, s: a.reshape(s, -1, a.shape[-1])
    rows = lambda a: a.reshape(-1, a.shape[-1])
    pad_to = lambda a, n: jnp.pad(a, ((0, 0), (0, n - a.shape[1]), (0, 0)))
    hg = lambda sel, s, arrs: [seq3(sel(a), s) for a in arrs]

    oa_s = _sbd_call(seq3(run(q_s), ns), pad_to(seq3(run(kb_s), ns), KEY_BLOCK),
                     pad_to(seq3(run(vb_s), ns), KEY_BLOCK),
                     token_minor(cache_sb_k), token_minor(cache_sb_v))
    ob_s, st_s = _hg_call(*hg(run, ns, (hq_s, hk_s, hi_s, hlf_s)),
                          jnp.swapaxes(state_hgrn[0].astype(F32), -1, -2), t_new)
    post_norms = (row(ffn2_norm[0]), row(final_norm))
    y_s, wa, wb, wo, *f2 = _post_cast_call(
        run(x1_s), rows(oa_s), rows(ob_s), run(sg_s), run(gates_s), row(hg_out_norm[0]),
        w_branch_a[0], w_branch_b[0], w_out[0], post_norms[0],
        ffn2_w_gate[0], ffn2_w_up[0], ffn2_w_down[0], post_norms[1])

    zero_state = jnp.zeros((1, HG_HEADS, HG_HEAD_DIM, HG_HEAD_DIM), F32)
    _, st_meta = _hg_call(*hg(meta, 1, (hq_s, hk_s, hi_s, hlf_s)), zero_state, n_meta)
    x1_p = _ffn_call(x_prompt.reshape(n_frames, d), row(ffn1_norm[0]), *f1,
                     _row_tile(n_frames, FFN_TILE))
    (q_p, kb_p, vb_p, hq_p, hk_p, hi_p, hlf_p, sg_p, gates_p, kt_full, vt_full) = (
        _mix_frames_call(x1_p.reshape(nb, seq, d), mix_norms[0], w_in_b, *mix_norms[1:],
                         meta_kt, meta_vt, tm_p, n_meta))
    oa_p = _sb_call(q_p, kb_p, vb_p,
                    pad_to(meta(kb_s)[None], KEY_BLOCK), pad_to(meta(vb_s)[None], KEY_BLOCK),
                    n_meta, KEY_BLOCK)
    ob_p, st_p = _hg_call(hq_p, hk_p, hi_p, hlf_p, st_meta, HG_CHUNK, HG_CHUNKS_PER_STEP)
    y_p = _post_call(x1_p, rows(oa_p), rows(ob_p), rows(sg_p), rows(gates_p),
                     row(hg_out_norm[0]), wa, wb, wo, post_norms[0], *f2, post_norms[1], tm_p)

    heads = lambda a: a.reshape(1, ns, t_new, SB_HEADS, SB_HEAD_DIM)
    return (y_p.reshape(nb, seq, d), y_s.reshape(ns, t_new, d),
            token_major(kt_full), token_major(vt_full),
            jnp.swapaxes(st_p, -1, -2)[None],
            heads(run(k_s)), heads(run(v_s)),
            jnp.swapaxes(st_s, -1, -2)[None])
```

```python
import functools

import jax
import jax.numpy as jnp
from jax import lax
from jax.experimental import pallas as pl
from jax.experimental.pallas import tpu as pltpu

F32 = jnp.float32
BF16 = jnp.bfloat16
EPS = 1e-6
SB_HEADS = 8
SB_HEAD_DIM = 64
SB_WIDTH = SB_HEADS * SB_HEAD_DIM
HG_HEADS = 4
HG_HEAD_DIM = 128
HG_WIDTH = HG_HEADS * HG_HEAD_DIM
LANES = 128
FF_CHUNK = 256
ROW_PART = 256
FFN_TILE, FFN_ROW_PART = 1024, 512
KEY_BLOCK = 128
WIDE_BLOCK = 256
SB_BLOCKS_PER_STEP = 8
F32_EXP2_UNDERFLOW = -152.0
LOG2_E = 1.4426950408889634
HG_CHUNK = 128
HG_CHUNKS_PER_STEP = 4
HG_SAFE_LOG_DECAY = -60.0
VMEM_LIMIT_BYTES = 56 * 1024 * 1024

_NN = (((1,), (0,)), ((), ()))
_NT = (((1,), (1,)), ((), ()))
_TN = (((0,), (0,)), ((), ()))


def _const_spec(shape):
    nd = len(shape)
    return pl.BlockSpec(shape, lambda *_: (0,) * nd, pipeline_mode=pl.Buffered(1))


def _rows_spec(tm, width):
    return pl.BlockSpec((tm, width), lambda i: (i, 0))


def _params(sem):
    return pltpu.CompilerParams(dimension_semantics=sem, vmem_limit_bytes=VMEM_LIMIT_BYTES)


def _row_parts(rows, part=ROW_PART):
    if rows % part or rows == part:
        return [slice(0, rows)]
    return [slice(r, r + part) for r in range(0, rows, part)]


def _rms(x, g):
    return x * lax.rsqrt(jnp.mean(x * x, axis=-1, keepdims=True) + EPS) * g


def _ffn_chunk(h, wg, wu, wd):
    gate = jnp.dot(h, wg, preferred_element_type=F32)
    up = jnp.dot(h, wu, preferred_element_type=F32)
    act = (gate * jax.nn.sigmoid(gate) * up).astype(BF16)
    return jnp.dot(act, wd, preferred_element_type=F32)


def _half_ffn(x, g, wg_ref, wu_ref, wd_ref, acc_ref):
    h = _rms(x, g).astype(BF16)
    dff = wg_ref.shape[1]
    assert dff % FF_CHUNK == 0
    for c in range(dff // FF_CHUNK):
        cols = slice(c * FF_CHUNK, (c + 1) * FF_CHUNK)
        contrib = _ffn_chunk(h, wg_ref[:, cols], wu_ref[:, cols], wd_ref[cols, :])
        if c == 0:
            acc_ref[...] = contrib
        else:
            acc_ref[...] += contrib
    return x + 0.5 * acc_ref[...]


def _cast_ffn_chunk(h, w32_refs, w16_refs):
    wg, wu, wd = (r[...].astype(BF16) for r in w32_refs)
    for ref, wv in zip(w16_refs, (wg, wu, wd)):
        ref[...] = wv
    return _ffn_chunk(h, wg, wu, wd)


def _ff_chunk_specs(wg, wd, chunk_of_step):
    col = pl.BlockSpec((wg.shape[0], FF_CHUNK), lambda c: (0, chunk_of_step(c)))
    row = pl.BlockSpec((FF_CHUNK, wd.shape[1]), lambda c: (chunk_of_step(c), 0))
    return [col, col, row]


def _ffn_kernel(x_ref, n_ref, wg_ref, wu_ref, wd_ref, o_ref, acc_ref):
    for r in _row_parts(x_ref.shape[0], FFN_ROW_PART):
        o_ref[r, :] = _half_ffn(x_ref[r, :], n_ref[...], wg_ref, wu_ref, wd_ref, acc_ref.at[r, :])


def _ffn_cast_kernel(x_ref, n_ref, wg_ref, wu_ref, wd_ref, o_ref, wgb_ref, wub_ref, wdb_ref,
                     h_ref, acc_ref):
    c = pl.program_id(0)

    @pl.when(c == 0)
    def _():
        h_ref[...] = _rms(x_ref[...], n_ref[...]).astype(BF16)
        acc_ref[...] = jnp.zeros_like(acc_ref)

    acc_ref[...] += _cast_ffn_chunk(h_ref[...], (wg_ref, wu_ref, wd_ref),
                                    (wgb_ref, wub_ref, wdb_ref))

    @pl.when(c == pl.num_programs(0) - 1)
    def _():
        o_ref[...] = x_ref[...] + 0.5 * acc_ref[...]


def _ffn_cast_call(x, norm, wg, wu, wd):
    n, d = x.shape
    dff = wg.shape[1]
    assert dff % FF_CHUNK == 0
    chunk_specs = _ff_chunk_specs(wg, wd, lambda c: c)
    whole = pl.BlockSpec((n, d), lambda c: (0, 0))
    return pl.pallas_call(
        _ffn_cast_kernel,
        grid=(dff // FF_CHUNK,),
        in_specs=[whole, _const_spec(norm.shape)] + chunk_specs,
        out_specs=[whole] + chunk_specs,
        out_shape=[jax.ShapeDtypeStruct((n, d), F32)]
                  + [jax.ShapeDtypeStruct(a.shape, BF16) for a in (wg, wu, wd)],
        scratch_shapes=[pltpu.VMEM((n, d), BF16), pltpu.VMEM((n, d), F32)],
        compiler_params=_params(("arbitrary",)),
        name="ffn_cast",
    )(x, norm, wg, wu, wd)


def _ffn_call(x, norm, wg, wu, wd, tm):
    n, d = x.shape
    return pl.pallas_call(
        _ffn_kernel,
        grid=(n // tm,),
        in_specs=[_rows_spec(tm, d), _const_spec(norm.shape), _const_spec(wg.shape),
                  _const_spec(wu.shape), _const_spec(wd.shape)],
        out_specs=_rows_spec(tm, d),
        out_shape=jax.ShapeDtypeStruct((n, d), F32),
        scratch_shapes=[pltpu.VMEM((tm, d), F32)],
        compiler_params=_params(("parallel",)),
        name="ffn",
    )(x, norm, wg, wu, wd)


def _forget_lower_bound(lbl_ref):
    logits = lbl_ref[...]
    l0, l1 = logits[0:1], logits[1:2]
    m = jnp.maximum(l0, l1)
    e0, e1 = jnp.exp(l0 - m), jnp.exp(l1 - m)
    p0, p1 = e0 / (e0 + e1), e1 / (e0 + e1)
    return (p0 + p1) - p0


_N_MIX_GROUPS = 11
_MIX_GROUP_ORDER = (7, 8, 9, 10, 3, 6, 0, 1, 2, 4, 5)


def _mix_group(j, p, r, lb, bg_ref, outs):
    q_ref, kb_ref, vb_ref, hq_ref, hk_ref, hi_ref, hlf_ref, sg_ref, gates_ref = outs
    if j == 0:
        q_ref[r, :] = (p * (SB_HEAD_DIM ** -0.5 * LOG2_E)).astype(BF16)
    elif j == 1:
        kb_ref[r, :] = p.astype(BF16)
    elif j == 2:
        vb_ref[r, :] = p.astype(BF16)
    elif j == 3:
        f = lb + (1.0 - lb) * jax.nn.sigmoid(p)
        hk_ref[r, :] = (1.0 - f).astype(hk_ref.dtype)
        hlf_ref[r, :] = jnp.log(f)
    elif j == 4:
        hi_ref[r, :] = p.astype(hi_ref.dtype)
    elif j == 5:
        hq_ref[r, :] = p.astype(hq_ref.dtype)
    elif j == 6:
        sg_ref[r, :] = (p * jax.nn.sigmoid(p)).astype(sg_ref.dtype)
    else:
        cols = slice((j - 7) * SB_WIDTH, (j - 6) * SB_WIDTH)
        gates_ref[r, cols] = jax.nn.sigmoid(p + bg_ref[:, cols]).astype(gates_ref.dtype)


def _mix_body(x_ref, n_ref, win_ref, bg_ref, lbl_ref, *outs):
    w = SB_WIDTH
    lb = _forget_lower_bound(lbl_ref)
    kv = ([], [])
    for r in _row_parts(x_ref.shape[0]):
        h = _rms(x_ref[r, :], n_ref[...]).astype(BF16)
        for j in _MIX_GROUP_ORDER:
            p = jnp.dot(h, win_ref[:, j * w:(j + 1) * w], preferred_element_type=F32)
            _mix_group(j, p, r, lb, bg_ref, outs)
            if j in (1, 2):
                kv[j - 1].append(p)
    return jnp.concatenate(kv[0], axis=0), jnp.concatenate(kv[1], axis=0)


def _mix_small_kernel(*refs, n_run, n_meta):
    x_ref, n_ref, win_ref, bg_ref, lbl_ref = refs[:5]
    outs, (k_ref, v_ref, kt_ref, vt_ref, winb_ref, h_ref) = refs[5:14], refs[14:]
    c = pl.program_id(0)

    @pl.when(c == 0)
    def _():
        h_ref[...] = _rms(x_ref[...], n_ref[...]).astype(BF16)

    wb = win_ref[...].astype(BF16)
    winb_ref[...] = wb
    p = jnp.dot(h_ref[...], wb, preferred_element_type=F32)
    lb = _forget_lower_bound(lbl_ref)
    for j in range(_N_MIX_GROUPS):
        @pl.when(c == j)
        def _(j=j):
            _mix_group(j, p, slice(None), lb, bg_ref, outs)
            for group, rows_ref, cols_ref in ((1, k_ref, kt_ref), (2, v_ref, vt_ref)):
                if j == group:
                    rows_ref[...] = p
                    cols_ref[...] = _pad_rows(p[n_run:n_run + n_meta], LANES).T


def _mix_frames_kernel(*refs, tm, n_meta, n_tiles):
    x_ref, consts, mk_ref, mv_ref = refs[0], refs[1:5], refs[5], refs[6]
    row_outs, (kt_ref, vt_ref, ck_ref, cv_ref) = refs[7:16], refs[16:]
    t = pl.program_id(1)
    shape = (SB_HEADS, SB_HEAD_DIM, tm)

    @pl.when(t == 0)
    def _():
        ck_ref[...] = mk_ref[...]
        cv_ref[...] = mv_ref[...]

    @pl.when(t < n_tiles)
    def _():
        k, v = _mix_body(x_ref.at[0], *consts, *[r.at[0] for r in row_outs])
        for rows, out_ref, carry_ref in ((k, kt_ref, ck_ref), (v, vt_ref, cv_ref)):
            cols = rows.T
            late = jnp.concatenate([carry_ref[:, :n_meta], cols[:, :tm - n_meta]], axis=1)
            out_ref[0, 0] = late.reshape(shape)
            carry_ref[:, :n_meta] = cols[:, tm - n_meta:]

    @pl.when(t == n_tiles)
    def _():
        for out_ref, carry_ref in ((kt_ref, ck_ref), (vt_ref, cv_ref)):
            tail = jnp.concatenate(
                [carry_ref[:, :n_meta], jnp.zeros((SB_WIDTH, tm - n_meta), F32)], axis=1)
            out_ref[0, 0] = tail.reshape(shape)


_MIX_WIDTHS = [(SB_WIDTH, BF16)] * 6 + [(SB_WIDTH, F32), (SB_WIDTH, BF16)]


def _mix_small_call(x, norm, w_in, b_gate, lb_logits, n_run, n_meta):
    n, d = x.shape
    assert w_in.shape[1] == _N_MIX_GROUPS * SB_WIDTH
    widths = _MIX_WIDTHS + [(2 * d, BF16), (SB_WIDTH, F32), (SB_WIDTH, F32)]
    meta_t = jax.ShapeDtypeStruct((SB_WIDTH, LANES), F32)
    whole = lambda shape: pl.BlockSpec(shape, lambda c: (0, 0))
    group = pl.BlockSpec((d, SB_WIDTH), lambda c: (0, c))
    return pl.pallas_call(
        functools.partial(_mix_small_kernel, n_run=n_run, n_meta=n_meta),
        grid=(_N_MIX_GROUPS,),
        in_specs=[whole((n, d)), _const_spec(norm.shape), group,
                  _const_spec(b_gate.shape), _const_spec(lb_logits.shape)],
        out_specs=[whole((n, wd)) for wd, _ in widths] + [whole(meta_t.shape)] * 2 + [group],
        out_shape=[jax.ShapeDtypeStruct((n, wd), dt) for wd, dt in widths] + [meta_t] * 2
                  + [jax.ShapeDtypeStruct(w_in.shape, BF16)],
        scratch_shapes=[pltpu.VMEM((n, d), BF16)],
        compiler_params=_params(("arbitrary",)),
        name="mix_small",
    )(x, norm, w_in, b_gate, lb_logits)


def _mix_frames_call(x, norm, w_in, b_gate, lb_logits, meta_kt, meta_vt, tm, n_meta):
    nb, seq, d = x.shape
    assert seq % tm == 0 and n_meta < tm
    n_tiles = seq // tm
    widths = _MIX_WIDTHS + [(2 * d, BF16)]
    rows_map = lambda b, t: (b, jnp.minimum(t, n_tiles - 1), 0)
    const2 = lambda shape: pl.BlockSpec(shape, lambda b, t: (0, 0), pipeline_mode=pl.Buffered(1))
    kv_shape = jax.ShapeDtypeStruct((1, nb, SB_HEADS, SB_HEAD_DIM, n_meta + seq), F32)
    kv_spec = pl.BlockSpec((1, 1, SB_HEADS, SB_HEAD_DIM, tm), lambda b, t: (0, b, 0, 0, t))
    carry = pltpu.VMEM((SB_WIDTH, LANES), F32)
    return pl.pallas_call(
        functools.partial(_mix_frames_kernel, tm=tm, n_meta=n_meta, n_tiles=n_tiles),
        grid=(nb, n_tiles + 1),
        in_specs=[pl.BlockSpec((1, tm, d), rows_map)]
                 + [const2(a.shape) for a in (norm, w_in, b_gate, lb_logits, meta_kt, meta_vt)],
        out_specs=[pl.BlockSpec((1, tm, wd), rows_map) for wd, _ in widths] + [kv_spec] * 2,
        out_shape=[jax.ShapeDtypeStruct((nb, seq, wd), dt) for wd, dt in widths] + [kv_shape] * 2,
        scratch_shapes=[carry, carry],
        compiler_params=_params(("arbitrary", "arbitrary")),
        name="mix_frames",
    )(x, norm, w_in, b_gate, lb_logits, meta_kt, meta_vt)


def _suffix_ones(width):
    r = lax.broadcasted_iota(jnp.int32, (width, width), 0)
    c = lax.broadcasted_iota(jnp.int32, (width, width), 1)
    return jnp.where(r >= c, 1.0, 0.0).astype(BF16)


def _key_before(tq, width, k0, q_first):
    row = lax.broadcasted_iota(jnp.int32, (tq, width), 0)
    col = lax.broadcasted_iota(jnp.int32, (tq, width), 1)
    return col + k0 < row + q_first


def _sb_sweep(q_ops, blocks, r_ref, acc_ref, token_minor=False):
    heads = range(SB_HEADS)
    k_dims, v_dims = (_NN, _NT) if token_minor else (_NT, _NN)
    tq = q_ops[0].shape[0]
    zs = [[lax.dot_general(q_ops[h], k_ops[h], k_dims, preferred_element_type=F32)
           for h in heads] for k_ops, _, _, _ in blocks]
    stays = []
    for z_blk, (_, _, _, mask) in zip(zs, blocks):
        stays.append([])
        for z in z_blk:
            nz = -z
            ls = jnp.minimum(nz, 0.0) - jnp.log2(1.0 + jnp.exp2(jnp.minimum(z, nz)))
            if mask is not None:
                ls = jnp.where(mask, ls, 0.0)
            stays[-1].append(ls.astype(BF16))
    cums = [[jnp.dot(st, ones, preferred_element_type=F32) for st in st_blk]
            for st_blk, (_, _, ones, _) in zip(stays, blocks)]
    ws = []
    for z_blk, cum_blk, (_, _, ones, mask) in zip(zs, cums, blocks):
        ws.append([])
        for h in heads:
            r = r_ref[h]
            cr = cum_blk[h] + jnp.concatenate([r] * (ones.shape[1] // LANES), axis=1)
            a = jnp.exp2(z_blk[h] + cr)
            if mask is not None:
                a = jnp.where(mask, a, 0.0)
            ws[-1].append(a.astype(BF16))
            r_ref[h] = jnp.broadcast_to(cr[:, 0:1], (tq, LANES))
    for w_blk, (_, v_ops, _, _) in zip(ws, blocks):
        for h in heads:
            acc_ref[h] += lax.dot_general(w_blk[h], v_ops[h], v_dims, preferred_element_type=F32)


def _sb_live(r_ref):
    m = r_ref[0]
    for h in range(1, SB_HEADS):
        m = jnp.maximum(m, r_ref[h])
    return (jnp.max(m) >= F32_EXP2_UNDERFLOW).astype(jnp.int32)


def _sweep_while(n_blocks, alive, r_ref, block, first=0):
    def body(carry):
        t, _ = carry
        block(t)
        return t + 1, _sb_live(r_ref)

    _, alive = lax.while_loop(lambda c: jnp.logical_and(c[0] < n_blocks, c[1] > 0),
                              body, (jnp.int32(first), alive))
    return alive


def _sb_kernel(q_ref, kn_ref, vn_ref, ko_ref, vo_ref, o_ref, qm_ref, acc_ref, r_ref, *,
               tq, n_old):
    kb, wb = KEY_BLOCK, WIDE_BLOCK
    blocks_per_step = q_ref.shape[1] // tq
    lane = lax.broadcasted_iota(jnp.int32, (tq, LANES), 1)
    ones = {width: _suffix_ones(width) for width in (kb, wb)}

    def block(k_ref, v_ref, start, width, mask):
        k_all = k_ref[0, pl.ds(start, width), :].astype(BF16)
        v_all = v_ref[0, pl.ds(start, width), :].astype(BF16)
        pair = lambda a: [a[:, (h // 2) * LANES:(h // 2 + 1) * LANES] for h in range(SB_HEADS)]
        return pair(k_all), pair(v_all), ones[width], mask

    def sweep(*blocks):
        _sb_sweep([qm_ref[h] for h in range(SB_HEADS)], blocks, r_ref, acc_ref)

    def query_block(j, carry):
        i = pl.program_id(1) * blocks_per_step + j
        rows = pl.ds(pl.multiple_of(j * tq, tq), tq)

        for h in range(SB_HEADS):
            p = h // 2
            qp = q_ref[0, rows, p * LANES:(p + 1) * LANES]
            keep = (lane < SB_HEAD_DIM) if h % 2 == 0 else (lane >= SB_HEAD_DIM)
            qm_ref[h] = jnp.where(keep, qp, jnp.zeros_like(qp))
            r_ref[h] = jnp.zeros((tq, LANES), F32)
            acc_ref[h] = jnp.zeros((tq, LANES), F32)

        q0 = i * tq
        below = q0 // kb
        d0 = pl.multiple_of(below * kb, kb)
        own = lambda: block(kn_ref, vn_ref, d0, kb, _key_before(tq, kb, d0, q0))
        paired = below >= 2

        @pl.when(paired)
        def _():
            sweep(own(), block(kn_ref, vn_ref, pl.multiple_of(d0 - wb, kb), wb, None))

        @pl.when(jnp.logical_not(paired))
        def _():
            sweep(own())

        top = jnp.where(paired, d0 - wb, d0)
        left = jnp.where(paired, below - 2, below)
        alive = _sweep_while(left // 2, _sb_live(r_ref), r_ref, lambda t: sweep(block(
            kn_ref, vn_ref, pl.multiple_of(top - (t + 1) * wb, kb), wb, None)))
        alive = _sweep_while(left % 2, alive, r_ref,
                             lambda t: sweep(block(kn_ref, vn_ref, 0, kb, None)))
        _sweep_while(1, alive, r_ref, lambda t: sweep(block(
            ko_ref, vo_ref, 0, kb, lax.broadcasted_iota(jnp.int32, (tq, kb), 1) < n_old)))

        for p in range(SB_HEADS // 2):
            o_ref[0, rows, p * LANES:(p + 1) * LANES] = jnp.where(
                lane < SB_HEAD_DIM, acc_ref[2 * p], acc_ref[2 * p + 1]).astype(o_ref.dtype)
        return carry

    lax.fori_loop(0, blocks_per_step, query_block, 0)


def _sbd_kernel(q_ref, kn_ref, vn_ref, ck_ref, cv_ref, o_ref, acc_ref, r_ref, *, tq, n_old):
    kb, wb, hd = KEY_BLOCK, WIDE_BLOCK, SB_HEAD_DIM
    heads = range(SB_HEADS)
    n_tail, n_full = n_old % kb, n_old // kb
    n_wide = (n_full * kb) // wb
    rest = n_full * kb - n_wide * wb
    per_head = lambda a: [a[:, h * hd:(h + 1) * hd] for h in heads]

    q_ops = per_head(q_ref[0])
    for h in heads:
        r_ref[h] = jnp.zeros((tq, LANES), F32)
        acc_ref[h] = jnp.zeros((tq, hd), F32)

    assert kn_ref.shape[1] == kb
    _sb_sweep(q_ops, [(per_head(kn_ref[0]), per_head(vn_ref[0]), _suffix_ones(kb),
                       _key_before(tq, kb, 0, 0))], r_ref, acc_ref)
    alive = _sb_live(r_ref)

    def cached(start, width, valid):
        def load(ref, h):
            cols = ref[0, 0, h, :, pl.ds(start, valid)].astype(BF16)
            if valid < width:
                cols = jnp.concatenate([cols, jnp.zeros((hd, width - valid), BF16)], axis=1)
            return cols

        mask = None if valid == width else (
            lax.broadcasted_iota(jnp.int32, (tq, width), 1) < valid)
        _sb_sweep(q_ops, [([load(ck_ref, h) for h in heads], [load(cv_ref, h) for h in heads],
                           _suffix_ones(width), mask)], r_ref, acc_ref, token_minor=True)

    if n_tail:
        alive = _sweep_while(1, alive, r_ref, lambda t: cached(n_full * kb, kb, n_tail))
    if n_wide:
        alive = _sweep_while(n_wide, alive, r_ref, lambda t: cached(
            pl.multiple_of(rest + (n_wide - 1 - t) * wb, kb), wb, wb))
    if rest:
        _sweep_while(1, alive, r_ref, lambda t: cached(0, rest, rest))

    o_ref[0] = jnp.concatenate([acc_ref[h] for h in heads], axis=1).astype(o_ref.dtype)


def _sb_call(q, k_new, v_new, k_old, v_old, n_old, tq):
    s, lq, w = q.shape
    ln = k_new.shape[1]
    lo = k_old.shape[1]
    assert lq % tq == 0 and ln % KEY_BLOCK == 0 and n_old <= KEY_BLOCK <= lo
    assert tq <= KEY_BLOCK and KEY_BLOCK % tq == 0
    step = tq * SB_BLOCKS_PER_STEP if lq % (tq * SB_BLOCKS_PER_STEP) == 0 else tq
    old_map = (lambda b, i: (b, 0, 0)) if k_old.shape[0] == s else (lambda b, i: (0, 0, 0))
    return pl.pallas_call(
        functools.partial(_sb_kernel, tq=tq, n_old=n_old),
        grid=(s, lq // step),
        in_specs=[pl.BlockSpec((1, step, w), lambda b, i: (b, i, 0)),
                  pl.BlockSpec((1, ln, w), lambda b, i: (b, 0, 0)),
                  pl.BlockSpec((1, ln, w), lambda b, i: (b, 0, 0)),
                  pl.BlockSpec((1, lo, w), old_map),
                  pl.BlockSpec((1, lo, w), old_map)],
        out_specs=pl.BlockSpec((1, step, w), lambda b, i: (b, i, 0)),
        out_shape=jax.ShapeDtypeStruct((s, lq, w), BF16),
        scratch_shapes=[pltpu.VMEM((SB_HEADS, tq, LANES), BF16),
                        pltpu.VMEM((SB_HEADS, tq, LANES), F32),
                        pltpu.VMEM((SB_HEADS, tq, LANES), F32)],
        compiler_params=_params(("parallel", "parallel")),
        name="sb",
    )(q, k_new, v_new, k_old, v_old)


def _sbd_call(q, k_new, v_new, cache_k, cache_v):
    s, tq, w = q.shape
    n_old = cache_k.shape[-1]
    assert tq <= KEY_BLOCK and k_new.shape[1] == KEY_BLOCK
    new_spec = pl.BlockSpec((1, KEY_BLOCK, w), lambda b: (b, 0, 0))
    cache_spec = pl.BlockSpec((1, 1) + cache_k.shape[2:], lambda b: (0, b, 0, 0, 0))
    return pl.pallas_call(
        functools.partial(_sbd_kernel, tq=tq, n_old=n_old),
        grid=(s,),
        in_specs=[pl.BlockSpec((1, tq, w), lambda b: (b, 0, 0)), new_spec, new_spec,
                  cache_spec, cache_spec],
        out_specs=pl.BlockSpec((1, tq, w), lambda b: (b, 0, 0)),
        out_shape=jax.ShapeDtypeStruct((s, tq, w), BF16),
        scratch_shapes=[pltpu.VMEM((SB_HEADS, tq, SB_HEAD_DIM), F32),
                        pltpu.VMEM((SB_HEADS, tq, LANES), F32)],
        compiler_params=_params(("parallel",)),
        name="sb_decode",
    )(q, k_new, v_new, cache_k, cache_v)


def _pad_rows(x, rows):
    if x.shape[0] == rows:
        return x
    return jnp.concatenate([x, jnp.zeros((rows - x.shape[0], x.shape[1]), x.dtype)], axis=0)


def _hg_kernel(q_ref, k_ref, i_ref, lf_ref, s0_ref, o_ref, sout_ref, st_ref, *, chunk):
    c = pl.program_id(0)
    hd = HG_HEAD_DIM
    ns, n_chunks = q_ref.shape[0], q_ref.shape[1] // chunk
    streams = range(ns)
    units = [(g, b) for g in range(n_chunks) for b in streams]
    rows = lambda g: slice(g * chunk, (g + 1) * chunk)
    head_cols = [slice(h * hd, (h + 1) * hd) for h in range(HG_HEADS)]

    @pl.when(c == 0)
    def _():
        for b in streams:
            st_ref[b] = s0_ref[b if s0_ref.shape[0] == ns else 0]

    trow = lax.broadcasted_iota(jnp.int32, (chunk, chunk), 0)
    tcol = lax.broadcasted_iota(jnp.int32, (chunk, chunk), 1)
    causal = trow >= tcol

    token = lax.broadcasted_iota(jnp.int32, (chunk, lf_ref.shape[2]), 0)
    cum = {}
    for g, b in units:
        part = lf_ref[b, rows(g), :]
        shift = 1
        while shift < chunk:
            part = part + jnp.where(token >= shift, pltpu.roll(part, shift, axis=0), 0.0)
            shift *= 2
        cum[g, b] = part
    last = {u: cum[u][chunk - 1:chunk, :] for u in units}
    weakest = functools.reduce(jnp.minimum, last.values())
    safe = jnp.min(weakest) >= HG_SAFE_LOG_DECAY

    @pl.when(safe)
    def _():
        qd, kd, kl, iv, dl = {}, {}, {}, {}, {}
        for g, b in units:
            k = k_ref[b, rows(g), :]
            qd[g, b] = (q_ref[b, rows(g), :] * jnp.exp(cum[g, b])).astype(BF16)
            kd[g, b] = (k * jnp.exp(-cum[g, b])).astype(BF16)
            kl[g, b] = (k * jnp.exp(last[g, b] - cum[g, b])).astype(BF16)
            iv[g, b] = i_ref[b, rows(g), :].astype(BF16)
            dl[g, b] = jnp.exp(last[g, b])
        sc = {(u, h): lax.dot_general(qd[u][:, sl], kd[u][:, sl], _NT, preferred_element_type=F32)
              for u in units for h, sl in enumerate(head_cols)}
        grown = {(u, h): lax.dot_general(iv[u][:, sl], kl[u][:, sl], _TN,
                                         preferred_element_type=F32)
                 for u in units for h, sl in enumerate(head_cols)}
        within = {(u, h): jnp.dot(jnp.where(causal, sc[u, h], 0.0).astype(BF16), iv[u][:, sl],
                                  preferred_element_type=F32)
                  for u in units for h, sl in enumerate(head_cols)}
        for g in range(n_chunks):
            carried = {(b, h): lax.dot_general(qd[g, b][:, sl], st_ref[b, h].astype(BF16), _NT,
                                               preferred_element_type=F32)
                       for b in streams for h, sl in enumerate(head_cols)}
            for b in streams:
                for h, sl in enumerate(head_cols):
                    o_ref[b, rows(g), sl] = (within[(g, b), h] + carried[b, h]).astype(o_ref.dtype)
                    st_ref[b, h] = st_ref[b, h] * dl[g, b][:, sl] + grown[(g, b), h]

    @pl.when(jnp.logical_not(safe))
    def _():
        lane_t = lax.broadcasted_iota(jnp.int32, (hd, hd), 1)
        row_t = lax.broadcasted_iota(jnp.int32, (hd, hd), 0)

        def one_chunk(n, carry):
            b, r0 = n // n_chunks, pl.multiple_of((n % n_chunks) * chunk, chunk)
            for h, sl in enumerate(head_cols):
                load = lambda ref: ref[b, pl.ds(r0, chunk), sl].astype(F32)
                f_t = _pad_rows(jnp.exp(load(lf_ref)), hd).T
                k_t = _pad_rows(load(k_ref), hd).T
                q_t = _pad_rows(load(q_ref), hd).T
                i_p = _pad_rows(load(i_ref), hd)

                def token(t, sc, f_t=f_t, k_t=k_t, q_t=q_t, i_p=i_p):
                    s, o = sc
                    sel = lane_t == t
                    fc = jnp.sum(jnp.where(sel, f_t, 0.0), axis=1, keepdims=True)
                    kc = jnp.sum(jnp.where(sel, k_t, 0.0), axis=1, keepdims=True)
                    qc = jnp.sum(jnp.where(sel, q_t, 0.0), axis=1, keepdims=True)
                    i_row = jnp.sum(jnp.where(row_t == t, i_p, 0.0), axis=0, keepdims=True)
                    s = s * fc + kc * i_row
                    o = jnp.where(row_t == t, jnp.sum(qc * s, axis=0, keepdims=True), o)
                    return s, o

                s, o = lax.fori_loop(0, chunk, token,
                                     (st_ref[b, h].T, jnp.zeros((hd, hd), F32)))
                st_ref[b, h] = s.T
                o_ref[b, pl.ds(r0, chunk), sl] = o[:chunk].astype(o_ref.dtype)
            return carry

        lax.fori_loop(0, ns * n_chunks, one_chunk, 0)

    @pl.when(c == pl.num_programs(0) - 1)
    def _():
        sout_ref[...] = st_ref[...]


def _hg_call(q, k, iv, lf, s0t, chunk, chunks_per_step=1):
    s, l, w = q.shape
    rows = chunk * chunks_per_step
    assert l % rows == 0
    seq_spec = pl.BlockSpec((s, rows, w), lambda c: (0, c, 0))
    st_shape = (s, HG_HEADS, HG_HEAD_DIM, HG_HEAD_DIM)
    return pl.pallas_call(
        functools.partial(_hg_kernel, chunk=chunk),
        grid=(l // rows,),
        in_specs=[seq_spec, seq_spec, seq_spec, seq_spec,
                  pl.BlockSpec(s0t.shape, lambda c: (0, 0, 0, 0))],
        out_specs=[seq_spec, pl.BlockSpec(st_shape, lambda c: (0, 0, 0, 0))],
        out_shape=[jax.ShapeDtypeStruct((s, l, w), BF16), jax.ShapeDtypeStruct(st_shape, F32)],
        scratch_shapes=[pltpu.VMEM(st_shape, F32)],
        compiler_params=_params(("arbitrary",)),
        name="hgrn",
    )(q, k, iv, lf, s0t)


def _merge_branches(x, oa, ob, sg, gates, hn, wa, wb, wo):
    hd = HG_HEAD_DIM
    d = x.shape[1]
    heads = []
    for h in range(HG_HEADS):
        sl = slice(h * hd, (h + 1) * hd)
        heads.append(_rms(ob[:, sl].astype(F32), hn[:, sl]) * sg[:, sl].astype(F32))
    obn = jnp.concatenate(heads, axis=-1).astype(BF16)
    ma = jnp.dot(oa, wa, preferred_element_type=F32)
    mb = jnp.dot(obn, wb, preferred_element_type=F32)
    merged = gates[:, :d].astype(F32) * ma + gates[:, d:].astype(F32) * mb
    return x + jnp.dot(merged.astype(BF16), wo, preferred_element_type=F32)


def _post_kernel(x_ref, oa_ref, ob_ref, sg_ref, gates_ref, hn_ref, wa_ref, wb_ref, wo_ref,
                 n_ref, wg_ref, wu_ref, wd_ref, fn_ref, y_ref, acc_ref):
    x = jnp.concatenate(
        [_merge_branches(x_ref[r, :], oa_ref[r, :], ob_ref[r, :], sg_ref[r, :], gates_ref[r, :],
                         hn_ref[...], wa_ref[...], wb_ref[...], wo_ref[...])
         for r in _row_parts(x_ref.shape[0])], axis=0)
    x = _half_ffn(x, n_ref[...], wg_ref, wu_ref, wd_ref, acc_ref)
    y_ref[...] = _rms(x, fn_ref[...])


def _post_cast_kernel(x_ref, oa_ref, ob_ref, sg_ref, gates_ref, hn_ref, wa_ref, wb_ref, wo_ref,
                      n_ref, wg_ref, wu_ref, wd_ref, fn_ref,
                      y_ref, wab_ref, wbb_ref, wob_ref, wgb_ref, wub_ref, wdb_ref,
                      x2_ref, h_ref, acc_ref):
    c = pl.program_id(0)

    @pl.when(c == 0)
    def _():
        wa, wb, wo = (r[...].astype(BF16) for r in (wa_ref, wb_ref, wo_ref))
        for ref, wv in zip((wab_ref, wbb_ref, wob_ref), (wa, wb, wo)):
            ref[...] = wv
        x2 = _merge_branches(x_ref[...], oa_ref[...], ob_ref[...], sg_ref[...], gates_ref[...],
                             hn_ref[...], wa, wb, wo)
        x2_ref[...] = x2
        h_ref[...] = _rms(x2, n_ref[...]).astype(BF16)
        acc_ref[...] = jnp.zeros_like(acc_ref)

    @pl.when(c > 0)
    def _():
        acc_ref[...] += _cast_ffn_chunk(h_ref[...], (wg_ref, wu_ref, wd_ref),
                                        (wgb_ref, wub_ref, wdb_ref))

    @pl.when(c == pl.num_programs(0) - 1)
    def _():
        y_ref[...] = _rms(x2_ref[...] + 0.5 * acc_ref[...], fn_ref[...])


def _post_cast_call(x, oa, ob, sg, gates, hn, wa, wb, wo, norm, wg, wu, wd, fnorm):
    n, d = x.shape
    dff = wg.shape[1]
    assert dff % FF_CHUNK == 0
    chunk_specs = _ff_chunk_specs(wg, wd, lambda c: jnp.maximum(c - 1, 0))
    whole = lambda a: pl.BlockSpec(a.shape, lambda c: (0,) * a.ndim)
    return pl.pallas_call(
        _post_cast_kernel,
        grid=(dff // FF_CHUNK + 1,),
        in_specs=[whole(a) for a in (x, oa, ob, sg, gates)]
                 + [_const_spec(a.shape) for a in (hn, wa, wb, wo, norm)] + chunk_specs
                 + [_const_spec(fnorm.shape)],
        out_specs=[whole(x)] + [whole(a) for a in (wa, wb, wo)] + chunk_specs,
        out_shape=[jax.ShapeDtypeStruct((n, d), F32)]
                  + [jax.ShapeDtypeStruct(a.shape, BF16) for a in (wa, wb, wo, wg, wu, wd)],
        scratch_shapes=[pltpu.VMEM((n, d), F32), pltpu.VMEM((n, d), BF16),
                        pltpu.VMEM((n, d), F32)],
        compiler_params=_params(("arbitrary",)),
        name="post_cast",
    )(x, oa, ob, sg, gates, hn, wa, wb, wo, norm, wg, wu, wd, fnorm)


def _post_call(x, oa, ob, sg, gates, hn, wa, wb, wo, norm, wg, wu, wd, fnorm, tm):
    n, d = x.shape
    consts = [hn, wa, wb, wo, norm, wg, wu, wd, fnorm]
    return pl.pallas_call(
        _post_kernel,
        grid=(n // tm,),
        in_specs=[_rows_spec(tm, d), _rows_spec(tm, oa.shape[1]), _rows_spec(tm, ob.shape[1]),
                  _rows_spec(tm, sg.shape[1]), _rows_spec(tm, gates.shape[1])]
                 + [_const_spec(a.shape) for a in consts],
        out_specs=_rows_spec(tm, d),
        out_shape=jax.ShapeDtypeStruct((n, d), F32),
        scratch_shapes=[pltpu.VMEM((tm, d), F32)],
        compiler_params=_params(("parallel",)),
        name="post",
    )(x, oa, ob, sg, gates, *consts)


def _row_tile(n, target):
    tm = min(n, target)
    assert n % tm == 0 and tm % 8 == 0
    return tm


def kernel(x_prompt, x_sample, cache_sb_k, cache_sb_v, state_hgrn, meta_tokens, ffn1_norm, ffn1_w_gate, ffn1_w_up, ffn1_w_down, mix_norm, w_in, b_gate, hg_lb_logits, hg_out_norm, w_branch_a, w_branch_b, w_out, ffn2_norm, ffn2_w_gate, ffn2_w_up, ffn2_w_down, final_norm):
    depth = w_in.shape[0]
    assert depth == 1, "single-layer step"
    nb, seq, d = x_prompt.shape
    ns, t_new, _ = x_sample.shape
    n_meta = meta_tokens.shape[0]
    row = lambda a: a.reshape(1, -1)

    mix_norms = (row(mix_norm[0]), row(b_gate[0]), hg_lb_logits)

    n_frames = nb * seq
    tm_p = _row_tile(seq, 512)
    n_run = ns * t_new
    small = jnp.concatenate([x_sample.reshape(n_run, d), meta_tokens.astype(F32)], axis=0)
    x1_s, *f1 = _ffn_cast_call(small, row(ffn1_norm[0]), ffn1_w_gate[0], ffn1_w_up[0],
                               ffn1_w_down[0])
    (q_s, kb_s, vb_s, hq_s, hk_s, hi_s, hlf_s, sg_s, gates_s, k_s, v_s, meta_kt, meta_vt,
     w_in_b) = _mix_small_call(x1_s, mix_norms[0], w_in[0], *mix_norms[1:], n_run, n_meta)
    run = lambda a: a[:n_run]
    meta = lambda a: a[n_run:]
    token_minor = lambda a: jnp.transpose(a, (0, 1, 3, 4, 2))
    token_major = lambda a: jnp.transpose(a, (0, 1, 4, 2, 3))
    seq3 = lambda a, s: a.reshape(s, -1, a.shape[-1])
    rows = lambda a: a.reshape(-1, a.shape[-1])
    pad_to = lambda a, n: jnp.pad(a, ((0, 0), (0, n - a.shape[1]), (0, 0)))
    hg = lambda sel, s, arrs: [seq3(sel(a), s) for a in arrs]

    oa_s = _sbd_call(seq3(run(q_s), ns), pad_to(seq3(run(kb_s), ns), KEY_BLOCK),
                     pad_to(seq3(run(vb_s), ns), KEY_BLOCK),
                     token_minor(cache_sb_k), token_minor(cache_sb_v))
    ob_s, st_s = _hg_call(*hg(run, ns, (hq_s, hk_s, hi_s, hlf_s)),
                          jnp.swapaxes(state_hgrn[0].astype(F32), -1, -2), t_new)
    post_norms = (row(ffn2_norm[0]), row(final_norm))
    y_s, wa, wb, wo, *f2 = _post_cast_call(
        run(x1_s), rows(oa_s), rows(ob_s), run(sg_s), run(gates_s), row(hg_out_norm[0]),
        w_branch_a[0], w_branch_b[0], w_out[0], post_norms[0],
        ffn2_w_gate[0], ffn2_w_up[0], ffn2_w_down[0], post_norms[1])

    zero_state = jnp.zeros((1, HG_HEADS, HG_HEAD_DIM, HG_HEAD_DIM), F32)
    _, st_meta = _hg_call(*hg(meta, 1, (hq_s, hk_s, hi_s, hlf_s)), zero_state, n_meta)
    x1_p = _ffn_call(x_prompt.reshape(n_frames, d), row(ffn1_norm[0]), *f1,
                     _row_tile(n_frames, FFN_TILE))
    (q_p, kb_p, vb_p, hq_p, hk_p, hi_p, hlf_p, sg_p, gates_p, kt_full, vt_full) = (
        _mix_frames_call(x1_p.reshape(nb, seq, d), mix_norms[0], w_in_b, *mix_norms[1:],
                         meta_kt, meta_vt, tm_p, n_meta))
    oa_p = _sb_call(q_p, kb_p, vb_p,
                    pad_to(meta(kb_s)[None], KEY_BLOCK), pad_to(meta(vb_s)[None], KEY_BLOCK),
                    n_meta, KEY_BLOCK)
    ob_p, st_p = _hg_call(hq_p, hk_p, hi_p, hlf_p, st_meta, HG_CHUNK, HG_CHUNKS_PER_STEP)
    y_p = _post_call(x1_p, rows(oa_p), rows(ob_p), rows(sg_p), rows(gates_p),
                     row(hg_out_norm[0]), wa, wb, wo, post_norms[0], *f2, post_norms[1], tm_p)

    heads = lambda a: a.reshape(1, ns, t_new, SB_HEADS, SB_HEAD_DIM)
    return (y_p.reshape(nb, seq, d), y_s.reshape(ns, t_new, d),
            token_major(kt_full), token_major(vt_full),
            jnp.swapaxes(st_p, -1, -2)[None],
            heads(run(k_s)), heads(run(v_s)),
            jnp.swapaxes(st_s, -1, -2)[None])
```

```python
import functools

import jax
import jax.numpy as jnp
from jax import lax
from jax.experimental import pallas as pl
from jax.experimental.pallas import tpu as pltpu

F32 = jnp.float32
BF16 = jnp.bfloat16
EPS = 1e-6
SB_HEADS = 8
SB_HEAD_DIM = 64
SB_WIDTH = SB_HEADS * SB_HEAD_DIM
HG_HEADS = 4
HG_HEAD_DIM = 128
HG_WIDTH = HG_HEADS * HG_HEAD_DIM
LANES = 128
FF_CHUNK = 256
ROW_PART = 256
FFN_TILE, FFN_ROW_PART = 1024, 512
KEY_BLOCK = 128
WIDE_BLOCK = 256
SB_BLOCKS_PER_STEP = 8
F32_EXP2_UNDERFLOW = -152.0
LOG2_E = 1.4426950408889634
HG_CHUNK = 128
HG_CHUNKS_PER_STEP = 4
HG_SAFE_LOG_DECAY = -60.0
VMEM_LIMIT_BYTES = 56 * 1024 * 1024

_NN = (((1,), (0,)), ((), ()))
_NT = (((1,), (1,)), ((), ()))
_TN = (((0,), (0,)), ((), ()))


def _const_spec(shape):
    nd = len(shape)
    return pl.BlockSpec(shape, lambda *_: (0,) * nd, pipeline_mode=pl.Buffered(1))


def _rows_spec(tm, width):
    return pl.BlockSpec((tm, width), lambda i: (i, 0))


def _params(sem):
    return pltpu.CompilerParams(dimension_semantics=sem, vmem_limit_bytes=VMEM_LIMIT_BYTES)


def _row_parts(rows, part=ROW_PART):
    if rows % part or rows == part:
        return [slice(0, rows)]
    return [slice(r, r + part) for r in range(0, rows, part)]


def _rms(x, g):
    return x * lax.rsqrt(jnp.mean(x * x, axis=-1, keepdims=True) + EPS) * g


def _ffn_chunk(h, wg, wu, wd):
    gate = jnp.dot(h, wg, preferred_element_type=F32)
    up = jnp.dot(h, wu, preferred_element_type=F32)
    act = (gate * jax.nn.sigmoid(gate) * up).astype(BF16)
    return jnp.dot(act, wd, preferred_element_type=F32)


def _half_ffn(x, g, wg_ref, wu_ref, wd_ref, acc_ref):
    h = _rms(x, g).astype(BF16)
    dff = wg_ref.shape[1]
    assert dff % FF_CHUNK == 0
    for c in range(dff // FF_CHUNK):
        cols = slice(c * FF_CHUNK, (c + 1) * FF_CHUNK)
        contrib = _ffn_chunk(h, wg_ref[:, cols], wu_ref[:, cols], wd_ref[cols, :])
        if c == 0:
            acc_ref[...] = contrib
        else:
            acc_ref[...] += contrib
    return x + 0.5 * acc_ref[...]


def _cast_ffn_chunk(h, w32_refs, w16_refs):
    wg, wu, wd = (r[...].astype(BF16) for r in w32_refs)
    for ref, wv in zip(w16_refs, (wg, wu, wd)):
        ref[...] = wv
    return _ffn_chunk(h, wg, wu, wd)


def _ff_chunk_specs(wg, wd, chunk_of_step):
    col = pl.BlockSpec((wg.shape[0], FF_CHUNK), lambda c: (0, chunk_of_step(c)))
    row = pl.BlockSpec((FF_CHUNK, wd.shape[1]), lambda c: (chunk_of_step(c), 0))
    return [col, col, row]


def _ffn_kernel(x_ref, n_ref, wg_ref, wu_ref, wd_ref, o_ref, acc_ref):
    for r in _row_parts(x_ref.shape[0], FFN_ROW_PART):
        o_ref[r, :] = _half_ffn(x_ref[r, :], n_ref[...], wg_ref, wu_ref, wd_ref, acc_ref.at[r, :])


def _ffn_cast_kernel(x_ref, n_ref, wg_ref, wu_ref, wd_ref, o_ref, wgb_ref, wub_ref, wdb_ref,
                     h_ref, acc_ref):
    c = pl.program_id(0)

    @pl.when(c == 0)
    def _():
        h_ref[...] = _rms(x_ref[...], n_ref[...]).astype(BF16)
        acc_ref[...] = jnp.zeros_like(acc_ref)

    acc_ref[...] += _cast_ffn_chunk(h_ref[...], (wg_ref, wu_ref, wd_ref),
                                    (wgb_ref, wub_ref, wdb_ref))

    @pl.when(c == pl.num_programs(0) - 1)
    def _():
        o_ref[...] = x_ref[...] + 0.5 * acc_ref[...]


def _ffn_cast_call(x, norm, wg, wu, wd):
    n, d = x.shape
    dff = wg.shape[1]
    assert dff % FF_CHUNK == 0
    chunk_specs = _ff_chunk_specs(wg, wd, lambda c: c)
    whole = pl.BlockSpec((n, d), lambda c: (0, 0))
    return pl.pallas_call(
        _ffn_cast_kernel,
        grid=(dff // FF_CHUNK,),
        in_specs=[whole, _const_spec(norm.shape)] + chunk_specs,
        out_specs=[whole] + chunk_specs,
        out_shape=[jax.ShapeDtypeStruct((n, d), F32)]
                  + [jax.ShapeDtypeStruct(a.shape, BF16) for a in (wg, wu, wd)],
        scratch_shapes=[pltpu.VMEM((n, d), BF16), pltpu.VMEM((n, d), F32)],
        compiler_params=_params(("arbitrary",)),
        name="ffn_cast",
    )(x, norm, wg, wu, wd)


def _ffn_call(x, norm, wg, wu, wd, tm):
    n, d = x.shape
    return pl.pallas_call(
        _ffn_kernel,
        grid=(n // tm,),
        in_specs=[_rows_spec(tm, d), _const_spec(norm.shape), _const_spec(wg.shape),
                  _const_spec(wu.shape), _const_spec(wd.shape)],
        out_specs=_rows_spec(tm, d),
        out_shape=jax.ShapeDtypeStruct((n, d), F32),
        scratch_shapes=[pltpu.VMEM((tm, d), F32)],
        compiler_params=_params(("parallel",)),
        name="ffn",
    )(x, norm, wg, wu, wd)


def _forget_lower_bound(lbl_ref):
    logits = lbl_ref[...]
    l0, l1 = logits[0:1], logits[1:2]
    m = jnp.maximum(l0, l1)
    e0, e1 = jnp.exp(l0 - m), jnp.exp(l1 - m)
    p0, p1 = e0 / (e0 + e1), e1 / (e0 + e1)
    return (p0 + p1) - p0


_N_MIX_GROUPS = 11
_MIX_GROUP_ORDER = (7, 8, 9, 10, 3, 6, 0, 1, 2, 4, 5)


def _mix_group(j, p, r, lb, bg_ref, outs):
    q_ref, kb_ref, vb_ref, hq_ref, hk_ref, hi_ref, hlf_ref, sg_ref, gates_ref = outs
    if j == 0:
        q_ref[r, :] = (p * (SB_HEAD_DIM ** -0.5 * LOG2_E)).astype(BF16)
    elif j == 1:
        kb_ref[r, :] = p.astype(BF16)
    elif j == 2:
        vb_ref[r, :] = p.astype(BF16)
    elif j == 3:
        f = lb + (1.0 - lb) * jax.nn.sigmoid(p)
        hk_ref[r, :] = (1.0 - f).astype(hk_ref.dtype)
        hlf_ref[r, :] = jnp.log(f)
    elif j == 4:
        hi_ref[r, :] = p.astype(hi_ref.dtype)
    elif j == 5:
        hq_ref[r, :] = p.astype(hq_ref.dtype)
    elif j == 6:
        sg_ref[r, :] = (p * jax.nn.sigmoid(p)).astype(sg_ref.dtype)
    else:
        cols = slice((j - 7) * SB_WIDTH, (j - 6) * SB_WIDTH)
        gates_ref[r, cols] = jax.nn.sigmoid(p + bg_ref[:, cols]).astype(gates_ref.dtype)


def _mix_body(x_ref, n_ref, win_ref, bg_ref, lbl_ref, *outs):
    w = SB_WIDTH
    lb = _forget_lower_bound(lbl_ref)
    kv = ([], [])
    for r in _row_parts(x_ref.shape[0]):
        h = _rms(x_ref[r, :], n_ref[...]).astype(BF16)
        for j in _MIX_GROUP_ORDER:
            p = jnp.dot(h, win_ref[:, j * w:(j + 1) * w], preferred_element_type=F32)
            _mix_group(j, p, r, lb, bg_ref, outs)
            if j in (1, 2):
                kv[j - 1].append(p)
    return jnp.concatenate(kv[0], axis=0), jnp.concatenate(kv[1], axis=0)


def _mix_small_kernel(*refs, n_run, n_meta):
    x_ref, n_ref, win_ref, bg_ref, lbl_ref = refs[:5]
    outs, (k_ref, v_ref, kt_ref, vt_ref, winb_ref, h_ref) = refs[5:14], refs[14:]
    c = pl.program_id(0)

    @pl.when(c == 0)
    def _():
        h_ref[...] = _rms(x_ref[...], n_ref[...]).astype(BF16)

    wb = win_ref[...].astype(BF16)
    winb_ref[...] = wb
    p = jnp.dot(h_ref[...], wb, preferred_element_type=F32)
    lb = _forget_lower_bound(lbl_ref)
    for j in range(_N_MIX_GROUPS):
        @pl.when(c == j)
        def _(j=j):
            _mix_group(j, p, slice(None), lb, bg_ref, outs)
            for group, rows_ref, cols_ref in ((1, k_ref, kt_ref), (2, v_ref, vt_ref)):
                if j == group:
                    rows_ref[...] = p
                    cols_ref[...] = _pad_rows(p[n_run:n_run + n_meta], LANES).T


def _mix_frames_kernel(*refs, tm, n_meta, n_tiles):
    x_ref, consts, mk_ref, mv_ref = refs[0], refs[1:5], refs[5], refs[6]
    row_outs, (kt_ref, vt_ref, ck_ref, cv_ref) = refs[7:16], refs[16:]
    t = pl.program_id(1)
    shape = (SB_HEADS, SB_HEAD_DIM, tm)

    @pl.when(t == 0)
    def _():
        ck_ref[...] = mk_ref[...]
        cv_ref[...] = mv_ref[...]

    @pl.when(t < n_tiles)
    def _():
        k, v = _mix_body(x_ref.at[0], *consts, *[r.at[0] for r in row_outs])
        for rows, out_ref, carry_ref in ((k, kt_ref, ck_ref), (v, vt_ref, cv_ref)):
            cols = rows.T
            late = jnp.concatenate([carry_ref[:, :n_meta], cols[:, :tm - n_meta]], axis=1)
            out_ref[0, 0] = late.reshape(shape)
            carry_ref[:, :n_meta] = cols[:, tm - n_meta:]

    @pl.when(t == n_tiles)
    def _():
        for out_ref, carry_ref in ((kt_ref, ck_ref), (vt_ref, cv_ref)):
            tail = jnp.concatenate(
                [carry_ref[:, :n_meta], jnp.zeros((SB_WIDTH, tm - n_meta), F32)], axis=1)
            out_ref[0, 0] = tail.reshape(shape)


_MIX_WIDTHS = [(SB_WIDTH, BF16)] * 6 + [(SB_WIDTH, F32), (SB_WIDTH, BF16)]


def _mix_small_call(x, norm, w_in, b_gate, lb_logits, n_run, n_meta):
    n, d = x.shape
    assert w_in.shape[1] == _N_MIX_GROUPS * SB_WIDTH
    widths = _MIX_WIDTHS + [(2 * d, BF16), (SB_WIDTH, F32), (SB_WIDTH, F32)]
    meta_t = jax.ShapeDtypeStruct((SB_WIDTH, LANES), F32)
    whole = lambda shape: pl.BlockSpec(shape, lambda c: (0, 0))
    group = pl.BlockSpec((d, SB_WIDTH), lambda c: (0, c))
    return pl.pallas_call(
        functools.partial(_mix_small_kernel, n_run=n_run, n_meta=n_meta),
        grid=(_N_MIX_GROUPS,),
        in_specs=[whole((n, d)), _const_spec(norm.shape), group,
                  _const_spec(b_gate.shape), _const_spec(lb_logits.shape)],
        out_specs=[whole((n, wd)) for wd, _ in widths] + [whole(meta_t.shape)] * 2 + [group],
        out_shape=[jax.ShapeDtypeStruct((n, wd), dt) for wd, dt in widths] + [meta_t] * 2
                  + [jax.ShapeDtypeStruct(w_in.shape, BF16)],
        scratch_shapes=[pltpu.VMEM((n, d), BF16)],
        compiler_params=_params(("arbitrary",)),
        name="mix_small",
    )(x, norm, w_in, b_gate, lb_logits)


def _mix_frames_call(x, norm, w_in, b_gate, lb_logits, meta_kt, meta_vt, tm, n_meta):
    nb, seq, d = x.shape
    assert seq % tm == 0 and n_meta < tm
    n_tiles = seq // tm
    widths = _MIX_WIDTHS + [(2 * d, BF16)]
    rows_map = lambda b, t: (b, jnp.minimum(t, n_tiles - 1), 0)
    const2 = lambda shape: pl.BlockSpec(shape, lambda b, t: (0, 0), pipeline_mode=pl.Buffered(1))
    kv_shape = jax.ShapeDtypeStruct((1, nb, SB_HEADS, SB_HEAD_DIM, n_meta + seq), F32)
    kv_spec = pl.BlockSpec((1, 1, SB_HEADS, SB_HEAD_DIM, tm), lambda b, t: (0, b, 0, 0, t))
    carry = pltpu.VMEM((SB_WIDTH, LANES), F32)
    return pl.pallas_call(
        functools.partial(_mix_frames_kernel, tm=tm, n_meta=n_meta, n_tiles=n_tiles),
        grid=(nb, n_tiles + 1),
        in_specs=[pl.BlockSpec((1, tm, d), rows_map)]
                 + [const2(a.shape) for a in (norm, w_in, b_gate, lb_logits, meta_kt, meta_vt)],
        out_specs=[pl.BlockSpec((1, tm, wd), rows_map) for wd, _ in widths] + [kv_spec] * 2,
        out_shape=[jax.ShapeDtypeStruct((nb, seq, wd), dt) for wd, dt in widths] + [kv_shape] * 2,
        scratch_shapes=[carry, carry],
        compiler_params=_params(("arbitrary", "arbitrary")),
        name="mix_frames",
    )(x, norm, w_in, b_gate, lb_logits, meta_kt, meta_vt)


def _suffix_ones(width):
    r = lax.broadcasted_iota(jnp.int32, (width, width), 0)
    c = lax.broadcasted_iota(jnp.int32, (width, width), 1)
    return jnp.where(r >= c, 1.0, 0.0).astype(BF16)


def _key_before(tq, width, k0, q_first):
    row = lax.broadcasted_iota(jnp.int32, (tq, width), 0)
    col = lax.broadcasted_iota(jnp.int32, (tq, width), 1)
    return col + k0 < row + q_first


def _sb_sweep(blocks, token_minor=False):
    heads = range(SB_HEADS)
    k_dims, v_dims = (_NN, _NT) if token_minor else (_NT, _NN)
    zs = [[lax.dot_general(q_ops[h], k_ops[h], k_dims, preferred_element_type=F32)
           for h in heads] for q_ops, k_ops, *_ in blocks]
    stays = []
    for z_blk, (_, _, _, _, mask, _, _) in zip(zs, blocks):
        stays.append([])
        for z in z_blk:
            nz = -z
            ls = jnp.minimum(nz, 0.0) - jnp.log2(1.0 + jnp.exp2(jnp.minimum(z, nz)))
            if mask is not None:
                ls = jnp.where(mask, ls, 0.0)
            stays[-1].append(ls.astype(BF16))
    cums = [[jnp.dot(st, blk[3], preferred_element_type=F32) for st in st_blk]
            for st_blk, blk in zip(stays, blocks)]
    ws = []
    for z_blk, cum_blk, (q_ops, _, _, ones, mask, r_ref, _) in zip(zs, cums, blocks):
        ws.append([])
        for h in heads:
            r = r_ref[h]
            cr = cum_blk[h] + jnp.concatenate([r] * (ones.shape[1] // LANES), axis=1)
            a = jnp.exp2(z_blk[h] + cr)
            if mask is not None:
                a = jnp.where(mask, a, 0.0)
            ws[-1].append(a.astype(BF16))
            r_ref[h] = jnp.broadcast_to(cr[:, 0:1], (q_ops[h].shape[0], LANES))
    for w_blk, (_, _, v_ops, _, _, _, acc_ref) in zip(ws, blocks):
        for h in heads:
            acc_ref[h] += lax.dot_general(w_blk[h], v_ops[h], v_dims, preferred_element_type=F32)


def _sb_live(r_ref):
    m = r_ref[0]
    for h in range(1, SB_HEADS):
        m = jnp.maximum(m, r_ref[h])
    return (jnp.max(m) >= F32_EXP2_UNDERFLOW).astype(jnp.int32)


def _sweep_while(n_blocks, alive, r_ref, block, first=0):
    def body(carry):
        t, _ = carry
        block(t)
        return t + 1, _sb_live(r_ref)

    _, alive = lax.while_loop(lambda c: jnp.logical_and(c[0] < n_blocks, c[1] > 0),
                              body, (jnp.int32(first), alive))
    return alive


def _sb_kernel(q_ref, kn_ref, vn_ref, ko_ref, vo_ref, o_ref, qm_ref, acc_ref, r_ref, live_ref, *,
               tq, n_old):
    kb, wb = KEY_BLOCK, WIDE_BLOCK
    blocks_per_step = q_ref.shape[1] // tq
    lane = lax.broadcasted_iota(jnp.int32, (tq, LANES), 1)
    ones = {width: _suffix_ones(width) for width in (kb, wb)}
    old_rows = lax.broadcasted_iota(jnp.int32, (tq, kb), 1) < n_old

    def block(j, k_ref, v_ref, start, width, mask):
        k_all = k_ref[0, pl.ds(start, width), :].astype(BF16)
        v_all = v_ref[0, pl.ds(start, width), :].astype(BF16)
        pair = lambda a: [a[:, (h // 2) * LANES:(h // 2 + 1) * LANES] for h in range(SB_HEADS)]
        return ([qm_ref[j, h] for h in range(SB_HEADS)], pair(k_all), pair(v_all), ones[width],
                mask, r_ref.at[j], acc_ref.at[j])

    def place(j):
        q0 = (pl.program_id(1) * blocks_per_step + j) * tq
        below = q0 // kb
        return q0, below, pl.multiple_of(below * kb, kb)

    def write_out(j):
        rows = pl.ds(pl.multiple_of(j * tq, tq), tq)
        for p in range(SB_HEADS // 2):
            o_ref[0, rows, p * LANES:(p + 1) * LANES] = jnp.where(
                lane < SB_HEAD_DIM, acc_ref[j, 2 * p], acc_ref[j, 2 * p + 1]).astype(o_ref.dtype)

    def first_sweep(j, carry):
        q0, below, d0 = place(j)
        rows = pl.ds(pl.multiple_of(j * tq, tq), tq)
        for h in range(SB_HEADS):
            p = h // 2
            qp = q_ref[0, rows, p * LANES:(p + 1) * LANES]
            keep = (lane < SB_HEAD_DIM) if h % 2 == 0 else (lane >= SB_HEAD_DIM)
            qm_ref[j, h] = jnp.where(keep, qp, jnp.zeros_like(qp))
            r_ref[j, h] = jnp.zeros((tq, LANES), F32)
            acc_ref[j, h] = jnp.zeros((tq, LANES), F32)

        own = lambda: block(j, kn_ref, vn_ref, d0, kb, _key_before(tq, kb, d0, q0))
        old = lambda: block(j, ko_ref, vo_ref, 0, kb, old_rows)

        @pl.when(below == 0)
        def _():
            _sb_sweep([own(), old()])

        @pl.when(below == 1)
        def _():
            _sb_sweep([own(), block(j, kn_ref, vn_ref, 0, kb, None), old()])

        @pl.when(below >= 2)
        def _():
            _sb_sweep([own(), block(j, kn_ref, vn_ref, pl.multiple_of(d0 - wb, kb), wb, None)])
            reach = r_ref[j, 0]
            for h in range(1, SB_HEADS):
                reach = jnp.maximum(reach, r_ref[j, h])
            live_ref[...] = jnp.maximum(live_ref[...], reach)

        write_out(j)
        return carry

    def further_sweeps(j, carry):
        _, below, d0 = place(j)
        top = d0 - wb
        left = jnp.maximum(below - 2, 0)
        live = r_ref.at[j]
        alive = _sweep_while(left // 2, _sb_live(live), live, lambda t: _sb_sweep([block(
            j, kn_ref, vn_ref, pl.multiple_of(top - (t + 1) * wb, kb), wb, None)]))
        alive = _sweep_while(left % 2, alive, live,
                             lambda t: _sb_sweep([block(j, kn_ref, vn_ref, 0, kb, None)]))
        _sweep_while((below >= 2).astype(jnp.int32), alive, live,
                     lambda t: _sb_sweep([block(j, ko_ref, vo_ref, 0, kb, old_rows)]))
        write_out(j)
        return carry

    live_ref[...] = jnp.full(live_ref.shape, -jnp.inf, F32)
    lax.fori_loop(0, blocks_per_step, first_sweep, 0)

    @pl.when(jnp.max(live_ref[...]) >= F32_EXP2_UNDERFLOW)
    def _():
        lax.fori_loop(0, blocks_per_step, further_sweeps, 0)


def _sbd_kernel(q_ref, kn_ref, vn_ref, ck_ref, cv_ref, o_ref, acc_ref, r_ref, *, tq, n_old):
    kb, wb, hd = KEY_BLOCK, WIDE_BLOCK, SB_HEAD_DIM
    heads = range(SB_HEADS)
    n_tail, n_full = n_old % kb, n_old // kb
    n_wide = (n_full * kb) // wb
    rest = n_full * kb - n_wide * wb
    per_head = lambda a: [a[:, h * hd:(h + 1) * hd] for h in heads]

    q_ops = per_head(q_ref[0])
    for h in heads:
        r_ref[h] = jnp.zeros((tq, LANES), F32)
        acc_ref[h] = jnp.zeros((tq, hd), F32)

    assert kn_ref.shape[1] == kb
    _sb_sweep([(q_ops, per_head(kn_ref[0]), per_head(vn_ref[0]), _suffix_ones(kb),
                _key_before(tq, kb, 0, 0), r_ref, acc_ref)])
    alive = _sb_live(r_ref)

    def cached(start, width, valid):
        def load(ref, h):
            cols = ref[0, 0, h, :, pl.ds(start, valid)].astype(BF16)
            if valid < width:
                cols = jnp.concatenate([cols, jnp.zeros((hd, width - valid), BF16)], axis=1)
            return cols

        mask = None if valid == width else (
            lax.broadcasted_iota(jnp.int32, (tq, width), 1) < valid)
        _sb_sweep([(q_ops, [load(ck_ref, h) for h in heads], [load(cv_ref, h) for h in heads],
                    _suffix_ones(width), mask, r_ref, acc_ref)], token_minor=True)

    if n_tail:
        alive = _sweep_while(1, alive, r_ref, lambda t: cached(n_full * kb, kb, n_tail))
    if n_wide:
        alive = _sweep_while(n_wide, alive, r_ref, lambda t: cached(
            pl.multiple_of(rest + (n_wide - 1 - t) * wb, kb), wb, wb))
    if rest:
        _sweep_while(1, alive, r_ref, lambda t: cached(0, rest, rest))

    o_ref[0] = jnp.concatenate([acc_ref[h] for h in heads], axis=1).astype(o_ref.dtype)


def _sb_call(q, k_new, v_new, k_old, v_old, n_old, tq):
    s, lq, w = q.shape
    ln = k_new.shape[1]
    lo = k_old.shape[1]
    assert lq % tq == 0 and ln % KEY_BLOCK == 0 and n_old <= KEY_BLOCK <= lo
    assert tq <= KEY_BLOCK and KEY_BLOCK % tq == 0
    step = tq * SB_BLOCKS_PER_STEP if lq % (tq * SB_BLOCKS_PER_STEP) == 0 else tq
    old_map = (lambda b, i: (b, 0, 0)) if k_old.shape[0] == s else (lambda b, i: (0, 0, 0))
    return pl.pallas_call(
        functools.partial(_sb_kernel, tq=tq, n_old=n_old),
        grid=(s, lq // step),
        in_specs=[pl.BlockSpec((1, step, w), lambda b, i: (b, i, 0)),
                  pl.BlockSpec((1, ln, w), lambda b, i: (b, 0, 0)),
                  pl.BlockSpec((1, ln, w), lambda b, i: (b, 0, 0)),
                  pl.BlockSpec((1, lo, w), old_map),
                  pl.BlockSpec((1, lo, w), old_map)],
        out_specs=pl.BlockSpec((1, step, w), lambda b, i: (b, i, 0)),
        out_shape=jax.ShapeDtypeStruct((s, lq, w), BF16),
        scratch_shapes=[pltpu.VMEM((step // tq, SB_HEADS, tq, LANES), BF16),
                        pltpu.VMEM((step // tq, SB_HEADS, tq, LANES), F32),
                        pltpu.VMEM((step // tq, SB_HEADS, tq, LANES), F32),
                        pltpu.VMEM((tq, LANES), F32)],
        compiler_params=_params(("parallel", "parallel")),
        name="sb",
    )(q, k_new, v_new, k_old, v_old)


def _sbd_call(q, k_new, v_new, cache_k, cache_v):
    s, tq, w = q.shape
    n_old = cache_k.shape[-1]
    assert tq <= KEY_BLOCK and k_new.shape[1] == KEY_BLOCK
    new_spec = pl.BlockSpec((1, KEY_BLOCK, w), lambda b: (b, 0, 0))
    cache_spec = pl.BlockSpec((1, 1) + cache_k.shape[2:], lambda b: (0, b, 0, 0, 0))
    return pl.pallas_call(
        functools.partial(_sbd_kernel, tq=tq, n_old=n_old),
        grid=(s,),
        in_specs=[pl.BlockSpec((1, tq, w), lambda b: (b, 0, 0)), new_spec, new_spec,
                  cache_spec, cache_spec],
        out_specs=pl.BlockSpec((1, tq, w), lambda b: (b, 0, 0)),
        out_shape=jax.ShapeDtypeStruct((s, tq, w), BF16),
        scratch_shapes=[pltpu.VMEM((SB_HEADS, tq, SB_HEAD_DIM), F32),
                        pltpu.VMEM((SB_HEADS, tq, LANES), F32)],
        compiler_params=_params(("parallel",)),
        name="sb_decode",
    )(q, k_new, v_new, cache_k, cache_v)


def _pad_rows(x, rows):
    if x.shape[0] == rows:
        return x
    return jnp.concatenate([x, jnp.zeros((rows - x.shape[0], x.shape[1]), x.dtype)], axis=0)


def _hg_kernel(q_ref, k_ref, i_ref, lf_ref, s0_ref, o_ref, sout_ref, st_ref, *, chunk):
    c = pl.program_id(0)
    hd = HG_HEAD_DIM
    ns, n_chunks = q_ref.shape[0], q_ref.shape[1] // chunk
    streams = range(ns)
    units = [(g, b) for g in range(n_chunks) for b in streams]
    rows = lambda g: slice(g * chunk, (g + 1) * chunk)
    head_cols = [slice(h * hd, (h + 1) * hd) for h in range(HG_HEADS)]

    @pl.when(c == 0)
    def _():
        for b in streams:
            st_ref[b] = s0_ref[b if s0_ref.shape[0] == ns else 0]

    trow = lax.broadcasted_iota(jnp.int32, (chunk, chunk), 0)
    tcol = lax.broadcasted_iota(jnp.int32, (chunk, chunk), 1)
    causal = trow >= tcol

    token = lax.broadcasted_iota(jnp.int32, (chunk, lf_ref.shape[2]), 0)
    cum = {}
    for g, b in units:
        part = lf_ref[b, rows(g), :]
        shift = 1
        while shift < chunk:
            part = part + jnp.where(token >= shift, pltpu.roll(part, shift, axis=0), 0.0)
            shift *= 2
        cum[g, b] = part
    last = {u: cum[u][chunk - 1:chunk, :] for u in units}
    weakest = functools.reduce(jnp.minimum, last.values())
    safe = jnp.min(weakest) >= HG_SAFE_LOG_DECAY

    @pl.when(safe)
    def _():
        qd, kd, kl, iv, dl = {}, {}, {}, {}, {}
        for g, b in units:
            k = k_ref[b, rows(g), :]
            qd[g, b] = (q_ref[b, rows(g), :] * jnp.exp(cum[g, b])).astype(BF16)
            kd[g, b] = (k * jnp.exp(-cum[g, b])).astype(BF16)
            kl[g, b] = (k * jnp.exp(last[g, b] - cum[g, b])).astype(BF16)
            iv[g, b] = i_ref[b, rows(g), :].astype(BF16)
            dl[g, b] = jnp.exp(last[g, b])
        sc = {(u, h): lax.dot_general(qd[u][:, sl], kd[u][:, sl], _NT, preferred_element_type=F32)
              for u in units for h, sl in enumerate(head_cols)}
        grown = {(u, h): lax.dot_general(iv[u][:, sl], kl[u][:, sl], _TN,
                                         preferred_element_type=F32)
                 for u in units for h, sl in enumerate(head_cols)}
        within = {(u, h): jnp.dot(jnp.where(causal, sc[u, h], 0.0).astype(BF16), iv[u][:, sl],
                                  preferred_element_type=F32)
                  for u in units for h, sl in enumerate(head_cols)}
        for g in range(n_chunks):
            carried = {(b, h): lax.dot_general(qd[g, b][:, sl], st_ref[b, h].astype(BF16), _NT,
                                               preferred_element_type=F32)
                       for b in streams for h, sl in enumerate(head_cols)}
            for b in streams:
                for h, sl in enumerate(head_cols):
                    o_ref[b, rows(g), sl] = (within[(g, b), h] + carried[b, h]).astype(o_ref.dtype)
                    st_ref[b, h] = st_ref[b, h] * dl[g, b][:, sl] + grown[(g, b), h]

    @pl.when(jnp.logical_not(safe))
    def _():
        lane_t = lax.broadcasted_iota(jnp.int32, (hd, hd), 1)
        row_t = lax.broadcasted_iota(jnp.int32, (hd, hd), 0)

        def one_chunk(n, carry):
            b, r0 = n // n_chunks, pl.multiple_of((n % n_chunks) * chunk, chunk)
            for h, sl in enumerate(head_cols):
                load = lambda ref: ref[b, pl.ds(r0, chunk), sl].astype(F32)
                f_t = _pad_rows(jnp.exp(load(lf_ref)), hd).T
                k_t = _pad_rows(load(k_ref), hd).T
                q_t = _pad_rows(load(q_ref), hd).T
                i_p = _pad_rows(load(i_ref), hd)

                def token(t, sc, f_t=f_t, k_t=k_t, q_t=q_t, i_p=i_p):
                    s, o = sc
                    sel = lane_t == t
                    fc = jnp.sum(jnp.where(sel, f_t, 0.0), axis=1, keepdims=True)
                    kc = jnp.sum(jnp.where(sel, k_t, 0.0), axis=1, keepdims=True)
                    qc = jnp.sum(jnp.where(sel, q_t, 0.0), axis=1, keepdims=True)
                    i_row = jnp.sum(jnp.where(row_t == t, i_p, 0.0), axis=0, keepdims=True)
                    s = s * fc + kc * i_row
                    o = jnp.where(row_t == t, jnp.sum(qc * s, axis=0, keepdims=True), o)
                    return s, o

                s, o = lax.fori_loop(0, chunk, token,
                                     (st_ref[b, h].T, jnp.zeros((hd, hd), F32)))
                st_ref[b, h] = s.T
                o_ref[b, pl.ds(r0, chunk), sl] = o[:chunk].astype(o_ref.dtype)
            return carry

        lax.fori_loop(0, ns * n_chunks, one_chunk, 0)

    @pl.when(c == pl.num_programs(0) - 1)
    def _():
        sout_ref[...] = st_ref[...]


def _hg_call(q, k, iv, lf, s0t, chunk, chunks_per_step=1):
    s, l, w = q.shape
    rows = chunk * chunks_per_step if l % (chunk * chunks_per_step) == 0 else chunk
    assert l % rows == 0
    seq_spec = pl.BlockSpec((s, rows, w), lambda c: (0, c, 0))
    st_shape = (s, HG_HEADS, HG_HEAD_DIM, HG_HEAD_DIM)
    return pl.pallas_call(
        functools.partial(_hg_kernel, chunk=chunk),
        grid=(l // rows,),
        in_specs=[seq_spec, seq_spec, seq_spec, seq_spec,
                  pl.BlockSpec(s0t.shape, lambda c: (0, 0, 0, 0))],
        out_specs=[seq_spec, pl.BlockSpec(st_shape, lambda c: (0, 0, 0, 0))],
        out_shape=[jax.ShapeDtypeStruct((s, l, w), BF16), jax.ShapeDtypeStruct(st_shape, F32)],
        scratch_shapes=[pltpu.VMEM(st_shape, F32)],
        compiler_params=_params(("arbitrary",)),
        name="hgrn",
    )(q, k, iv, lf, s0t)


def _merge_branches(x, oa, ob, sg, gates, hn, wa, wb, wo):
    hd = HG_HEAD_DIM
    d = x.shape[1]
    heads = []
    for h in range(HG_HEADS):
        sl = slice(h * hd, (h + 1) * hd)
        heads.append(_rms(ob[:, sl].astype(F32), hn[:, sl]) * sg[:, sl].astype(F32))
    obn = jnp.concatenate(heads, axis=-1).astype(BF16)
    ma = jnp.dot(oa, wa, preferred_element_type=F32)
    mb = jnp.dot(obn, wb, preferred_element_type=F32)
    merged = gates[:, :d].astype(F32) * ma + gates[:, d:].astype(F32) * mb
    return x + jnp.dot(merged.astype(BF16), wo, preferred_element_type=F32)


def _post_kernel(x_ref, oa_ref, ob_ref, sg_ref, gates_ref, hn_ref, wa_ref, wb_ref, wo_ref,
                 n_ref, wg_ref, wu_ref, wd_ref, fn_ref, y_ref, acc_ref):
    x = jnp.concatenate(
        [_merge_branches(x_ref[r, :], oa_ref[r, :], ob_ref[r, :], sg_ref[r, :], gates_ref[r, :],
                         hn_ref[...], wa_ref[...], wb_ref[...], wo_ref[...])
         for r in _row_parts(x_ref.shape[0])], axis=0)
    x = _half_ffn(x, n_ref[...], wg_ref, wu_ref, wd_ref, acc_ref)
    y_ref[...] = _rms(x, fn_ref[...])


def _post_cast_kernel(x_ref, oa_ref, ob_ref, sg_ref, gates_ref, hn_ref, wa_ref, wb_ref, wo_ref,
                      n_ref, wg_ref, wu_ref, wd_ref, fn_ref,
                      y_ref, wab_ref, wbb_ref, wob_ref, wgb_ref, wub_ref, wdb_ref,
                      x2_ref, h_ref, acc_ref):
    c = pl.program_id(0)

    @pl.when(c == 0)
    def _():
        wa, wb, wo = (r[...].astype(BF16) for r in (wa_ref, wb_ref, wo_ref))
        for ref, wv in zip((wab_ref, wbb_ref, wob_ref), (wa, wb, wo)):
            ref[...] = wv
        x2 = _merge_branches(x_ref[...], oa_ref[...], ob_ref[...], sg_ref[...], gates_ref[...],
                             hn_ref[...], wa, wb, wo)
        x2_ref[...] = x2
        h_ref[...] = _rms(x2, n_ref[...]).astype(BF16)
        acc_ref[...] = jnp.zeros_like(acc_ref)

    @pl.when(c > 0)
    def _():
        acc_ref[...] += _cast_ffn_chunk(h_ref[...], (wg_ref, wu_ref, wd_ref),
                                        (wgb_ref, wub_ref, wdb_ref))

    @pl.when(c == pl.num_programs(0) - 1)
    def _():
        y_ref[...] = _rms(x2_ref[...] + 0.5 * acc_ref[...], fn_ref[...])


def _post_cast_call(x, oa, ob, sg, gates, hn, wa, wb, wo, norm, wg, wu, wd, fnorm):
    n, d = x.shape
    dff = wg.shape[1]
    assert dff % FF_CHUNK == 0
    chunk_specs = _ff_chunk_specs(wg, wd, lambda c: jnp.maximum(c - 1, 0))
    whole = lambda a: pl.BlockSpec(a.shape, lambda c: (0,) * a.ndim)
    return pl.pallas_call(
        _post_cast_kernel,
        grid=(dff // FF_CHUNK + 1,),
        in_specs=[whole(a) for a in (x, oa, ob, sg, gates)]
                 + [_const_spec(a.shape) for a in (hn, wa, wb, wo, norm)] + chunk_specs
                 + [_const_spec(fnorm.shape)],
        out_specs=[whole(x)] + [whole(a) for a in (wa, wb, wo)] + chunk_specs,
        out_shape=[jax.ShapeDtypeStruct((n, d), F32)]
                  + [jax.ShapeDtypeStruct(a.shape, BF16) for a in (wa, wb, wo, wg, wu, wd)],
        scratch_shapes=[pltpu.VMEM((n, d), F32), pltpu.VMEM((n, d), BF16),
                        pltpu.VMEM((n, d), F32)],
        compiler_params=_params(("arbitrary",)),
        name="post_cast",
    )(x, oa, ob, sg, gates, hn, wa, wb, wo, norm, wg, wu, wd, fnorm)


def _post_call(x, oa, ob, sg, gates, hn, wa, wb, wo, norm, wg, wu, wd, fnorm, tm):
    n, d = x.shape
    consts = [hn, wa, wb, wo, norm, wg, wu, wd, fnorm]
    return pl.pallas_call(
        _post_kernel,
        grid=(n // tm,),
        in_specs=[_rows_spec(tm, d), _rows_spec(tm, oa.shape[1]), _rows_spec(tm, ob.shape[1]),
                  _rows_spec(tm, sg.shape[1]), _rows_spec(tm, gates.shape[1])]
                 + [_const_spec(a.shape) for a in consts],
        out_specs=_rows_spec(tm, d),
        out_shape=jax.ShapeDtypeStruct((n, d), F32),
        scratch_shapes=[pltpu.VMEM((tm, d), F32)],
        compiler_params=_params(("parallel",)),
        name="post",
    )(x, oa, ob, sg, gates, *consts)


def _row_tile(n, target):
    tm = min(n, target)
    assert n % tm == 0 and tm % 8 == 0
    return tm


def kernel(x_prompt, x_sample, cache_sb_k, cache_sb_v, state_hgrn, meta_tokens, ffn1_norm, ffn1_w_gate, ffn1_w_up, ffn1_w_down, mix_norm, w_in, b_gate, hg_lb_logits, hg_out_norm, w_branch_a, w_branch_b, w_out, ffn2_norm, ffn2_w_gate, ffn2_w_up, ffn2_w_down, final_norm):
    depth = w_in.shape[0]
    assert depth == 1, "single-layer step"
    nb, seq, d = x_prompt.shape
    ns, t_new, _ = x_sample.shape
    n_meta = meta_tokens.shape[0]
    row = lambda a: a.reshape(1, -1)

    mix_norms = (row(mix_norm[0]), row(b_gate[0]), hg_lb_logits)

    n_frames = nb * seq
    tm_p = _row_tile(seq, 512)
    n_run = ns * t_new
    small = jnp.concatenate([x_sample.reshape(n_run, d), meta_tokens.astype(F32)], axis=0)
    x1_s, *f1 = _ffn_cast_call(small, row(ffn1_norm[0]), ffn1_w_gate[0], ffn1_w_up[0],
                               ffn1_w_down[0])
    (q_s, kb_s, vb_s, hq_s, hk_s, hi_s, hlf_s, sg_s, gates_s, k_s, v_s, meta_kt, meta_vt,
     w_in_b) = _mix_small_call(x1_s, mix_norms[0], w_in[0], *mix_norms[1:], n_run, n_meta)
    run = lambda a: a[:n_run]
    meta = lambda a: a[n_run:]
    token_minor = lambda a: jnp.transpose(a, (0, 1, 3, 4, 2))
    token_major = lambda a: jnp.transpose(a, (0, 1, 4, 2, 3))
    seq3 = lambda a, s: a.reshape(s, -1, a.shape[-1])
    rows = lambda a: a.reshape(-1, a.shape[-1])
    pad_to = lambda a, n: jnp.pad(a, ((0, 0), (0, n - a.shape[1]), (0, 0)))
    hg = lambda sel, s, arrs: [seq3(sel(a), s) for a in arrs]

    oa_s = _sbd_call(seq3(run(q_s), ns), pad_to(seq3(run(kb_s), ns), KEY_BLOCK),
                     pad_to(seq3(run(vb_s), ns), KEY_BLOCK),
                     token_minor(cache_sb_k), token_minor(cache_sb_v))
    ob_s, st_s = _hg_call(*hg(run, ns, (hq_s, hk_s, hi_s, hlf_s)),
                          jnp.swapaxes(state_hgrn[0].astype(F32), -1, -2), t_new)
    post_norms = (row(ffn2_norm[0]), row(final_norm))
    y_s, wa, wb, wo, *f2 = _post_cast_call(
        run(x1_s), rows(oa_s), rows(ob_s), run(sg_s), run(gates_s), row(hg_out_norm[0]),
        w_branch_a[0], w_branch_b[0], w_out[0], post_norms[0],
        ffn2_w_gate[0], ffn2_w_up[0], ffn2_w_down[0], post_norms[1])

    zero_state = jnp.zeros((1, HG_HEADS, HG_HEAD_DIM, HG_HEAD_DIM), F32)
    _, st_meta = _hg_call(*hg(meta, 1, (hq_s, hk_s, hi_s, hlf_s)), zero_state, n_meta)
    x1_p = _ffn_call(x_prompt.reshape(n_frames, d), row(ffn1_norm[0]), *f1,
                     _row_tile(n_frames, FFN_TILE))
    (q_p, kb_p, vb_p, hq_p, hk_p, hi_p, hlf_p, sg_p, gates_p, kt_full, vt_full) = (
        _mix_frames_call(x1_p.reshape(nb, seq, d), mix_norms[0], w_in_b, *mix_norms[1:],
                         meta_kt, meta_vt, tm_p, n_meta))
    oa_p = _sb_call(q_p, kb_p, vb_p,
                    pad_to(meta(kb_s)[None], KEY_BLOCK), pad_to(meta(vb_s)[None], KEY_BLOCK),
                    n_meta, KEY_BLOCK)
    ob_p, st_p = _hg_call(hq_p, hk_p, hi_p, hlf_p, st_meta, HG_CHUNK, HG_CHUNKS_PER_STEP)
    y_p = _post_call(x1_p, rows(oa_p), rows(ob_p), rows(sg_p), rows(gates_p),
                     row(hg_out_norm[0]), wa, wb, wo, post_norms[0], *f2, post_norms[1], tm_p)

    heads = lambda a: a.reshape(1, ns, t_new, SB_HEADS, SB_HEAD_DIM)
    return (y_p.reshape(nb, seq, d), y_s.reshape(ns, t_new, d),
            token_major(kt_full), token_major(vt_full),
            jnp.swapaxes(st_p, -1, -2)[None],
            heads(run(k_s)), heads(run(v_s)),
            jnp.swapaxes(st_s, -1, -2)[None])
```

```python
import functools

import jax
import jax.numpy as jnp
from jax import lax
from jax.experimental import pallas as pl
from jax.experimental.pallas import tpu as pltpu

F32 = jnp.float32
BF16 = jnp.bfloat16
EPS = 1e-6
SB_HEADS = 8
SB_HEAD_DIM = 64
SB_WIDTH = SB_HEADS * SB_HEAD_DIM
HG_HEADS = 4
HG_HEAD_DIM = 128
HG_WIDTH = HG_HEADS * HG_HEAD_DIM
LANES = 128
FF_CHUNK = 256
ROW_PART = 256
FFN_TILE, FFN_ROW_PART = 1024, 512
KEY_BLOCK = 128
WIDE_BLOCK = 256
SB_BLOCKS_PER_STEP = 8
F32_EXP2_UNDERFLOW = -152.0
LOG2_E = 1.4426950408889634
HG_CHUNK = 128
HG_CHUNKS_PER_STEP = 4
HG_SAFE_LOG_DECAY = -60.0
VMEM_LIMIT_BYTES = 56 * 1024 * 1024

_NN = (((1,), (0,)), ((), ()))
_NT = (((1,), (1,)), ((), ()))
_TN = (((0,), (0,)), ((), ()))


def _const_spec(shape):
    nd = len(shape)
    return pl.BlockSpec(shape, lambda *_: (0,) * nd, pipeline_mode=pl.Buffered(1))


def _rows_spec(tm, width):
    return pl.BlockSpec((tm, width), lambda i: (i, 0))


def _params(sem):
    return pltpu.CompilerParams(dimension_semantics=sem, vmem_limit_bytes=VMEM_LIMIT_BYTES)


def _row_parts(rows, part=ROW_PART):
    if rows % part or rows == part:
        return [slice(0, rows)]
    return [slice(r, r + part) for r in range(0, rows, part)]


def _rms(x, g):
    return x * lax.rsqrt(jnp.mean(x * x, axis=-1, keepdims=True) + EPS) * g


def _ffn_chunk(h, wg, wu, wd):
    gate = jnp.dot(h, wg, preferred_element_type=F32)
    up = jnp.dot(h, wu, preferred_element_type=F32)
    act = (gate * jax.nn.sigmoid(gate) * up).astype(BF16)
    return jnp.dot(act, wd, preferred_element_type=F32)


def _half_ffn(x, g, wg_ref, wu_ref, wd_ref, acc_ref):
    h = _rms(x, g).astype(BF16)
    dff = wg_ref.shape[1]
    assert dff % FF_CHUNK == 0
    for c in range(dff // FF_CHUNK):
        cols = slice(c * FF_CHUNK, (c + 1) * FF_CHUNK)
        contrib = _ffn_chunk(h, wg_ref[:, cols], wu_ref[:, cols], wd_ref[cols, :])
        if c == 0:
            acc_ref[...] = contrib
        else:
            acc_ref[...] += contrib
    return x + 0.5 * acc_ref[...]


def _cast_ffn_chunk(h, w32_refs, w16_refs):
    wg, wu, wd = (r[...].astype(BF16) for r in w32_refs)
    for ref, wv in zip(w16_refs, (wg, wu, wd)):
        ref[...] = wv
    return _ffn_chunk(h, wg, wu, wd)


def _ff_chunk_specs(wg, wd, chunk_of_step):
    col = pl.BlockSpec((wg.shape[0], FF_CHUNK), lambda c: (0, chunk_of_step(c)))
    row = pl.BlockSpec((FF_CHUNK, wd.shape[1]), lambda c: (chunk_of_step(c), 0))
    return [col, col, row]


def _ffn_kernel(x_ref, n_ref, wg_ref, wu_ref, wd_ref, o_ref, acc_ref):
    for r in _row_parts(x_ref.shape[0], FFN_ROW_PART):
        o_ref[r, :] = _half_ffn(x_ref[r, :], n_ref[...], wg_ref, wu_ref, wd_ref, acc_ref.at[r, :])


def _ffn_cast_kernel(x_ref, n_ref, wg_ref, wu_ref, wd_ref, o_ref, wgb_ref, wub_ref, wdb_ref,
                     h_ref, acc_ref):
    c = pl.program_id(0)

    @pl.when(c == 0)
    def _():
        h_ref[...] = _rms(x_ref[...], n_ref[...]).astype(BF16)
        acc_ref[...] = jnp.zeros_like(acc_ref)

    acc_ref[...] += _cast_ffn_chunk(h_ref[...], (wg_ref, wu_ref, wd_ref),
                                    (wgb_ref, wub_ref, wdb_ref))

    @pl.when(c == pl.num_programs(0) - 1)
    def _():
        o_ref[...] = x_ref[...] + 0.5 * acc_ref[...]


def _ffn_cast_call(x, norm, wg, wu, wd):
    n, d = x.shape
    dff = wg.shape[1]
    assert dff % FF_CHUNK == 0
    chunk_specs = _ff_chunk_specs(wg, wd, lambda c: c)
    whole = pl.BlockSpec((n, d), lambda c: (0, 0))
    return pl.pallas_call(
        _ffn_cast_kernel,
        grid=(dff // FF_CHUNK,),
        in_specs=[whole, _const_spec(norm.shape)] + chunk_specs,
        out_specs=[whole] + chunk_specs,
        out_shape=[jax.ShapeDtypeStruct((n, d), F32)]
                  + [jax.ShapeDtypeStruct(a.shape, BF16) for a in (wg, wu, wd)],
        scratch_shapes=[pltpu.VMEM((n, d), BF16), pltpu.VMEM((n, d), F32)],
        compiler_params=_params(("arbitrary",)),
        name="ffn_cast",
    )(x, norm, wg, wu, wd)


def _ffn_call(x, norm, wg, wu, wd, tm):
    n, d = x.shape
    return pl.pallas_call(
        _ffn_kernel,
        grid=(n // tm,),
        in_specs=[_rows_spec(tm, d), _const_spec(norm.shape), _const_spec(wg.shape),
                  _const_spec(wu.shape), _const_spec(wd.shape)],
        out_specs=_rows_spec(tm, d),
        out_shape=jax.ShapeDtypeStruct((n, d), F32),
        scratch_shapes=[pltpu.VMEM((tm, d), F32)],
        compiler_params=_params(("parallel",)),
        name="ffn",
    )(x, norm, wg, wu, wd)


def _forget_lower_bound(lbl_ref):
    logits = lbl_ref[...]
    l0, l1 = logits[0:1], logits[1:2]
    m = jnp.maximum(l0, l1)
    e0, e1 = jnp.exp(l0 - m), jnp.exp(l1 - m)
    p0, p1 = e0 / (e0 + e1), e1 / (e0 + e1)
    return (p0 + p1) - p0


_N_MIX_GROUPS = 11
_MIX_GROUP_ORDER = (7, 8, 9, 10, 3, 6, 0, 1, 2, 4, 5)


def _mix_group(j, p, r, lb, bg_ref, outs):
    q_ref, kb_ref, vb_ref, hq_ref, hk_ref, hi_ref, hlf_ref, sg_ref, gates_ref = outs
    if j == 0:
        q_ref[r, :] = (p * (SB_HEAD_DIM ** -0.5 * LOG2_E)).astype(BF16)
    elif j == 1:
        kb_ref[r, :] = p.astype(BF16)
    elif j == 2:
        vb_ref[r, :] = p.astype(BF16)
    elif j == 3:
        f = lb + (1.0 - lb) * jax.nn.sigmoid(p)
        hk_ref[r, :] = (1.0 - f).astype(hk_ref.dtype)
        hlf_ref[r, :] = jnp.log(f)
    elif j == 4:
        hi_ref[r, :] = p.astype(hi_ref.dtype)
    elif j == 5:
        hq_ref[r, :] = p.astype(hq_ref.dtype)
    elif j == 6:
        sg_ref[r, :] = (p * jax.nn.sigmoid(p)).astype(sg_ref.dtype)
    else:
        cols = slice((j - 7) * SB_WIDTH, (j - 6) * SB_WIDTH)
        gates_ref[r, cols] = jax.nn.sigmoid(p + bg_ref[:, cols]).astype(gates_ref.dtype)


def _mix_body(x_ref, n_ref, win_ref, bg_ref, lbl_ref, *outs):
    w = SB_WIDTH
    lb = _forget_lower_bound(lbl_ref)
    kv = ([], [])
    for r in _row_parts(x_ref.shape[0]):
        h = _rms(x_ref[r, :], n_ref[...]).astype(BF16)
        for j in _MIX_GROUP_ORDER:
            p = jnp.dot(h, win_ref[:, j * w:(j + 1) * w], preferred_element_type=F32)
            _mix_group(j, p, r, lb, bg_ref, outs)
            if j in (1, 2):
                kv[j - 1].append(p)
    return jnp.concatenate(kv[0], axis=0), jnp.concatenate(kv[1], axis=0)


def _mix_small_kernel(*refs, n_run, n_meta):
    x_ref, n_ref, win_ref, bg_ref, lbl_ref = refs[:5]
    outs, (k_ref, v_ref, kt_ref, vt_ref, winb_ref, h_ref) = refs[5:14], refs[14:]
    c = pl.program_id(0)

    @pl.when(c == 0)
    def _():
        h_ref[...] = _rms(x_ref[...], n_ref[...]).astype(BF16)

    wb = win_ref[...].astype(BF16)
    winb_ref[...] = wb
    p = jnp.dot(h_ref[...], wb, preferred_element_type=F32)
    lb = _forget_lower_bound(lbl_ref)
    for j in range(_N_MIX_GROUPS):
        @pl.when(c == j)
        def _(j=j):
            _mix_group(j, p, slice(None), lb, bg_ref, outs)
            for group, rows_ref, cols_ref in ((1, k_ref, kt_ref), (2, v_ref, vt_ref)):
                if j == group:
                    rows_ref[...] = p
                    cols_ref[...] = _pad_rows(p[n_run:n_run + n_meta], LANES).T


def _mix_frames_kernel(*refs, tm, n_meta, n_tiles):
    x_ref, consts, mk_ref, mv_ref = refs[0], refs[1:5], refs[5], refs[6]
    row_outs, (kt_ref, vt_ref, ck_ref, cv_ref) = refs[7:16], refs[16:]
    t = pl.program_id(1)
    shape = (SB_HEADS, SB_HEAD_DIM, tm)

    @pl.when(t == 0)
    def _():
        ck_ref[...] = mk_ref[...]
        cv_ref[...] = mv_ref[...]

    @pl.when(t < n_tiles)
    def _():
        k, v = _mix_body(x_ref.at[0], *consts, *[r.at[0] for r in row_outs])
        for rows, out_ref, carry_ref in ((k, kt_ref, ck_ref), (v, vt_ref, cv_ref)):
            cols = rows.T
            late = jnp.concatenate([carry_ref[:, :n_meta], cols[:, :tm - n_meta]], axis=1)
            out_ref[0, 0] = late.reshape(shape)
            carry_ref[:, :n_meta] = cols[:, tm - n_meta:]

    @pl.when(t == n_tiles)
    def _():
        for out_ref, carry_ref in ((kt_ref, ck_ref), (vt_ref, cv_ref)):
            tail = jnp.concatenate(
                [carry_ref[:, :n_meta], jnp.zeros((SB_WIDTH, tm - n_meta), F32)], axis=1)
            out_ref[0, 0] = tail.reshape(shape)


_MIX_WIDTHS = [(SB_WIDTH, BF16)] * 6 + [(SB_WIDTH, F32), (SB_WIDTH, BF16)]


def _mix_small_call(x, norm, w_in, b_gate, lb_logits, n_run, n_meta):
    n, d = x.shape
    assert w_in.shape[1] == _N_MIX_GROUPS * SB_WIDTH
    widths = _MIX_WIDTHS + [(2 * d, BF16), (SB_WIDTH, F32), (SB_WIDTH, F32)]
    meta_t = jax.ShapeDtypeStruct((SB_WIDTH, LANES), F32)
    whole = lambda shape: pl.BlockSpec(shape, lambda c: (0, 0))
    group = pl.BlockSpec((d, SB_WIDTH), lambda c: (0, c))
    return pl.pallas_call(
        functools.partial(_mix_small_kernel, n_run=n_run, n_meta=n_meta),
        grid=(_N_MIX_GROUPS,),
        in_specs=[whole((n, d)), _const_spec(norm.shape), group,
                  _const_spec(b_gate.shape), _const_spec(lb_logits.shape)],
        out_specs=[whole((n, wd)) for wd, _ in widths] + [whole(meta_t.shape)] * 2 + [group],
        out_shape=[jax.ShapeDtypeStruct((n, wd), dt) for wd, dt in widths] + [meta_t] * 2
                  + [jax.ShapeDtypeStruct(w_in.shape, BF16)],
        scratch_shapes=[pltpu.VMEM((n, d), BF16)],
        compiler_params=_params(("arbitrary",)),
        name="mix_small",
    )(x, norm, w_in, b_gate, lb_logits)


def _mix_frames_call(x, norm, w_in, b_gate, lb_logits, meta_kt, meta_vt, tm, n_meta):
    nb, seq, d = x.shape
    assert seq % tm == 0 and n_meta < tm
    n_tiles = seq // tm
    widths = _MIX_WIDTHS + [(2 * d, BF16)]
    rows_map = lambda b, t: (b, jnp.minimum(t, n_tiles - 1), 0)
    const2 = lambda shape: pl.BlockSpec(shape, lambda b, t: (0, 0), pipeline_mode=pl.Buffered(1))
    kv_shape = jax.ShapeDtypeStruct((1, nb, SB_HEADS, SB_HEAD_DIM, n_meta + seq), F32)
    kv_spec = pl.BlockSpec((1, 1, SB_HEADS, SB_HEAD_DIM, tm), lambda b, t: (0, b, 0, 0, t))
    carry = pltpu.VMEM((SB_WIDTH, LANES), F32)
    return pl.pallas_call(
        functools.partial(_mix_frames_kernel, tm=tm, n_meta=n_meta, n_tiles=n_tiles),
        grid=(nb, n_tiles + 1),
        in_specs=[pl.BlockSpec((1, tm, d), rows_map)]
                 + [const2(a.shape) for a in (norm, w_in, b_gate, lb_logits, meta_kt, meta_vt)],
        out_specs=[pl.BlockSpec((1, tm, wd), rows_map) for wd, _ in widths] + [kv_spec] * 2,
        out_shape=[jax.ShapeDtypeStruct((nb, seq, wd), dt) for wd, dt in widths] + [kv_shape] * 2,
        scratch_shapes=[carry, carry],
        compiler_params=_params(("arbitrary", "arbitrary")),
        name="mix_frames",
    )(x, norm, w_in, b_gate, lb_logits, meta_kt, meta_vt)


def _suffix_ones(width):
    r = lax.broadcasted_iota(jnp.int32, (width, width), 0)
    c = lax.broadcasted_iota(jnp.int32, (width, width), 1)
    return jnp.where(r >= c, 1.0, 0.0).astype(BF16)


def _key_before(tq, width, k0, q_first):
    row = lax.broadcasted_iota(jnp.int32, (tq, width), 0)
    col = lax.broadcasted_iota(jnp.int32, (tq, width), 1)
    return col + k0 < row + q_first


def _sb_scores(blocks, token_minor):
    k_dims = _NN if token_minor else _NT
    return [[lax.dot_general(q_ops[h], k_ops[h], k_dims, preferred_element_type=F32)
             for h in range(SB_HEADS)] for q_ops, k_ops, *_ in blocks]


def _sb_stay_sums(blocks, zs):
    cums = []
    for z_blk, (_, _, _, ones, mask, _, _) in zip(zs, blocks):
        stays = []
        for z in z_blk:
            nz = -z
            ls = jnp.minimum(nz, 0.0) - jnp.log2(1.0 + jnp.exp2(jnp.minimum(z, nz)))
            if mask is not None:
                ls = jnp.where(mask, ls, 0.0)
            stays.append(ls.astype(BF16))
        cums.append([jnp.dot(st, ones, preferred_element_type=F32) for st in stays])
    return cums


def _sb_apply(blocks, zs, cums, token_minor):
    heads = range(SB_HEADS)
    v_dims = _NT if token_minor else _NN
    ws = []
    for z_blk, cum_blk, (q_ops, _, _, ones, mask, r_ref, _) in zip(zs, cums, blocks):
        ws.append([])
        for h in heads:
            r = r_ref[h]
            cr = cum_blk[h] + jnp.concatenate([r] * (ones.shape[1] // LANES), axis=1)
            a = jnp.exp2(z_blk[h] + cr)
            if mask is not None:
                a = jnp.where(mask, a, 0.0)
            ws[-1].append(a.astype(BF16))
            r_ref[h] = jnp.broadcast_to(cr[:, 0:1], (q_ops[h].shape[0], LANES))
    for w_blk, (_, _, v_ops, _, _, _, acc_ref) in zip(ws, blocks):
        for h in heads:
            acc_ref[h] += lax.dot_general(w_blk[h], v_ops[h], v_dims, preferred_element_type=F32)


def _sb_sweep(blocks, token_minor=False):
    zs = _sb_scores(blocks, token_minor)
    _sb_apply(blocks, zs, _sb_stay_sums(blocks, zs), token_minor)


def _sb_sweep_two(first, second):
    z1 = _sb_scores(first, False)
    c1 = _sb_stay_sums(first, z1)
    z2 = _sb_scores(second, False)
    _sb_apply(first, z1, c1, False)
    _sb_apply(second, z2, _sb_stay_sums(second, z2), False)


def _sb_live(r_ref):
    m = r_ref[0]
    for h in range(1, SB_HEADS):
        m = jnp.maximum(m, r_ref[h])
    return (jnp.max(m) >= F32_EXP2_UNDERFLOW).astype(jnp.int32)


def _sweep_while(n_blocks, alive, r_ref, block, first=0):
    def body(carry):
        t, _ = carry
        block(t)
        return t + 1, _sb_live(r_ref)

    _, alive = lax.while_loop(lambda c: jnp.logical_and(c[0] < n_blocks, c[1] > 0),
                              body, (jnp.int32(first), alive))
    return alive


def _sb_kernel(q_ref, kn_ref, vn_ref, ko_ref, vo_ref, o_ref, qm_ref, acc_ref, r_ref, live_ref, *,
               tq, n_old):
    kb, wb = KEY_BLOCK, WIDE_BLOCK
    blocks_per_step = q_ref.shape[1] // tq
    lane = lax.broadcasted_iota(jnp.int32, (tq, LANES), 1)
    ones = {width: _suffix_ones(width) for width in (kb, wb)}
    old_rows = lax.broadcasted_iota(jnp.int32, (tq, kb), 1) < n_old

    def block(j, k_ref, v_ref, start, width, mask):
        k_all = k_ref[0, pl.ds(start, width), :].astype(BF16)
        v_all = v_ref[0, pl.ds(start, width), :].astype(BF16)
        pair = lambda a: [a[:, (h // 2) * LANES:(h // 2 + 1) * LANES] for h in range(SB_HEADS)]
        return ([qm_ref[j, h] for h in range(SB_HEADS)], pair(k_all), pair(v_all), ones[width],
                mask, r_ref.at[j], acc_ref.at[j])

    def place(j):
        q0 = (pl.program_id(1) * blocks_per_step + j) * tq
        below = q0 // kb
        return q0, below, pl.multiple_of(below * kb, kb)

    def write_out(j):
        rows = pl.ds(pl.multiple_of(j * tq, tq), tq)
        for p in range(SB_HEADS // 2):
            o_ref[0, rows, p * LANES:(p + 1) * LANES] = jnp.where(
                lane < SB_HEAD_DIM, acc_ref[j, 2 * p], acc_ref[j, 2 * p + 1]).astype(o_ref.dtype)

    def start(j):
        rows = pl.ds(pl.multiple_of(j * tq, tq), tq)
        for h in range(SB_HEADS):
            p = h // 2
            qp = q_ref[0, rows, p * LANES:(p + 1) * LANES]
            keep = (lane < SB_HEAD_DIM) if h % 2 == 0 else (lane >= SB_HEAD_DIM)
            qm_ref[j, h] = jnp.where(keep, qp, jnp.zeros_like(qp))
            r_ref[j, h] = jnp.zeros((tq, LANES), F32)
            acc_ref[j, h] = jnp.zeros((tq, LANES), F32)

    def own(j):
        q0, _, d0 = place(j)
        return block(j, kn_ref, vn_ref, d0, kb, _key_before(tq, kb, d0, q0))

    def own_and_under(j):
        _, _, d0 = place(j)
        return [own(j), block(j, kn_ref, vn_ref, pl.multiple_of(d0 - wb, kb), wb, None)]

    def note_reach(j):
        reach = r_ref[j, 0]
        for h in range(1, SB_HEADS):
            reach = jnp.maximum(reach, r_ref[j, h])
        live_ref[...] = jnp.maximum(live_ref[...], reach)

    def first_sweep(j, carry):
        _, below, _ = place(j)
        start(j)
        old = lambda: block(j, ko_ref, vo_ref, 0, kb, old_rows)

        @pl.when(below == 0)
        def _():
            _sb_sweep([own(j), old()])

        @pl.when(below == 1)
        def _():
            _sb_sweep([own(j), block(j, kn_ref, vn_ref, 0, kb, None), old()])

        @pl.when(below >= 2)
        def _():
            _sb_sweep(own_and_under(j))
            note_reach(j)

        write_out(j)
        return carry

    def first_sweep_two(jj, carry):
        j0, j1 = 2 * jj, 2 * jj + 1
        _, below, _ = place(j0)
        start(j0)
        start(j1)
        old = lambda j: block(j, ko_ref, vo_ref, 0, kb, old_rows)

        @pl.when(below == 0)
        def _():
            _sb_sweep([own(j0), old(j0)])
            _sb_sweep([own(j1), block(j1, kn_ref, vn_ref, 0, kb, None), old(j1)])

        @pl.when(below >= 2)
        def _():
            _sb_sweep_two(own_and_under(j0), own_and_under(j1))
            note_reach(j0)
            note_reach(j1)

        write_out(j0)
        write_out(j1)
        return carry

    def further_sweeps(j, carry):
        _, below, d0 = place(j)
        top = d0 - wb
        left = jnp.maximum(below - 2, 0)
        live = r_ref.at[j]
        alive = _sweep_while(left // 2, _sb_live(live), live, lambda t: _sb_sweep([block(
            j, kn_ref, vn_ref, pl.multiple_of(top - (t + 1) * wb, kb), wb, None)]))
        alive = _sweep_while(left % 2, alive, live,
                             lambda t: _sb_sweep([block(j, kn_ref, vn_ref, 0, kb, None)]))
        _sweep_while((below >= 2).astype(jnp.int32), alive, live,
                     lambda t: _sb_sweep([block(j, ko_ref, vo_ref, 0, kb, old_rows)]))
        write_out(j)
        return carry

    live_ref[...] = jnp.full(live_ref.shape, -jnp.inf, F32)
    if blocks_per_step % 2 == 0 and KEY_BLOCK == tq:
        lax.fori_loop(0, blocks_per_step // 2, first_sweep_two, 0)
    else:
        lax.fori_loop(0, blocks_per_step, first_sweep, 0)

    @pl.when(jnp.max(live_ref[...]) >= F32_EXP2_UNDERFLOW)
    def _():
        lax.fori_loop(0, blocks_per_step, further_sweeps, 0)


def _sbd_kernel(q_ref, kn_ref, vn_ref, ck_ref, cv_ref, o_ref, acc_ref, r_ref, *, tq, n_old):
    kb, wb, hd = KEY_BLOCK, WIDE_BLOCK, SB_HEAD_DIM
    heads = range(SB_HEADS)
    n_tail, n_full = n_old % kb, n_old // kb
    n_wide = (n_full * kb) // wb
    rest = n_full * kb - n_wide * wb
    per_head = lambda a: [a[:, h * hd:(h + 1) * hd] for h in heads]

    q_ops = per_head(q_ref[0])
    for h in heads:
        r_ref[h] = jnp.zeros((tq, LANES), F32)
        acc_ref[h] = jnp.zeros((tq, hd), F32)

    assert kn_ref.shape[1] == kb
    _sb_sweep([(q_ops, per_head(kn_ref[0]), per_head(vn_ref[0]), _suffix_ones(kb),
                _key_before(tq, kb, 0, 0), r_ref, acc_ref)])
    alive = _sb_live(r_ref)

    def cached(start, width, valid):
        def load(ref, h):
            cols = ref[0, 0, h, :, pl.ds(start, valid)].astype(BF16)
            if valid < width:
                cols = jnp.concatenate([cols, jnp.zeros((hd, width - valid), BF16)], axis=1)
            return cols

        mask = None if valid == width else (
            lax.broadcasted_iota(jnp.int32, (tq, width), 1) < valid)
        _sb_sweep([(q_ops, [load(ck_ref, h) for h in heads], [load(cv_ref, h) for h in heads],
                    _suffix_ones(width), mask, r_ref, acc_ref)], token_minor=True)

    if n_tail:
        alive = _sweep_while(1, alive, r_ref, lambda t: cached(n_full * kb, kb, n_tail))
    if n_wide:
        alive = _sweep_while(n_wide, alive, r_ref, lambda t: cached(
            pl.multiple_of(rest + (n_wide - 1 - t) * wb, kb), wb, wb))
    if rest:
        _sweep_while(1, alive, r_ref, lambda t: cached(0, rest, rest))

    o_ref[0] = jnp.concatenate([acc_ref[h] for h in heads], axis=1).astype(o_ref.dtype)


def _sb_call(q, k_new, v_new, k_old, v_old, n_old, tq):
    s, lq, w = q.shape
    ln = k_new.shape[1]
    lo = k_old.shape[1]
    assert lq % tq == 0 and ln % KEY_BLOCK == 0 and n_old <= KEY_BLOCK <= lo
    assert tq <= KEY_BLOCK and KEY_BLOCK % tq == 0
    step = tq * SB_BLOCKS_PER_STEP if lq % (tq * SB_BLOCKS_PER_STEP) == 0 else tq
    old_map = (lambda b, i: (b, 0, 0)) if k_old.shape[0] == s else (lambda b, i: (0, 0, 0))
    return pl.pallas_call(
        functools.partial(_sb_kernel, tq=tq, n_old=n_old),
        grid=(s, lq // step),
        in_specs=[pl.BlockSpec((1, step, w), lambda b, i: (b, i, 0)),
                  pl.BlockSpec((1, ln, w), lambda b, i: (b, 0, 0)),
                  pl.BlockSpec((1, ln, w), lambda b, i: (b, 0, 0)),
                  pl.BlockSpec((1, lo, w), old_map),
                  pl.BlockSpec((1, lo, w), old_map)],
        out_specs=pl.BlockSpec((1, step, w), lambda b, i: (b, i, 0)),
        out_shape=jax.ShapeDtypeStruct((s, lq, w), BF16),
        scratch_shapes=[pltpu.VMEM((step // tq, SB_HEADS, tq, LANES), BF16),
                        pltpu.VMEM((step // tq, SB_HEADS, tq, LANES), F32),
                        pltpu.VMEM((step // tq, SB_HEADS, tq, LANES), F32),
                        pltpu.VMEM((tq, LANES), F32)],
        compiler_params=_params(("parallel", "parallel")),
        name="sb",
    )(q, k_new, v_new, k_old, v_old)


def _sbd_call(q, k_new, v_new, cache_k, cache_v):
    s, tq, w = q.shape
    n_old = cache_k.shape[-1]
    assert tq <= KEY_BLOCK and k_new.shape[1] == KEY_BLOCK
    new_spec = pl.BlockSpec((1, KEY_BLOCK, w), lambda b: (b, 0, 0))
    cache_spec = pl.BlockSpec((1, 1) + cache_k.shape[2:], lambda b: (0, b, 0, 0, 0))
    return pl.pallas_call(
        functools.partial(_sbd_kernel, tq=tq, n_old=n_old),
        grid=(s,),
        in_specs=[pl.BlockSpec((1, tq, w), lambda b: (b, 0, 0)), new_spec, new_spec,
                  cache_spec, cache_spec],
        out_specs=pl.BlockSpec((1, tq, w), lambda b: (b, 0, 0)),
        out_shape=jax.ShapeDtypeStruct((s, tq, w), BF16),
        scratch_shapes=[pltpu.VMEM((SB_HEADS, tq, SB_HEAD_DIM), F32),
                        pltpu.VMEM((SB_HEADS, tq, LANES), F32)],
        compiler_params=_params(("parallel",)),
        name="sb_decode",
    )(q, k_new, v_new, cache_k, cache_v)


def _pad_rows(x, rows):
    if x.shape[0] == rows:
        return x
    return jnp.concatenate([x, jnp.zeros((rows - x.shape[0], x.shape[1]), x.dtype)], axis=0)


def _hg_kernel(q_ref, k_ref, i_ref, lf_ref, s0_ref, o_ref, sout_ref, st_ref, *, chunk):
    c = pl.program_id(0)
    hd = HG_HEAD_DIM
    ns, n_chunks = q_ref.shape[0], q_ref.shape[1] // chunk
    streams = range(ns)
    units = [(g, b) for g in range(n_chunks) for b in streams]
    rows = lambda g: slice(g * chunk, (g + 1) * chunk)
    head_cols = [slice(h * hd, (h + 1) * hd) for h in range(HG_HEADS)]

    @pl.when(c == 0)
    def _():
        for b in streams:
            st_ref[b] = s0_ref[b if s0_ref.shape[0] == ns else 0]

    trow = lax.broadcasted_iota(jnp.int32, (chunk, chunk), 0)
    tcol = lax.broadcasted_iota(jnp.int32, (chunk, chunk), 1)
    causal = trow >= tcol

    token = lax.broadcasted_iota(jnp.int32, (chunk, lf_ref.shape[2]), 0)
    cum = {}
    for g, b in units:
        part = lf_ref[b, rows(g), :]
        shift = 1
        while shift < chunk:
            part = part + jnp.where(token >= shift, pltpu.roll(part, shift, axis=0), 0.0)
            shift *= 2
        cum[g, b] = part
    last = {u: cum[u][chunk - 1:chunk, :] for u in units}
    weakest = functools.reduce(jnp.minimum, last.values())
    safe = jnp.min(weakest) >= HG_SAFE_LOG_DECAY

    @pl.when(safe)
    def _():
        qd, kd, kl, iv, dl = {}, {}, {}, {}, {}
        for g, b in units:
            k = k_ref[b, rows(g), :]
            qd[g, b] = (q_ref[b, rows(g), :] * jnp.exp(cum[g, b])).astype(BF16)
            kd[g, b] = (k * jnp.exp(-cum[g, b])).astype(BF16)
            kl[g, b] = (k * jnp.exp(last[g, b] - cum[g, b])).astype(BF16)
            iv[g, b] = i_ref[b, rows(g), :].astype(BF16)
            dl[g, b] = jnp.exp(last[g, b])
        sc = {(u, h): lax.dot_general(qd[u][:, sl], kd[u][:, sl], _NT, preferred_element_type=F32)
              for u in units for h, sl in enumerate(head_cols)}
        grown = {(u, h): lax.dot_general(iv[u][:, sl], kl[u][:, sl], _TN,
                                         preferred_element_type=F32)
                 for u in units for h, sl in enumerate(head_cols)}
        within = {(u, h): jnp.dot(jnp.where(causal, sc[u, h], 0.0).astype(BF16), iv[u][:, sl],
                                  preferred_element_type=F32)
                  for u in units for h, sl in enumerate(head_cols)}
        for g in range(n_chunks):
            carried = {(b, h): lax.dot_general(qd[g, b][:, sl], st_ref[b, h].astype(BF16), _NT,
                                               preferred_element_type=F32)
                       for b in streams for h, sl in enumerate(head_cols)}
            for b in streams:
                for h, sl in enumerate(head_cols):
                    o_ref[b, rows(g), sl] = (within[(g, b), h] + carried[b, h]).astype(o_ref.dtype)
                    st_ref[b, h] = st_ref[b, h] * dl[g, b][:, sl] + grown[(g, b), h]

    @pl.when(jnp.logical_not(safe))
    def _():
        lane_t = lax.broadcasted_iota(jnp.int32, (hd, hd), 1)
        row_t = lax.broadcasted_iota(jnp.int32, (hd, hd), 0)

        def one_chunk(n, carry):
            b, r0 = n // n_chunks, pl.multiple_of((n % n_chunks) * chunk, chunk)
            for h, sl in enumerate(head_cols):
                load = lambda ref: ref[b, pl.ds(r0, chunk), sl].astype(F32)
                f_t = _pad_rows(jnp.exp(load(lf_ref)), hd).T
                k_t = _pad_rows(load(k_ref), hd).T
                q_t = _pad_rows(load(q_ref), hd).T
                i_p = _pad_rows(load(i_ref), hd)

                def token(t, sc, f_t=f_t, k_t=k_t, q_t=q_t, i_p=i_p):
                    s, o = sc
                    sel = lane_t == t
                    fc = jnp.sum(jnp.where(sel, f_t, 0.0), axis=1, keepdims=True)
                    kc = jnp.sum(jnp.where(sel, k_t, 0.0), axis=1, keepdims=True)
                    qc = jnp.sum(jnp.where(sel, q_t, 0.0), axis=1, keepdims=True)
                    i_row = jnp.sum(jnp.where(row_t == t, i_p, 0.0), axis=0, keepdims=True)
                    s = s * fc + kc * i_row
                    o = jnp.where(row_t == t, jnp.sum(qc * s, axis=0, keepdims=True), o)
                    return s, o

                s, o = lax.fori_loop(0, chunk, token,
                                     (st_ref[b, h].T, jnp.zeros((hd, hd), F32)))
                st_ref[b, h] = s.T
                o_ref[b, pl.ds(r0, chunk), sl] = o[:chunk].astype(o_ref.dtype)
            return carry

        lax.fori_loop(0, ns * n_chunks, one_chunk, 0)

    @pl.when(c == pl.num_programs(0) - 1)
    def _():
        sout_ref[...] = st_ref[...]


def _hg_call(q, k, iv, lf, s0t, chunk, chunks_per_step=1):
    s, l, w = q.shape
    rows = chunk * chunks_per_step if l % (chunk * chunks_per_step) == 0 else chunk
    assert l % rows == 0
    seq_spec = pl.BlockSpec((s, rows, w), lambda c: (0, c, 0))
    st_shape = (s, HG_HEADS, HG_HEAD_DIM, HG_HEAD_DIM)
    return pl.pallas_call(
        functools.partial(_hg_kernel, chunk=chunk),
        grid=(l // rows,),
        in_specs=[seq_spec, seq_spec, seq_spec, seq_spec,
                  pl.BlockSpec(s0t.shape, lambda c: (0, 0, 0, 0))],
        out_specs=[seq_spec, pl.BlockSpec(st_shape, lambda c: (0, 0, 0, 0))],
        out_shape=[jax.ShapeDtypeStruct((s, l, w), BF16), jax.ShapeDtypeStruct(st_shape, F32)],
        scratch_shapes=[pltpu.VMEM(st_shape, F32)],
        compiler_params=_params(("arbitrary",)),
        name="hgrn",
    )(q, k, iv, lf, s0t)


def _merge_branches(x, oa, ob, sg, gates, hn, wa, wb, wo):
    hd = HG_HEAD_DIM
    d = x.shape[1]
    heads = []
    for h in range(HG_HEADS):
        sl = slice(h * hd, (h + 1) * hd)
        heads.append(_rms(ob[:, sl].astype(F32), hn[:, sl]) * sg[:, sl].astype(F32))
    obn = jnp.concatenate(heads, axis=-1).astype(BF16)
    ma = jnp.dot(oa, wa, preferred_element_type=F32)
    mb = jnp.dot(obn, wb, preferred_element_type=F32)
    merged = gates[:, :d].astype(F32) * ma + gates[:, d:].astype(F32) * mb
    return x + jnp.dot(merged.astype(BF16), wo, preferred_element_type=F32)


def _post_kernel(x_ref, oa_ref, ob_ref, sg_ref, gates_ref, hn_ref, wa_ref, wb_ref, wo_ref,
                 n_ref, wg_ref, wu_ref, wd_ref, fn_ref, y_ref, acc_ref):
    x = jnp.concatenate(
        [_merge_branches(x_ref[r, :], oa_ref[r, :], ob_ref[r, :], sg_ref[r, :], gates_ref[r, :],
                         hn_ref[...], wa_ref[...], wb_ref[...], wo_ref[...])
         for r in _row_parts(x_ref.shape[0])], axis=0)
    x = _half_ffn(x, n_ref[...], wg_ref, wu_ref, wd_ref, acc_ref)
    y_ref[...] = _rms(x, fn_ref[...])


def _post_cast_kernel(x_ref, oa_ref, ob_ref, sg_ref, gates_ref, hn_ref, wa_ref, wb_ref, wo_ref,
                      n_ref, wg_ref, wu_ref, wd_ref, fn_ref,
                      y_ref, wab_ref, wbb_ref, wob_ref, wgb_ref, wub_ref, wdb_ref,
                      x2_ref, h_ref, acc_ref):
    c = pl.program_id(0)

    @pl.when(c == 0)
    def _():
        wa, wb, wo = (r[...].astype(BF16) for r in (wa_ref, wb_ref, wo_ref))
        for ref, wv in zip((wab_ref, wbb_ref, wob_ref), (wa, wb, wo)):
            ref[...] = wv
        x2 = _merge_branches(x_ref[...], oa_ref[...], ob_ref[...], sg_ref[...], gates_ref[...],
                             hn_ref[...], wa, wb, wo)
        x2_ref[...] = x2
        h_ref[...] = _rms(x2, n_ref[...]).astype(BF16)
        acc_ref[...] = jnp.zeros_like(acc_ref)

    @pl.when(c > 0)
    def _():
        acc_ref[...] += _cast_ffn_chunk(h_ref[...], (wg_ref, wu_ref, wd_ref),
                                        (wgb_ref, wub_ref, wdb_ref))

    @pl.when(c == pl.num_programs(0) - 1)
    def _():
        y_ref[...] = _rms(x2_ref[...] + 0.5 * acc_ref[...], fn_ref[...])


def _post_cast_call(x, oa, ob, sg, gates, hn, wa, wb, wo, norm, wg, wu, wd, fnorm):
    n, d = x.shape
    dff = wg.shape[1]
    assert dff % FF_CHUNK == 0
    chunk_specs = _ff_chunk_specs(wg, wd, lambda c: jnp.maximum(c - 1, 0))
    whole = lambda a: pl.BlockSpec(a.shape, lambda c: (0,) * a.ndim)
    return pl.pallas_call(
        _post_cast_kernel,
        grid=(dff // FF_CHUNK + 1,),
        in_specs=[whole(a) for a in (x, oa, ob, sg, gates)]
                 + [_const_spec(a.shape) for a in (hn, wa, wb, wo, norm)] + chunk_specs
                 + [_const_spec(fnorm.shape)],
        out_specs=[whole(x)] + [whole(a) for a in (wa, wb, wo)] + chunk_specs,
        out_shape=[jax.ShapeDtypeStruct((n, d), F32)]
                  + [jax.ShapeDtypeStruct(a.shape, BF16) for a in (wa, wb, wo, wg, wu, wd)],
        scratch_shapes=[pltpu.VMEM((n, d), F32), pltpu.VMEM((n, d), BF16),
                        pltpu.VMEM((n, d), F32)],
        compiler_params=_params(("arbitrary",)),
        name="post_cast",
    )(x, oa, ob, sg, gates, hn, wa, wb, wo, norm, wg, wu, wd, fnorm)


def _post_call(x, oa, ob, sg, gates, hn, wa, wb, wo, norm, wg, wu, wd, fnorm, tm):
    n, d = x.shape
    consts = [hn, wa, wb, wo, norm, wg, wu, wd, fnorm]
    return pl.pallas_call(
        _post_kernel,
        grid=(n // tm,),
        in_specs=[_rows_spec(tm, d), _rows_spec(tm, oa.shape[1]), _rows_spec(tm, ob.shape[1]),
                  _rows_spec(tm, sg.shape[1]), _rows_spec(tm, gates.shape[1])]
                 + [_const_spec(a.shape) for a in consts],
        out_specs=_rows_spec(tm, d),
        out_shape=jax.ShapeDtypeStruct((n, d), F32),
        scratch_shapes=[pltpu.VMEM((tm, d), F32)],
        compiler_params=_params(("parallel",)),
        name="post",
    )(x, oa, ob, sg, gates, *consts)


def _row_tile(n, target):
    tm = min(n, target)
    assert n % tm == 0 and tm % 8 == 0
    return tm


def kernel(x_prompt, x_sample, cache_sb_k, cache_sb_v, state_hgrn, meta_tokens, ffn1_norm, ffn1_w_gate, ffn1_w_up, ffn1_w_down, mix_norm, w_in, b_gate, hg_lb_logits, hg_out_norm, w_branch_a, w_branch_b, w_out, ffn2_norm, ffn2_w_gate, ffn2_w_up, ffn2_w_down, final_norm):
    depth = w_in.shape[0]
    assert depth == 1, "single-layer step"
    nb, seq, d = x_prompt.shape
    ns, t_new, _ = x_sample.shape
    n_meta = meta_tokens.shape[0]
    row = lambda a: a.reshape(1, -1)

    mix_norms = (row(mix_norm[0]), row(b_gate[0]), hg_lb_logits)

    n_frames = nb * seq
    tm_p = _row_tile(seq, 512)
    n_run = ns * t_new
    small = jnp.concatenate([x_sample.reshape(n_run, d), meta_tokens.astype(F32)], axis=0)
    x1_s, *f1 = _ffn_cast_call(small, row(ffn1_norm[0]), ffn1_w_gate[0], ffn1_w_up[0],
                               ffn1_w_down[0])
    (q_s, kb_s, vb_s, hq_s, hk_s, hi_s, hlf_s, sg_s, gates_s, k_s, v_s, meta_kt, meta_vt,
     w_in_b) = _mix_small_call(x1_s, mix_norms[0], w_in[0], *mix_norms[1:], n_run, n_meta)
    run = lambda a: a[:n_run]
    meta = lambda a: a[n_run:]
    token_minor = lambda a: jnp.transpose(a, (0, 1, 3, 4, 2))
    token_major = lambda a: jnp.transpose(a, (0, 1, 4, 2, 3))
    seq3 = lambda a, s: a.reshape(s, -1, a.shape[-1])
    rows = lambda a: a.reshape(-1, a.shape[-1])
    pad_to = lambda a, n: jnp.pad(a, ((0, 0), (0, n - a.shape[1]), (0, 0)))
    hg = lambda sel, s, arrs: [seq3(sel(a), s) for a in arrs]

    oa_s = _sbd_call(seq3(run(q_s), ns), pad_to(seq3(run(kb_s), ns), KEY_BLOCK),
                     pad_to(seq3(run(vb_s), ns), KEY_BLOCK),
                     token_minor(cache_sb_k), token_minor(cache_sb_v))
    ob_s, st_s = _hg_call(*hg(run, ns, (hq_s, hk_s, hi_s, hlf_s)),
                          jnp.swapaxes(state_hgrn[0].astype(F32), -1, -2), t_new)
    post_norms = (row(ffn2_norm[0]), row(final_norm))
    y_s, wa, wb, wo, *f2 = _post_cast_call(
        run(x1_s), rows(oa_s), rows(ob_s), run(sg_s), run(gates_s), row(hg_out_norm[0]),
        w_branch_a[0], w_branch_b[0], w_out[0], post_norms[0],
        ffn2_w_gate[0], ffn2_w_up[0], ffn2_w_down[0], post_norms[1])

    zero_state = jnp.zeros((1, HG_HEADS, HG_HEAD_DIM, HG_HEAD_DIM), F32)
    _, st_meta = _hg_call(*hg(meta, 1, (hq_s, hk_s, hi_s, hlf_s)), zero_state, n_meta)
    x1_p = _ffn_call(x_prompt.reshape(n_frames, d), row(ffn1_norm[0]), *f1,
                     _row_tile(n_frames, FFN_TILE))
    (q_p, kb_p, vb_p, hq_p, hk_p, hi_p, hlf_p, sg_p, gates_p, kt_full, vt_full) = (
        _mix_frames_call(x1_p.reshape(nb, seq, d), mix_norms[0], w_in_b, *mix_norms[1:],
                         meta_kt, meta_vt, tm_p, n_meta))
    oa_p = _sb_call(q_p, kb_p, vb_p,
                    pad_to(meta(kb_s)[None], KEY_BLOCK), pad_to(meta(vb_s)[None], KEY_BLOCK),
                    n_meta, KEY_BLOCK)
    ob_p, st_p = _hg_call(hq_p, hk_p, hi_p, hlf_p, st_meta, HG_CHUNK, HG_CHUNKS_PER_STEP)
    y_p = _post_call(x1_p, rows(oa_p), rows(ob_p), rows(sg_p), rows(gates_p),
                     row(hg_out_norm[0]), wa, wb, wo, post_norms[0], *f2, post_norms[1], tm_p)

    heads = lambda a: a.reshape(1, ns, t_new, SB_HEADS, SB_HEAD_DIM)
    return (y_p.reshape(nb, seq, d), y_s.reshape(ns, t_new, d),
            token_major(kt_full), token_major(vt_full),
            jnp.swapaxes(st_p, -1, -2)[None],
            heads(run(k_s)), heads(run(v_s)),
            jnp.swapaxes(st_s, -1, -2)[None])
```

```python
import functools

import jax
import jax.numpy as jnp
from jax import lax
from jax.experimental import pallas as pl
from jax.experimental.pallas import tpu as pltpu

F32 = jnp.float32
BF16 = jnp.bfloat16
EPS = 1e-6
SB_HEADS = 8
SB_HEAD_DIM = 64
SB_WIDTH = SB_HEADS * SB_HEAD_DIM
HG_HEADS = 4
HG_HEAD_DIM = 128
HG_WIDTH = HG_HEADS * HG_HEAD_DIM
LANES = 128
FF_CHUNK = 256
ROW_PART = 256
FFN_TILE, FFN_ROW_PART = 1024, 512
KEY_BLOCK = 128
WIDE_BLOCK = 256
SB_BLOCKS_PER_STEP = 8
SB_BLOCKS_SKEWED = 2
F32_EXP2_UNDERFLOW = -152.0
LOG2_E = 1.4426950408889634
HG_CHUNK = 128
HG_CHUNKS_PER_STEP = 4
HG_SAFE_LOG_DECAY = -60.0
VMEM_LIMIT_BYTES = 56 * 1024 * 1024

_NN = (((1,), (0,)), ((), ()))
_NT = (((1,), (1,)), ((), ()))
_TN = (((0,), (0,)), ((), ()))


def _const_spec(shape):
    nd = len(shape)
    return pl.BlockSpec(shape, lambda *_: (0,) * nd, pipeline_mode=pl.Buffered(1))


def _rows_spec(tm, width):
    return pl.BlockSpec((tm, width), lambda i: (i, 0))


def _params(sem):
    return pltpu.CompilerParams(dimension_semantics=sem, vmem_limit_bytes=VMEM_LIMIT_BYTES)


def _row_parts(rows, part=ROW_PART):
    if rows % part or rows == part:
        return [slice(0, rows)]
    return [slice(r, r + part) for r in range(0, rows, part)]


def _rms(x, g):
    return x * lax.rsqrt(jnp.mean(x * x, axis=-1, keepdims=True) + EPS) * g


def _ffn_chunk(h, wg, wu, wd):
    gate = jnp.dot(h, wg, preferred_element_type=F32)
    up = jnp.dot(h, wu, preferred_element_type=F32)
    act = (gate * jax.nn.sigmoid(gate) * up).astype(BF16)
    return jnp.dot(act, wd, preferred_element_type=F32)


def _half_ffn(x, g, wg_ref, wu_ref, wd_ref, acc_ref):
    h = _rms(x, g).astype(BF16)
    dff = wg_ref.shape[1]
    assert dff % FF_CHUNK == 0
    for c in range(dff // FF_CHUNK):
        cols = slice(c * FF_CHUNK, (c + 1) * FF_CHUNK)
        contrib = _ffn_chunk(h, wg_ref[:, cols], wu_ref[:, cols], wd_ref[cols, :])
        if c == 0:
            acc_ref[...] = contrib
        else:
            acc_ref[...] += contrib
    return x + 0.5 * acc_ref[...]


def _cast_ffn_chunk(h, w32_refs, w16_refs):
    wg, wu, wd = (r[...].astype(BF16) for r in w32_refs)
    for ref, wv in zip(w16_refs, (wg, wu, wd)):
        ref[...] = wv
    return _ffn_chunk(h, wg, wu, wd)


def _ff_chunk_specs(wg, wd, chunk_of_step):
    col = pl.BlockSpec((wg.shape[0], FF_CHUNK), lambda c: (0, chunk_of_step(c)))
    row = pl.BlockSpec((FF_CHUNK, wd.shape[1]), lambda c: (chunk_of_step(c), 0))
    return [col, col, row]


def _ffn_kernel(x_ref, n_ref, wg_ref, wu_ref, wd_ref, o_ref, acc_ref):
    for r in _row_parts(x_ref.shape[0], FFN_ROW_PART):
        o_ref[r, :] = _half_ffn(x_ref[r, :], n_ref[...], wg_ref, wu_ref, wd_ref, acc_ref.at[r, :])


def _ffn_cast_kernel(x_ref, n_ref, wg_ref, wu_ref, wd_ref, o_ref, wgb_ref, wub_ref, wdb_ref,
                     h_ref, acc_ref):
    c = pl.program_id(0)

    @pl.when(c == 0)
    def _():
        h_ref[...] = _rms(x_ref[...], n_ref[...]).astype(BF16)
        acc_ref[...] = jnp.zeros_like(acc_ref)

    acc_ref[...] += _cast_ffn_chunk(h_ref[...], (wg_ref, wu_ref, wd_ref),
                                    (wgb_ref, wub_ref, wdb_ref))

    @pl.when(c == pl.num_programs(0) - 1)
    def _():
        o_ref[...] = x_ref[...] + 0.5 * acc_ref[...]


def _ffn_cast_call(x, norm, wg, wu, wd):
    n, d = x.shape
    dff = wg.shape[1]
    assert dff % FF_CHUNK == 0
    chunk_specs = _ff_chunk_specs(wg, wd, lambda c: c)
    whole = pl.BlockSpec((n, d), lambda c: (0, 0))
    return pl.pallas_call(
        _ffn_cast_kernel,
        grid=(dff // FF_CHUNK,),
        in_specs=[whole, _const_spec(norm.shape)] + chunk_specs,
        out_specs=[whole] + chunk_specs,
        out_shape=[jax.ShapeDtypeStruct((n, d), F32)]
                  + [jax.ShapeDtypeStruct(a.shape, BF16) for a in (wg, wu, wd)],
        scratch_shapes=[pltpu.VMEM((n, d), BF16), pltpu.VMEM((n, d), F32)],
        compiler_params=_params(("arbitrary",)),
        name="ffn_cast",
    )(x, norm, wg, wu, wd)


def _ffn_call(x, norm, wg, wu, wd, tm):
    n, d = x.shape
    return pl.pallas_call(
        _ffn_kernel,
        grid=(n // tm,),
        in_specs=[_rows_spec(tm, d), _const_spec(norm.shape), _const_spec(wg.shape),
                  _const_spec(wu.shape), _const_spec(wd.shape)],
        out_specs=_rows_spec(tm, d),
        out_shape=jax.ShapeDtypeStruct((n, d), F32),
        scratch_shapes=[pltpu.VMEM((tm, d), F32)],
        compiler_params=_params(("parallel",)),
        name="ffn",
    )(x, norm, wg, wu, wd)


def _forget_lower_bound(lbl_ref):
    logits = lbl_ref[...]
    l0, l1 = logits[0:1], logits[1:2]
    m = jnp.maximum(l0, l1)
    e0, e1 = jnp.exp(l0 - m), jnp.exp(l1 - m)
    p0, p1 = e0 / (e0 + e1), e1 / (e0 + e1)
    return (p0 + p1) - p0


_N_MIX_GROUPS = 11
_MIX_GROUP_ORDER = (7, 8, 9, 10, 3, 6, 0, 1, 2, 4, 5)


def _mix_group(j, p, r, lb, bg_ref, outs):
    q_ref, kb_ref, vb_ref, hq_ref, hk_ref, hi_ref, hlf_ref, sg_ref, gates_ref = outs
    if j == 0:
        q_ref[r, :] = (p * (SB_HEAD_DIM ** -0.5 * LOG2_E)).astype(BF16)
    elif j == 1:
        kb_ref[r, :] = p.astype(BF16)
    elif j == 2:
        vb_ref[r, :] = p.astype(BF16)
    elif j == 3:
        f = lb + (1.0 - lb) * jax.nn.sigmoid(p)
        hk_ref[r, :] = (1.0 - f).astype(hk_ref.dtype)
        hlf_ref[r, :] = jnp.log(f)
    elif j == 4:
        hi_ref[r, :] = p.astype(hi_ref.dtype)
    elif j == 5:
        hq_ref[r, :] = p.astype(hq_ref.dtype)
    elif j == 6:
        sg_ref[r, :] = (p * jax.nn.sigmoid(p)).astype(sg_ref.dtype)
    else:
        cols = slice((j - 7) * SB_WIDTH, (j - 6) * SB_WIDTH)
        gates_ref[r, cols] = jax.nn.sigmoid(p + bg_ref[:, cols]).astype(gates_ref.dtype)


def _mix_body(x_ref, n_ref, win_ref, bg_ref, lbl_ref, *outs):
    w = SB_WIDTH
    lb = _forget_lower_bound(lbl_ref)
    kv = ([], [])
    for r in _row_parts(x_ref.shape[0]):
        h = _rms(x_ref[r, :], n_ref[...]).astype(BF16)
        for j in _MIX_GROUP_ORDER:
            p = jnp.dot(h, win_ref[:, j * w:(j + 1) * w], preferred_element_type=F32)
            _mix_group(j, p, r, lb, bg_ref, outs)
            if j in (1, 2):
                kv[j - 1].append(p)
    return jnp.concatenate(kv[0], axis=0), jnp.concatenate(kv[1], axis=0)


def _mix_small_kernel(*refs, n_run, n_meta):
    x_ref, n_ref, win_ref, bg_ref, lbl_ref = refs[:5]
    outs, (k_ref, v_ref, kt_ref, vt_ref, mk_ref, mv_ref, winb_ref, h_ref) = refs[5:14], refs[14:]
    c = pl.program_id(0)

    @pl.when(c == 0)
    def _():
        h_ref[...] = _rms(x_ref[...], n_ref[...]).astype(BF16)

    wb = win_ref[...].astype(BF16)
    winb_ref[...] = wb
    p = jnp.dot(h_ref[...], wb, preferred_element_type=F32)
    lb = _forget_lower_bound(lbl_ref)
    for j in range(_N_MIX_GROUPS):
        @pl.when(c == j)
        def _(j=j):
            _mix_group(j, p, slice(None), lb, bg_ref, outs)
            for group, rows_ref, cols_ref, meta_ref in ((1, k_ref, kt_ref, mk_ref),
                                                        (2, v_ref, vt_ref, mv_ref)):
                if j == group:
                    meta_rows = _pad_rows(p[n_run:n_run + n_meta], KEY_BLOCK)
                    rows_ref[...] = p
                    cols_ref[...] = meta_rows.T
                    meta_ref[...] = meta_rows.astype(BF16)


def _mix_frames_kernel(*refs, tm, n_meta, n_tiles):
    x_ref, consts, mk_ref, mv_ref = refs[0], refs[1:5], refs[5], refs[6]
    row_outs, (kt_ref, vt_ref, ck_ref, cv_ref) = refs[7:16], refs[16:]
    t = pl.program_id(1)
    shape = (SB_HEADS, SB_HEAD_DIM, tm)

    @pl.when(t == 0)
    def _():
        ck_ref[...] = mk_ref[...]
        cv_ref[...] = mv_ref[...]

    @pl.when(t < n_tiles)
    def _():
        k, v = _mix_body(x_ref.at[0], *consts, *[r.at[0] for r in row_outs])
        for rows, out_ref, carry_ref in ((k, kt_ref, ck_ref), (v, vt_ref, cv_ref)):
            cols = rows.T
            late = jnp.concatenate([carry_ref[:, :n_meta], cols[:, :tm - n_meta]], axis=1)
            out_ref[0, 0] = late.reshape(shape)
            carry_ref[:, :n_meta] = cols[:, tm - n_meta:]

    @pl.when(t == n_tiles)
    def _():
        for out_ref, carry_ref in ((kt_ref, ck_ref), (vt_ref, cv_ref)):
            tail = jnp.concatenate(
                [carry_ref[:, :n_meta], jnp.zeros((SB_WIDTH, tm - n_meta), F32)], axis=1)
            out_ref[0, 0] = tail.reshape(shape)


_MIX_WIDTHS = [(SB_WIDTH, BF16)] * 6 + [(SB_WIDTH, F32), (SB_WIDTH, BF16)]


def _mix_small_call(x, norm, w_in, b_gate, lb_logits, n_run, n_meta):
    n, d = x.shape
    assert w_in.shape[1] == _N_MIX_GROUPS * SB_WIDTH
    widths = _MIX_WIDTHS + [(2 * d, BF16), (SB_WIDTH, F32), (SB_WIDTH, F32)]
    assert KEY_BLOCK == LANES
    meta_t = jax.ShapeDtypeStruct((SB_WIDTH, LANES), F32)
    meta_rows = jax.ShapeDtypeStruct((KEY_BLOCK, SB_WIDTH), BF16)
    whole = lambda shape: pl.BlockSpec(shape, lambda c: (0, 0))
    group = pl.BlockSpec((d, SB_WIDTH), lambda c: (0, c))
    return pl.pallas_call(
        functools.partial(_mix_small_kernel, n_run=n_run, n_meta=n_meta),
        grid=(_N_MIX_GROUPS,),
        in_specs=[whole((n, d)), _const_spec(norm.shape), group,
                  _const_spec(b_gate.shape), _const_spec(lb_logits.shape)],
        out_specs=[whole((n, wd)) for wd, _ in widths] + [whole(meta_t.shape)] * 2
                  + [whole(meta_rows.shape)] * 2 + [group],
        out_shape=[jax.ShapeDtypeStruct((n, wd), dt) for wd, dt in widths] + [meta_t] * 2
                  + [meta_rows] * 2 + [jax.ShapeDtypeStruct(w_in.shape, BF16)],
        scratch_shapes=[pltpu.VMEM((n, d), BF16)],
        compiler_params=_params(("arbitrary",)),
        name="mix_small",
    )(x, norm, w_in, b_gate, lb_logits)


def _mix_frames_call(x, norm, w_in, b_gate, lb_logits, meta_kt, meta_vt, tm, n_meta):
    nb, seq, d = x.shape
    assert seq % tm == 0 and n_meta < tm
    n_tiles = seq // tm
    widths = _MIX_WIDTHS + [(2 * d, BF16)]
    rows_map = lambda b, t: (b, jnp.minimum(t, n_tiles - 1), 0)
    const2 = lambda shape: pl.BlockSpec(shape, lambda b, t: (0, 0), pipeline_mode=pl.Buffered(1))
    kv_shape = jax.ShapeDtypeStruct((1, nb, SB_HEADS, SB_HEAD_DIM, n_meta + seq), F32)
    kv_spec = pl.BlockSpec((1, 1, SB_HEADS, SB_HEAD_DIM, tm), lambda b, t: (0, b, 0, 0, t))
    carry = pltpu.VMEM((SB_WIDTH, LANES), F32)
    return pl.pallas_call(
        functools.partial(_mix_frames_kernel, tm=tm, n_meta=n_meta, n_tiles=n_tiles),
        grid=(nb, n_tiles + 1),
        in_specs=[pl.BlockSpec((1, tm, d), rows_map)]
                 + [const2(a.shape) for a in (norm, w_in, b_gate, lb_logits, meta_kt, meta_vt)],
        out_specs=[pl.BlockSpec((1, tm, wd), rows_map) for wd, _ in widths] + [kv_spec] * 2,
        out_shape=[jax.ShapeDtypeStruct((nb, seq, wd), dt) for wd, dt in widths] + [kv_shape] * 2,
        scratch_shapes=[carry, carry],
        compiler_params=_params(("arbitrary", "arbitrary")),
        name="mix_frames",
    )(x, norm, w_in, b_gate, lb_logits, meta_kt, meta_vt)


def _suffix_ones(width):
    r = lax.broadcasted_iota(jnp.int32, (width, width), 0)
    c = lax.broadcasted_iota(jnp.int32, (width, width), 1)
    return jnp.where(r >= c, 1.0, 0.0).astype(BF16)


def _key_before(tq, width, k0, q_first):
    row = lax.broadcasted_iota(jnp.int32, (tq, width), 0)
    col = lax.broadcasted_iota(jnp.int32, (tq, width), 1)
    return col + k0 < row + q_first


def _sb_scores(blocks, token_minor):
    k_dims = _NN if token_minor else _NT
    return [[lax.dot_general(q_ops[h], k_ops[h], k_dims, preferred_element_type=F32)
             for h in range(SB_HEADS)] for q_ops, k_ops, *_ in blocks]


def _sb_stay_sums(blocks, zs):
    cums = []
    for z_blk, (_, _, _, ones, mask, _, _) in zip(zs, blocks):
        stays = []
        for z in z_blk:
            nz = -z
            ls = jnp.minimum(nz, 0.0) - jnp.log2(1.0 + jnp.exp2(jnp.minimum(z, nz)))
            if mask is not None:
                ls = jnp.where(mask, ls, 0.0)
            stays.append(ls.astype(BF16))
        cums.append([jnp.dot(st, ones, preferred_element_type=F32) for st in stays])
    return cums


def _sb_apply(blocks, zs, cums, token_minor):
    heads = range(SB_HEADS)
    v_dims = _NT if token_minor else _NN
    ws = []
    for z_blk, cum_blk, (q_ops, _, _, ones, mask, r_ref, _) in zip(zs, cums, blocks):
        ws.append([])
        for h in heads:
            r = r_ref[h]
            cr = cum_blk[h] + jnp.concatenate([r] * (ones.shape[1] // LANES), axis=1)
            a = jnp.exp2(z_blk[h] + cr)
            if mask is not None:
                a = jnp.where(mask, a, 0.0)
            ws[-1].append(a.astype(BF16))
            r_ref[h] = jnp.broadcast_to(cr[:, 0:1], (q_ops[h].shape[0], LANES))
    for w_blk, (_, _, v_ops, _, _, _, acc_ref) in zip(ws, blocks):
        for h in heads:
            acc_ref[h] += lax.dot_general(w_blk[h], v_ops[h], v_dims, preferred_element_type=F32)


def _sb_sweep(blocks, token_minor=False):
    zs = _sb_scores(blocks, token_minor)
    _sb_apply(blocks, zs, _sb_stay_sums(blocks, zs), token_minor)


def _sb_sweep_skewed(sweeps):
    zs = _sb_scores(sweeps[0], False)
    for n, blocks in enumerate(sweeps):
        cums = _sb_stay_sums(blocks, zs)
        z_next = _sb_scores(sweeps[n + 1], False) if n + 1 < len(sweeps) else None
        _sb_apply(blocks, zs, cums, False)
        zs = z_next


def _sb_live(r_ref):
    m = r_ref[0]
    for h in range(1, SB_HEADS):
        m = jnp.maximum(m, r_ref[h])
    return (jnp.max(m) >= F32_EXP2_UNDERFLOW).astype(jnp.int32)


def _sweep_while(n_blocks, alive, r_ref, block, first=0):
    def body(carry):
        t, _ = carry
        block(t)
        return t + 1, _sb_live(r_ref)

    _, alive = lax.while_loop(lambda c: jnp.logical_and(c[0] < n_blocks, c[1] > 0),
                              body, (jnp.int32(first), alive))
    return alive


def _sb_kernel(q_ref, kn_ref, vn_ref, ko_ref, vo_ref, o_ref, qm_ref, acc_ref, r_ref, live_ref, *,
               tq, n_old):
    kb, wb = KEY_BLOCK, WIDE_BLOCK
    blocks_per_step = q_ref.shape[1] // tq
    lane = lax.broadcasted_iota(jnp.int32, (tq, LANES), 1)
    ones = {width: _suffix_ones(width) for width in (kb, wb)}
    old_rows = lax.broadcasted_iota(jnp.int32, (tq, kb), 1) < n_old

    def block(j, k_ref, v_ref, start, width, mask):
        k_all = k_ref[0, pl.ds(start, width), :].astype(BF16)
        v_all = v_ref[0, pl.ds(start, width), :].astype(BF16)
        pair = lambda a: [a[:, (h // 2) * LANES:(h // 2 + 1) * LANES] for h in range(SB_HEADS)]
        return ([qm_ref[j, h] for h in range(SB_HEADS)], pair(k_all), pair(v_all), ones[width],
                mask, r_ref.at[j], acc_ref.at[j])

    def place(j):
        q0 = (pl.program_id(1) * blocks_per_step + j) * tq
        below = q0 // kb
        return q0, below, pl.multiple_of(below * kb, kb)

    def write_out(j):
        rows = pl.ds(pl.multiple_of(j * tq, tq), tq)
        for p in range(SB_HEADS // 2):
            o_ref[0, rows, p * LANES:(p + 1) * LANES] = jnp.where(
                lane < SB_HEAD_DIM, acc_ref[j, 2 * p], acc_ref[j, 2 * p + 1]).astype(o_ref.dtype)

    def start(j):
        rows = pl.ds(pl.multiple_of(j * tq, tq), tq)
        for h in range(SB_HEADS):
            p = h // 2
            qp = q_ref[0, rows, p * LANES:(p + 1) * LANES]
            keep = (lane < SB_HEAD_DIM) if h % 2 == 0 else (lane >= SB_HEAD_DIM)
            qm_ref[j, h] = jnp.where(keep, qp, jnp.zeros_like(qp))
            r_ref[j, h] = jnp.zeros((tq, LANES), F32)
            acc_ref[j, h] = jnp.zeros((tq, LANES), F32)

    def own(j):
        q0, _, d0 = place(j)
        return block(j, kn_ref, vn_ref, d0, kb, _key_before(tq, kb, d0, q0))

    def own_and_under(j):
        _, _, d0 = place(j)
        return [own(j), block(j, kn_ref, vn_ref, pl.multiple_of(d0 - wb, kb), wb, None)]

    def note_reach(j):
        reach = r_ref[j, 0]
        for h in range(1, SB_HEADS):
            reach = jnp.maximum(reach, r_ref[j, h])
        live_ref[...] = jnp.maximum(live_ref[...], reach)

    def first_sweep(j, carry):
        _, below, _ = place(j)
        start(j)
        old = lambda: block(j, ko_ref, vo_ref, 0, kb, old_rows)

        @pl.when(below == 0)
        def _():
            _sb_sweep([own(j), old()])

        @pl.when(below == 1)
        def _():
            _sb_sweep([own(j), block(j, kn_ref, vn_ref, 0, kb, None), old()])

        @pl.when(below >= 2)
        def _():
            _sb_sweep(own_and_under(j))
            note_reach(j)

        write_out(j)
        return carry

    def first_sweep_group(jj, carry):
        js = [group * jj + n for n in range(group)]
        _, below, _ = place(js[0])
        for j in js:
            start(j)
        old = lambda j: block(j, ko_ref, vo_ref, 0, kb, old_rows)

        @pl.when(below == 0)
        def _():
            _sb_sweep([own(js[0]), old(js[0])])
            _sb_sweep([own(js[1]), block(js[1], kn_ref, vn_ref, 0, kb, None), old(js[1])])
            if group > 2:
                _sb_sweep_skewed([own_and_under(j) for j in js[2:]])
                for j in js[2:]:
                    note_reach(j)

        @pl.when(below >= 2)
        def _():
            _sb_sweep_skewed([own_and_under(j) for j in js])
            for j in js:
                note_reach(j)

        for j in js:
            write_out(j)
        return carry

    def further_sweeps(j, carry):
        _, below, d0 = place(j)
        top = d0 - wb
        left = jnp.maximum(below - 2, 0)
        live = r_ref.at[j]
        alive = _sweep_while(left // 2, _sb_live(live), live, lambda t: _sb_sweep([block(
            j, kn_ref, vn_ref, pl.multiple_of(top - (t + 1) * wb, kb), wb, None)]))
        alive = _sweep_while(left % 2, alive, live,
                             lambda t: _sb_sweep([block(j, kn_ref, vn_ref, 0, kb, None)]))
        _sweep_while((below >= 2).astype(jnp.int32), alive, live,
                     lambda t: _sb_sweep([block(j, ko_ref, vo_ref, 0, kb, old_rows)]))
        write_out(j)
        return carry

    live_ref[...] = jnp.full(live_ref.shape, -jnp.inf, F32)
    group = SB_BLOCKS_SKEWED
    if blocks_per_step % group == 0 and KEY_BLOCK == tq:
        lax.fori_loop(0, blocks_per_step // group, first_sweep_group, 0)
    else:
        lax.fori_loop(0, blocks_per_step, first_sweep, 0)

    @pl.when(jnp.max(live_ref[...]) >= F32_EXP2_UNDERFLOW)
    def _():
        lax.fori_loop(0, blocks_per_step, further_sweeps, 0)


def _sbd_kernel(q_ref, kn_ref, vn_ref, ck_ref, cv_ref, o_ref, acc_ref, r_ref, *, tq, n_old):
    kb, wb, hd = KEY_BLOCK, WIDE_BLOCK, SB_HEAD_DIM
    heads = range(SB_HEADS)
    n_tail, n_full = n_old % kb, n_old // kb
    n_wide = (n_full * kb) // wb
    rest = n_full * kb - n_wide * wb
    per_head = lambda a: [a[:, h * hd:(h + 1) * hd] for h in heads]

    q_ops = per_head(q_ref[...])
    for h in heads:
        r_ref[h] = jnp.zeros((tq, LANES), F32)
        acc_ref[h] = jnp.zeros((tq, hd), F32)

    _sb_sweep([(q_ops, per_head(_pad_rows(kn_ref[...], kb)), per_head(_pad_rows(vn_ref[...], kb)),
                _suffix_ones(kb), _key_before(tq, kb, 0, 0), r_ref, acc_ref)])
    alive = _sb_live(r_ref)

    def cached(start, width, valid):
        def load(ref, h):
            cols = ref[0, 0, h, :, pl.ds(start, valid)].astype(BF16)
            if valid < width:
                cols = jnp.concatenate([cols, jnp.zeros((hd, width - valid), BF16)], axis=1)
            return cols

        mask = None if valid == width else (
            lax.broadcasted_iota(jnp.int32, (tq, width), 1) < valid)
        _sb_sweep([(q_ops, [load(ck_ref, h) for h in heads], [load(cv_ref, h) for h in heads],
                    _suffix_ones(width), mask, r_ref, acc_ref)], token_minor=True)

    if n_tail:
        alive = _sweep_while(1, alive, r_ref, lambda t: cached(n_full * kb, kb, n_tail))
    if n_wide:
        alive = _sweep_while(n_wide, alive, r_ref, lambda t: cached(
            pl.multiple_of(rest + (n_wide - 1 - t) * wb, kb), wb, wb))
    if rest:
        _sweep_while(1, alive, r_ref, lambda t: cached(0, rest, rest))

    o_ref[...] = jnp.concatenate([acc_ref[h] for h in heads], axis=1).astype(o_ref.dtype)


def _sb_call(q, k_new, v_new, k_old, v_old, n_old, tq):
    s, lq, w = q.shape
    ln = k_new.shape[1]
    lo = k_old.shape[1]
    assert lq % tq == 0 and ln % KEY_BLOCK == 0 and n_old <= KEY_BLOCK <= lo
    assert tq <= KEY_BLOCK and KEY_BLOCK % tq == 0
    step = tq * SB_BLOCKS_PER_STEP if lq % (tq * SB_BLOCKS_PER_STEP) == 0 else tq
    old_map = (lambda b, i: (b, 0, 0)) if k_old.shape[0] == s else (lambda b, i: (0, 0, 0))
    return pl.pallas_call(
        functools.partial(_sb_kernel, tq=tq, n_old=n_old),
        grid=(s, lq // step),
        in_specs=[pl.BlockSpec((1, step, w), lambda b, i: (b, i, 0)),
                  pl.BlockSpec((1, ln, w), lambda b, i: (b, 0, 0)),
                  pl.BlockSpec((1, ln, w), lambda b, i: (b, 0, 0)),
                  pl.BlockSpec((1, lo, w), old_map),
                  pl.BlockSpec((1, lo, w), old_map)],
        out_specs=pl.BlockSpec((1, step, w), lambda b, i: (b, i, 0)),
        out_shape=jax.ShapeDtypeStruct((s, lq, w), BF16),
        scratch_shapes=[pltpu.VMEM((step // tq, SB_HEADS, tq, LANES), BF16),
                        pltpu.VMEM((step // tq, SB_HEADS, tq, LANES), F32),
                        pltpu.VMEM((step // tq, SB_HEADS, tq, LANES), F32),
                        pltpu.VMEM((tq, LANES), F32)],
        compiler_params=_params(("parallel", "parallel")),
        name="sb",
    )(q, k_new, v_new, k_old, v_old)


def _sbd_call(q, k_new, v_new, cache_k, cache_v, tq):
    s, w = cache_k.shape[1], q.shape[1]
    n_old = cache_k.shape[-1]
    assert tq <= KEY_BLOCK and q.shape[0] >= s * tq
    new_spec = pl.BlockSpec((tq, w), lambda b: (b, 0))
    cache_spec = pl.BlockSpec((1, 1) + cache_k.shape[2:], lambda b: (0, b, 0, 0, 0))
    return pl.pallas_call(
        functools.partial(_sbd_kernel, tq=tq, n_old=n_old),
        grid=(s,),
        in_specs=[new_spec, new_spec, new_spec, cache_spec, cache_spec],
        out_specs=new_spec,
        out_shape=jax.ShapeDtypeStruct((s * tq, w), BF16),
        scratch_shapes=[pltpu.VMEM((SB_HEADS, tq, SB_HEAD_DIM), F32),
                        pltpu.VMEM((SB_HEADS, tq, LANES), F32)],
        compiler_params=_params(("parallel",)),
        name="sb_decode",
    )(q, k_new, v_new, cache_k, cache_v)


def _pad_rows(x, rows):
    if x.shape[0] == rows:
        return x
    return jnp.concatenate([x, jnp.zeros((rows - x.shape[0], x.shape[1]), x.dtype)], axis=0)


def _hg_kernel(q_ref, k_ref, i_ref, lf_ref, s0_ref, o_ref, sout_ref, st_ref, *, chunk):
    c = pl.program_id(0)
    hd = HG_HEAD_DIM
    ns, n_chunks = q_ref.shape[0], q_ref.shape[1] // chunk
    streams = range(ns)
    units = [(g, b) for g in range(n_chunks) for b in streams]
    rows = lambda g: slice(g * chunk, (g + 1) * chunk)
    head_cols = [slice(h * hd, (h + 1) * hd) for h in range(HG_HEADS)]

    @pl.when(c == 0)
    def _():
        for b in streams:
            st_ref[b] = s0_ref[b if s0_ref.shape[0] == ns else 0]

    trow = lax.broadcasted_iota(jnp.int32, (chunk, chunk), 0)
    tcol = lax.broadcasted_iota(jnp.int32, (chunk, chunk), 1)
    causal = trow >= tcol

    token = lax.broadcasted_iota(jnp.int32, (chunk, lf_ref.shape[2]), 0)
    cum = {}
    for g, b in units:
        part = lf_ref[b, rows(g), :]
        shift = 1
        while shift < chunk:
            part = part + jnp.where(token >= shift, pltpu.roll(part, shift, axis=0), 0.0)
            shift *= 2
        cum[g, b] = part
    last = {u: cum[u][chunk - 1:chunk, :] for u in units}
    weakest = functools.reduce(jnp.minimum, last.values())
    safe = jnp.min(weakest) >= HG_SAFE_LOG_DECAY

    @pl.when(safe)
    def _():
        qd, kd, kl, iv, dl = {}, {}, {}, {}, {}
        for g, b in units:
            k = k_ref[b, rows(g), :]
            qd[g, b] = (q_ref[b, rows(g), :] * jnp.exp(cum[g, b])).astype(BF16)
            kd[g, b] = (k * jnp.exp(-cum[g, b])).astype(BF16)
            kl[g, b] = (k * jnp.exp(last[g, b] - cum[g, b])).astype(BF16)
            iv[g, b] = i_ref[b, rows(g), :].astype(BF16)
            dl[g, b] = jnp.exp(last[g, b])
        sc = {(u, h): lax.dot_general(qd[u][:, sl], kd[u][:, sl], _NT, preferred_element_type=F32)
              for u in units for h, sl in enumerate(head_cols)}
        grown = {(u, h): lax.dot_general(iv[u][:, sl], kl[u][:, sl], _TN,
                                         preferred_element_type=F32)
                 for u in units for h, sl in enumerate(head_cols)}
        within = {(u, h): jnp.dot(jnp.where(causal, sc[u, h], 0.0).astype(BF16), iv[u][:, sl],
                                  preferred_element_type=F32)
                  for u in units for h, sl in enumerate(head_cols)}
        for g in range(n_chunks):
            carried = {(b, h): lax.dot_general(qd[g, b][:, sl], st_ref[b, h].astype(BF16), _NT,
                                               preferred_element_type=F32)
                       for b in streams for h, sl in enumerate(head_cols)}
            for b in streams:
                for h, sl in enumerate(head_cols):
                    o_ref[b, rows(g), sl] = (within[(g, b), h] + carried[b, h]).astype(o_ref.dtype)
                    st_ref[b, h] = st_ref[b, h] * dl[g, b][:, sl] + grown[(g, b), h]

    @pl.when(jnp.logical_not(safe))
    def _():
        lane_t = lax.broadcasted_iota(jnp.int32, (hd, hd), 1)
        row_t = lax.broadcasted_iota(jnp.int32, (hd, hd), 0)

        def one_chunk(n, carry):
            b, r0 = n // n_chunks, pl.multiple_of((n % n_chunks) * chunk, chunk)
            for h, sl in enumerate(head_cols):
                load = lambda ref: ref[b, pl.ds(r0, chunk), sl].astype(F32)
                f_t = _pad_rows(jnp.exp(load(lf_ref)), hd).T
                k_t = _pad_rows(load(k_ref), hd).T
                q_t = _pad_rows(load(q_ref), hd).T
                i_p = _pad_rows(load(i_ref), hd)

                def token(t, sc, f_t=f_t, k_t=k_t, q_t=q_t, i_p=i_p):
                    s, o = sc
                    sel = lane_t == t
                    fc = jnp.sum(jnp.where(sel, f_t, 0.0), axis=1, keepdims=True)
                    kc = jnp.sum(jnp.where(sel, k_t, 0.0), axis=1, keepdims=True)
                    qc = jnp.sum(jnp.where(sel, q_t, 0.0), axis=1, keepdims=True)
                    i_row = jnp.sum(jnp.where(row_t == t, i_p, 0.0), axis=0, keepdims=True)
                    s = s * fc + kc * i_row
                    o = jnp.where(row_t == t, jnp.sum(qc * s, axis=0, keepdims=True), o)
                    return s, o

                s, o = lax.fori_loop(0, chunk, token,
                                     (st_ref[b, h].T, jnp.zeros((hd, hd), F32)))
                st_ref[b, h] = s.T
                o_ref[b, pl.ds(r0, chunk), sl] = o[:chunk].astype(o_ref.dtype)
            return carry

        lax.fori_loop(0, ns * n_chunks, one_chunk, 0)

    @pl.when(c == pl.num_programs(0) - 1)
    def _():
        sout_ref[...] = st_ref[...]


def _hg_call(q, k, iv, lf, s0t, chunk, chunks_per_step=1):
    s, l, w = q.shape
    rows = chunk * chunks_per_step if l % (chunk * chunks_per_step) == 0 else chunk
    assert l % rows == 0
    seq_spec = pl.BlockSpec((s, rows, w), lambda c: (0, c, 0))
    st_shape = (s, HG_HEADS, HG_HEAD_DIM, HG_HEAD_DIM)
    return pl.pallas_call(
        functools.partial(_hg_kernel, chunk=chunk),
        grid=(l // rows,),
        in_specs=[seq_spec, seq_spec, seq_spec, seq_spec,
                  pl.BlockSpec(s0t.shape, lambda c: (0, 0, 0, 0))],
        out_specs=[seq_spec, pl.BlockSpec(st_shape, lambda c: (0, 0, 0, 0))],
        out_shape=[jax.ShapeDtypeStruct((s, l, w), BF16), jax.ShapeDtypeStruct(st_shape, F32)],
        scratch_shapes=[pltpu.VMEM(st_shape, F32)],
        compiler_params=_params(("arbitrary",)),
        name="hgrn",
    )(q, k, iv, lf, s0t)


def _merge_branches(x, oa, ob, sg, gates, hn, wa, wb, wo):
    hd = HG_HEAD_DIM
    d = x.shape[1]
    heads = []
    for h in range(HG_HEADS):
        sl = slice(h * hd, (h + 1) * hd)
        heads.append(_rms(ob[:, sl].astype(F32), hn[:, sl]) * sg[:, sl].astype(F32))
    obn = jnp.concatenate(heads, axis=-1).astype(BF16)
    ma = jnp.dot(oa, wa, preferred_element_type=F32)
    mb = jnp.dot(obn, wb, preferred_element_type=F32)
    merged = gates[:, :d].astype(F32) * ma + gates[:, d:].astype(F32) * mb
    return x + jnp.dot(merged.astype(BF16), wo, preferred_element_type=F32)


def _post_kernel(x_ref, oa_ref, ob_ref, sg_ref, gates_ref, hn_ref, wa_ref, wb_ref, wo_ref,
                 n_ref, wg_ref, wu_ref, wd_ref, fn_ref, y_ref, acc_ref):
    x = jnp.concatenate(
        [_merge_branches(x_ref[r, :], oa_ref[r, :], ob_ref[r, :], sg_ref[r, :], gates_ref[r, :],
                         hn_ref[...], wa_ref[...], wb_ref[...], wo_ref[...])
         for r in _row_parts(x_ref.shape[0])], axis=0)
    x = _half_ffn(x, n_ref[...], wg_ref, wu_ref, wd_ref, acc_ref)
    y_ref[...] = _rms(x, fn_ref[...])


def _post_cast_kernel(x_ref, oa_ref, ob_ref, sg_ref, gates_ref, hn_ref, wa_ref, wb_ref, wo_ref,
                      n_ref, wg_ref, wu_ref, wd_ref, fn_ref,
                      y_ref, wab_ref, wbb_ref, wob_ref, wgb_ref, wub_ref, wdb_ref,
                      x2_ref, h_ref, acc_ref):
    c = pl.program_id(0)

    @pl.when(c == 0)
    def _():
        wa, wb, wo = (r[...].astype(BF16) for r in (wa_ref, wb_ref, wo_ref))
        for ref, wv in zip((wab_ref, wbb_ref, wob_ref), (wa, wb, wo)):
            ref[...] = wv
        x2 = _merge_branches(x_ref[...], oa_ref[...], ob_ref[...], sg_ref[...], gates_ref[...],
                             hn_ref[...], wa, wb, wo)
        x2_ref[...] = x2
        h_ref[...] = _rms(x2, n_ref[...]).astype(BF16)
        acc_ref[...] = jnp.zeros_like(acc_ref)

    @pl.when(c > 0)
    def _():
        acc_ref[...] += _cast_ffn_chunk(h_ref[...], (wg_ref, wu_ref, wd_ref),
                                        (wgb_ref, wub_ref, wdb_ref))

    @pl.when(c == pl.num_programs(0) - 1)
    def _():
        y_ref[...] = _rms(x2_ref[...] + 0.5 * acc_ref[...], fn_ref[...])


def _post_cast_call(x, oa, ob, sg, gates, hn, wa, wb, wo, norm, wg, wu, wd, fnorm):
    n, d = oa.shape[0], x.shape[1]
    dff = wg.shape[1]
    assert dff % FF_CHUNK == 0
    chunk_specs = _ff_chunk_specs(wg, wd, lambda c: jnp.maximum(c - 1, 0))
    whole = lambda a: pl.BlockSpec(a.shape, lambda c: (0,) * a.ndim)
    top = lambda a: pl.BlockSpec((n, a.shape[1]), lambda c: (0, 0))
    return pl.pallas_call(
        _post_cast_kernel,
        grid=(dff // FF_CHUNK + 1,),
        in_specs=[top(a) for a in (x, oa, ob, sg, gates)]
                 + [_const_spec(a.shape) for a in (hn, wa, wb, wo, norm)] + chunk_specs
                 + [_const_spec(fnorm.shape)],
        out_specs=[top(x)] + [whole(a) for a in (wa, wb, wo)] + chunk_specs,
        out_shape=[jax.ShapeDtypeStruct((n, d), F32)]
                  + [jax.ShapeDtypeStruct(a.shape, BF16) for a in (wa, wb, wo, wg, wu, wd)],
        scratch_shapes=[pltpu.VMEM((n, d), F32), pltpu.VMEM((n, d), BF16),
                        pltpu.VMEM((n, d), F32)],
        compiler_params=_params(("arbitrary",)),
        name="post_cast",
    )(x, oa, ob, sg, gates, hn, wa, wb, wo, norm, wg, wu, wd, fnorm)


def _post_call(x, oa, ob, sg, gates, hn, wa, wb, wo, norm, wg, wu, wd, fnorm, tm):
    n, d = x.shape
    consts = [hn, wa, wb, wo, norm, wg, wu, wd, fnorm]
    return pl.pallas_call(
        _post_kernel,
        grid=(n // tm,),
        in_specs=[_rows_spec(tm, d), _rows_spec(tm, oa.shape[1]), _rows_spec(tm, ob.shape[1]),
                  _rows_spec(tm, sg.shape[1]), _rows_spec(tm, gates.shape[1])]
                 + [_const_spec(a.shape) for a in consts],
        out_specs=_rows_spec(tm, d),
        out_shape=jax.ShapeDtypeStruct((n, d), F32),
        scratch_shapes=[pltpu.VMEM((tm, d), F32)],
        compiler_params=_params(("parallel",)),
        name="post",
    )(x, oa, ob, sg, gates, *consts)


def _row_tile(n, target):
    tm = min(n, target)
    assert n % tm == 0 and tm % 8 == 0
    return tm


def kernel(x_prompt, x_sample, cache_sb_k, cache_sb_v, state_hgrn, meta_tokens, ffn1_norm, ffn1_w_gate, ffn1_w_up, ffn1_w_down, mix_norm, w_in, b_gate, hg_lb_logits, hg_out_norm, w_branch_a, w_branch_b, w_out, ffn2_norm, ffn2_w_gate, ffn2_w_up, ffn2_w_down, final_norm):
    depth = w_in.shape[0]
    assert depth == 1, "single-layer step"
    nb, seq, d = x_prompt.shape
    ns, t_new, _ = x_sample.shape
    n_meta = meta_tokens.shape[0]
    row = lambda a: a.reshape(1, -1)

    mix_norms = (row(mix_norm[0]), row(b_gate[0]), hg_lb_logits)

    n_frames = nb * seq
    tm_p = _row_tile(seq, 512)
    n_run = ns * t_new
    small = jnp.concatenate([x_sample.reshape(n_run, d), meta_tokens.astype(F32)], axis=0)
    x1_s, *f1 = _ffn_cast_call(small, row(ffn1_norm[0]), ffn1_w_gate[0], ffn1_w_up[0],
                               ffn1_w_down[0])
    (q_s, kb_s, vb_s, hq_s, hk_s, hi_s, hlf_s, sg_s, gates_s, k_s, v_s, meta_kt, meta_vt,
     meta_kb, meta_vb, w_in_b) = _mix_small_call(x1_s, mix_norms[0], w_in[0], *mix_norms[1:],
                                                 n_run, n_meta)
    run = lambda a: a[:n_run]
    meta = lambda a: a[n_run:]
    token_minor = lambda a: jnp.transpose(a, (0, 1, 3, 4, 2))
    token_major = lambda a: jnp.transpose(a, (0, 1, 4, 2, 3))
    seq3 = lambda a, s: a.reshape(s, -1, a.shape[-1])
    rows = lambda a: a.reshape(-1, a.shape[-1])
    hg = lambda sel, s, arrs: [seq3(sel(a), s) for a in arrs]

    oa_s = _sbd_call(q_s, kb_s, vb_s, token_minor(cache_sb_k), token_minor(cache_sb_v), t_new)
    ob_s, st_s = _hg_call(*hg(run, ns, (hq_s, hk_s, hi_s, hlf_s)),
                          jnp.swapaxes(state_hgrn[0].astype(F32), -1, -2), t_new)
    post_norms = (row(ffn2_norm[0]), row(final_norm))
    y_s, wa, wb, wo, *f2 = _post_cast_call(
        x1_s, oa_s, rows(ob_s), sg_s, gates_s, row(hg_out_norm[0]),
        w_branch_a[0], w_branch_b[0], w_out[0], post_norms[0],
        ffn2_w_gate[0], ffn2_w_up[0], ffn2_w_down[0], post_norms[1])

    zero_state = jnp.zeros((1, HG_HEADS, HG_HEAD_DIM, HG_HEAD_DIM), F32)
    _, st_meta = _hg_call(*hg(meta, 1, (hq_s, hk_s, hi_s, hlf_s)), zero_state, n_meta)
    x1_p = _ffn_call(x_prompt.reshape(n_frames, d), row(ffn1_norm[0]), *f1,
                     _row_tile(n_frames, FFN_TILE))
    (q_p, kb_p, vb_p, hq_p, hk_p, hi_p, hlf_p, sg_p, gates_p, kt_full, vt_full) = (
        _mix_frames_call(x1_p.reshape(nb, seq, d), mix_norms[0], w_in_b, *mix_norms[1:],
                         meta_kt, meta_vt, tm_p, n_meta))
    oa_p = _sb_call(q_p, kb_p, vb_p, meta_kb[None], meta_vb[None], n_meta, KEY_BLOCK)
    ob_p, st_p = _hg_call(hq_p, hk_p, hi_p, hlf_p, st_meta, HG_CHUNK, HG_CHUNKS_PER_STEP)
    y_p = _post_call(x1_p, rows(oa_p), rows(ob_p), rows(sg_p), rows(gates_p),
                     row(hg_out_norm[0]), wa, wb, wo, post_norms[0], *f2, post_norms[1], tm_p)

    heads = lambda a: a.reshape(1, ns, t_new, SB_HEADS, SB_HEAD_DIM)
    return (y_p.reshape(nb, seq, d), y_s.reshape(ns, t_new, d),
            token_major(kt_full), token_major(vt_full),
            jnp.swapaxes(st_p, -1, -2)[None],
            heads(run(k_s)), heads(run(v_s)),
            jnp.swapaxes(st_s, -1, -2)[None])
```

```python
import functools

import jax
import jax.numpy as jnp
from jax import lax
from jax.experimental import pallas as pl
from jax.experimental.pallas import tpu as pltpu

F32 = jnp.float32
BF16 = jnp.bfloat16
EPS = 1e-6
SB_HEADS = 8
SB_HEAD_DIM = 64
SB_WIDTH = SB_HEADS * SB_HEAD_DIM
HG_HEADS = 4
HG_HEAD_DIM = 128
HG_WIDTH = HG_HEADS * HG_HEAD_DIM
LANES = 128
FF_CHUNK = 256
ROW_PART = 256
FFN_TILE, FFN_ROW_PART = 1024, 512
KEY_BLOCK = 128
WIDE_BLOCK = 256
SB_BLOCKS_PER_STEP = 8
SB_BLOCKS_SKEWED = 2
F32_EXP2_UNDERFLOW = -152.0
LOG2_E = 1.4426950408889634
HG_CHUNK = 128
HG_CHUNKS_PER_STEP = 4
HG_SAFE_LOG_DECAY = -60.0
VMEM_LIMIT_BYTES = 56 * 1024 * 1024

_NN = (((1,), (0,)), ((), ()))
_NT = (((1,), (1,)), ((), ()))
_TN = (((0,), (0,)), ((), ()))


def _const_spec(shape):
    nd = len(shape)
    return pl.BlockSpec(shape, lambda *_: (0,) * nd, pipeline_mode=pl.Buffered(1))


def _rows_spec(tm, width):
    return pl.BlockSpec((tm, width), lambda i: (i, 0))


def _params(sem):
    return pltpu.CompilerParams(dimension_semantics=sem, vmem_limit_bytes=VMEM_LIMIT_BYTES)


def _row_parts(rows, part=ROW_PART):
    if rows % part or rows == part:
        return [slice(0, rows)]
    return [slice(r, r + part) for r in range(0, rows, part)]


def _rms(x, g):
    return x * lax.rsqrt(jnp.mean(x * x, axis=-1, keepdims=True) + EPS) * g


def _ffn_chunk(h, wg, wu, wd):
    gate = jnp.dot(h, wg, preferred_element_type=F32)
    up = jnp.dot(h, wu, preferred_element_type=F32)
    act = (gate * jax.nn.sigmoid(gate) * up).astype(BF16)
    return jnp.dot(act, wd, preferred_element_type=F32)


def _half_ffn(x, g, wg_ref, wu_ref, wd_ref, acc_ref):
    h = _rms(x, g).astype(BF16)
    dff = wg_ref.shape[1]
    assert dff % FF_CHUNK == 0
    for c in range(dff // FF_CHUNK):
        cols = slice(c * FF_CHUNK, (c + 1) * FF_CHUNK)
        contrib = _ffn_chunk(h, wg_ref[:, cols], wu_ref[:, cols], wd_ref[cols, :])
        if c == 0:
            acc_ref[...] = contrib
        else:
            acc_ref[...] += contrib
    return x + 0.5 * acc_ref[...]


def _cast_ffn_chunk(h, w32_refs, w16_refs):
    wg, wu, wd = (r[...].astype(BF16) for r in w32_refs)
    for ref, wv in zip(w16_refs, (wg, wu, wd)):
        ref[...] = wv
    return _ffn_chunk(h, wg, wu, wd)


def _ff_chunk_specs(wg, wd, chunk_of_step):
    col = pl.BlockSpec((wg.shape[0], FF_CHUNK), lambda c: (0, chunk_of_step(c)))
    row = pl.BlockSpec((FF_CHUNK, wd.shape[1]), lambda c: (chunk_of_step(c), 0))
    return [col, col, row]


def _ffn_kernel(x_ref, n_ref, wg_ref, wu_ref, wd_ref, o_ref, acc_ref):
    for r in _row_parts(x_ref.shape[0], FFN_ROW_PART):
        o_ref[r, :] = _half_ffn(x_ref[r, :], n_ref[...], wg_ref, wu_ref, wd_ref, acc_ref.at[r, :])


def _ffn_cast_kernel(xa_ref, xb_ref, n_ref, wg_ref, wu_ref, wd_ref,
                     o_ref, wgb_ref, wub_ref, wdb_ref, x_ref, h_ref, acc_ref):
    c = pl.program_id(0)

    @pl.when(c == 0)
    def _():
        x_ref[...] = jnp.concatenate([xa_ref[...], xb_ref[...]], axis=0)
        h_ref[...] = _rms(x_ref[...], n_ref[...]).astype(BF16)
        acc_ref[...] = jnp.zeros_like(acc_ref)

    acc_ref[...] += _cast_ffn_chunk(h_ref[...], (wg_ref, wu_ref, wd_ref),
                                    (wgb_ref, wub_ref, wdb_ref))

    @pl.when(c == pl.num_programs(0) - 1)
    def _():
        o_ref[...] = x_ref[...] + 0.5 * acc_ref[...]


def _ffn_cast_call(xa, xb, norm, wg, wu, wd):
    n, d = xa.shape[0] + xb.shape[0], xa.shape[1]
    dff = wg.shape[1]
    assert dff % FF_CHUNK == 0 and xa.shape[0] % 8 == 0
    chunk_specs = _ff_chunk_specs(wg, wd, lambda c: c)
    whole = lambda rows: pl.BlockSpec((rows, d), lambda c: (0, 0))
    return pl.pallas_call(
        _ffn_cast_kernel,
        grid=(dff // FF_CHUNK,),
        in_specs=[whole(xa.shape[0]), whole(xb.shape[0]), _const_spec(norm.shape)] + chunk_specs,
        out_specs=[whole(n)] + chunk_specs,
        out_shape=[jax.ShapeDtypeStruct((n, d), F32)]
                  + [jax.ShapeDtypeStruct(a.shape, BF16) for a in (wg, wu, wd)],
        scratch_shapes=[pltpu.VMEM((n, d), F32), pltpu.VMEM((n, d), BF16),
                        pltpu.VMEM((n, d), F32)],
        compiler_params=_params(("arbitrary",)),
        name="ffn_cast",
    )(xa, xb, norm, wg, wu, wd)


def _ffn_call(x, norm, wg, wu, wd, tm):
    n, d = x.shape
    return pl.pallas_call(
        _ffn_kernel,
        grid=(n // tm,),
        in_specs=[_rows_spec(tm, d), _const_spec(norm.shape), _const_spec(wg.shape),
                  _const_spec(wu.shape), _const_spec(wd.shape)],
        out_specs=_rows_spec(tm, d),
        out_shape=jax.ShapeDtypeStruct((n, d), F32),
        scratch_shapes=[pltpu.VMEM((tm, d), F32)],
        compiler_params=_params(("parallel",)),
        name="ffn",
    )(x, norm, wg, wu, wd)


def _forget_lower_bound(lbl_ref):
    logits = lbl_ref[...]
    l0, l1 = logits[0:1], logits[1:2]
    m = jnp.maximum(l0, l1)
    e0, e1 = jnp.exp(l0 - m), jnp.exp(l1 - m)
    p0, p1 = e0 / (e0 + e1), e1 / (e0 + e1)
    return (p0 + p1) - p0


_N_MIX_GROUPS = 11
_MIX_GROUP_ORDER = (7, 8, 9, 10, 3, 6, 0, 1, 2, 4, 5)


def _mix_group(j, p, r, lb, bg_ref, outs):
    q_ref, kb_ref, vb_ref, hq_ref, hk_ref, hi_ref, hlf_ref, sg_ref, gates_ref = outs
    if j == 0:
        q_ref[r, :] = (p * (SB_HEAD_DIM ** -0.5 * LOG2_E)).astype(BF16)
    elif j == 1:
        kb_ref[r, :] = p.astype(BF16)
    elif j == 2:
        vb_ref[r, :] = p.astype(BF16)
    elif j == 3:
        f = lb + (1.0 - lb) * jax.nn.sigmoid(p)
        hk_ref[r, :] = (1.0 - f).astype(hk_ref.dtype)
        hlf_ref[r, :] = jnp.log(f)
    elif j == 4:
        hi_ref[r, :] = p.astype(hi_ref.dtype)
    elif j == 5:
        hq_ref[r, :] = p.astype(hq_ref.dtype)
    elif j == 6:
        sg_ref[r, :] = (p * jax.nn.sigmoid(p)).astype(sg_ref.dtype)
    else:
        cols = slice((j - 7) * SB_WIDTH, (j - 6) * SB_WIDTH)
        gates_ref[r, cols] = jax.nn.sigmoid(p + bg_ref[:, cols]).astype(gates_ref.dtype)


def _mix_body(x_ref, n_ref, win_ref, bg_ref, lbl_ref, *outs):
    w = SB_WIDTH
    lb = _forget_lower_bound(lbl_ref)
    kv = ([], [])
    for r in _row_parts(x_ref.shape[0]):
        h = _rms(x_ref[r, :], n_ref[...]).astype(BF16)
        for j in _MIX_GROUP_ORDER:
            p = jnp.dot(h, win_ref[:, j * w:(j + 1) * w], preferred_element_type=F32)
            _mix_group(j, p, r, lb, bg_ref, outs)
            if j in (1, 2):
                kv[j - 1].append(p)
    return jnp.concatenate(kv[0], axis=0), jnp.concatenate(kv[1], axis=0)


def _mix_small_kernel(*refs, n_run, n_meta):
    x_ref, n_ref, win_ref, bg_ref, lbl_ref = refs[:5]
    outs, (k_ref, v_ref, kt_ref, vt_ref, mk_ref, mv_ref, winb_ref, h_ref) = refs[5:14], refs[14:]
    c = pl.program_id(0)

    @pl.when(c == 0)
    def _():
        h_ref[...] = _rms(x_ref[...], n_ref[...]).astype(BF16)

    wb = win_ref[...].astype(BF16)
    winb_ref[...] = wb
    p = jnp.dot(h_ref[...], wb, preferred_element_type=F32)
    lb = _forget_lower_bound(lbl_ref)
    for j in range(_N_MIX_GROUPS):
        @pl.when(c == j)
        def _(j=j):
            _mix_group(j, p, slice(None), lb, bg_ref, outs)
            for group, rows_ref, cols_ref, meta_ref in ((1, k_ref, kt_ref, mk_ref),
                                                        (2, v_ref, vt_ref, mv_ref)):
                if j == group:
                    meta_rows = _pad_rows(p[n_run:n_run + n_meta], KEY_BLOCK)
                    rows_ref[...] = p
                    cols_ref[...] = meta_rows.T
                    meta_ref[...] = meta_rows.astype(BF16)


def _mix_frames_kernel(*refs, tm, n_meta, n_tiles):
    x_ref, consts, mk_ref, mv_ref = refs[0], refs[1:5], refs[5], refs[6]
    row_outs, (kt_ref, vt_ref, ck_ref, cv_ref) = refs[7:16], refs[16:]
    t = pl.program_id(1)
    shape = (SB_HEADS, SB_HEAD_DIM, tm)

    @pl.when(t == 0)
    def _():
        ck_ref[...] = mk_ref[...]
        cv_ref[...] = mv_ref[...]

    @pl.when(t < n_tiles)
    def _():
        k, v = _mix_body(x_ref.at[0], *consts, *[r.at[0] for r in row_outs])
        for rows, out_ref, carry_ref in ((k, kt_ref, ck_ref), (v, vt_ref, cv_ref)):
            cols = rows.T
            late = jnp.concatenate([carry_ref[:, :n_meta], cols[:, :tm - n_meta]], axis=1)
            out_ref[0, 0] = late.reshape(shape)
            carry_ref[:, :n_meta] = cols[:, tm - n_meta:]

    @pl.when(t == n_tiles)
    def _():
        for out_ref, carry_ref in ((kt_ref, ck_ref), (vt_ref, cv_ref)):
            tail = jnp.concatenate(
                [carry_ref[:, :n_meta], jnp.zeros((SB_WIDTH, tm - n_meta), F32)], axis=1)
            out_ref[0, 0] = tail.reshape(shape)


_MIX_WIDTHS = [(SB_WIDTH, BF16)] * 6 + [(SB_WIDTH, F32), (SB_WIDTH, BF16)]


def _mix_small_call(x, norm, w_in, b_gate, lb_logits, n_run, n_meta):
    n, d = x.shape
    assert w_in.shape[1] == _N_MIX_GROUPS * SB_WIDTH
    widths = _MIX_WIDTHS + [(2 * d, BF16), (SB_WIDTH, F32), (SB_WIDTH, F32)]
    assert KEY_BLOCK == LANES
    meta_t = jax.ShapeDtypeStruct((SB_WIDTH, LANES), F32)
    meta_rows = jax.ShapeDtypeStruct((KEY_BLOCK, SB_WIDTH), BF16)
    whole = lambda shape: pl.BlockSpec(shape, lambda c: (0, 0))
    group = pl.BlockSpec((d, SB_WIDTH), lambda c: (0, c))
    return pl.pallas_call(
        functools.partial(_mix_small_kernel, n_run=n_run, n_meta=n_meta),
        grid=(_N_MIX_GROUPS,),
        in_specs=[whole((n, d)), _const_spec(norm.shape), group,
                  _const_spec(b_gate.shape), _const_spec(lb_logits.shape)],
        out_specs=[whole((n, wd)) for wd, _ in widths] + [whole(meta_t.shape)] * 2
                  + [whole(meta_rows.shape)] * 2 + [group],
        out_shape=[jax.ShapeDtypeStruct((n, wd), dt) for wd, dt in widths] + [meta_t] * 2
                  + [meta_rows] * 2 + [jax.ShapeDtypeStruct(w_in.shape, BF16)],
        scratch_shapes=[pltpu.VMEM((n, d), BF16)],
        compiler_params=_params(("arbitrary",)),
        name="mix_small",
    )(x, norm, w_in, b_gate, lb_logits)


def _mix_frames_call(x, norm, w_in, b_gate, lb_logits, meta_kt, meta_vt, tm, n_meta):
    nb, seq, d = x.shape
    assert seq % tm == 0 and n_meta < tm
    n_tiles = seq // tm
    widths = _MIX_WIDTHS + [(2 * d, BF16)]
    rows_map = lambda b, t: (b, jnp.minimum(t, n_tiles - 1), 0)
    const2 = lambda shape: pl.BlockSpec(shape, lambda b, t: (0, 0), pipeline_mode=pl.Buffered(1))
    kv_shape = jax.ShapeDtypeStruct((1, nb, SB_HEADS, SB_HEAD_DIM, n_meta + seq), F32)
    kv_spec = pl.BlockSpec((1, 1, SB_HEADS, SB_HEAD_DIM, tm), lambda b, t: (0, b, 0, 0, t))
    carry = pltpu.VMEM((SB_WIDTH, LANES), F32)
    return pl.pallas_call(
        functools.partial(_mix_frames_kernel, tm=tm, n_meta=n_meta, n_tiles=n_tiles),
        grid=(nb, n_tiles + 1),
        in_specs=[pl.BlockSpec((1, tm, d), rows_map)]
                 + [const2(a.shape) for a in (norm, w_in, b_gate, lb_logits, meta_kt, meta_vt)],
        out_specs=[pl.BlockSpec((1, tm, wd), rows_map) for wd, _ in widths] + [kv_spec] * 2,
        out_shape=[jax.ShapeDtypeStruct((nb, seq, wd), dt) for wd, dt in widths] + [kv_shape] * 2,
        scratch_shapes=[carry, carry],
        compiler_params=_params(("arbitrary", "arbitrary")),
        name="mix_frames",
    )(x, norm, w_in, b_gate, lb_logits, meta_kt, meta_vt)


def _suffix_ones(width):
    r = lax.broadcasted_iota(jnp.int32, (width, width), 0)
    c = lax.broadcasted_iota(jnp.int32, (width, width), 1)
    return jnp.where(r >= c, 1.0, 0.0).astype(BF16)


def _key_before(tq, width, k0, q_first):
    row = lax.broadcasted_iota(jnp.int32, (tq, width), 0)
    col = lax.broadcasted_iota(jnp.int32, (tq, width), 1)
    return col + k0 < row + q_first


def _sb_scores(blocks, token_minor):
    k_dims = _NN if token_minor else _NT
    return [[lax.dot_general(q_ops[h], k_ops[h], k_dims, preferred_element_type=F32)
             for h in range(SB_HEADS)] for q_ops, k_ops, *_ in blocks]


def _sb_stay_sums(blocks, zs):
    cums = []
    for z_blk, (_, _, _, ones, mask, _, _) in zip(zs, blocks):
        stays = []
        for z in z_blk:
            nz = -z
            ls = jnp.minimum(nz, 0.0) - jnp.log2(1.0 + jnp.exp2(jnp.minimum(z, nz)))
            if mask is not None:
                ls = jnp.where(mask, ls, 0.0)
            stays.append(ls.astype(BF16))
        cums.append([jnp.dot(st, ones, preferred_element_type=F32) for st in stays])
    return cums


def _sb_apply(blocks, zs, cums, token_minor):
    heads = range(SB_HEADS)
    v_dims = _NT if token_minor else _NN
    ws = []
    for z_blk, cum_blk, (q_ops, _, _, ones, mask, r_ref, _) in zip(zs, cums, blocks):
        ws.append([])
        for h in heads:
            r = r_ref[h]
            cr = cum_blk[h] + jnp.concatenate([r] * (ones.shape[1] // LANES), axis=1)
            a = jnp.exp2(z_blk[h] + cr)
            if mask is not None:
                a = jnp.where(mask, a, 0.0)
            ws[-1].append(a.astype(BF16))
            r_ref[h] = jnp.broadcast_to(cr[:, 0:1], (q_ops[h].shape[0], LANES))
    for w_blk, (_, _, v_ops, _, _, _, acc_ref) in zip(ws, blocks):
        for h in heads:
            acc_ref[h] += lax.dot_general(w_blk[h], v_ops[h], v_dims, preferred_element_type=F32)


def _sb_sweep(blocks, token_minor=False):
    zs = _sb_scores(blocks, token_minor)
    _sb_apply(blocks, zs, _sb_stay_sums(blocks, zs), token_minor)


def _sb_sweep_skewed(sweeps):
    zs = _sb_scores(sweeps[0], False)
    for n, blocks in enumerate(sweeps):
        cums = _sb_stay_sums(blocks, zs)
        z_next = _sb_scores(sweeps[n + 1], False) if n + 1 < len(sweeps) else None
        _sb_apply(blocks, zs, cums, False)
        zs = z_next


def _sb_live(r_ref):
    m = r_ref[0]
    for h in range(1, SB_HEADS):
        m = jnp.maximum(m, r_ref[h])
    return (jnp.max(m) >= F32_EXP2_UNDERFLOW).astype(jnp.int32)


def _sweep_while(n_blocks, alive, r_ref, block, first=0):
    def body(carry):
        t, _ = carry
        block(t)
        return t + 1, _sb_live(r_ref)

    _, alive = lax.while_loop(lambda c: jnp.logical_and(c[0] < n_blocks, c[1] > 0),
                              body, (jnp.int32(first), alive))
    return alive


def _sb_kernel(q_ref, kn_ref, vn_ref, ko_ref, vo_ref, o_ref, qm_ref, acc_ref, r_ref, live_ref, *,
               tq, n_old):
    kb, wb = KEY_BLOCK, WIDE_BLOCK
    blocks_per_step = q_ref.shape[1] // tq
    lane = lax.broadcasted_iota(jnp.int32, (tq, LANES), 1)
    ones = {width: _suffix_ones(width) for width in (kb, wb)}
    old_rows = lax.broadcasted_iota(jnp.int32, (tq, kb), 1) < n_old

    def block(j, k_ref, v_ref, start, width, mask):
        k_all = k_ref[0, pl.ds(start, width), :].astype(BF16)
        v_all = v_ref[0, pl.ds(start, width), :].astype(BF16)
        pair = lambda a: [a[:, (h // 2) * LANES:(h // 2 + 1) * LANES] for h in range(SB_HEADS)]
        return ([qm_ref[j, h] for h in range(SB_HEADS)], pair(k_all), pair(v_all), ones[width],
                mask, r_ref.at[j], acc_ref.at[j])

    def place(j):
        q0 = (pl.program_id(1) * blocks_per_step + j) * tq
        below = q0 // kb
        return q0, below, pl.multiple_of(below * kb, kb)

    def write_out(j):
        rows = pl.ds(pl.multiple_of(j * tq, tq), tq)
        for p in range(SB_HEADS // 2):
            o_ref[0, rows, p * LANES:(p + 1) * LANES] = jnp.where(
                lane < SB_HEAD_DIM, acc_ref[j, 2 * p], acc_ref[j, 2 * p + 1]).astype(o_ref.dtype)

    def start(j):
        rows = pl.ds(pl.multiple_of(j * tq, tq), tq)
        for h in range(SB_HEADS):
            p = h // 2
            qp = q_ref[0, rows, p * LANES:(p + 1) * LANES]
            keep = (lane < SB_HEAD_DIM) if h % 2 == 0 else (lane >= SB_HEAD_DIM)
            qm_ref[j, h] = jnp.where(keep, qp, jnp.zeros_like(qp))
            r_ref[j, h] = jnp.zeros((tq, LANES), F32)
            acc_ref[j, h] = jnp.zeros((tq, LANES), F32)

    def own(j):
        q0, _, d0 = place(j)
        return block(j, kn_ref, vn_ref, d0, kb, _key_before(tq, kb, d0, q0))

    def own_and_under(j):
        _, _, d0 = place(j)
        return [own(j), block(j, kn_ref, vn_ref, pl.multiple_of(d0 - wb, kb), wb, None)]

    def note_reach(j):
        reach = r_ref[j, 0]
        for h in range(1, SB_HEADS):
            reach = jnp.maximum(reach, r_ref[j, h])
        live_ref[...] = jnp.maximum(live_ref[...], reach)

    def first_sweep(j, carry):
        _, below, _ = place(j)
        start(j)
        old = lambda: block(j, ko_ref, vo_ref, 0, kb, old_rows)

        @pl.when(below == 0)
        def _():
            _sb_sweep([own(j), old()])

        @pl.when(below == 1)
        def _():
            _sb_sweep([own(j), block(j, kn_ref, vn_ref, 0, kb, None), old()])

        @pl.when(below >= 2)
        def _():
            _sb_sweep(own_and_under(j))
            note_reach(j)

        write_out(j)
        return carry

    def first_sweep_group(jj, carry):
        js = [group * jj + n for n in range(group)]
        _, below, _ = place(js[0])
        for j in js:
            start(j)
        old = lambda j: block(j, ko_ref, vo_ref, 0, kb, old_rows)

        @pl.when(below == 0)
        def _():
            _sb_sweep([own(js[0]), old(js[0])])
            _sb_sweep([own(js[1]), block(js[1], kn_ref, vn_ref, 0, kb, None), old(js[1])])
            if group > 2:
                _sb_sweep_skewed([own_and_under(j) for j in js[2:]])
                for j in js[2:]:
                    note_reach(j)

        @pl.when(below >= 2)
        def _():
            _sb_sweep_skewed([own_and_under(j) for j in js])
            for j in js:
                note_reach(j)

        for j in js:
            write_out(j)
        return carry

    def further_sweeps(j, carry):
        _, below, d0 = place(j)
        top = d0 - wb
        left = jnp.maximum(below - 2, 0)
        live = r_ref.at[j]
        alive = _sweep_while(left // 2, _sb_live(live), live, lambda t: _sb_sweep([block(
            j, kn_ref, vn_ref, pl.multiple_of(top - (t + 1) * wb, kb), wb, None)]))
        alive = _sweep_while(left % 2, alive, live,
                             lambda t: _sb_sweep([block(j, kn_ref, vn_ref, 0, kb, None)]))
        _sweep_while((below >= 2).astype(jnp.int32), alive, live,
                     lambda t: _sb_sweep([block(j, ko_ref, vo_ref, 0, kb, old_rows)]))
        write_out(j)
        return carry

    live_ref[...] = jnp.full(live_ref.shape, -jnp.inf, F32)
    group = SB_BLOCKS_SKEWED
    if blocks_per_step % group == 0 and KEY_BLOCK == tq:
        lax.fori_loop(0, blocks_per_step // group, first_sweep_group, 0)
    else:
        lax.fori_loop(0, blocks_per_step, first_sweep, 0)

    @pl.when(jnp.max(live_ref[...]) >= F32_EXP2_UNDERFLOW)
    def _():
        lax.fori_loop(0, blocks_per_step, further_sweeps, 0)


def _sbd_kernel(q_ref, kn_ref, vn_ref, ck_ref, cv_ref, o_ref, acc_ref, r_ref, *, tq, n_old):
    kb, wb, hd = KEY_BLOCK, WIDE_BLOCK, SB_HEAD_DIM
    heads = range(SB_HEADS)
    n_tail, n_full = n_old % kb, n_old // kb
    n_wide = (n_full * kb) // wb
    rest = n_full * kb - n_wide * wb
    per_head = lambda a: [a[:, h * hd:(h + 1) * hd] for h in heads]

    q_ops = per_head(q_ref[...])
    for h in heads:
        r_ref[h] = jnp.zeros((tq, LANES), F32)
        acc_ref[h] = jnp.zeros((tq, hd), F32)

    _sb_sweep([(q_ops, per_head(_pad_rows(kn_ref[...], kb)), per_head(_pad_rows(vn_ref[...], kb)),
                _suffix_ones(kb), _key_before(tq, kb, 0, 0), r_ref, acc_ref)])
    alive = _sb_live(r_ref)

    def cached(start, width, valid):
        def load(ref, h):
            cols = ref[0, 0, h, :, pl.ds(start, valid)].astype(BF16)
            if valid < width:
                cols = jnp.concatenate([cols, jnp.zeros((hd, width - valid), BF16)], axis=1)
            return cols

        mask = None if valid == width else (
            lax.broadcasted_iota(jnp.int32, (tq, width), 1) < valid)
        _sb_sweep([(q_ops, [load(ck_ref, h) for h in heads], [load(cv_ref, h) for h in heads],
                    _suffix_ones(width), mask, r_ref, acc_ref)], token_minor=True)

    if n_tail:
        alive = _sweep_while(1, alive, r_ref, lambda t: cached(n_full * kb, kb, n_tail))
    if n_wide:
        alive = _sweep_while(n_wide, alive, r_ref, lambda t: cached(
            pl.multiple_of(rest + (n_wide - 1 - t) * wb, kb), wb, wb))
    if rest:
        _sweep_while(1, alive, r_ref, lambda t: cached(0, rest, rest))

    o_ref[...] = jnp.concatenate([acc_ref[h] for h in heads], axis=1).astype(o_ref.dtype)


def _sb_call(q, k_new, v_new, k_old, v_old, n_old, tq):
    s, lq, w = q.shape
    ln = k_new.shape[1]
    lo = k_old.shape[1]
    assert lq % tq == 0 and ln % KEY_BLOCK == 0 and n_old <= KEY_BLOCK <= lo
    assert tq <= KEY_BLOCK and KEY_BLOCK % tq == 0
    step = tq * SB_BLOCKS_PER_STEP if lq % (tq * SB_BLOCKS_PER_STEP) == 0 else tq
    old_map = (lambda b, i: (b, 0, 0)) if k_old.shape[0] == s else (lambda b, i: (0, 0, 0))
    return pl.pallas_call(
        functools.partial(_sb_kernel, tq=tq, n_old=n_old),
        grid=(s, lq // step),
        in_specs=[pl.BlockSpec((1, step, w), lambda b, i: (b, i, 0)),
                  pl.BlockSpec((1, ln, w), lambda b, i: (b, 0, 0)),
                  pl.BlockSpec((1, ln, w), lambda b, i: (b, 0, 0)),
                  pl.BlockSpec((1, lo, w), old_map),
                  pl.BlockSpec((1, lo, w), old_map)],
        out_specs=pl.BlockSpec((1, step, w), lambda b, i: (b, i, 0)),
        out_shape=jax.ShapeDtypeStruct((s, lq, w), BF16),
        scratch_shapes=[pltpu.VMEM((step // tq, SB_HEADS, tq, LANES), BF16),
                        pltpu.VMEM((step // tq, SB_HEADS, tq, LANES), F32),
                        pltpu.VMEM((step // tq, SB_HEADS, tq, LANES), F32),
                        pltpu.VMEM((tq, LANES), F32)],
        compiler_params=_params(("parallel", "parallel")),
        name="sb",
    )(q, k_new, v_new, k_old, v_old)


def _sbd_call(q, k_new, v_new, cache_k, cache_v, tq):
    s, w = cache_k.shape[1], q.shape[1]
    n_old = cache_k.shape[-1]
    assert tq <= KEY_BLOCK and q.shape[0] >= s * tq
    new_spec = pl.BlockSpec((tq, w), lambda b: (b, 0))
    cache_spec = pl.BlockSpec((1, 1) + cache_k.shape[2:], lambda b: (0, b, 0, 0, 0))
    return pl.pallas_call(
        functools.partial(_sbd_kernel, tq=tq, n_old=n_old),
        grid=(s,),
        in_specs=[new_spec, new_spec, new_spec, cache_spec, cache_spec],
        out_specs=new_spec,
        out_shape=jax.ShapeDtypeStruct((s * tq, w), BF16),
        scratch_shapes=[pltpu.VMEM((SB_HEADS, tq, SB_HEAD_DIM), F32),
                        pltpu.VMEM((SB_HEADS, tq, LANES), F32)],
        compiler_params=_params(("parallel",)),
        name="sb_decode",
    )(q, k_new, v_new, cache_k, cache_v)


def _pad_rows(x, rows):
    if x.shape[0] == rows:
        return x
    return jnp.concatenate([x, jnp.zeros((rows - x.shape[0], x.shape[1]), x.dtype)], axis=0)


def _hg_kernel(q_ref, k_ref, i_ref, lf_ref, s0_ref, o_ref, sout_ref, st_ref, *, chunk):
    c = pl.program_id(0)
    hd = HG_HEAD_DIM
    ns, n_chunks = q_ref.shape[0], q_ref.shape[1] // chunk
    streams = range(ns)
    units = [(g, b) for g in range(n_chunks) for b in streams]
    rows = lambda g: slice(g * chunk, (g + 1) * chunk)
    head_cols = [slice(h * hd, (h + 1) * hd) for h in range(HG_HEADS)]

    @pl.when(c == 0)
    def _():
        for b in streams:
            for h in range(HG_HEADS):
                st_ref[b, h] = s0_ref[b if s0_ref.shape[0] == ns else 0, h].T

    trow = lax.broadcasted_iota(jnp.int32, (chunk, chunk), 0)
    tcol = lax.broadcasted_iota(jnp.int32, (chunk, chunk), 1)
    causal = trow >= tcol

    token = lax.broadcasted_iota(jnp.int32, (chunk, lf_ref.shape[2]), 0)
    cum = {}
    for g, b in units:
        part = lf_ref[b, rows(g), :]
        shift = 1
        while shift < chunk:
            part = part + jnp.where(token >= shift, pltpu.roll(part, shift, axis=0), 0.0)
            shift *= 2
        cum[g, b] = part
    last = {u: cum[u][chunk - 1:chunk, :] for u in units}
    weakest = functools.reduce(jnp.minimum, last.values())
    safe = jnp.min(weakest) >= HG_SAFE_LOG_DECAY

    @pl.when(safe)
    def _():
        qd, kd, kl, iv, dl = {}, {}, {}, {}, {}
        for g, b in units:
            k = k_ref[b, rows(g), :]
            qd[g, b] = (q_ref[b, rows(g), :] * jnp.exp(cum[g, b])).astype(BF16)
            kd[g, b] = (k * jnp.exp(-cum[g, b])).astype(BF16)
            kl[g, b] = (k * jnp.exp(last[g, b] - cum[g, b])).astype(BF16)
            iv[g, b] = i_ref[b, rows(g), :].astype(BF16)
            dl[g, b] = jnp.exp(last[g, b])
        sc = {(u, h): lax.dot_general(qd[u][:, sl], kd[u][:, sl], _NT, preferred_element_type=F32)
              for u in units for h, sl in enumerate(head_cols)}
        grown = {(u, h): lax.dot_general(iv[u][:, sl], kl[u][:, sl], _TN,
                                         preferred_element_type=F32)
                 for u in units for h, sl in enumerate(head_cols)}
        within = {(u, h): jnp.dot(jnp.where(causal, sc[u, h], 0.0).astype(BF16), iv[u][:, sl],
                                  preferred_element_type=F32)
                  for u in units for h, sl in enumerate(head_cols)}
        for g in range(n_chunks):
            carried = {(b, h): lax.dot_general(qd[g, b][:, sl], st_ref[b, h].astype(BF16), _NT,
                                               preferred_element_type=F32)
                       for b in streams for h, sl in enumerate(head_cols)}
            for b in streams:
                for h, sl in enumerate(head_cols):
                    o_ref[b, rows(g), sl] = (within[(g, b), h] + carried[b, h]).astype(o_ref.dtype)
                    st_ref[b, h] = st_ref[b, h] * dl[g, b][:, sl] + grown[(g, b), h]

    @pl.when(jnp.logical_not(safe))
    def _():
        lane_t = lax.broadcasted_iota(jnp.int32, (hd, hd), 1)
        row_t = lax.broadcasted_iota(jnp.int32, (hd, hd), 0)

        def one_chunk(n, carry):
            b, r0 = n // n_chunks, pl.multiple_of((n % n_chunks) * chunk, chunk)
            for h, sl in enumerate(head_cols):
                load = lambda ref: ref[b, pl.ds(r0, chunk), sl].astype(F32)
                f_t = _pad_rows(jnp.exp(load(lf_ref)), hd).T
                k_t = _pad_rows(load(k_ref), hd).T
                q_t = _pad_rows(load(q_ref), hd).T
                i_p = _pad_rows(load(i_ref), hd)

                def token(t, sc, f_t=f_t, k_t=k_t, q_t=q_t, i_p=i_p):
                    s, o = sc
                    sel = lane_t == t
                    fc = jnp.sum(jnp.where(sel, f_t, 0.0), axis=1, keepdims=True)
                    kc = jnp.sum(jnp.where(sel, k_t, 0.0), axis=1, keepdims=True)
                    qc = jnp.sum(jnp.where(sel, q_t, 0.0), axis=1, keepdims=True)
                    i_row = jnp.sum(jnp.where(row_t == t, i_p, 0.0), axis=0, keepdims=True)
                    s = s * fc + kc * i_row
                    o = jnp.where(row_t == t, jnp.sum(qc * s, axis=0, keepdims=True), o)
                    return s, o

                s, o = lax.fori_loop(0, chunk, token,
                                     (st_ref[b, h].T, jnp.zeros((hd, hd), F32)))
                st_ref[b, h] = s.T
                o_ref[b, pl.ds(r0, chunk), sl] = o[:chunk].astype(o_ref.dtype)
            return carry

        lax.fori_loop(0, ns * n_chunks, one_chunk, 0)

    @pl.when(c == pl.num_programs(0) - 1)
    def _():
        for b in streams:
            for h in range(HG_HEADS):
                sout_ref[b, h] = st_ref[b, h].T


def _hg_call(q, k, iv, lf, s0t, chunk, chunks_per_step=1):
    s, l, w = q.shape
    rows = chunk * chunks_per_step if l % (chunk * chunks_per_step) == 0 else chunk
    assert l % rows == 0
    seq_spec = pl.BlockSpec((s, rows, w), lambda c: (0, c, 0))
    st_shape = (s, HG_HEADS, HG_HEAD_DIM, HG_HEAD_DIM)
    return pl.pallas_call(
        functools.partial(_hg_kernel, chunk=chunk),
        grid=(l // rows,),
        in_specs=[seq_spec, seq_spec, seq_spec, seq_spec,
                  pl.BlockSpec(s0t.shape, lambda c: (0, 0, 0, 0))],
        out_specs=[seq_spec, pl.BlockSpec(st_shape, lambda c: (0, 0, 0, 0))],
        out_shape=[jax.ShapeDtypeStruct((s, l, w), BF16), jax.ShapeDtypeStruct(st_shape, F32)],
        scratch_shapes=[pltpu.VMEM(st_shape, F32)],
        compiler_params=_params(("arbitrary",)),
        name="hgrn",
    )(q, k, iv, lf, s0t)


def _merge_branches(x, oa, ob, sg, gates, hn, wa, wb, wo):
    hd = HG_HEAD_DIM
    d = x.shape[1]
    heads = []
    for h in range(HG_HEADS):
        sl = slice(h * hd, (h + 1) * hd)
        heads.append(_rms(ob[:, sl].astype(F32), hn[:, sl]) * sg[:, sl].astype(F32))
    obn = jnp.concatenate(heads, axis=-1).astype(BF16)
    ma = jnp.dot(oa, wa, preferred_element_type=F32)
    mb = jnp.dot(obn, wb, preferred_element_type=F32)
    merged = gates[:, :d].astype(F32) * ma + gates[:, d:].astype(F32) * mb
    return x + jnp.dot(merged.astype(BF16), wo, preferred_element_type=F32)


def _post_kernel(x_ref, oa_ref, ob_ref, sg_ref, gates_ref, hn_ref, wa_ref, wb_ref, wo_ref,
                 n_ref, wg_ref, wu_ref, wd_ref, fn_ref, y_ref, acc_ref):
    x = jnp.concatenate(
        [_merge_branches(x_ref[r, :], oa_ref[r, :], ob_ref[r, :], sg_ref[r, :], gates_ref[r, :],
                         hn_ref[...], wa_ref[...], wb_ref[...], wo_ref[...])
         for r in _row_parts(x_ref.shape[0])], axis=0)
    x = _half_ffn(x, n_ref[...], wg_ref, wu_ref, wd_ref, acc_ref)
    y_ref[...] = _rms(x, fn_ref[...])


def _post_cast_kernel(x_ref, oa_ref, ob_ref, sg_ref, gates_ref, hn_ref, wa_ref, wb_ref, wo_ref,
                      n_ref, wg_ref, wu_ref, wd_ref, fn_ref,
                      y_ref, wab_ref, wbb_ref, wob_ref, wgb_ref, wub_ref, wdb_ref,
                      x2_ref, h_ref, acc_ref):
    c = pl.program_id(0)

    @pl.when(c == 0)
    def _():
        wa, wb, wo = (r[...].astype(BF16) for r in (wa_ref, wb_ref, wo_ref))
        for ref, wv in zip((wab_ref, wbb_ref, wob_ref), (wa, wb, wo)):
            ref[...] = wv
        x2 = _merge_branches(x_ref[...], oa_ref[...], ob_ref[...], sg_ref[...], gates_ref[...],
                             hn_ref[...], wa, wb, wo)
        x2_ref[...] = x2
        h_ref[...] = _rms(x2, n_ref[...]).astype(BF16)
        acc_ref[...] = jnp.zeros_like(acc_ref)

    @pl.when(c > 0)
    def _():
        acc_ref[...] += _cast_ffn_chunk(h_ref[...], (wg_ref, wu_ref, wd_ref),
                                        (wgb_ref, wub_ref, wdb_ref))

    @pl.when(c == pl.num_programs(0) - 1)
    def _():
        y_ref[...] = _rms(x2_ref[...] + 0.5 * acc_ref[...], fn_ref[...])


def _post_cast_call(x, oa, ob, sg, gates, hn, wa, wb, wo, norm, wg, wu, wd, fnorm):
    n, d = oa.shape[0], x.shape[1]
    dff = wg.shape[1]
    assert dff % FF_CHUNK == 0
    chunk_specs = _ff_chunk_specs(wg, wd, lambda c: jnp.maximum(c - 1, 0))
    whole = lambda a: pl.BlockSpec(a.shape, lambda c: (0,) * a.ndim)
    top = lambda a: pl.BlockSpec((n, a.shape[1]), lambda c: (0, 0))
    return pl.pallas_call(
        _post_cast_kernel,
        grid=(dff // FF_CHUNK + 1,),
        in_specs=[top(a) for a in (x, oa, ob, sg, gates)]
                 + [_const_spec(a.shape) for a in (hn, wa, wb, wo, norm)] + chunk_specs
                 + [_const_spec(fnorm.shape)],
        out_specs=[top(x)] + [whole(a) for a in (wa, wb, wo)] + chunk_specs,
        out_shape=[jax.ShapeDtypeStruct((n, d), F32)]
                  + [jax.ShapeDtypeStruct(a.shape, BF16) for a in (wa, wb, wo, wg, wu, wd)],
        scratch_shapes=[pltpu.VMEM((n, d), F32), pltpu.VMEM((n, d), BF16),
                        pltpu.VMEM((n, d), F32)],
        compiler_params=_params(("arbitrary",)),
        name="post_cast",
    )(x, oa, ob, sg, gates, hn, wa, wb, wo, norm, wg, wu, wd, fnorm)


def _post_call(x, oa, ob, sg, gates, hn, wa, wb, wo, norm, wg, wu, wd, fnorm, tm):
    n, d = x.shape
    consts = [hn, wa, wb, wo, norm, wg, wu, wd, fnorm]
    return pl.pallas_call(
        _post_kernel,
        grid=(n // tm,),
        in_specs=[_rows_spec(tm, d), _rows_spec(tm, oa.shape[1]), _rows_spec(tm, ob.shape[1]),
                  _rows_spec(tm, sg.shape[1]), _rows_spec(tm, gates.shape[1])]
                 + [_const_spec(a.shape) for a in consts],
        out_specs=_rows_spec(tm, d),
        out_shape=jax.ShapeDtypeStruct((n, d), F32),
        scratch_shapes=[pltpu.VMEM((tm, d), F32)],
        compiler_params=_params(("parallel",)),
        name="post",
    )(x, oa, ob, sg, gates, *consts)


def _row_tile(n, target):
    tm = min(n, target)
    assert n % tm == 0 and tm % 8 == 0
    return tm


def kernel(x_prompt, x_sample, cache_sb_k, cache_sb_v, state_hgrn, meta_tokens, ffn1_norm, ffn1_w_gate, ffn1_w_up, ffn1_w_down, mix_norm, w_in, b_gate, hg_lb_logits, hg_out_norm, w_branch_a, w_branch_b, w_out, ffn2_norm, ffn2_w_gate, ffn2_w_up, ffn2_w_down, final_norm):
    depth = w_in.shape[0]
    assert depth == 1, "single-layer step"
    nb, seq, d = x_prompt.shape
    ns, t_new, _ = x_sample.shape
    n_meta = meta_tokens.shape[0]
    row = lambda a: a.reshape(1, -1)

    mix_norms = (row(mix_norm[0]), row(b_gate[0]), hg_lb_logits)

    n_frames = nb * seq
    tm_p = _row_tile(seq, 512)
    n_run = ns * t_new
    x1_s, *f1 = _ffn_cast_call(x_sample.reshape(n_run, d), meta_tokens.astype(F32),
                               row(ffn1_norm[0]), ffn1_w_gate[0], ffn1_w_up[0], ffn1_w_down[0])
    (q_s, kb_s, vb_s, hq_s, hk_s, hi_s, hlf_s, sg_s, gates_s, k_s, v_s, meta_kt, meta_vt,
     meta_kb, meta_vb, w_in_b) = _mix_small_call(x1_s, mix_norms[0], w_in[0], *mix_norms[1:],
                                                 n_run, n_meta)
    run = lambda a: a[:n_run]
    meta = lambda a: a[n_run:]
    token_minor = lambda a: jnp.transpose(a, (0, 1, 3, 4, 2))
    token_major = lambda a: jnp.transpose(a, (0, 1, 4, 2, 3))
    seq3 = lambda a, s: a.reshape(s, -1, a.shape[-1])
    rows = lambda a: a.reshape(-1, a.shape[-1])
    hg = lambda sel, s, arrs: [seq3(sel(a), s) for a in arrs]

    oa_s = _sbd_call(q_s, kb_s, vb_s, token_minor(cache_sb_k), token_minor(cache_sb_v), t_new)
    ob_s, st_s = _hg_call(*hg(run, ns, (hq_s, hk_s, hi_s, hlf_s)), state_hgrn[0].astype(F32),
                          t_new)
    post_norms = (row(ffn2_norm[0]), row(final_norm))
    y_s, wa, wb, wo, *f2 = _post_cast_call(
        x1_s, oa_s, rows(ob_s), sg_s, gates_s, row(hg_out_norm[0]),
        w_branch_a[0], w_branch_b[0], w_out[0], post_norms[0],
        ffn2_w_gate[0], ffn2_w_up[0], ffn2_w_down[0], post_norms[1])

    zero_state = jnp.zeros((1, HG_HEADS, HG_HEAD_DIM, HG_HEAD_DIM), F32)
    _, st_meta = _hg_call(*hg(meta, 1, (hq_s, hk_s, hi_s, hlf_s)), zero_state, n_meta)
    x1_p = _ffn_call(x_prompt.reshape(n_frames, d), row(ffn1_norm[0]), *f1,
                     _row_tile(n_frames, FFN_TILE))
    (q_p, kb_p, vb_p, hq_p, hk_p, hi_p, hlf_p, sg_p, gates_p, kt_full, vt_full) = (
        _mix_frames_call(x1_p.reshape(nb, seq, d), mix_norms[0], w_in_b, *mix_norms[1:],
                         meta_kt, meta_vt, tm_p, n_meta))
    oa_p = _sb_call(q_p, kb_p, vb_p, meta_kb[None], meta_vb[None], n_meta, KEY_BLOCK)
    ob_p, st_p = _hg_call(hq_p, hk_p, hi_p, hlf_p, st_meta, HG_CHUNK, HG_CHUNKS_PER_STEP)
    y_p = _post_call(x1_p, rows(oa_p), rows(ob_p), rows(sg_p), rows(gates_p),
                     row(hg_out_norm[0]), wa, wb, wo, post_norms[0], *f2, post_norms[1], tm_p)

    heads = lambda a: a.reshape(1, ns, t_new, SB_HEADS, SB_HEAD_DIM)
    return (y_p.reshape(nb, seq, d), y_s.reshape(ns, t_new, d),
            token_major(kt_full), token_major(vt_full), st_p[None],
            heads(run(k_s)), heads(run(v_s)), st_s[None])
```

```python
import functools

import jax
import jax.numpy as jnp
from jax import lax
from jax.experimental import pallas as pl
from jax.experimental.pallas import tpu as pltpu

F32 = jnp.float32
BF16 = jnp.bfloat16
EPS = 1e-6
SB_HEADS = 8
SB_HEAD_DIM = 64
SB_WIDTH = SB_HEADS * SB_HEAD_DIM
HG_HEADS = 4
HG_HEAD_DIM = 128
HG_WIDTH = HG_HEADS * HG_HEAD_DIM
LANES = 128
FF_CHUNK = 256
ROW_PART = 256
FFN_TILE, FFN_ROW_PART = 1024, 512
KEY_BLOCK = 128
WIDE_BLOCK = 256
SB_BLOCKS_PER_STEP = 16
SB_BLOCKS_SKEWED = 2
F32_EXP2_UNDERFLOW = -152.0
LOG2_E = 1.4426950408889634
HG_CHUNK = 128
HG_CHUNKS_PER_STEP = 4
HG_SAFE_LOG_DECAY = -60.0
VMEM_LIMIT_BYTES = 56 * 1024 * 1024

_NN = (((1,), (0,)), ((), ()))
_NT = (((1,), (1,)), ((), ()))
_TN = (((0,), (0,)), ((), ()))


def _const_spec(shape):
    nd = len(shape)
    return pl.BlockSpec(shape, lambda *_: (0,) * nd, pipeline_mode=pl.Buffered(1))


def _rows_spec(tm, width):
    return pl.BlockSpec((tm, width), lambda i: (i, 0))


def _params(sem):
    return pltpu.CompilerParams(dimension_semantics=sem, vmem_limit_bytes=VMEM_LIMIT_BYTES)


def _row_parts(rows, part=ROW_PART):
    if rows % part or rows == part:
        return [slice(0, rows)]
    return [slice(r, r + part) for r in range(0, rows, part)]


def _rms(x, g):
    return x * lax.rsqrt(jnp.mean(x * x, axis=-1, keepdims=True) + EPS) * g


def _ffn_chunk(h, wg, wu, wd):
    gate = jnp.dot(h, wg, preferred_element_type=F32)
    up = jnp.dot(h, wu, preferred_element_type=F32)
    act = (gate * jax.nn.sigmoid(gate) * up).astype(BF16)
    return jnp.dot(act, wd, preferred_element_type=F32)


def _half_ffn(x, g, wg_ref, wu_ref, wd_ref, acc_ref):
    h = _rms(x, g).astype(BF16)
    dff = wg_ref.shape[1]
    assert dff % FF_CHUNK == 0
    for c in range(dff // FF_CHUNK):
        cols = slice(c * FF_CHUNK, (c + 1) * FF_CHUNK)
        contrib = _ffn_chunk(h, wg_ref[:, cols], wu_ref[:, cols], wd_ref[cols, :])
        if c == 0:
            acc_ref[...] = contrib
        else:
            acc_ref[...] += contrib
    return x + 0.5 * acc_ref[...]


def _cast_ffn_chunk(h, w32_refs, w16_refs):
    wg, wu, wd = (r[...].astype(BF16) for r in w32_refs)
    for ref, wv in zip(w16_refs, (wg, wu, wd)):
        ref[...] = wv
    return _ffn_chunk(h, wg, wu, wd)


def _ff_chunk_specs(wg, wd, chunk_of_step):
    col = pl.BlockSpec((wg.shape[0], FF_CHUNK), lambda c: (0, chunk_of_step(c)))
    row = pl.BlockSpec((FF_CHUNK, wd.shape[1]), lambda c: (chunk_of_step(c), 0))
    return [col, col, row]


def _ffn_kernel(x_ref, n_ref, wg_ref, wu_ref, wd_ref, o_ref, acc_ref):
    for r in _row_parts(x_ref.shape[0], FFN_ROW_PART):
        o_ref[r, :] = _half_ffn(x_ref[r, :], n_ref[...], wg_ref, wu_ref, wd_ref, acc_ref.at[r, :])


def _ffn_cast_kernel(xa_ref, xb_ref, n_ref, wg_ref, wu_ref, wd_ref,
                     o_ref, wgb_ref, wub_ref, wdb_ref, x_ref, h_ref, acc_ref):
    c = pl.program_id(0)

    @pl.when(c == 0)
    def _():
        x_ref[...] = jnp.concatenate([xa_ref[...], xb_ref[...]], axis=0)
        h_ref[...] = _rms(x_ref[...], n_ref[...]).astype(BF16)
        acc_ref[...] = jnp.zeros_like(acc_ref)

    acc_ref[...] += _cast_ffn_chunk(h_ref[...], (wg_ref, wu_ref, wd_ref),
                                    (wgb_ref, wub_ref, wdb_ref))

    @pl.when(c == pl.num_programs(0) - 1)
    def _():
        o_ref[...] = x_ref[...] + 0.5 * acc_ref[...]


def _ffn_cast_call(xa, xb, norm, wg, wu, wd):
    n, d = xa.shape[0] + xb.shape[0], xa.shape[1]
    dff = wg.shape[1]
    assert dff % FF_CHUNK == 0 and xa.shape[0] % 8 == 0
    chunk_specs = _ff_chunk_specs(wg, wd, lambda c: c)
    whole = lambda rows: pl.BlockSpec((rows, d), lambda c: (0, 0))
    return pl.pallas_call(
        _ffn_cast_kernel,
        grid=(dff // FF_CHUNK,),
        in_specs=[whole(xa.shape[0]), whole(xb.shape[0]), _const_spec(norm.shape)] + chunk_specs,
        out_specs=[whole(n)] + chunk_specs,
        out_shape=[jax.ShapeDtypeStruct((n, d), F32)]
                  + [jax.ShapeDtypeStruct(a.shape, BF16) for a in (wg, wu, wd)],
        scratch_shapes=[pltpu.VMEM((n, d), F32), pltpu.VMEM((n, d), BF16),
                        pltpu.VMEM((n, d), F32)],
        compiler_params=_params(("arbitrary",)),
        name="ffn_cast",
    )(xa, xb, norm, wg, wu, wd)


def _ffn_call(x, norm, wg, wu, wd, tm):
    n, d = x.shape
    return pl.pallas_call(
        _ffn_kernel,
        grid=(n // tm,),
        in_specs=[_rows_spec(tm, d), _const_spec(norm.shape), _const_spec(wg.shape),
                  _const_spec(wu.shape), _const_spec(wd.shape)],
        out_specs=_rows_spec(tm, d),
        out_shape=jax.ShapeDtypeStruct((n, d), F32),
        scratch_shapes=[pltpu.VMEM((tm, d), F32)],
        compiler_params=_params(("parallel",)),
        name="ffn",
    )(x, norm, wg, wu, wd)


def _forget_lower_bound(lbl_ref):
    logits = lbl_ref[...]
    l0, l1 = logits[0:1], logits[1:2]
    m = jnp.maximum(l0, l1)
    e0, e1 = jnp.exp(l0 - m), jnp.exp(l1 - m)
    p0, p1 = e0 / (e0 + e1), e1 / (e0 + e1)
    return (p0 + p1) - p0


_N_MIX_GROUPS = 11
_MIX_GROUP_ORDER = (7, 8, 9, 10, 3, 6, 0, 1, 2, 4, 5)


def _mix_group(j, p, r, lb, bg_ref, outs):
    q_ref, kb_ref, vb_ref, hq_ref, hk_ref, hi_ref, hlf_ref, sg_ref, gates_ref = outs
    if j == 0:
        q_ref[r, :] = (p * (SB_HEAD_DIM ** -0.5 * LOG2_E)).astype(BF16)
    elif j == 1:
        kb_ref[r, :] = p.astype(BF16)
    elif j == 2:
        vb_ref[r, :] = p.astype(BF16)
    elif j == 3:
        f = lb + (1.0 - lb) * jax.nn.sigmoid(p)
        hk_ref[r, :] = (1.0 - f).astype(hk_ref.dtype)
        hlf_ref[r, :] = jnp.log(f)
    elif j == 4:
        hi_ref[r, :] = p.astype(hi_ref.dtype)
    elif j == 5:
        hq_ref[r, :] = p.astype(hq_ref.dtype)
    elif j == 6:
        sg_ref[r, :] = (p * jax.nn.sigmoid(p)).astype(sg_ref.dtype)
    else:
        cols = slice((j - 7) * SB_WIDTH, (j - 6) * SB_WIDTH)
        gates_ref[r, cols] = jax.nn.sigmoid(p + bg_ref[:, cols]).astype(gates_ref.dtype)


def _mix_body(x_ref, n_ref, win_ref, bg_ref, lbl_ref, *outs):
    w = SB_WIDTH
    lb = _forget_lower_bound(lbl_ref)
    kv = ([], [])
    for r in _row_parts(x_ref.shape[0]):
        h = _rms(x_ref[r, :], n_ref[...]).astype(BF16)
        for j in _MIX_GROUP_ORDER:
            p = jnp.dot(h, win_ref[:, j * w:(j + 1) * w], preferred_element_type=F32)
            _mix_group(j, p, r, lb, bg_ref, outs)
            if j in (1, 2):
                kv[j - 1].append(p)
    return jnp.concatenate(kv[0], axis=0), jnp.concatenate(kv[1], axis=0)


def _mix_small_kernel(*refs, n_run, n_meta):
    x_ref, n_ref, win_ref, bg_ref, lbl_ref = refs[:5]
    outs, (k_ref, v_ref, kt_ref, vt_ref, mk_ref, mv_ref, winb_ref, h_ref) = refs[5:14], refs[14:]
    c = pl.program_id(0)

    @pl.when(c == 0)
    def _():
        h_ref[...] = _rms(x_ref[...], n_ref[...]).astype(BF16)

    wb = win_ref[...].astype(BF16)
    winb_ref[...] = wb
    p = jnp.dot(h_ref[...], wb, preferred_element_type=F32)
    lb = _forget_lower_bound(lbl_ref)
    for j in range(_N_MIX_GROUPS):
        @pl.when(c == j)
        def _(j=j):
            _mix_group(j, p, slice(None), lb, bg_ref, outs)
            for group, rows_ref, cols_ref, meta_ref in ((1, k_ref, kt_ref, mk_ref),
                                                        (2, v_ref, vt_ref, mv_ref)):
                if j == group:
                    meta_rows = _pad_rows(p[n_run:n_run + n_meta], KEY_BLOCK)
                    rows_ref[...] = p
                    cols_ref[...] = meta_rows.T
                    meta_ref[...] = meta_rows.astype(BF16)


def _mix_frames_kernel(*refs, tm, n_meta, n_tiles):
    x_ref, consts, mk_ref, mv_ref = refs[0], refs[1:5], refs[5], refs[6]
    row_outs, (kt_ref, vt_ref, ck_ref, cv_ref) = refs[7:16], refs[16:]
    t = pl.program_id(1)
    shape = (SB_HEADS, SB_HEAD_DIM, tm)

    @pl.when(t == 0)
    def _():
        ck_ref[...] = mk_ref[...]
        cv_ref[...] = mv_ref[...]

    @pl.when(t < n_tiles)
    def _():
        k, v = _mix_body(x_ref.at[0], *consts, *[r.at[0] for r in row_outs])
        for rows, out_ref, carry_ref in ((k, kt_ref, ck_ref), (v, vt_ref, cv_ref)):
            cols = rows.T
            late = jnp.concatenate([carry_ref[:, :n_meta], cols[:, :tm - n_meta]], axis=1)
            out_ref[0, 0] = late.reshape(shape)
            carry_ref[:, :n_meta] = cols[:, tm - n_meta:]

    @pl.when(t == n_tiles)
    def _():
        for out_ref, carry_ref in ((kt_ref, ck_ref), (vt_ref, cv_ref)):
            tail = jnp.concatenate(
                [carry_ref[:, :n_meta], jnp.zeros((SB_WIDTH, tm - n_meta), F32)], axis=1)
            out_ref[0, 0] = tail.reshape(shape)


_MIX_WIDTHS = [(SB_WIDTH, BF16)] * 6 + [(SB_WIDTH, F32), (SB_WIDTH, BF16)]


def _mix_small_call(x, norm, w_in, b_gate, lb_logits, n_run, n_meta):
    n, d = x.shape
    assert w_in.shape[1] == _N_MIX_GROUPS * SB_WIDTH
    widths = _MIX_WIDTHS + [(2 * d, BF16), (SB_WIDTH, F32), (SB_WIDTH, F32)]
    assert KEY_BLOCK == LANES
    meta_t = jax.ShapeDtypeStruct((SB_WIDTH, LANES), F32)
    meta_rows = jax.ShapeDtypeStruct((KEY_BLOCK, SB_WIDTH), BF16)
    whole = lambda shape: pl.BlockSpec(shape, lambda c: (0, 0))
    group = pl.BlockSpec((d, SB_WIDTH), lambda c: (0, c))
    return pl.pallas_call(
        functools.partial(_mix_small_kernel, n_run=n_run, n_meta=n_meta),
        grid=(_N_MIX_GROUPS,),
        in_specs=[whole((n, d)), _const_spec(norm.shape), group,
                  _const_spec(b_gate.shape), _const_spec(lb_logits.shape)],
        out_specs=[whole((n, wd)) for wd, _ in widths] + [whole(meta_t.shape)] * 2
                  + [whole(meta_rows.shape)] * 2 + [group],
        out_shape=[jax.ShapeDtypeStruct((n, wd), dt) for wd, dt in widths] + [meta_t] * 2
                  + [meta_rows] * 2 + [jax.ShapeDtypeStruct(w_in.shape, BF16)],
        scratch_shapes=[pltpu.VMEM((n, d), BF16)],
        compiler_params=_params(("arbitrary",)),
        name="mix_small",
    )(x, norm, w_in, b_gate, lb_logits)


def _mix_frames_call(x, norm, w_in, b_gate, lb_logits, meta_kt, meta_vt, tm, n_meta):
    nb, seq, d = x.shape
    assert seq % tm == 0 and n_meta < tm
    n_tiles = seq // tm
    widths = _MIX_WIDTHS + [(2 * d, BF16)]
    rows_map = lambda b, t: (b, jnp.minimum(t, n_tiles - 1), 0)
    const2 = lambda shape: pl.BlockSpec(shape, lambda b, t: (0, 0), pipeline_mode=pl.Buffered(1))
    kv_shape = jax.ShapeDtypeStruct((1, nb, SB_HEADS, SB_HEAD_DIM, n_meta + seq), F32)
    kv_spec = pl.BlockSpec((1, 1, SB_HEADS, SB_HEAD_DIM, tm), lambda b, t: (0, b, 0, 0, t))
    carry = pltpu.VMEM((SB_WIDTH, LANES), F32)
    return pl.pallas_call(
        functools.partial(_mix_frames_kernel, tm=tm, n_meta=n_meta, n_tiles=n_tiles),
        grid=(nb, n_tiles + 1),
        in_specs=[pl.BlockSpec((1, tm, d), rows_map)]
                 + [const2(a.shape) for a in (norm, w_in, b_gate, lb_logits, meta_kt, meta_vt)],
        out_specs=[pl.BlockSpec((1, tm, wd), rows_map) for wd, _ in widths] + [kv_spec] * 2,
        out_shape=[jax.ShapeDtypeStruct((nb, seq, wd), dt) for wd, dt in widths] + [kv_shape] * 2,
        scratch_shapes=[carry, carry],
        compiler_params=_params(("arbitrary", "arbitrary")),
        name="mix_frames",
    )(x, norm, w_in, b_gate, lb_logits, meta_kt, meta_vt)


def _suffix_ones(width):
    r = lax.broadcasted_iota(jnp.int32, (width, width), 0)
    c = lax.broadcasted_iota(jnp.int32, (width, width), 1)
    return jnp.where(r >= c, 1.0, 0.0).astype(BF16)


def _key_before(tq, width, k0, q_first):
    row = lax.broadcasted_iota(jnp.int32, (tq, width), 0)
    col = lax.broadcasted_iota(jnp.int32, (tq, width), 1)
    return col + k0 < row + q_first


def _sb_scores(blocks, token_minor):
    k_dims = _NN if token_minor else _NT
    return [[lax.dot_general(q_ops[h], k_ops[h], k_dims, preferred_element_type=F32)
             for h in range(SB_HEADS)] for q_ops, k_ops, *_ in blocks]


def _sb_stay_sums(blocks, zs):
    cums = []
    for z_blk, (_, _, _, ones, mask, _, _) in zip(zs, blocks):
        stays = []
        for z in z_blk:
            nz = -z
            ls = jnp.minimum(nz, 0.0) - jnp.log2(1.0 + jnp.exp2(jnp.minimum(z, nz)))
            if mask is not None:
                ls = jnp.where(mask, ls, 0.0)
            stays.append(ls.astype(BF16))
        cums.append([jnp.dot(st, ones, preferred_element_type=F32) for st in stays])
    return cums


def _sb_apply(blocks, zs, cums, token_minor):
    heads = range(SB_HEADS)
    v_dims = _NT if token_minor else _NN
    ws = []
    for z_blk, cum_blk, (q_ops, _, _, ones, mask, r_ref, _) in zip(zs, cums, blocks):
        ws.append([])
        for h in heads:
            r = r_ref[h]
            cr = cum_blk[h] + jnp.concatenate([r] * (ones.shape[1] // LANES), axis=1)
            a = jnp.exp2(z_blk[h] + cr)
            if mask is not None:
                a = jnp.where(mask, a, 0.0)
            ws[-1].append(a.astype(BF16))
            r_ref[h] = jnp.broadcast_to(cr[:, 0:1], (q_ops[h].shape[0], LANES))
    for w_blk, (_, _, v_ops, _, _, _, acc_ref) in zip(ws, blocks):
        for h in heads:
            acc_ref[h] += lax.dot_general(w_blk[h], v_ops[h], v_dims, preferred_element_type=F32)


def _sb_sweep(blocks, token_minor=False):
    zs = _sb_scores(blocks, token_minor)
    _sb_apply(blocks, zs, _sb_stay_sums(blocks, zs), token_minor)


def _sb_sweep_skewed(sweeps):
    zs = _sb_scores(sweeps[0], False)
    for n, blocks in enumerate(sweeps):
        cums = _sb_stay_sums(blocks, zs)
        z_next = _sb_scores(sweeps[n + 1], False) if n + 1 < len(sweeps) else None
        _sb_apply(blocks, zs, cums, False)
        zs = z_next


def _sb_live(r_ref):
    m = r_ref[0]
    for h in range(1, SB_HEADS):
        m = jnp.maximum(m, r_ref[h])
    return (jnp.max(m) >= F32_EXP2_UNDERFLOW).astype(jnp.int32)


def _sweep_while(n_blocks, alive, r_ref, block, first=0):
    def body(carry):
        t, _ = carry
        block(t)
        return t + 1, _sb_live(r_ref)

    _, alive = lax.while_loop(lambda c: jnp.logical_and(c[0] < n_blocks, c[1] > 0),
                              body, (jnp.int32(first), alive))
    return alive


def _sb_kernel(q_ref, kn_ref, vn_ref, ko_ref, vo_ref, o_ref, qm_ref, acc_ref, r_ref, live_ref, *,
               tq, n_old):
    kb, wb = KEY_BLOCK, WIDE_BLOCK
    blocks_per_step = q_ref.shape[1] // tq
    lane = lax.broadcasted_iota(jnp.int32, (tq, LANES), 1)
    ones = {width: _suffix_ones(width) for width in (kb, wb)}
    old_rows = lax.broadcasted_iota(jnp.int32, (tq, kb), 1) < n_old

    def block(j, k_ref, v_ref, start, width, mask):
        k_all = k_ref[0, pl.ds(start, width), :].astype(BF16)
        v_all = v_ref[0, pl.ds(start, width), :].astype(BF16)
        pair = lambda a: [a[:, (h // 2) * LANES:(h // 2 + 1) * LANES] for h in range(SB_HEADS)]
        return ([qm_ref[j, h] for h in range(SB_HEADS)], pair(k_all), pair(v_all), ones[width],
                mask, r_ref.at[j], acc_ref.at[j])

    def place(j):
        q0 = (pl.program_id(1) * blocks_per_step + j) * tq
        below = q0 // kb
        return q0, below, pl.multiple_of(below * kb, kb)

    def write_out(j):
        rows = pl.ds(pl.multiple_of(j * tq, tq), tq)
        for p in range(SB_HEADS // 2):
            o_ref[0, rows, p * LANES:(p + 1) * LANES] = jnp.where(
                lane < SB_HEAD_DIM, acc_ref[j, 2 * p], acc_ref[j, 2 * p + 1]).astype(o_ref.dtype)

    def start(j):
        rows = pl.ds(pl.multiple_of(j * tq, tq), tq)
        for h in range(SB_HEADS):
            p = h // 2
            qp = q_ref[0, rows, p * LANES:(p + 1) * LANES]
            keep = (lane < SB_HEAD_DIM) if h % 2 == 0 else (lane >= SB_HEAD_DIM)
            qm_ref[j, h] = jnp.where(keep, qp, jnp.zeros_like(qp))
            r_ref[j, h] = jnp.zeros((tq, LANES), F32)
            acc_ref[j, h] = jnp.zeros((tq, LANES), F32)

    def own(j):
        q0, _, d0 = place(j)
        return block(j, kn_ref, vn_ref, d0, kb, _key_before(tq, kb, d0, q0))

    def own_and_under(j):
        _, _, d0 = place(j)
        return [own(j), block(j, kn_ref, vn_ref, pl.multiple_of(d0 - wb, kb), wb, None)]

    def note_reach(j):
        reach = r_ref[j, 0]
        for h in range(1, SB_HEADS):
            reach = jnp.maximum(reach, r_ref[j, h])
        live_ref[...] = jnp.maximum(live_ref[...], reach)

    def first_sweep(j, carry):
        _, below, _ = place(j)
        start(j)
        old = lambda: block(j, ko_ref, vo_ref, 0, kb, old_rows)

        @pl.when(below == 0)
        def _():
            _sb_sweep([own(j), old()])

        @pl.when(below == 1)
        def _():
            _sb_sweep([own(j), block(j, kn_ref, vn_ref, 0, kb, None), old()])

        @pl.when(below >= 2)
        def _():
            _sb_sweep(own_and_under(j))
            note_reach(j)

        write_out(j)
        return carry

    def first_sweep_group(jj, carry):
        js = [group * jj + n for n in range(group)]
        _, below, _ = place(js[0])
        for j in js:
            start(j)
        old = lambda j: block(j, ko_ref, vo_ref, 0, kb, old_rows)

        @pl.when(below == 0)
        def _():
            _sb_sweep([own(js[0]), old(js[0])])
            _sb_sweep([own(js[1]), block(js[1], kn_ref, vn_ref, 0, kb, None), old(js[1])])
            if group > 2:
                _sb_sweep_skewed([own_and_under(j) for j in js[2:]])
                for j in js[2:]:
                    note_reach(j)

        @pl.when(below >= 2)
        def _():
            _sb_sweep_skewed([own_and_under(j) for j in js])
            for j in js:
                note_reach(j)

        for j in js:
            write_out(j)
        return carry

    def further_sweeps(j, carry):
        _, below, d0 = place(j)
        top = d0 - wb
        left = jnp.maximum(below - 2, 0)
        live = r_ref.at[j]
        alive = _sweep_while(left // 2, _sb_live(live), live, lambda t: _sb_sweep([block(
            j, kn_ref, vn_ref, pl.multiple_of(top - (t + 1) * wb, kb), wb, None)]))
        alive = _sweep_while(left % 2, alive, live,
                             lambda t: _sb_sweep([block(j, kn_ref, vn_ref, 0, kb, None)]))
        _sweep_while((below >= 2).astype(jnp.int32), alive, live,
                     lambda t: _sb_sweep([block(j, ko_ref, vo_ref, 0, kb, old_rows)]))
        write_out(j)
        return carry

    live_ref[...] = jnp.full(live_ref.shape, -jnp.inf, F32)
    group = SB_BLOCKS_SKEWED
    if blocks_per_step % group == 0 and KEY_BLOCK == tq:
        lax.fori_loop(0, blocks_per_step // group, first_sweep_group, 0)
    else:
        lax.fori_loop(0, blocks_per_step, first_sweep, 0)

    @pl.when(jnp.max(live_ref[...]) >= F32_EXP2_UNDERFLOW)
    def _():
        lax.fori_loop(0, blocks_per_step, further_sweeps, 0)


def _sbd_kernel(q_ref, kn_ref, vn_ref, ck_ref, cv_ref, o_ref, acc_ref, r_ref, *, tq, n_old):
    kb, wb, hd = KEY_BLOCK, WIDE_BLOCK, SB_HEAD_DIM
    heads = range(SB_HEADS)
    n_tail, n_full = n_old % kb, n_old // kb
    n_wide = (n_full * kb) // wb
    rest = n_full * kb - n_wide * wb
    per_head = lambda a: [a[:, h * hd:(h + 1) * hd] for h in heads]

    q_ops = per_head(q_ref[...])
    for h in heads:
        r_ref[h] = jnp.zeros((tq, LANES), F32)
        acc_ref[h] = jnp.zeros((tq, hd), F32)

    _sb_sweep([(q_ops, per_head(_pad_rows(kn_ref[...], kb)), per_head(_pad_rows(vn_ref[...], kb)),
                _suffix_ones(kb), _key_before(tq, kb, 0, 0), r_ref, acc_ref)])
    alive = _sb_live(r_ref)

    def cached(start, width, valid):
        def load(ref, h):
            cols = ref[0, 0, h, :, pl.ds(start, valid)].astype(BF16)
            if valid < width:
                cols = jnp.concatenate([cols, jnp.zeros((hd, width - valid), BF16)], axis=1)
            return cols

        mask = None if valid == width else (
            lax.broadcasted_iota(jnp.int32, (tq, width), 1) < valid)
        _sb_sweep([(q_ops, [load(ck_ref, h) for h in heads], [load(cv_ref, h) for h in heads],
                    _suffix_ones(width), mask, r_ref, acc_ref)], token_minor=True)

    if n_tail:
        alive = _sweep_while(1, alive, r_ref, lambda t: cached(n_full * kb, kb, n_tail))
    if n_wide:
        alive = _sweep_while(n_wide, alive, r_ref, lambda t: cached(
            pl.multiple_of(rest + (n_wide - 1 - t) * wb, kb), wb, wb))
    if rest:
        _sweep_while(1, alive, r_ref, lambda t: cached(0, rest, rest))

    o_ref[...] = jnp.concatenate([acc_ref[h] for h in heads], axis=1).astype(o_ref.dtype)


def _sb_call(q, k_new, v_new, k_old, v_old, n_old, tq):
    s, lq, w = q.shape
    ln = k_new.shape[1]
    lo = k_old.shape[1]
    assert lq % tq == 0 and ln % KEY_BLOCK == 0 and n_old <= KEY_BLOCK <= lo
    assert tq <= KEY_BLOCK and KEY_BLOCK % tq == 0
    step = tq * SB_BLOCKS_PER_STEP if lq % (tq * SB_BLOCKS_PER_STEP) == 0 else tq
    old_map = (lambda b, i: (b, 0, 0)) if k_old.shape[0] == s else (lambda b, i: (0, 0, 0))
    return pl.pallas_call(
        functools.partial(_sb_kernel, tq=tq, n_old=n_old),
        grid=(s, lq // step),
        in_specs=[pl.BlockSpec((1, step, w), lambda b, i: (b, i, 0)),
                  pl.BlockSpec((1, ln, w), lambda b, i: (b, 0, 0)),
                  pl.BlockSpec((1, ln, w), lambda b, i: (b, 0, 0)),
                  pl.BlockSpec((1, lo, w), old_map),
                  pl.BlockSpec((1, lo, w), old_map)],
        out_specs=pl.BlockSpec((1, step, w), lambda b, i: (b, i, 0)),
        out_shape=jax.ShapeDtypeStruct((s, lq, w), BF16),
        scratch_shapes=[pltpu.VMEM((step // tq, SB_HEADS, tq, LANES), BF16),
                        pltpu.VMEM((step // tq, SB_HEADS, tq, LANES), F32),
                        pltpu.VMEM((step // tq, SB_HEADS, tq, LANES), F32),
                        pltpu.VMEM((tq, LANES), F32)],
        compiler_params=_params(("parallel", "parallel")),
        name="sb",
    )(q, k_new, v_new, k_old, v_old)


def _sbd_call(q, k_new, v_new, cache_k, cache_v, tq):
    s, w = cache_k.shape[1], q.shape[1]
    n_old = cache_k.shape[-1]
    assert tq <= KEY_BLOCK and q.shape[0] >= s * tq
    new_spec = pl.BlockSpec((tq, w), lambda b: (b, 0))
    cache_spec = pl.BlockSpec((1, 1) + cache_k.shape[2:], lambda b: (0, b, 0, 0, 0))
    return pl.pallas_call(
        functools.partial(_sbd_kernel, tq=tq, n_old=n_old),
        grid=(s,),
        in_specs=[new_spec, new_spec, new_spec, cache_spec, cache_spec],
        out_specs=new_spec,
        out_shape=jax.ShapeDtypeStruct((s * tq, w), BF16),
        scratch_shapes=[pltpu.VMEM((SB_HEADS, tq, SB_HEAD_DIM), F32),
                        pltpu.VMEM((SB_HEADS, tq, LANES), F32)],
        compiler_params=_params(("parallel",)),
        name="sb_decode",
    )(q, k_new, v_new, cache_k, cache_v)


def _pad_rows(x, rows):
    if x.shape[0] == rows:
        return x
    return jnp.concatenate([x, jnp.zeros((rows - x.shape[0], x.shape[1]), x.dtype)], axis=0)


def _hg_kernel(q_ref, k_ref, i_ref, lf_ref, s0_ref, o_ref, sout_ref, st_ref, *, chunk):
    c = pl.program_id(0)
    hd = HG_HEAD_DIM
    ns, n_chunks = q_ref.shape[0], q_ref.shape[1] // chunk
    streams = range(ns)
    units = [(g, b) for g in range(n_chunks) for b in streams]
    rows = lambda g: slice(g * chunk, (g + 1) * chunk)
    head_cols = [slice(h * hd, (h + 1) * hd) for h in range(HG_HEADS)]

    @pl.when(c == 0)
    def _():
        for b in streams:
            for h in range(HG_HEADS):
                st_ref[b, h] = s0_ref[b if s0_ref.shape[0] == ns else 0, h].T

    trow = lax.broadcasted_iota(jnp.int32, (chunk, chunk), 0)
    tcol = lax.broadcasted_iota(jnp.int32, (chunk, chunk), 1)
    causal = trow >= tcol

    token = lax.broadcasted_iota(jnp.int32, (chunk, lf_ref.shape[2]), 0)
    cum = {}
    for g, b in units:
        part = lf_ref[b, rows(g), :]
        shift = 1
        while shift < chunk:
            part = part + jnp.where(token >= shift, pltpu.roll(part, shift, axis=0), 0.0)
            shift *= 2
        cum[g, b] = part
    last = {u: cum[u][chunk - 1:chunk, :] for u in units}
    weakest = functools.reduce(jnp.minimum, last.values())
    safe = jnp.min(weakest) >= HG_SAFE_LOG_DECAY

    @pl.when(safe)
    def _():
        qd, kd, kl, iv, dl = {}, {}, {}, {}, {}
        for g, b in units:
            k = k_ref[b, rows(g), :]
            qd[g, b] = (q_ref[b, rows(g), :] * jnp.exp(cum[g, b])).astype(BF16)
            kd[g, b] = (k * jnp.exp(-cum[g, b])).astype(BF16)
            kl[g, b] = (k * jnp.exp(last[g, b] - cum[g, b])).astype(BF16)
            iv[g, b] = i_ref[b, rows(g), :].astype(BF16)
            dl[g, b] = jnp.exp(last[g, b])
        sc = {(u, h): lax.dot_general(qd[u][:, sl], kd[u][:, sl], _NT, preferred_element_type=F32)
              for u in units for h, sl in enumerate(head_cols)}
        grown = {(u, h): lax.dot_general(iv[u][:, sl], kl[u][:, sl], _TN,
                                         preferred_element_type=F32)
                 for u in units for h, sl in enumerate(head_cols)}
        within = {(u, h): jnp.dot(jnp.where(causal, sc[u, h], 0.0).astype(BF16), iv[u][:, sl],
                                  preferred_element_type=F32)
                  for u in units for h, sl in enumerate(head_cols)}
        for g in range(n_chunks):
            carried = {(b, h): lax.dot_general(qd[g, b][:, sl], st_ref[b, h].astype(BF16), _NT,
                                               preferred_element_type=F32)
                       for b in streams for h, sl in enumerate(head_cols)}
            for b in streams:
                for h, sl in enumerate(head_cols):
                    o_ref[b, rows(g), sl] = (within[(g, b), h] + carried[b, h]).astype(o_ref.dtype)
                    st_ref[b, h] = st_ref[b, h] * dl[g, b][:, sl] + grown[(g, b), h]

    @pl.when(jnp.logical_not(safe))
    def _():
        lane_t = lax.broadcasted_iota(jnp.int32, (hd, hd), 1)
        row_t = lax.broadcasted_iota(jnp.int32, (hd, hd), 0)

        def one_chunk(n, carry):
            b, r0 = n // n_chunks, pl.multiple_of((n % n_chunks) * chunk, chunk)
            for h, sl in enumerate(head_cols):
                load = lambda ref: ref[b, pl.ds(r0, chunk), sl].astype(F32)
                f_t = _pad_rows(jnp.exp(load(lf_ref)), hd).T
                k_t = _pad_rows(load(k_ref), hd).T
                q_t = _pad_rows(load(q_ref), hd).T
                i_p = _pad_rows(load(i_ref), hd)

                def token(t, sc, f_t=f_t, k_t=k_t, q_t=q_t, i_p=i_p):
                    s, o = sc
                    sel = lane_t == t
                    fc = jnp.sum(jnp.where(sel, f_t, 0.0), axis=1, keepdims=True)
                    kc = jnp.sum(jnp.where(sel, k_t, 0.0), axis=1, keepdims=True)
                    qc = jnp.sum(jnp.where(sel, q_t, 0.0), axis=1, keepdims=True)
                    i_row = jnp.sum(jnp.where(row_t == t, i_p, 0.0), axis=0, keepdims=True)
                    s = s * fc + kc * i_row
                    o = jnp.where(row_t == t, jnp.sum(qc * s, axis=0, keepdims=True), o)
                    return s, o

                s, o = lax.fori_loop(0, chunk, token,
                                     (st_ref[b, h].T, jnp.zeros((hd, hd), F32)))
                st_ref[b, h] = s.T
                o_ref[b, pl.ds(r0, chunk), sl] = o[:chunk].astype(o_ref.dtype)
            return carry

        lax.fori_loop(0, ns * n_chunks, one_chunk, 0)

    @pl.when(c == pl.num_programs(0) - 1)
    def _():
        for b in streams:
            for h in range(HG_HEADS):
                sout_ref[b, h] = st_ref[b, h].T


def _hg_call(q, k, iv, lf, s0t, chunk, chunks_per_step=1):
    s, l, w = q.shape
    rows = chunk * chunks_per_step if l % (chunk * chunks_per_step) == 0 else chunk
    assert l % rows == 0
    seq_spec = pl.BlockSpec((s, rows, w), lambda c: (0, c, 0))
    st_shape = (s, HG_HEADS, HG_HEAD_DIM, HG_HEAD_DIM)
    return pl.pallas_call(
        functools.partial(_hg_kernel, chunk=chunk),
        grid=(l // rows,),
        in_specs=[seq_spec, seq_spec, seq_spec, seq_spec,
                  pl.BlockSpec(s0t.shape, lambda c: (0, 0, 0, 0))],
        out_specs=[seq_spec, pl.BlockSpec(st_shape, lambda c: (0, 0, 0, 0))],
        out_shape=[jax.ShapeDtypeStruct((s, l, w), BF16), jax.ShapeDtypeStruct(st_shape, F32)],
        scratch_shapes=[pltpu.VMEM(st_shape, F32)],
        compiler_params=_params(("arbitrary",)),
        name="hgrn",
    )(q, k, iv, lf, s0t)


def _merge_branches(x, oa, ob, sg, gates, hn, wa, wb, wo):
    hd = HG_HEAD_DIM
    d = x.shape[1]
    heads = []
    for h in range(HG_HEADS):
        sl = slice(h * hd, (h + 1) * hd)
        heads.append(_rms(ob[:, sl].astype(F32), hn[:, sl]) * sg[:, sl].astype(F32))
    obn = jnp.concatenate(heads, axis=-1).astype(BF16)
    ma = jnp.dot(oa, wa, preferred_element_type=F32)
    mb = jnp.dot(obn, wb, preferred_element_type=F32)
    merged = gates[:, :d].astype(F32) * ma + gates[:, d:].astype(F32) * mb
    return x + jnp.dot(merged.astype(BF16), wo, preferred_element_type=F32)


def _post_kernel(x_ref, oa_ref, ob_ref, sg_ref, gates_ref, hn_ref, wa_ref, wb_ref, wo_ref,
                 n_ref, wg_ref, wu_ref, wd_ref, fn_ref, y_ref, acc_ref):
    x = jnp.concatenate(
        [_merge_branches(x_ref[r, :], oa_ref[r, :], ob_ref[r, :], sg_ref[r, :], gates_ref[r, :],
                         hn_ref[...], wa_ref[...], wb_ref[...], wo_ref[...])
         for r in _row_parts(x_ref.shape[0])], axis=0)
    x = _half_ffn(x, n_ref[...], wg_ref, wu_ref, wd_ref, acc_ref)
    y_ref[...] = _rms(x, fn_ref[...])


def _post_cast_kernel(x_ref, oa_ref, ob_ref, sg_ref, gates_ref, hn_ref, wa_ref, wb_ref, wo_ref,
                      n_ref, wg_ref, wu_ref, wd_ref, fn_ref,
                      y_ref, wab_ref, wbb_ref, wob_ref, wgb_ref, wub_ref, wdb_ref,
                      x2_ref, h_ref, acc_ref):
    c = pl.program_id(0)

    @pl.when(c == 0)
    def _():
        wa, wb, wo = (r[...].astype(BF16) for r in (wa_ref, wb_ref, wo_ref))
        for ref, wv in zip((wab_ref, wbb_ref, wob_ref), (wa, wb, wo)):
            ref[...] = wv
        x2 = _merge_branches(x_ref[...], oa_ref[...], ob_ref[...], sg_ref[...], gates_ref[...],
                             hn_ref[...], wa, wb, wo)
        x2_ref[...] = x2
        h_ref[...] = _rms(x2, n_ref[...]).astype(BF16)
        acc_ref[...] = jnp.zeros_like(acc_ref)

    @pl.when(c > 0)
    def _():
        acc_ref[...] += _cast_ffn_chunk(h_ref[...], (wg_ref, wu_ref, wd_ref),
                                        (wgb_ref, wub_ref, wdb_ref))

    @pl.when(c == pl.num_programs(0) - 1)
    def _():
        y_ref[...] = _rms(x2_ref[...] + 0.5 * acc_ref[...], fn_ref[...])


def _post_cast_call(x, oa, ob, sg, gates, hn, wa, wb, wo, norm, wg, wu, wd, fnorm):
    n, d = oa.shape[0], x.shape[1]
    dff = wg.shape[1]
    assert dff % FF_CHUNK == 0
    chunk_specs = _ff_chunk_specs(wg, wd, lambda c: jnp.maximum(c - 1, 0))
    whole = lambda a: pl.BlockSpec(a.shape, lambda c: (0,) * a.ndim)
    top = lambda a: pl.BlockSpec((n, a.shape[1]), lambda c: (0, 0))
    return pl.pallas_call(
        _post_cast_kernel,
        grid=(dff // FF_CHUNK + 1,),
        in_specs=[top(a) for a in (x, oa, ob, sg, gates)]
                 + [_const_spec(a.shape) for a in (hn, wa, wb, wo, norm)] + chunk_specs
                 + [_const_spec(fnorm.shape)],
        out_specs=[top(x)] + [whole(a) for a in (wa, wb, wo)] + chunk_specs,
        out_shape=[jax.ShapeDtypeStruct((n, d), F32)]
                  + [jax.ShapeDtypeStruct(a.shape, BF16) for a in (wa, wb, wo, wg, wu, wd)],
        scratch_shapes=[pltpu.VMEM((n, d), F32), pltpu.VMEM((n, d), BF16),
                        pltpu.VMEM((n, d), F32)],
        compiler_params=_params(("arbitrary",)),
        name="post_cast",
    )(x, oa, ob, sg, gates, hn, wa, wb, wo, norm, wg, wu, wd, fnorm)


def _post_call(x, oa, ob, sg, gates, hn, wa, wb, wo, norm, wg, wu, wd, fnorm, tm):
    n, d = x.shape
    consts = [hn, wa, wb, wo, norm, wg, wu, wd, fnorm]
    return pl.pallas_call(
        _post_kernel,
        grid=(n // tm,),
        in_specs=[_rows_spec(tm, d), _rows_spec(tm, oa.shape[1]), _rows_spec(tm, ob.shape[1]),
                  _rows_spec(tm, sg.shape[1]), _rows_spec(tm, gates.shape[1])]
                 + [_const_spec(a.shape) for a in consts],
        out_specs=_rows_spec(tm, d),
        out_shape=jax.ShapeDtypeStruct((n, d), F32),
        scratch_shapes=[pltpu.VMEM((tm, d), F32)],
        compiler_params=_params(("parallel",)),
        name="post",
    )(x, oa, ob, sg, gates, *consts)


def _row_tile(n, target):
    tm = min(n, target)
    assert n % tm == 0 and tm % 8 == 0
    return tm


def kernel(x_prompt, x_sample, cache_sb_k, cache_sb_v, state_hgrn, meta_tokens, ffn1_norm, ffn1_w_gate, ffn1_w_up, ffn1_w_down, mix_norm, w_in, b_gate, hg_lb_logits, hg_out_norm, w_branch_a, w_branch_b, w_out, ffn2_norm, ffn2_w_gate, ffn2_w_up, ffn2_w_down, final_norm):
    depth = w_in.shape[0]
    assert depth == 1, "single-layer step"
    nb, seq, d = x_prompt.shape
    ns, t_new, _ = x_sample.shape
    n_meta = meta_tokens.shape[0]
    row = lambda a: a.reshape(1, -1)

    mix_norms = (row(mix_norm[0]), row(b_gate[0]), hg_lb_logits)

    n_frames = nb * seq
    tm_p = _row_tile(seq, 512)
    n_run = ns * t_new
    x1_s, *f1 = _ffn_cast_call(x_sample.reshape(n_run, d), meta_tokens.astype(F32),
                               row(ffn1_norm[0]), ffn1_w_gate[0], ffn1_w_up[0], ffn1_w_down[0])
    (q_s, kb_s, vb_s, hq_s, hk_s, hi_s, hlf_s, sg_s, gates_s, k_s, v_s, meta_kt, meta_vt,
     meta_kb, meta_vb, w_in_b) = _mix_small_call(x1_s, mix_norms[0], w_in[0], *mix_norms[1:],
                                                 n_run, n_meta)
    run = lambda a: a[:n_run]
    meta = lambda a: a[n_run:]
    token_minor = lambda a: jnp.transpose(a, (0, 1, 3, 4, 2))
    token_major = lambda a: jnp.transpose(a, (0, 1, 4, 2, 3))
    seq3 = lambda a, s: a.reshape(s, -1, a.shape[-1])
    rows = lambda a: a.reshape(-1, a.shape[-1])
    hg = lambda sel, s, arrs: [seq3(sel(a), s) for a in arrs]

    oa_s = _sbd_call(q_s, kb_s, vb_s, token_minor(cache_sb_k), token_minor(cache_sb_v), t_new)
    ob_s, st_s = _hg_call(*hg(run, ns, (hq_s, hk_s, hi_s, hlf_s)), state_hgrn[0].astype(F32),
                          t_new)
    post_norms = (row(ffn2_norm[0]), row(final_norm))
    y_s, wa, wb, wo, *f2 = _post_cast_call(
        x1_s, oa_s, rows(ob_s), sg_s, gates_s, row(hg_out_norm[0]),
        w_branch_a[0], w_branch_b[0], w_out[0], post_norms[0],
        ffn2_w_gate[0], ffn2_w_up[0], ffn2_w_down[0], post_norms[1])

    zero_state = jnp.zeros((1, HG_HEADS, HG_HEAD_DIM, HG_HEAD_DIM), F32)
    _, st_meta = _hg_call(*hg(meta, 1, (hq_s, hk_s, hi_s, hlf_s)), zero_state, n_meta)
    x1_p = _ffn_call(x_prompt.reshape(n_frames, d), row(ffn1_norm[0]), *f1,
                     _row_tile(n_frames, FFN_TILE))
    (q_p, kb_p, vb_p, hq_p, hk_p, hi_p, hlf_p, sg_p, gates_p, kt_full, vt_full) = (
        _mix_frames_call(x1_p.reshape(nb, seq, d), mix_norms[0], w_in_b, *mix_norms[1:],
                         meta_kt, meta_vt, tm_p, n_meta))
    oa_p = _sb_call(q_p, kb_p, vb_p, meta_kb[None], meta_vb[None], n_meta, KEY_BLOCK)
    ob_p, st_p = _hg_call(hq_p, hk_p, hi_p, hlf_p, st_meta, HG_CHUNK, HG_CHUNKS_PER_STEP)
    y_p = _post_call(x1_p, rows(oa_p), rows(ob_p), rows(sg_p), rows(gates_p),
                     row(hg_out_norm[0]), wa, wb, wo, post_norms[0], *f2, post_norms[1], tm_p)

    heads = lambda a: a.reshape(1, ns, t_new, SB_HEADS, SB_HEAD_DIM)
    return (y_p.reshape(nb, seq, d), y_s.reshape(ns, t_new, d),
            token_major(kt_full), token_major(vt_full), st_p[None],
            heads(run(k_s)), heads(run(v_s)), st_s[None])
```

```python
import functools

import jax
import jax.numpy as jnp
from jax import lax
from jax.experimental import pallas as pl
from jax.experimental.pallas import tpu as pltpu

F32 = jnp.float32
BF16 = jnp.bfloat16
EPS = 1e-6
SB_HEADS = 8
SB_HEAD_DIM = 64
SB_WIDTH = SB_HEADS * SB_HEAD_DIM
HG_HEADS = 4
HG_HEAD_DIM = 128
HG_WIDTH = HG_HEADS * HG_HEAD_DIM
LANES = 128
FF_CHUNK = 256
ROW_PART = 256
FFN_TILE, FFN_ROW_PART = 1024, 512
KEY_BLOCK = 128
WIDE_BLOCK = 256
SB_BLOCKS_PER_STEP = 8
SB_BLOCKS_SKEWED = 2
F32_EXP2_UNDERFLOW = -152.0
LOG2_E = 1.4426950408889634
HG_CHUNK = 128
HG_CHUNKS_PER_STEP = 4
HG_SAFE_LOG_DECAY = -60.0
VMEM_LIMIT_BYTES = 56 * 1024 * 1024

_NN = (((1,), (0,)), ((), ()))
_NT = (((1,), (1,)), ((), ()))
_TN = (((0,), (0,)), ((), ()))


def _const_spec(shape):
    nd = len(shape)
    return pl.BlockSpec(shape, lambda *_: (0,) * nd, pipeline_mode=pl.Buffered(1))


def _rows_spec(tm, width):
    return pl.BlockSpec((tm, width), lambda i: (i, 0))


def _params(sem):
    return pltpu.CompilerParams(dimension_semantics=sem, vmem_limit_bytes=VMEM_LIMIT_BYTES)


def _row_parts(rows, part=ROW_PART):
    if rows % part or rows == part:
        return [slice(0, rows)]
    return [slice(r, r + part) for r in range(0, rows, part)]


def _rms(x, g):
    return x * lax.rsqrt(jnp.mean(x * x, axis=-1, keepdims=True) + EPS) * g


def _ffn_chunk(h, wg, wu, wd):
    gate = jnp.dot(h, wg, preferred_element_type=F32)
    up = jnp.dot(h, wu, preferred_element_type=F32)
    act = (gate * jax.nn.sigmoid(gate) * up).astype(BF16)
    return jnp.dot(act, wd, preferred_element_type=F32)


def _half_ffn(x, g, wg_ref, wu_ref, wd_ref, acc_ref):
    h = _rms(x, g).astype(BF16)
    dff = wg_ref.shape[1]
    assert dff % FF_CHUNK == 0
    for c in range(dff // FF_CHUNK):
        cols = slice(c * FF_CHUNK, (c + 1) * FF_CHUNK)
        contrib = _ffn_chunk(h, wg_ref[:, cols], wu_ref[:, cols], wd_ref[cols, :])
        if c == 0:
            acc_ref[...] = contrib
        else:
            acc_ref[...] += contrib
    return x + 0.5 * acc_ref[...]


def _cast_ffn_chunk(h, w32_refs, w16_refs):
    wg, wu, wd = (r[...].astype(BF16) for r in w32_refs)
    for ref, wv in zip(w16_refs, (wg, wu, wd)):
        ref[...] = wv
    return _ffn_chunk(h, wg, wu, wd)


def _ff_chunk_specs(wg, wd, chunk_of_step):
    col = pl.BlockSpec((wg.shape[0], FF_CHUNK), lambda c: (0, chunk_of_step(c)))
    row = pl.BlockSpec((FF_CHUNK, wd.shape[1]), lambda c: (chunk_of_step(c), 0))
    return [col, col, row]


def _ffn_kernel(x_ref, n_ref, wg_ref, wu_ref, wd_ref, o_ref, acc_ref):
    for r in _row_parts(x_ref.shape[0], FFN_ROW_PART):
        o_ref[r, :] = _half_ffn(x_ref[r, :], n_ref[...], wg_ref, wu_ref, wd_ref, acc_ref.at[r, :])


def _ffn_cast_kernel(xa_ref, xb_ref, n_ref, wg_ref, wu_ref, wd_ref,
                     o_ref, wgb_ref, wub_ref, wdb_ref, x_ref, h_ref, acc_ref):
    c = pl.program_id(0)

    @pl.when(c == 0)
    def _():
        x_ref[...] = jnp.concatenate([xa_ref[...], xb_ref[...]], axis=0)
        h_ref[...] = _rms(x_ref[...], n_ref[...]).astype(BF16)
        acc_ref[...] = jnp.zeros_like(acc_ref)

    acc_ref[...] += _cast_ffn_chunk(h_ref[...], (wg_ref, wu_ref, wd_ref),
                                    (wgb_ref, wub_ref, wdb_ref))

    @pl.when(c == pl.num_programs(0) - 1)
    def _():
        o_ref[...] = x_ref[...] + 0.5 * acc_ref[...]


def _ffn_cast_call(xa, xb, norm, wg, wu, wd):
    n, d = xa.shape[0] + xb.shape[0], xa.shape[1]
    dff = wg.shape[1]
    assert dff % FF_CHUNK == 0 and xa.shape[0] % 8 == 0
    chunk_specs = _ff_chunk_specs(wg, wd, lambda c: c)
    whole = lambda rows: pl.BlockSpec((rows, d), lambda c: (0, 0))
    return pl.pallas_call(
        _ffn_cast_kernel,
        grid=(dff // FF_CHUNK,),
        in_specs=[whole(xa.shape[0]), whole(xb.shape[0]), _const_spec(norm.shape)] + chunk_specs,
        out_specs=[whole(n)] + chunk_specs,
        out_shape=[jax.ShapeDtypeStruct((n, d), F32)]
                  + [jax.ShapeDtypeStruct(a.shape, BF16) for a in (wg, wu, wd)],
        scratch_shapes=[pltpu.VMEM((n, d), F32), pltpu.VMEM((n, d), BF16),
                        pltpu.VMEM((n, d), F32)],
        compiler_params=_params(("arbitrary",)),
        name="ffn_cast",
    )(xa, xb, norm, wg, wu, wd)


def _ffn_call(x, norm, wg, wu, wd, tm):
    n, d = x.shape
    return pl.pallas_call(
        _ffn_kernel,
        grid=(n // tm,),
        in_specs=[_rows_spec(tm, d), _const_spec(norm.shape), _const_spec(wg.shape),
                  _const_spec(wu.shape), _const_spec(wd.shape)],
        out_specs=_rows_spec(tm, d),
        out_shape=jax.ShapeDtypeStruct((n, d), F32),
        scratch_shapes=[pltpu.VMEM((tm, d), F32)],
        compiler_params=_params(("parallel",)),
        name="ffn",
    )(x, norm, wg, wu, wd)


def _forget_lower_bound(lbl_ref):
    logits = lbl_ref[...]
    l0, l1 = logits[0:1], logits[1:2]
    m = jnp.maximum(l0, l1)
    e0, e1 = jnp.exp(l0 - m), jnp.exp(l1 - m)
    p0, p1 = e0 / (e0 + e1), e1 / (e0 + e1)
    return (p0 + p1) - p0


_N_MIX_GROUPS = 11
_MIX_GROUP_ORDER = (7, 8, 9, 10, 3, 6, 0, 1, 2, 4, 5)


def _mix_group(j, p, r, lb, bg_ref, outs):
    q_ref, kb_ref, vb_ref, hq_ref, hk_ref, hi_ref, hlf_ref, sg_ref, gates_ref = outs
    if j == 0:
        q_ref[r, :] = (p * (SB_HEAD_DIM ** -0.5 * LOG2_E)).astype(BF16)
    elif j == 1:
        kb_ref[r, :] = p.astype(BF16)
    elif j == 2:
        vb_ref[r, :] = p.astype(BF16)
    elif j == 3:
        f = lb + (1.0 - lb) * jax.nn.sigmoid(p)
        hk_ref[r, :] = (1.0 - f).astype(hk_ref.dtype)
        hlf_ref[r, :] = jnp.log(f)
    elif j == 4:
        hi_ref[r, :] = p.astype(hi_ref.dtype)
    elif j == 5:
        hq_ref[r, :] = p.astype(hq_ref.dtype)
    elif j == 6:
        sg_ref[r, :] = (p * jax.nn.sigmoid(p)).astype(sg_ref.dtype)
    else:
        cols = slice((j - 7) * SB_WIDTH, (j - 6) * SB_WIDTH)
        gates_ref[r, cols] = jax.nn.sigmoid(p + bg_ref[:, cols]).astype(gates_ref.dtype)


def _mix_body(x_ref, n_ref, win_ref, bg_ref, lbl_ref, *outs):
    w = SB_WIDTH
    lb = _forget_lower_bound(lbl_ref)
    kv = ([], [])
    for r in _row_parts(x_ref.shape[0]):
        h = _rms(x_ref[r, :], n_ref[...]).astype(BF16)
        for j in _MIX_GROUP_ORDER:
            p = jnp.dot(h, win_ref[:, j * w:(j + 1) * w], preferred_element_type=F32)
            _mix_group(j, p, r, lb, bg_ref, outs)
            if j in (1, 2):
                kv[j - 1].append(p)
    return jnp.concatenate(kv[0], axis=0), jnp.concatenate(kv[1], axis=0)


def _mix_small_kernel(*refs, n_run, n_meta):
    x_ref, n_ref, win_ref, bg_ref, lbl_ref = refs[:5]
    outs, (k_ref, v_ref, kt_ref, vt_ref, mk_ref, mv_ref, winb_ref, h_ref) = refs[5:14], refs[14:]
    c = pl.program_id(0)

    @pl.when(c == 0)
    def _():
        h_ref[...] = _rms(x_ref[...], n_ref[...]).astype(BF16)

    wb = win_ref[...].astype(BF16)
    winb_ref[...] = wb
    p = jnp.dot(h_ref[...], wb, preferred_element_type=F32)
    lb = _forget_lower_bound(lbl_ref)
    for j in range(_N_MIX_GROUPS):
        @pl.when(c == j)
        def _(j=j):
            _mix_group(j, p, slice(None), lb, bg_ref, outs)
            for group, rows_ref, cols_ref, meta_ref in ((1, k_ref, kt_ref, mk_ref),
                                                        (2, v_ref, vt_ref, mv_ref)):
                if j == group:
                    meta_rows = _pad_rows(p[n_run:n_run + n_meta], KEY_BLOCK)
                    rows_ref[...] = p
                    cols_ref[...] = meta_rows.T
                    meta_ref[...] = meta_rows.astype(BF16)


def _mix_frames_kernel(*refs, tm, n_meta, n_tiles):
    x_ref, consts, mk_ref, mv_ref = refs[0], refs[1:5], refs[5], refs[6]
    row_outs, (kt_ref, vt_ref, ck_ref, cv_ref) = refs[7:16], refs[16:]
    t = pl.program_id(1)
    shape = (SB_HEADS, SB_HEAD_DIM, tm)

    @pl.when(t == 0)
    def _():
        ck_ref[...] = mk_ref[...]
        cv_ref[...] = mv_ref[...]

    @pl.when(t < n_tiles)
    def _():
        k, v = _mix_body(x_ref.at[0], *consts, *[r.at[0] for r in row_outs])
        for rows, out_ref, carry_ref in ((k, kt_ref, ck_ref), (v, vt_ref, cv_ref)):
            cols = rows.T
            late = jnp.concatenate([carry_ref[:, :n_meta], cols[:, :tm - n_meta]], axis=1)
            out_ref[0, 0] = late.reshape(shape)
            carry_ref[:, :n_meta] = cols[:, tm - n_meta:]

    @pl.when(t == n_tiles)
    def _():
        for out_ref, carry_ref in ((kt_ref, ck_ref), (vt_ref, cv_ref)):
            tail = jnp.concatenate(
                [carry_ref[:, :n_meta], jnp.zeros((SB_WIDTH, tm - n_meta), F32)], axis=1)
            out_ref[0, 0] = tail.reshape(shape)


_MIX_WIDTHS = [(SB_WIDTH, BF16)] * 6 + [(SB_WIDTH, F32), (SB_WIDTH, BF16)]


def _mix_small_call(x, norm, w_in, b_gate, lb_logits, n_run, n_meta):
    n, d = x.shape
    assert w_in.shape[1] == _N_MIX_GROUPS * SB_WIDTH
    widths = _MIX_WIDTHS + [(2 * d, BF16), (SB_WIDTH, F32), (SB_WIDTH, F32)]
    assert KEY_BLOCK == LANES
    meta_t = jax.ShapeDtypeStruct((SB_WIDTH, LANES), F32)
    meta_rows = jax.ShapeDtypeStruct((KEY_BLOCK, SB_WIDTH), BF16)
    whole = lambda shape: pl.BlockSpec(shape, lambda c: (0, 0))
    group = pl.BlockSpec((d, SB_WIDTH), lambda c: (0, c))
    return pl.pallas_call(
        functools.partial(_mix_small_kernel, n_run=n_run, n_meta=n_meta),
        grid=(_N_MIX_GROUPS,),
        in_specs=[whole((n, d)), _const_spec(norm.shape), group,
                  _const_spec(b_gate.shape), _const_spec(lb_logits.shape)],
        out_specs=[whole((n, wd)) for wd, _ in widths] + [whole(meta_t.shape)] * 2
                  + [whole(meta_rows.shape)] * 2 + [group],
        out_shape=[jax.ShapeDtypeStruct((n, wd), dt) for wd, dt in widths] + [meta_t] * 2
                  + [meta_rows] * 2 + [jax.ShapeDtypeStruct(w_in.shape, BF16)],
        scratch_shapes=[pltpu.VMEM((n, d), BF16)],
        compiler_params=_params(("arbitrary",)),
        name="mix_small",
    )(x, norm, w_in, b_gate, lb_logits)


def _mix_frames_call(x, norm, w_in, b_gate, lb_logits, meta_kt, meta_vt, tm, n_meta):
    nb, seq, d = x.shape
    assert seq % tm == 0 and n_meta < tm
    n_tiles = seq // tm
    widths = _MIX_WIDTHS + [(2 * d, BF16)]
    rows_map = lambda b, t: (b, jnp.minimum(t, n_tiles - 1), 0)
    const2 = lambda shape: pl.BlockSpec(shape, lambda b, t: (0, 0), pipeline_mode=pl.Buffered(1))
    kv_shape = jax.ShapeDtypeStruct((1, nb, SB_HEADS, SB_HEAD_DIM, n_meta + seq), F32)
    kv_spec = pl.BlockSpec((1, 1, SB_HEADS, SB_HEAD_DIM, tm), lambda b, t: (0, b, 0, 0, t))
    carry = pltpu.VMEM((SB_WIDTH, LANES), F32)
    return pl.pallas_call(
        functools.partial(_mix_frames_kernel, tm=tm, n_meta=n_meta, n_tiles=n_tiles),
        grid=(nb, n_tiles + 1),
        in_specs=[pl.BlockSpec((1, tm, d), rows_map)]
                 + [const2(a.shape) for a in (norm, w_in, b_gate, lb_logits, meta_kt, meta_vt)],
        out_specs=[pl.BlockSpec((1, tm, wd), rows_map) for wd, _ in widths] + [kv_spec] * 2,
        out_shape=[jax.ShapeDtypeStruct((nb, seq, wd), dt) for wd, dt in widths] + [kv_shape] * 2,
        scratch_shapes=[carry, carry],
        compiler_params=_params(("arbitrary", "arbitrary")),
        name="mix_frames",
    )(x, norm, w_in, b_gate, lb_logits, meta_kt, meta_vt)


def _suffix_ones(width):
    r = lax.broadcasted_iota(jnp.int32, (width, width), 0)
    c = lax.broadcasted_iota(jnp.int32, (width, width), 1)
    return jnp.where(r >= c, 1.0, 0.0).astype(BF16)


def _key_before(tq, width, k0, q_first):
    row = lax.broadcasted_iota(jnp.int32, (tq, width), 0)
    col = lax.broadcasted_iota(jnp.int32, (tq, width), 1)
    return col + k0 < row + q_first


def _sb_scores(blocks, token_minor):
    k_dims = _NN if token_minor else _NT
    return [[lax.dot_general(q_ops[h], k_ops[h], k_dims, preferred_element_type=F32)
             for h in range(SB_HEADS)] for q_ops, k_ops, *_ in blocks]


def _sb_stay_sums(blocks, zs):
    cums = []
    for z_blk, (_, _, _, ones, mask, _, _) in zip(zs, blocks):
        stays = []
        for z in z_blk:
            nz = -z
            ls = jnp.minimum(nz, 0.0) - jnp.log2(1.0 + jnp.exp2(jnp.minimum(z, nz)))
            if mask is not None:
                ls = jnp.where(mask, ls, 0.0)
            stays.append(ls.astype(BF16))
        cums.append([jnp.dot(st, ones, preferred_element_type=F32) for st in stays])
    return cums


def _sb_apply(blocks, zs, cums, token_minor):
    heads = range(SB_HEADS)
    v_dims = _NT if token_minor else _NN
    ws = []
    for z_blk, cum_blk, (q_ops, _, _, ones, mask, r_ref, _) in zip(zs, cums, blocks):
        ws.append([])
        for h in heads:
            r = r_ref[h]
            cr = cum_blk[h] + jnp.concatenate([r] * (ones.shape[1] // LANES), axis=1)
            a = jnp.exp2(z_blk[h] + cr)
            if mask is not None:
                a = jnp.where(mask, a, 0.0)
            ws[-1].append(a.astype(BF16))
            r_ref[h] = jnp.broadcast_to(cr[:, 0:1], (q_ops[h].shape[0], LANES))
    for w_blk, (_, _, v_ops, _, _, _, acc_ref) in zip(ws, blocks):
        for h in heads:
            acc_ref[h] += lax.dot_general(w_blk[h], v_ops[h], v_dims, preferred_element_type=F32)


def _sb_sweep(blocks, token_minor=False):
    zs = _sb_scores(blocks, token_minor)
    _sb_apply(blocks, zs, _sb_stay_sums(blocks, zs), token_minor)


def _sb_sweep_skewed(sweeps):
    zs = _sb_scores(sweeps[0], False)
    for n, blocks in enumerate(sweeps):
        cums = _sb_stay_sums(blocks, zs)
        z_next = _sb_scores(sweeps[n + 1], False) if n + 1 < len(sweeps) else None
        _sb_apply(blocks, zs, cums, False)
        zs = z_next


def _sb_live(r_ref):
    m = r_ref[0]
    for h in range(1, SB_HEADS):
        m = jnp.maximum(m, r_ref[h])
    return (jnp.max(m) >= F32_EXP2_UNDERFLOW).astype(jnp.int32)


def _sweep_while(n_blocks, alive, r_ref, block, first=0):
    def body(carry):
        t, _ = carry
        block(t)
        return t + 1, _sb_live(r_ref)

    _, alive = lax.while_loop(lambda c: jnp.logical_and(c[0] < n_blocks, c[1] > 0),
                              body, (jnp.int32(first), alive))
    return alive


def _sb_kernel(q_ref, kn_ref, vn_ref, ko_ref, vo_ref, o_ref, qm_ref, acc_ref, r_ref, live_ref, *,
               tq, n_old):
    kb, wb = KEY_BLOCK, WIDE_BLOCK
    blocks_per_step = q_ref.shape[1] // tq
    lane = lax.broadcasted_iota(jnp.int32, (tq, LANES), 1)
    ones = {width: _suffix_ones(width) for width in (kb, wb)}
    old_rows = lax.broadcasted_iota(jnp.int32, (tq, kb), 1) < n_old

    def block(j, k_ref, v_ref, start, width, mask):
        k_all = k_ref[0, pl.ds(start, width), :].astype(BF16)
        v_all = v_ref[0, pl.ds(start, width), :].astype(BF16)
        pair = lambda a: [a[:, (h // 2) * LANES:(h // 2 + 1) * LANES] for h in range(SB_HEADS)]
        return ([qm_ref[j, h] for h in range(SB_HEADS)], pair(k_all), pair(v_all), ones[width],
                mask, r_ref.at[j], acc_ref.at[j])

    def place(j):
        q0 = (pl.program_id(1) * blocks_per_step + j) * tq
        below = q0 // kb
        return q0, below, pl.multiple_of(below * kb, kb)

    def write_out(j):
        rows = pl.ds(pl.multiple_of(j * tq, tq), tq)
        for p in range(SB_HEADS // 2):
            o_ref[0, rows, p * LANES:(p + 1) * LANES] = jnp.where(
                lane < SB_HEAD_DIM, acc_ref[j, 2 * p], acc_ref[j, 2 * p + 1]).astype(o_ref.dtype)

    def start(j):
        rows = pl.ds(pl.multiple_of(j * tq, tq), tq)
        for h in range(SB_HEADS):
            p = h // 2
            qp = q_ref[0, rows, p * LANES:(p + 1) * LANES]
            keep = (lane < SB_HEAD_DIM) if h % 2 == 0 else (lane >= SB_HEAD_DIM)
            qm_ref[j, h] = jnp.where(keep, qp, jnp.zeros_like(qp))
            r_ref[j, h] = jnp.zeros((tq, LANES), F32)
            acc_ref[j, h] = jnp.zeros((tq, LANES), F32)

    def own(j):
        q0, _, d0 = place(j)
        return block(j, kn_ref, vn_ref, d0, kb, _key_before(tq, kb, d0, q0))

    def own_and_under(j):
        _, _, d0 = place(j)
        return [own(j), block(j, kn_ref, vn_ref, pl.multiple_of(d0 - wb, kb), wb, None)]

    def note_reach(j):
        reach = r_ref[j, 0]
        for h in range(1, SB_HEADS):
            reach = jnp.maximum(reach, r_ref[j, h])
        live_ref[...] = jnp.maximum(live_ref[...], reach)

    def first_sweep(j, carry):
        _, below, _ = place(j)
        start(j)
        old = lambda: block(j, ko_ref, vo_ref, 0, kb, old_rows)

        @pl.when(below == 0)
        def _():
            _sb_sweep([own(j), old()])

        @pl.when(below == 1)
        def _():
            _sb_sweep([own(j), block(j, kn_ref, vn_ref, 0, kb, None), old()])

        @pl.when(below >= 2)
        def _():
            _sb_sweep(own_and_under(j))
            note_reach(j)

        write_out(j)
        return carry

    def first_sweep_group(jj, carry):
        js = [group * jj + n for n in range(group)]
        _, below, _ = place(js[0])
        for j in js:
            start(j)
        old = lambda j: block(j, ko_ref, vo_ref, 0, kb, old_rows)

        @pl.when(below == 0)
        def _():
            _sb_sweep([own(js[0]), old(js[0])])
            _sb_sweep([own(js[1]), block(js[1], kn_ref, vn_ref, 0, kb, None), old(js[1])])
            if group > 2:
                _sb_sweep_skewed([own_and_under(j) for j in js[2:]])
                for j in js[2:]:
                    note_reach(j)

        @pl.when(below >= 2)
        def _():
            _sb_sweep_skewed([own_and_under(j) for j in js])
            for j in js:
                note_reach(j)

        for j in js:
            write_out(j)
        return carry

    def further_sweeps(j, carry):
        _, below, d0 = place(j)
        top = d0 - wb
        left = jnp.maximum(below - 2, 0)
        live = r_ref.at[j]
        alive = _sweep_while(left // 2, _sb_live(live), live, lambda t: _sb_sweep([block(
            j, kn_ref, vn_ref, pl.multiple_of(top - (t + 1) * wb, kb), wb, None)]))
        alive = _sweep_while(left % 2, alive, live,
                             lambda t: _sb_sweep([block(j, kn_ref, vn_ref, 0, kb, None)]))
        _sweep_while((below >= 2).astype(jnp.int32), alive, live,
                     lambda t: _sb_sweep([block(j, ko_ref, vo_ref, 0, kb, old_rows)]))
        write_out(j)
        return carry

    live_ref[...] = jnp.full(live_ref.shape, -jnp.inf, F32)
    group = SB_BLOCKS_SKEWED
    if blocks_per_step % group == 0 and KEY_BLOCK == tq:
        lax.fori_loop(0, blocks_per_step // group, first_sweep_group, 0)
    else:
        lax.fori_loop(0, blocks_per_step, first_sweep, 0)

    @pl.when(jnp.max(live_ref[...]) >= F32_EXP2_UNDERFLOW)
    def _():
        lax.fori_loop(0, blocks_per_step, further_sweeps, 0)


def _sbd_kernel(q_ref, kn_ref, vn_ref, tk_ref, tv_ref, ck_hbm, cv_hbm, o_ref,
                kbuf, vbuf, sem, acc_ref, r_ref, *, tq, n_old):
    kb, wb, hd = KEY_BLOCK, WIDE_BLOCK, SB_HEAD_DIM
    heads = range(SB_HEADS)
    stream = pl.program_id(0)
    n_tail, n_full = n_old % kb, n_old // kb
    n_wide = (n_full * kb) // wb
    rest = n_full * kb - n_wide * wb
    per_head = lambda a: [a[:, h * hd:(h + 1) * hd] for h in heads]

    def fetch(start, width):
        return [pltpu.make_async_copy(hbm.at[0, stream, :, :, pl.ds(start, width)],
                                      buf.at[:, :, pl.ds(0, width)], sem.at[n])
                for n, (hbm, buf) in enumerate(((ck_hbm, kbuf), (cv_hbm, vbuf)))]

    wide_start = lambda t: pl.multiple_of(rest + (n_wide - 1 - t) * wb, kb)
    if n_wide:
        for cp in fetch(wide_start(0), wb):
            cp.start()

    q_ops = per_head(q_ref[...])
    for h in heads:
        r_ref[h] = jnp.zeros((tq, LANES), F32)
        acc_ref[h] = jnp.zeros((tq, hd), F32)

    _sb_sweep([(q_ops, per_head(_pad_rows(kn_ref[...], kb)), per_head(_pad_rows(vn_ref[...], kb)),
                _suffix_ones(kb), _key_before(tq, kb, 0, 0), r_ref, acc_ref)])
    alive = _sb_live(r_ref)

    def swept(load, width, valid):
        def cols(which, h):
            c = load(which, h).astype(BF16)
            if valid < width:
                c = jnp.concatenate([c, jnp.zeros((hd, width - valid), BF16)], axis=1)
            return c

        mask = None if valid == width else (
            lax.broadcasted_iota(jnp.int32, (tq, width), 1) < valid)
        _sb_sweep([(q_ops, [cols(0, h) for h in heads], [cols(1, h) for h in heads],
                    _suffix_ones(width), mask, r_ref, acc_ref)], token_minor=True)

    from_tail = lambda which, h: (tk_ref, tv_ref)[which][0, 0, h, :, 0:n_tail]
    from_buf = lambda width: lambda which, h: (kbuf, vbuf)[which][h, :, pl.ds(0, width)]

    def fetched(start, width):
        for cp in fetch(start, width):
            cp.start()
        for cp in fetch(start, width):
            cp.wait()
        swept(from_buf(width), width, width)

    if n_tail:
        alive = _sweep_while(1, alive, r_ref, lambda t: swept(from_tail, kb, n_tail))
    if n_wide:
        for cp in fetch(wide_start(0), wb):
            cp.wait()
        alive = _sweep_while(1, alive, r_ref, lambda t: swept(from_buf(wb), wb, wb))
        alive = _sweep_while(n_wide, alive, r_ref, lambda t: fetched(wide_start(t), wb), first=1)
    if rest:
        _sweep_while(1, alive, r_ref, lambda t: fetched(0, rest))

    o_ref[...] = jnp.concatenate([acc_ref[h] for h in heads], axis=1).astype(o_ref.dtype)


def _sb_call(q, k_new, v_new, k_old, v_old, n_old, tq):
    s, lq, w = q.shape
    ln = k_new.shape[1]
    lo = k_old.shape[1]
    assert lq % tq == 0 and ln % KEY_BLOCK == 0 and n_old <= KEY_BLOCK <= lo
    assert tq <= KEY_BLOCK and KEY_BLOCK % tq == 0
    step = tq * SB_BLOCKS_PER_STEP if lq % (tq * SB_BLOCKS_PER_STEP) == 0 else tq
    old_map = (lambda b, i: (b, 0, 0)) if k_old.shape[0] == s else (lambda b, i: (0, 0, 0))
    return pl.pallas_call(
        functools.partial(_sb_kernel, tq=tq, n_old=n_old),
        grid=(s, lq // step),
        in_specs=[pl.BlockSpec((1, step, w), lambda b, i: (b, i, 0)),
                  pl.BlockSpec((1, ln, w), lambda b, i: (b, 0, 0)),
                  pl.BlockSpec((1, ln, w), lambda b, i: (b, 0, 0)),
                  pl.BlockSpec((1, lo, w), old_map),
                  pl.BlockSpec((1, lo, w), old_map)],
        out_specs=pl.BlockSpec((1, step, w), lambda b, i: (b, i, 0)),
        out_shape=jax.ShapeDtypeStruct((s, lq, w), BF16),
        scratch_shapes=[pltpu.VMEM((step // tq, SB_HEADS, tq, LANES), BF16),
                        pltpu.VMEM((step // tq, SB_HEADS, tq, LANES), F32),
                        pltpu.VMEM((step // tq, SB_HEADS, tq, LANES), F32),
                        pltpu.VMEM((tq, LANES), F32)],
        compiler_params=_params(("parallel", "parallel")),
        name="sb",
    )(q, k_new, v_new, k_old, v_old)


def _sbd_call(q, k_new, v_new, cache_k, cache_v, tq):
    s, w = cache_k.shape[1], q.shape[1]
    n_old = cache_k.shape[-1]
    assert tq <= KEY_BLOCK and q.shape[0] >= s * tq
    new_spec = pl.BlockSpec((tq, w), lambda b: (b, 0))
    last_block = (n_old - 1) // KEY_BLOCK
    tail_spec = pl.BlockSpec((1, 1, SB_HEADS, SB_HEAD_DIM, KEY_BLOCK),
                             lambda b: (0, b, 0, 0, last_block))
    in_hbm = pl.BlockSpec(memory_space=pl.ANY)
    buf = pltpu.VMEM((SB_HEADS, SB_HEAD_DIM, WIDE_BLOCK), F32)
    return pl.pallas_call(
        functools.partial(_sbd_kernel, tq=tq, n_old=n_old),
        grid=(s,),
        in_specs=[new_spec, new_spec, new_spec, tail_spec, tail_spec, in_hbm, in_hbm],
        out_specs=new_spec,
        out_shape=jax.ShapeDtypeStruct((s * tq, w), BF16),
        scratch_shapes=[buf, buf, pltpu.SemaphoreType.DMA((2,)),
                        pltpu.VMEM((SB_HEADS, tq, SB_HEAD_DIM), F32),
                        pltpu.VMEM((SB_HEADS, tq, LANES), F32)],
        compiler_params=_params(("arbitrary",)),
        name="sb_decode",
    )(q, k_new, v_new, cache_k, cache_v, cache_k, cache_v)


def _pad_rows(x, rows):
    if x.shape[0] == rows:
        return x
    return jnp.concatenate([x, jnp.zeros((rows - x.shape[0], x.shape[1]), x.dtype)], axis=0)


def _hg_kernel(q_ref, k_ref, i_ref, lf_ref, s0_ref, o_ref, sout_ref, st_ref, *, chunk):
    c = pl.program_id(0)
    hd = HG_HEAD_DIM
    ns, n_chunks = q_ref.shape[0], q_ref.shape[1] // chunk
    streams = range(ns)
    units = [(g, b) for g in range(n_chunks) for b in streams]
    rows = lambda g: slice(g * chunk, (g + 1) * chunk)
    head_cols = [slice(h * hd, (h + 1) * hd) for h in range(HG_HEADS)]

    @pl.when(c == 0)
    def _():
        for b in streams:
            for h in range(HG_HEADS):
                st_ref[b, h] = s0_ref[b if s0_ref.shape[0] == ns else 0, h].T

    trow = lax.broadcasted_iota(jnp.int32, (chunk, chunk), 0)
    tcol = lax.broadcasted_iota(jnp.int32, (chunk, chunk), 1)
    causal = trow >= tcol

    token = lax.broadcasted_iota(jnp.int32, (chunk, lf_ref.shape[2]), 0)
    cum = {}
    for g, b in units:
        part = lf_ref[b, rows(g), :]
        shift = 1
        while shift < chunk:
            part = part + jnp.where(token >= shift, pltpu.roll(part, shift, axis=0), 0.0)
            shift *= 2
        cum[g, b] = part
    last = {u: cum[u][chunk - 1:chunk, :] for u in units}
    weakest = functools.reduce(jnp.minimum, last.values())
    safe = jnp.min(weakest) >= HG_SAFE_LOG_DECAY

    @pl.when(safe)
    def _():
        qd, kd, kl, iv, dl = {}, {}, {}, {}, {}
        for g, b in units:
            k = k_ref[b, rows(g), :]
            qd[g, b] = (q_ref[b, rows(g), :] * jnp.exp(cum[g, b])).astype(BF16)
            kd[g, b] = (k * jnp.exp(-cum[g, b])).astype(BF16)
            kl[g, b] = (k * jnp.exp(last[g, b] - cum[g, b])).astype(BF16)
            iv[g, b] = i_ref[b, rows(g), :].astype(BF16)
            dl[g, b] = jnp.exp(last[g, b])
        sc = {(u, h): lax.dot_general(qd[u][:, sl], kd[u][:, sl], _NT, preferred_element_type=F32)
              for u in units for h, sl in enumerate(head_cols)}
        grown = {(u, h): lax.dot_general(iv[u][:, sl], kl[u][:, sl], _TN,
                                         preferred_element_type=F32)
                 for u in units for h, sl in enumerate(head_cols)}
        within = {(u, h): jnp.dot(jnp.where(causal, sc[u, h], 0.0).astype(BF16), iv[u][:, sl],
                                  preferred_element_type=F32)
                  for u in units for h, sl in enumerate(head_cols)}
        for g in range(n_chunks):
            carried = {(b, h): lax.dot_general(qd[g, b][:, sl], st_ref[b, h].astype(BF16), _NT,
                                               preferred_element_type=F32)
                       for b in streams for h, sl in enumerate(head_cols)}
            for b in streams:
                for h, sl in enumerate(head_cols):
                    o_ref[b, rows(g), sl] = (within[(g, b), h] + carried[b, h]).astype(o_ref.dtype)
                    st_ref[b, h] = st_ref[b, h] * dl[g, b][:, sl] + grown[(g, b), h]

    @pl.when(jnp.logical_not(safe))
    def _():
        lane_t = lax.broadcasted_iota(jnp.int32, (hd, hd), 1)
        row_t = lax.broadcasted_iota(jnp.int32, (hd, hd), 0)

        def one_chunk(n, carry):
            b, r0 = n // n_chunks, pl.multiple_of((n % n_chunks) * chunk, chunk)
            for h, sl in enumerate(head_cols):
                load = lambda ref: ref[b, pl.ds(r0, chunk), sl].astype(F32)
                f_t = _pad_rows(jnp.exp(load(lf_ref)), hd).T
                k_t = _pad_rows(load(k_ref), hd).T
                q_t = _pad_rows(load(q_ref), hd).T
                i_p = _pad_rows(load(i_ref), hd)

                def token(t, sc, f_t=f_t, k_t=k_t, q_t=q_t, i_p=i_p):
                    s, o = sc
                    sel = lane_t == t
                    fc = jnp.sum(jnp.where(sel, f_t, 0.0), axis=1, keepdims=True)
                    kc = jnp.sum(jnp.where(sel, k_t, 0.0), axis=1, keepdims=True)
                    qc = jnp.sum(jnp.where(sel, q_t, 0.0), axis=1, keepdims=True)
                    i_row = jnp.sum(jnp.where(row_t == t, i_p, 0.0), axis=0, keepdims=True)
                    s = s * fc + kc * i_row
                    o = jnp.where(row_t == t, jnp.sum(qc * s, axis=0, keepdims=True), o)
                    return s, o

                s, o = lax.fori_loop(0, chunk, token,
                                     (st_ref[b, h].T, jnp.zeros((hd, hd), F32)))
                st_ref[b, h] = s.T
                o_ref[b, pl.ds(r0, chunk), sl] = o[:chunk].astype(o_ref.dtype)
            return carry

        lax.fori_loop(0, ns * n_chunks, one_chunk, 0)

    @pl.when(c == pl.num_programs(0) - 1)
    def _():
        for b in streams:
            for h in range(HG_HEADS):
                sout_ref[b, h] = st_ref[b, h].T


def _hg_call(q, k, iv, lf, s0t, chunk, chunks_per_step=1):
    s, l, w = q.shape
    rows = chunk * chunks_per_step if l % (chunk * chunks_per_step) == 0 else chunk
    assert l % rows == 0
    seq_spec = pl.BlockSpec((s, rows, w), lambda c: (0, c, 0))
    st_shape = (s, HG_HEADS, HG_HEAD_DIM, HG_HEAD_DIM)
    return pl.pallas_call(
        functools.partial(_hg_kernel, chunk=chunk),
        grid=(l // rows,),
        in_specs=[seq_spec, seq_spec, seq_spec, seq_spec,
                  pl.BlockSpec(s0t.shape, lambda c: (0, 0, 0, 0))],
        out_specs=[seq_spec, pl.BlockSpec(st_shape, lambda c: (0, 0, 0, 0))],
        out_shape=[jax.ShapeDtypeStruct((s, l, w), BF16), jax.ShapeDtypeStruct(st_shape, F32)],
        scratch_shapes=[pltpu.VMEM(st_shape, F32)],
        compiler_params=_params(("arbitrary",)),
        name="hgrn",
    )(q, k, iv, lf, s0t)


def _merge_branches(x, oa, ob, sg, gates, hn, wa, wb, wo):
    hd = HG_HEAD_DIM
    d = x.shape[1]
    heads = []
    for h in range(HG_HEADS):
        sl = slice(h * hd, (h + 1) * hd)
        heads.append(_rms(ob[:, sl].astype(F32), hn[:, sl]) * sg[:, sl].astype(F32))
    obn = jnp.concatenate(heads, axis=-1).astype(BF16)
    ma = jnp.dot(oa, wa, preferred_element_type=F32)
    mb = jnp.dot(obn, wb, preferred_element_type=F32)
    merged = gates[:, :d].astype(F32) * ma + gates[:, d:].astype(F32) * mb
    return x + jnp.dot(merged.astype(BF16), wo, preferred_element_type=F32)


def _post_kernel(x_ref, oa_ref, ob_ref, sg_ref, gates_ref, hn_ref, wa_ref, wb_ref, wo_ref,
                 n_ref, wg_ref, wu_ref, wd_ref, fn_ref, y_ref, acc_ref):
    x = jnp.concatenate(
        [_merge_branches(x_ref[r, :], oa_ref[r, :], ob_ref[r, :], sg_ref[r, :], gates_ref[r, :],
                         hn_ref[...], wa_ref[...], wb_ref[...], wo_ref[...])
         for r in _row_parts(x_ref.shape[0])], axis=0)
    x = _half_ffn(x, n_ref[...], wg_ref, wu_ref, wd_ref, acc_ref)
    y_ref[...] = _rms(x, fn_ref[...])


def _post_cast_kernel(x_ref, oa_ref, ob_ref, sg_ref, gates_ref, hn_ref, wa_ref, wb_ref, wo_ref,
                      n_ref, wg_ref, wu_ref, wd_ref, fn_ref,
                      y_ref, wab_ref, wbb_ref, wob_ref, wgb_ref, wub_ref, wdb_ref,
                      x2_ref, h_ref, acc_ref):
    c = pl.program_id(0)

    @pl.when(c == 0)
    def _():
        wa, wb, wo = (r[...].astype(BF16) for r in (wa_ref, wb_ref, wo_ref))
        for ref, wv in zip((wab_ref, wbb_ref, wob_ref), (wa, wb, wo)):
            ref[...] = wv
        x2 = _merge_branches(x_ref[...], oa_ref[...], ob_ref[...], sg_ref[...], gates_ref[...],
                             hn_ref[...], wa, wb, wo)
        x2_ref[...] = x2
        h_ref[...] = _rms(x2, n_ref[...]).astype(BF16)
        acc_ref[...] = jnp.zeros_like(acc_ref)

    @pl.when(c > 0)
    def _():
        acc_ref[...] += _cast_ffn_chunk(h_ref[...], (wg_ref, wu_ref, wd_ref),
                                        (wgb_ref, wub_ref, wdb_ref))

    @pl.when(c == pl.num_programs(0) - 1)
    def _():
        y_ref[...] = _rms(x2_ref[...] + 0.5 * acc_ref[...], fn_ref[...])


def _post_cast_call(x, oa, ob, sg, gates, hn, wa, wb, wo, norm, wg, wu, wd, fnorm):
    n, d = oa.shape[0], x.shape[1]
    dff = wg.shape[1]
    assert dff % FF_CHUNK == 0
    chunk_specs = _ff_chunk_specs(wg, wd, lambda c: jnp.maximum(c - 1, 0))
    whole = lambda a: pl.BlockSpec(a.shape, lambda c: (0,) * a.ndim)
    top = lambda a: pl.BlockSpec((n, a.shape[1]), lambda c: (0, 0))
    return pl.pallas_call(
        _post_cast_kernel,
        grid=(dff // FF_CHUNK + 1,),
        in_specs=[top(a) for a in (x, oa, ob, sg, gates)]
                 + [_const_spec(a.shape) for a in (hn, wa, wb, wo, norm)] + chunk_specs
                 + [_const_spec(fnorm.shape)],
        out_specs=[top(x)] + [whole(a) for a in (wa, wb, wo)] + chunk_specs,
        out_shape=[jax.ShapeDtypeStruct((n, d), F32)]
                  + [jax.ShapeDtypeStruct(a.shape, BF16) for a in (wa, wb, wo, wg, wu, wd)],
        scratch_shapes=[pltpu.VMEM((n, d), F32), pltpu.VMEM((n, d), BF16),
                        pltpu.VMEM((n, d), F32)],
        compiler_params=_params(("arbitrary",)),
        name="post_cast",
    )(x, oa, ob, sg, gates, hn, wa, wb, wo, norm, wg, wu, wd, fnorm)


def _post_call(x, oa, ob, sg, gates, hn, wa, wb, wo, norm, wg, wu, wd, fnorm, tm):
    n, d = x.shape
    consts = [hn, wa, wb, wo, norm, wg, wu, wd, fnorm]
    return pl.pallas_call(
        _post_kernel,
        grid=(n // tm,),
        in_specs=[_rows_spec(tm, d), _rows_spec(tm, oa.shape[1]), _rows_spec(tm, ob.shape[1]),
                  _rows_spec(tm, sg.shape[1]), _rows_spec(tm, gates.shape[1])]
                 + [_const_spec(a.shape) for a in consts],
        out_specs=_rows_spec(tm, d),
        out_shape=jax.ShapeDtypeStruct((n, d), F32),
        scratch_shapes=[pltpu.VMEM((tm, d), F32)],
        compiler_params=_params(("parallel",)),
        name="post",
    )(x, oa, ob, sg, gates, *consts)


def _row_tile(n, target):
    tm = min(n, target)
    assert n % tm == 0 and tm % 8 == 0
    return tm


def kernel(x_prompt, x_sample, cache_sb_k, cache_sb_v, state_hgrn, meta_tokens, ffn1_norm, ffn1_w_gate, ffn1_w_up, ffn1_w_down, mix_norm, w_in, b_gate, hg_lb_logits, hg_out_norm, w_branch_a, w_branch_b, w_out, ffn2_norm, ffn2_w_gate, ffn2_w_up, ffn2_w_down, final_norm):
    depth = w_in.shape[0]
    assert depth == 1, "single-layer step"
    nb, seq, d = x_prompt.shape
    ns, t_new, _ = x_sample.shape
    n_meta = meta_tokens.shape[0]
    row = lambda a: a.reshape(1, -1)

    mix_norms = (row(mix_norm[0]), row(b_gate[0]), hg_lb_logits)

    n_frames = nb * seq
    tm_p = _row_tile(seq, 512)
    n_run = ns * t_new
    x1_s, *f1 = _ffn_cast_call(x_sample.reshape(n_run, d), meta_tokens.astype(F32),
                               row(ffn1_norm[0]), ffn1_w_gate[0], ffn1_w_up[0], ffn1_w_down[0])
    (q_s, kb_s, vb_s, hq_s, hk_s, hi_s, hlf_s, sg_s, gates_s, k_s, v_s, meta_kt, meta_vt,
     meta_kb, meta_vb, w_in_b) = _mix_small_call(x1_s, mix_norms[0], w_in[0], *mix_norms[1:],
                                                 n_run, n_meta)
    run = lambda a: a[:n_run]
    meta = lambda a: a[n_run:]
    token_minor = lambda a: jnp.transpose(a, (0, 1, 3, 4, 2))
    token_major = lambda a: jnp.transpose(a, (0, 1, 4, 2, 3))
    seq3 = lambda a, s: a.reshape(s, -1, a.shape[-1])
    rows = lambda a: a.reshape(-1, a.shape[-1])
    hg = lambda sel, s, arrs: [seq3(sel(a), s) for a in arrs]

    oa_s = _sbd_call(q_s, kb_s, vb_s, token_minor(cache_sb_k), token_minor(cache_sb_v), t_new)
    ob_s, st_s = _hg_call(*hg(run, ns, (hq_s, hk_s, hi_s, hlf_s)), state_hgrn[0].astype(F32),
                          t_new)
    post_norms = (row(ffn2_norm[0]), row(final_norm))
    y_s, wa, wb, wo, *f2 = _post_cast_call(
        x1_s, oa_s, rows(ob_s), sg_s, gates_s, row(hg_out_norm[0]),
        w_branch_a[0], w_branch_b[0], w_out[0], post_norms[0],
        ffn2_w_gate[0], ffn2_w_up[0], ffn2_w_down[0], post_norms[1])

    zero_state = jnp.zeros((1, HG_HEADS, HG_HEAD_DIM, HG_HEAD_DIM), F32)
    _, st_meta = _hg_call(*hg(meta, 1, (hq_s, hk_s, hi_s, hlf_s)), zero_state, n_meta)
    x1_p = _ffn_call(x_prompt.reshape(n_frames, d), row(ffn1_norm[0]), *f1,
                     _row_tile(n_frames, FFN_TILE))
    (q_p, kb_p, vb_p, hq_p, hk_p, hi_p, hlf_p, sg_p, gates_p, kt_full, vt_full) = (
        _mix_frames_call(x1_p.reshape(nb, seq, d), mix_norms[0], w_in_b, *mix_norms[1:],
                         meta_kt, meta_vt, tm_p, n_meta))
    oa_p = _sb_call(q_p, kb_p, vb_p, meta_kb[None], meta_vb[None], n_meta, KEY_BLOCK)
    ob_p, st_p = _hg_call(hq_p, hk_p, hi_p, hlf_p, st_meta, HG_CHUNK, HG_CHUNKS_PER_STEP)
    y_p = _post_call(x1_p, rows(oa_p), rows(ob_p), rows(sg_p), rows(gates_p),
                     row(hg_out_norm[0]), wa, wb, wo, post_norms[0], *f2, post_norms[1], tm_p)

    heads = lambda a: a.reshape(1, ns, t_new, SB_HEADS, SB_HEAD_DIM)
    return (y_p.reshape(nb, seq, d), y_s.reshape(ns, t_new, d),
            token_major(kt_full), token_major(vt_full), st_p[None],
            heads(run(k_s)), heads(run(v_s)), st_s[None])
```

```python
import functools

import jax
import jax.numpy as jnp
from jax import lax
from jax.experimental import pallas as pl
from jax.experimental.pallas import tpu as pltpu

F32 = jnp.float32
BF16 = jnp.bfloat16
EPS = 1e-6
SB_HEADS = 8
SB_HEAD_DIM = 64
SB_WIDTH = SB_HEADS * SB_HEAD_DIM
HG_HEADS = 4
HG_HEAD_DIM = 128
HG_WIDTH = HG_HEADS * HG_HEAD_DIM
LANES = 128
FF_CHUNK = 256
ROW_PART = 256
FFN_TILE, FFN_ROW_PART = 1024, 512
KEY_BLOCK = 128
WIDE_BLOCK = 256
SB_BLOCKS_PER_STEP = 8
SB_BLOCKS_SKEWED = 2
F32_EXP2_UNDERFLOW = -152.0
LOG2_E = 1.4426950408889634
HG_CHUNK = 128
HG_CHUNKS_PER_STEP = 4
HG_SAFE_LOG_DECAY = -60.0
VMEM_LIMIT_BYTES = 56 * 1024 * 1024

_NN = (((1,), (0,)), ((), ()))
_NT = (((1,), (1,)), ((), ()))
_TN = (((0,), (0,)), ((), ()))


def _const_spec(shape):
    nd = len(shape)
    return pl.BlockSpec(shape, lambda *_: (0,) * nd, pipeline_mode=pl.Buffered(1))


def _rows_spec(tm, width):
    return pl.BlockSpec((tm, width), lambda i: (i, 0))


def _params(sem):
    return pltpu.CompilerParams(dimension_semantics=sem, vmem_limit_bytes=VMEM_LIMIT_BYTES)


def _row_parts(rows, part=ROW_PART):
    if rows % part or rows == part:
        return [slice(0, rows)]
    return [slice(r, r + part) for r in range(0, rows, part)]


def _rms(x, g):
    return x * lax.rsqrt(jnp.mean(x * x, axis=-1, keepdims=True) + EPS) * g


def _ffn_chunk(h, wg, wu, wd):
    gate = jnp.dot(h, wg, preferred_element_type=F32)
    up = jnp.dot(h, wu, preferred_element_type=F32)
    act = (gate * jax.nn.sigmoid(gate) * up).astype(BF16)
    return jnp.dot(act, wd, preferred_element_type=F32)


def _half_ffn(x, g, wg_ref, wu_ref, wd_ref, acc_ref):
    h = _rms(x, g).astype(BF16)
    dff = wg_ref.shape[1]
    assert dff % FF_CHUNK == 0
    for c in range(dff // FF_CHUNK):
        cols = slice(c * FF_CHUNK, (c + 1) * FF_CHUNK)
        contrib = _ffn_chunk(h, wg_ref[:, cols], wu_ref[:, cols], wd_ref[cols, :])
        if c == 0:
            acc_ref[...] = contrib
        else:
            acc_ref[...] += contrib
    return x + 0.5 * acc_ref[...]


def _cast_ffn_chunk(h, w32_refs, w16_refs):
    wg, wu, wd = (r[...].astype(BF16) for r in w32_refs)
    for ref, wv in zip(w16_refs, (wg, wu, wd)):
        ref[...] = wv
    return _ffn_chunk(h, wg, wu, wd)


def _ff_chunk_specs(wg, wd, chunk_of_step):
    col = pl.BlockSpec((wg.shape[0], FF_CHUNK), lambda c: (0, chunk_of_step(c)))
    row = pl.BlockSpec((FF_CHUNK, wd.shape[1]), lambda c: (chunk_of_step(c), 0))
    return [col, col, row]


def _ffn_kernel(x_ref, n_ref, wg_ref, wu_ref, wd_ref, o_ref, acc_ref):
    for r in _row_parts(x_ref.shape[0], FFN_ROW_PART):
        o_ref[r, :] = _half_ffn(x_ref[r, :], n_ref[...], wg_ref, wu_ref, wd_ref, acc_ref.at[r, :])


def _ffn_cast_kernel(xa_ref, xb_ref, n_ref, wg_ref, wu_ref, wd_ref,
                     o_ref, wgb_ref, wub_ref, wdb_ref, x_ref, h_ref, acc_ref):
    c = pl.program_id(0)

    @pl.when(c == 0)
    def _():
        x_ref[...] = jnp.concatenate([xa_ref[...], xb_ref[...]], axis=0)
        h_ref[...] = _rms(x_ref[...], n_ref[...]).astype(BF16)
        acc_ref[...] = jnp.zeros_like(acc_ref)

    acc_ref[...] += _cast_ffn_chunk(h_ref[...], (wg_ref, wu_ref, wd_ref),
                                    (wgb_ref, wub_ref, wdb_ref))

    @pl.when(c == pl.num_programs(0) - 1)
    def _():
        o_ref[...] = x_ref[...] + 0.5 * acc_ref[...]


def _ffn_cast_call(xa, xb, norm, wg, wu, wd):
    n, d = xa.shape[0] + xb.shape[0], xa.shape[1]
    dff = wg.shape[1]
    assert dff % FF_CHUNK == 0 and xa.shape[0] % 8 == 0
    chunk_specs = _ff_chunk_specs(wg, wd, lambda c: c)
    whole = lambda rows: pl.BlockSpec((rows, d), lambda c: (0, 0))
    return pl.pallas_call(
        _ffn_cast_kernel,
        grid=(dff // FF_CHUNK,),
        in_specs=[whole(xa.shape[0]), whole(xb.shape[0]), _const_spec(norm.shape)] + chunk_specs,
        out_specs=[whole(n)] + chunk_specs,
        out_shape=[jax.ShapeDtypeStruct((n, d), F32)]
                  + [jax.ShapeDtypeStruct(a.shape, BF16) for a in (wg, wu, wd)],
        scratch_shapes=[pltpu.VMEM((n, d), F32), pltpu.VMEM((n, d), BF16),
                        pltpu.VMEM((n, d), F32)],
        compiler_params=_params(("arbitrary",)),
        name="ffn_cast",
    )(xa, xb, norm, wg, wu, wd)


def _ffn_call(x, norm, wg, wu, wd, tm):
    n, d = x.shape
    return pl.pallas_call(
        _ffn_kernel,
        grid=(n // tm,),
        in_specs=[_rows_spec(tm, d), _const_spec(norm.shape), _const_spec(wg.shape),
                  _const_spec(wu.shape), _const_spec(wd.shape)],
        out_specs=_rows_spec(tm, d),
        out_shape=jax.ShapeDtypeStruct((n, d), F32),
        scratch_shapes=[pltpu.VMEM((tm, d), F32)],
        compiler_params=_params(("parallel",)),
        name="ffn",
    )(x, norm, wg, wu, wd)


def _forget_lower_bound(lbl_ref):
    logits = lbl_ref[...]
    l0, l1 = logits[0:1], logits[1:2]
    m = jnp.maximum(l0, l1)
    e0, e1 = jnp.exp(l0 - m), jnp.exp(l1 - m)
    p0, p1 = e0 / (e0 + e1), e1 / (e0 + e1)
    return (p0 + p1) - p0


_N_MIX_GROUPS = 11
_MIX_GROUP_ORDER = (7, 8, 9, 10, 3, 6, 0, 1, 2, 4, 5)


def _mix_group(j, p, r, lb, bg_ref, outs):
    q_ref, kb_ref, vb_ref, hq_ref, hk_ref, hi_ref, hlf_ref, sg_ref, gates_ref = outs
    if j == 0:
        q_ref[r, :] = (p * (SB_HEAD_DIM ** -0.5 * LOG2_E)).astype(BF16)
    elif j == 1:
        kb_ref[r, :] = p.astype(BF16)
    elif j == 2:
        vb_ref[r, :] = p.astype(BF16)
    elif j == 3:
        f = lb + (1.0 - lb) * jax.nn.sigmoid(p)
        hk_ref[r, :] = (1.0 - f).astype(hk_ref.dtype)
        hlf_ref[r, :] = jnp.log(f)
    elif j == 4:
        hi_ref[r, :] = p.astype(hi_ref.dtype)
    elif j == 5:
        hq_ref[r, :] = p.astype(hq_ref.dtype)
    elif j == 6:
        sg_ref[r, :] = (p * jax.nn.sigmoid(p)).astype(sg_ref.dtype)
    else:
        cols = slice((j - 7) * SB_WIDTH, (j - 6) * SB_WIDTH)
        gates_ref[r, cols] = jax.nn.sigmoid(p + bg_ref[:, cols]).astype(gates_ref.dtype)


def _mix_body(x_ref, n_ref, win_ref, bg_ref, lbl_ref, *outs):
    w = SB_WIDTH
    lb = _forget_lower_bound(lbl_ref)
    kv = ([], [])
    for r in _row_parts(x_ref.shape[0]):
        h = _rms(x_ref[r, :], n_ref[...]).astype(BF16)
        for j in _MIX_GROUP_ORDER:
            p = jnp.dot(h, win_ref[:, j * w:(j + 1) * w], preferred_element_type=F32)
            _mix_group(j, p, r, lb, bg_ref, outs)
            if j in (1, 2):
                kv[j - 1].append(p)
    return jnp.concatenate(kv[0], axis=0), jnp.concatenate(kv[1], axis=0)


def _mix_small_kernel(*refs, n_run, n_meta):
    x_ref, n_ref, win_ref, bg_ref, lbl_ref = refs[:5]
    outs, (k_ref, v_ref, kt_ref, vt_ref, mk_ref, mv_ref, winb_ref, h_ref) = refs[5:14], refs[14:]
    c = pl.program_id(0)

    @pl.when(c == 0)
    def _():
        h_ref[...] = _rms(x_ref[...], n_ref[...]).astype(BF16)

    wb = win_ref[...].astype(BF16)
    winb_ref[...] = wb
    p = jnp.dot(h_ref[...], wb, preferred_element_type=F32)
    lb = _forget_lower_bound(lbl_ref)
    for j in range(_N_MIX_GROUPS):
        @pl.when(c == j)
        def _(j=j):
            _mix_group(j, p, slice(None), lb, bg_ref, outs)
            for group, rows_ref, cols_ref, meta_ref in ((1, k_ref, kt_ref, mk_ref),
                                                        (2, v_ref, vt_ref, mv_ref)):
                if j == group:
                    meta_rows = _pad_rows(p[n_run:n_run + n_meta], KEY_BLOCK)
                    rows_ref[...] = p
                    cols_ref[...] = meta_rows.T
                    meta_ref[...] = meta_rows.astype(BF16)


def _mix_frames_kernel(*refs, tm, n_meta, n_tiles):
    x_ref, consts, mk_ref, mv_ref = refs[0], refs[1:5], refs[5], refs[6]
    row_outs, (kt_ref, vt_ref, ck_ref, cv_ref) = refs[7:16], refs[16:]
    t = pl.program_id(1)
    shape = (SB_HEADS, SB_HEAD_DIM, tm)

    @pl.when(t == 0)
    def _():
        ck_ref[...] = mk_ref[...]
        cv_ref[...] = mv_ref[...]

    @pl.when(t < n_tiles)
    def _():
        k, v = _mix_body(x_ref.at[0], *consts, *[r.at[0] for r in row_outs])
        for rows, out_ref, carry_ref in ((k, kt_ref, ck_ref), (v, vt_ref, cv_ref)):
            cols = rows.T
            late = jnp.concatenate([carry_ref[:, :n_meta], cols[:, :tm - n_meta]], axis=1)
            out_ref[0, 0] = late.reshape(shape)
            carry_ref[:, :n_meta] = cols[:, tm - n_meta:]

    @pl.when(t == n_tiles)
    def _():
        for out_ref, carry_ref in ((kt_ref, ck_ref), (vt_ref, cv_ref)):
            tail = jnp.concatenate(
                [carry_ref[:, :n_meta], jnp.zeros((SB_WIDTH, tm - n_meta), F32)], axis=1)
            out_ref[0, 0] = tail.reshape(shape)


_MIX_WIDTHS = [(SB_WIDTH, BF16)] * 6 + [(SB_WIDTH, F32), (SB_WIDTH, BF16)]


def _mix_small_call(x, norm, w_in, b_gate, lb_logits, n_run, n_meta):
    n, d = x.shape
    assert w_in.shape[1] == _N_MIX_GROUPS * SB_WIDTH
    widths = _MIX_WIDTHS + [(2 * d, BF16), (SB_WIDTH, F32), (SB_WIDTH, F32)]
    assert KEY_BLOCK == LANES
    meta_t = jax.ShapeDtypeStruct((SB_WIDTH, LANES), F32)
    meta_rows = jax.ShapeDtypeStruct((KEY_BLOCK, SB_WIDTH), BF16)
    whole = lambda shape: pl.BlockSpec(shape, lambda c: (0, 0))
    group = pl.BlockSpec((d, SB_WIDTH), lambda c: (0, c))
    return pl.pallas_call(
        functools.partial(_mix_small_kernel, n_run=n_run, n_meta=n_meta),
        grid=(_N_MIX_GROUPS,),
        in_specs=[whole((n, d)), _const_spec(norm.shape), group,
                  _const_spec(b_gate.shape), _const_spec(lb_logits.shape)],
        out_specs=[whole((n, wd)) for wd, _ in widths] + [whole(meta_t.shape)] * 2
                  + [whole(meta_rows.shape)] * 2 + [group],
        out_shape=[jax.ShapeDtypeStruct((n, wd), dt) for wd, dt in widths] + [meta_t] * 2
                  + [meta_rows] * 2 + [jax.ShapeDtypeStruct(w_in.shape, BF16)],
        scratch_shapes=[pltpu.VMEM((n, d), BF16)],
        compiler_params=_params(("arbitrary",)),
        name="mix_small",
    )(x, norm, w_in, b_gate, lb_logits)


def _mix_frames_call(x, norm, w_in, b_gate, lb_logits, meta_kt, meta_vt, tm, n_meta):
    nb, seq, d = x.shape
    assert seq % tm == 0 and n_meta < tm
    n_tiles = seq // tm
    widths = _MIX_WIDTHS + [(2 * d, BF16)]
    rows_map = lambda b, t: (b, jnp.minimum(t, n_tiles - 1), 0)
    const2 = lambda shape: pl.BlockSpec(shape, lambda b, t: (0, 0), pipeline_mode=pl.Buffered(1))
    kv_shape = jax.ShapeDtypeStruct((1, nb, SB_HEADS, SB_HEAD_DIM, n_meta + seq), F32)
    kv_spec = pl.BlockSpec((1, 1, SB_HEADS, SB_HEAD_DIM, tm), lambda b, t: (0, b, 0, 0, t))
    carry = pltpu.VMEM((SB_WIDTH, LANES), F32)
    return pl.pallas_call(
        functools.partial(_mix_frames_kernel, tm=tm, n_meta=n_meta, n_tiles=n_tiles),
        grid=(nb, n_tiles + 1),
        in_specs=[pl.BlockSpec((1, tm, d), rows_map)]
                 + [const2(a.shape) for a in (norm, w_in, b_gate, lb_logits, meta_kt, meta_vt)],
        out_specs=[pl.BlockSpec((1, tm, wd), rows_map) for wd, _ in widths] + [kv_spec] * 2,
        out_shape=[jax.ShapeDtypeStruct((nb, seq, wd), dt) for wd, dt in widths] + [kv_shape] * 2,
        scratch_shapes=[carry, carry],
        compiler_params=_params(("arbitrary", "arbitrary")),
        name="mix_frames",
    )(x, norm, w_in, b_gate, lb_logits, meta_kt, meta_vt)


def _suffix_ones(width):
    r = lax.broadcasted_iota(jnp.int32, (width, width), 0)
    c = lax.broadcasted_iota(jnp.int32, (width, width), 1)
    return jnp.where(r >= c, 1.0, 0.0).astype(BF16)


def _key_before(tq, width, k0, q_first):
    row = lax.broadcasted_iota(jnp.int32, (tq, width), 0)
    col = lax.broadcasted_iota(jnp.int32, (tq, width), 1)
    return col + k0 < row + q_first


def _sb_scores(blocks, token_minor):
    k_dims = _NN if token_minor else _NT
    return [[lax.dot_general(q_ops[h], k_ops[h], k_dims, preferred_element_type=F32)
             for h in range(SB_HEADS)] for q_ops, k_ops, *_ in blocks]


def _sb_stay_sums(blocks, zs):
    cums = []
    for z_blk, (_, _, _, ones, mask, _, _) in zip(zs, blocks):
        stays = []
        for z in z_blk:
            nz = -z
            ls = jnp.minimum(nz, 0.0) - jnp.log2(1.0 + jnp.exp2(jnp.minimum(z, nz)))
            if mask is not None:
                ls = jnp.where(mask, ls, 0.0)
            stays.append(ls.astype(BF16))
        cums.append([jnp.dot(st, ones, preferred_element_type=F32) for st in stays])
    return cums


def _sb_apply(blocks, zs, cums, token_minor):
    heads = range(SB_HEADS)
    v_dims = _NT if token_minor else _NN
    ws = []
    for z_blk, cum_blk, (q_ops, _, _, ones, mask, r_ref, _) in zip(zs, cums, blocks):
        ws.append([])
        for h in heads:
            r = r_ref[h]
            cr = cum_blk[h] + jnp.concatenate([r] * (ones.shape[1] // LANES), axis=1)
            a = jnp.exp2(z_blk[h] + cr)
            if mask is not None:
                a = jnp.where(mask, a, 0.0)
            ws[-1].append(a.astype(BF16))
            r_ref[h] = jnp.broadcast_to(cr[:, 0:1], (q_ops[h].shape[0], LANES))
    for w_blk, (_, _, v_ops, _, _, _, acc_ref) in zip(ws, blocks):
        for h in heads:
            acc_ref[h] += lax.dot_general(w_blk[h], v_ops[h], v_dims, preferred_element_type=F32)


def _sb_sweep(blocks, token_minor=False):
    zs = _sb_scores(blocks, token_minor)
    _sb_apply(blocks, zs, _sb_stay_sums(blocks, zs), token_minor)


def _sb_sweep_skewed(sweeps):
    zs = _sb_scores(sweeps[0], False)
    for n, blocks in enumerate(sweeps):
        cums = _sb_stay_sums(blocks, zs)
        z_next = _sb_scores(sweeps[n + 1], False) if n + 1 < len(sweeps) else None
        _sb_apply(blocks, zs, cums, False)
        zs = z_next


def _sb_live(r_ref):
    m = r_ref[0]
    for h in range(1, SB_HEADS):
        m = jnp.maximum(m, r_ref[h])
    return (jnp.max(m) >= F32_EXP2_UNDERFLOW).astype(jnp.int32)


def _sweep_while(n_blocks, alive, r_ref, block, first=0):
    def body(carry):
        t, _ = carry
        block(t)
        return t + 1, _sb_live(r_ref)

    _, alive = lax.while_loop(lambda c: jnp.logical_and(c[0] < n_blocks, c[1] > 0),
                              body, (jnp.int32(first), alive))
    return alive


def _sb_kernel(q_ref, kn_ref, vn_ref, ko_ref, vo_ref, o_ref, qm_ref, acc_ref, r_ref, live_ref, *,
               tq, n_old):
    kb, wb = KEY_BLOCK, WIDE_BLOCK
    blocks_per_step = q_ref.shape[1] // tq
    lane = lax.broadcasted_iota(jnp.int32, (tq, LANES), 1)
    ones = {width: _suffix_ones(width) for width in (kb, wb)}
    old_rows = lax.broadcasted_iota(jnp.int32, (tq, kb), 1) < n_old

    def block(j, k_ref, v_ref, start, width, mask):
        k_all = k_ref[0, pl.ds(start, width), :].astype(BF16)
        v_all = v_ref[0, pl.ds(start, width), :].astype(BF16)
        pair = lambda a: [a[:, (h // 2) * LANES:(h // 2 + 1) * LANES] for h in range(SB_HEADS)]
        return ([qm_ref[j, h] for h in range(SB_HEADS)], pair(k_all), pair(v_all), ones[width],
                mask, r_ref.at[j], acc_ref.at[j])

    def place(j):
        q0 = (pl.program_id(1) * blocks_per_step + j) * tq
        below = q0 // kb
        return q0, below, pl.multiple_of(below * kb, kb)

    def write_out(j):
        rows = pl.ds(pl.multiple_of(j * tq, tq), tq)
        for p in range(SB_HEADS // 2):
            o_ref[0, rows, p * LANES:(p + 1) * LANES] = jnp.where(
                lane < SB_HEAD_DIM, acc_ref[j, 2 * p], acc_ref[j, 2 * p + 1]).astype(o_ref.dtype)

    def start(j):
        rows = pl.ds(pl.multiple_of(j * tq, tq), tq)
        for h in range(SB_HEADS):
            p = h // 2
            qp = q_ref[0, rows, p * LANES:(p + 1) * LANES]
            keep = (lane < SB_HEAD_DIM) if h % 2 == 0 else (lane >= SB_HEAD_DIM)
            qm_ref[j, h] = jnp.where(keep, qp, jnp.zeros_like(qp))
            r_ref[j, h] = jnp.zeros((tq, LANES), F32)
            acc_ref[j, h] = jnp.zeros((tq, LANES), F32)

    def own(j):
        q0, _, d0 = place(j)
        return block(j, kn_ref, vn_ref, d0, kb, _key_before(tq, kb, d0, q0))

    def own_and_under(j):
        _, _, d0 = place(j)
        return [own(j), block(j, kn_ref, vn_ref, pl.multiple_of(d0 - wb, kb), wb, None)]

    def note_reach(j):
        reach = r_ref[j, 0]
        for h in range(1, SB_HEADS):
            reach = jnp.maximum(reach, r_ref[j, h])
        live_ref[...] = jnp.maximum(live_ref[...], reach)

    def first_sweep(j, carry):
        _, below, _ = place(j)
        start(j)
        old = lambda: block(j, ko_ref, vo_ref, 0, kb, old_rows)

        @pl.when(below == 0)
        def _():
            _sb_sweep([own(j), old()])

        @pl.when(below == 1)
        def _():
            _sb_sweep([own(j), block(j, kn_ref, vn_ref, 0, kb, None), old()])

        @pl.when(below >= 2)
        def _():
            _sb_sweep(own_and_under(j))
            note_reach(j)

        write_out(j)
        return carry

    def first_sweep_group(jj, carry):
        js = [group * jj + n for n in range(group)]
        _, below, _ = place(js[0])
        for j in js:
            start(j)
        old = lambda j: block(j, ko_ref, vo_ref, 0, kb, old_rows)

        @pl.when(below == 0)
        def _():
            _sb_sweep([own(js[0]), old(js[0])])
            _sb_sweep([own(js[1]), block(js[1], kn_ref, vn_ref, 0, kb, None), old(js[1])])
            if group > 2:
                _sb_sweep_skewed([own_and_under(j) for j in js[2:]])
                for j in js[2:]:
                    note_reach(j)

        @pl.when(below >= 2)
        def _():
            _sb_sweep_skewed([own_and_under(j) for j in js])
            for j in js:
                note_reach(j)

        for j in js:
            write_out(j)
        return carry

    def further_sweeps(j, carry):
        _, below, d0 = place(j)
        top = d0 - wb
        left = jnp.maximum(below - 2, 0)
        live = r_ref.at[j]
        alive = _sweep_while(left // 2, _sb_live(live), live, lambda t: _sb_sweep([block(
            j, kn_ref, vn_ref, pl.multiple_of(top - (t + 1) * wb, kb), wb, None)]))
        alive = _sweep_while(left % 2, alive, live,
                             lambda t: _sb_sweep([block(j, kn_ref, vn_ref, 0, kb, None)]))
        _sweep_while((below >= 2).astype(jnp.int32), alive, live,
                     lambda t: _sb_sweep([block(j, ko_ref, vo_ref, 0, kb, old_rows)]))
        write_out(j)
        return carry

    live_ref[...] = jnp.full(live_ref.shape, -jnp.inf, F32)
    group = SB_BLOCKS_SKEWED
    if blocks_per_step % group == 0 and KEY_BLOCK == tq:
        lax.fori_loop(0, blocks_per_step // group, first_sweep_group, 0)
    else:
        lax.fori_loop(0, blocks_per_step, first_sweep, 0)

    @pl.when(jnp.max(live_ref[...]) >= F32_EXP2_UNDERFLOW)
    def _():
        lax.fori_loop(0, blocks_per_step, further_sweeps, 0)


def _sbd_kernel(q_ref, kn_ref, vn_ref, tk_ref, tv_ref, ck_hbm, cv_hbm, o_ref,
                kbuf, vbuf, sem, acc_ref, r_ref, *, tq, n_old):
    kb, wb, hd = KEY_BLOCK, WIDE_BLOCK, SB_HEAD_DIM
    heads = range(SB_HEADS)
    stream = pl.program_id(0)
    n_tail, n_full = n_old % kb, n_old // kb
    n_wide = (n_full * kb) // wb
    rest = n_full * kb - n_wide * wb
    per_head = lambda a: [a[:, h * hd:(h + 1) * hd] for h in heads]

    def fetch(start, width):
        return [pltpu.make_async_copy(hbm.at[0, stream, :, :, pl.ds(start, width)],
                                      buf.at[:, :, pl.ds(0, width)], sem.at[n])
                for n, (hbm, buf) in enumerate(((ck_hbm, kbuf), (cv_hbm, vbuf)))]

    wide_start = lambda t: pl.multiple_of(rest + (n_wide - 1 - t) * wb, kb)
    if n_wide:
        for cp in fetch(wide_start(0), wb):
            cp.start()

    q_ops = per_head(q_ref[...])
    for h in heads:
        r_ref[h] = jnp.zeros((tq, LANES), F32)
        acc_ref[h] = jnp.zeros((tq, hd), F32)

    _sb_sweep([(q_ops, per_head(_pad_rows(kn_ref[...], kb)), per_head(_pad_rows(vn_ref[...], kb)),
                _suffix_ones(kb), _key_before(tq, kb, 0, 0), r_ref, acc_ref)])
    alive = None if n_wide else _sb_live(r_ref)

    def swept(load, width, valid):
        def cols(which, h):
            c = load(which, h).astype(BF16)
            if valid < width:
                c = jnp.concatenate([c, jnp.zeros((hd, width - valid), BF16)], axis=1)
            return c

        mask = None if valid == width else (
            lax.broadcasted_iota(jnp.int32, (tq, width), 1) < valid)
        _sb_sweep([(q_ops, [cols(0, h) for h in heads], [cols(1, h) for h in heads],
                    _suffix_ones(width), mask, r_ref, acc_ref)], token_minor=True)

    from_tail = lambda which, h: (tk_ref, tv_ref)[which][0, 0, h, :, 0:n_tail]
    from_buf = lambda width: lambda which, h: (kbuf, vbuf)[which][h, :, pl.ds(0, width)]

    def fetched(start, width):
        for cp in fetch(start, width):
            cp.start()
        for cp in fetch(start, width):
            cp.wait()
        swept(from_buf(width), width, width)

    if n_tail:
        swept(from_tail, kb, n_tail)
    if n_wide:
        for cp in fetch(wide_start(0), wb):
            cp.wait()
        swept(from_buf(wb), wb, wb)
        alive = _sweep_while(n_wide, _sb_live(r_ref), r_ref,
                             lambda t: fetched(wide_start(t), wb), first=1)
    if rest:
        _sweep_while(1, alive, r_ref, lambda t: fetched(0, rest))

    o_ref[...] = jnp.concatenate([acc_ref[h] for h in heads], axis=1).astype(o_ref.dtype)


def _sb_call(q, k_new, v_new, k_old, v_old, n_old, tq):
    s, lq, w = q.shape
    ln = k_new.shape[1]
    lo = k_old.shape[1]
    assert lq % tq == 0 and ln % KEY_BLOCK == 0 and n_old <= KEY_BLOCK <= lo
    assert tq <= KEY_BLOCK and KEY_BLOCK % tq == 0
    step = tq * SB_BLOCKS_PER_STEP if lq % (tq * SB_BLOCKS_PER_STEP) == 0 else tq
    old_map = (lambda b, i: (b, 0, 0)) if k_old.shape[0] == s else (lambda b, i: (0, 0, 0))
    return pl.pallas_call(
        functools.partial(_sb_kernel, tq=tq, n_old=n_old),
        grid=(s, lq // step),
        in_specs=[pl.BlockSpec((1, step, w), lambda b, i: (b, i, 0)),
                  pl.BlockSpec((1, ln, w), lambda b, i: (b, 0, 0)),
                  pl.BlockSpec((1, ln, w), lambda b, i: (b, 0, 0)),
                  pl.BlockSpec((1, lo, w), old_map),
                  pl.BlockSpec((1, lo, w), old_map)],
        out_specs=pl.BlockSpec((1, step, w), lambda b, i: (b, i, 0)),
        out_shape=jax.ShapeDtypeStruct((s, lq, w), BF16),
        scratch_shapes=[pltpu.VMEM((step // tq, SB_HEADS, tq, LANES), BF16),
                        pltpu.VMEM((step // tq, SB_HEADS, tq, LANES), F32),
                        pltpu.VMEM((step // tq, SB_HEADS, tq, LANES), F32),
                        pltpu.VMEM((tq, LANES), F32)],
        compiler_params=_params(("parallel", "parallel")),
        name="sb",
    )(q, k_new, v_new, k_old, v_old)


def _sbd_call(q, k_new, v_new, cache_k, cache_v, tq):
    s, w = cache_k.shape[1], q.shape[1]
    n_old = cache_k.shape[-1]
    assert tq <= KEY_BLOCK and q.shape[0] >= s * tq
    new_spec = pl.BlockSpec((tq, w), lambda b: (b, 0))
    last_block = (n_old - 1) // KEY_BLOCK
    tail_spec = pl.BlockSpec((1, 1, SB_HEADS, SB_HEAD_DIM, KEY_BLOCK),
                             lambda b: (0, b, 0, 0, last_block))
    in_hbm = pl.BlockSpec(memory_space=pl.ANY)
    buf = pltpu.VMEM((SB_HEADS, SB_HEAD_DIM, WIDE_BLOCK), F32)
    return pl.pallas_call(
        functools.partial(_sbd_kernel, tq=tq, n_old=n_old),
        grid=(s,),
        in_specs=[new_spec, new_spec, new_spec, tail_spec, tail_spec, in_hbm, in_hbm],
        out_specs=new_spec,
        out_shape=jax.ShapeDtypeStruct((s * tq, w), BF16),
        scratch_shapes=[buf, buf, pltpu.SemaphoreType.DMA((2,)),
                        pltpu.VMEM((SB_HEADS, tq, SB_HEAD_DIM), F32),
                        pltpu.VMEM((SB_HEADS, tq, LANES), F32)],
        compiler_params=_params(("arbitrary",)),
        name="sb_decode",
    )(q, k_new, v_new, cache_k, cache_v, cache_k, cache_v)


def _pad_rows(x, rows):
    if x.shape[0] == rows:
        return x
    return jnp.concatenate([x, jnp.zeros((rows - x.shape[0], x.shape[1]), x.dtype)], axis=0)


def _hg_kernel(q_ref, k_ref, i_ref, lf_ref, s0_ref, o_ref, sout_ref, st_ref, *, chunk):
    c = pl.program_id(0)
    hd = HG_HEAD_DIM
    ns, n_chunks = q_ref.shape[0], q_ref.shape[1] // chunk
    streams = range(ns)
    units = [(g, b) for g in range(n_chunks) for b in streams]
    rows = lambda g: slice(g * chunk, (g + 1) * chunk)
    head_cols = [slice(h * hd, (h + 1) * hd) for h in range(HG_HEADS)]

    @pl.when(c == 0)
    def _():
        for b in streams:
            for h in range(HG_HEADS):
                st_ref[b, h] = s0_ref[b if s0_ref.shape[0] == ns else 0, h].T

    trow = lax.broadcasted_iota(jnp.int32, (chunk, chunk), 0)
    tcol = lax.broadcasted_iota(jnp.int32, (chunk, chunk), 1)
    causal = trow >= tcol

    token = lax.broadcasted_iota(jnp.int32, (chunk, lf_ref.shape[2]), 0)
    cum = {}
    for g, b in units:
        part = lf_ref[b, rows(g), :]
        shift = 1
        while shift < chunk:
            part = part + jnp.where(token >= shift, pltpu.roll(part, shift, axis=0), 0.0)
            shift *= 2
        cum[g, b] = part
    last = {u: cum[u][chunk - 1:chunk, :] for u in units}
    weakest = functools.reduce(jnp.minimum, last.values())
    safe = jnp.min(weakest) >= HG_SAFE_LOG_DECAY

    @pl.when(safe)
    def _():
        qd, kd, kl, iv, dl = {}, {}, {}, {}, {}
        for g, b in units:
            k = k_ref[b, rows(g), :]
            qd[g, b] = (q_ref[b, rows(g), :] * jnp.exp(cum[g, b])).astype(BF16)
            kd[g, b] = (k * jnp.exp(-cum[g, b])).astype(BF16)
            kl[g, b] = (k * jnp.exp(last[g, b] - cum[g, b])).astype(BF16)
            iv[g, b] = i_ref[b, rows(g), :].astype(BF16)
            dl[g, b] = jnp.exp(last[g, b])
        sc = {(u, h): lax.dot_general(qd[u][:, sl], kd[u][:, sl], _NT, preferred_element_type=F32)
              for u in units for h, sl in enumerate(head_cols)}
        grown = {(u, h): lax.dot_general(iv[u][:, sl], kl[u][:, sl], _TN,
                                         preferred_element_type=F32)
                 for u in units for h, sl in enumerate(head_cols)}
        within = {(u, h): jnp.dot(jnp.where(causal, sc[u, h], 0.0).astype(BF16), iv[u][:, sl],
                                  preferred_element_type=F32)
                  for u in units for h, sl in enumerate(head_cols)}
        for g in range(n_chunks):
            carried = {(b, h): lax.dot_general(qd[g, b][:, sl], st_ref[b, h].astype(BF16), _NT,
                                               preferred_element_type=F32)
                       for b in streams for h, sl in enumerate(head_cols)}
            for b in streams:
                for h, sl in enumerate(head_cols):
                    o_ref[b, rows(g), sl] = (within[(g, b), h] + carried[b, h]).astype(o_ref.dtype)
                    st_ref[b, h] = st_ref[b, h] * dl[g, b][:, sl] + grown[(g, b), h]

    @pl.when(jnp.logical_not(safe))
    def _():
        lane_t = lax.broadcasted_iota(jnp.int32, (hd, hd), 1)
        row_t = lax.broadcasted_iota(jnp.int32, (hd, hd), 0)

        def one_chunk(n, carry):
            b, r0 = n // n_chunks, pl.multiple_of((n % n_chunks) * chunk, chunk)
            for h, sl in enumerate(head_cols):
                load = lambda ref: ref[b, pl.ds(r0, chunk), sl].astype(F32)
                f_t = _pad_rows(jnp.exp(load(lf_ref)), hd).T
                k_t = _pad_rows(load(k_ref), hd).T
                q_t = _pad_rows(load(q_ref), hd).T
                i_p = _pad_rows(load(i_ref), hd)

                def token(t, sc, f_t=f_t, k_t=k_t, q_t=q_t, i_p=i_p):
                    s, o = sc
                    sel = lane_t == t
                    fc = jnp.sum(jnp.where(sel, f_t, 0.0), axis=1, keepdims=True)
                    kc = jnp.sum(jnp.where(sel, k_t, 0.0), axis=1, keepdims=True)
                    qc = jnp.sum(jnp.where(sel, q_t, 0.0), axis=1, keepdims=True)
                    i_row = jnp.sum(jnp.where(row_t == t, i_p, 0.0), axis=0, keepdims=True)
                    s = s * fc + kc * i_row
                    o = jnp.where(row_t == t, jnp.sum(qc * s, axis=0, keepdims=True), o)
                    return s, o

                s, o = lax.fori_loop(0, chunk, token,
                                     (st_ref[b, h].T, jnp.zeros((hd, hd), F32)))
                st_ref[b, h] = s.T
                o_ref[b, pl.ds(r0, chunk), sl] = o[:chunk].astype(o_ref.dtype)
            return carry

        lax.fori_loop(0, ns * n_chunks, one_chunk, 0)

    @pl.when(c == pl.num_programs(0) - 1)
    def _():
        for b in streams:
            for h in range(HG_HEADS):
                sout_ref[b, h] = st_ref[b, h].T


def _hg_call(q, k, iv, lf, s0t, chunk, chunks_per_step=1):
    s, l, w = q.shape
    rows = chunk * chunks_per_step if l % (chunk * chunks_per_step) == 0 else chunk
    assert l % rows == 0
    seq_spec = pl.BlockSpec((s, rows, w), lambda c: (0, c, 0))
    st_shape = (s, HG_HEADS, HG_HEAD_DIM, HG_HEAD_DIM)
    return pl.pallas_call(
        functools.partial(_hg_kernel, chunk=chunk),
        grid=(l // rows,),
        in_specs=[seq_spec, seq_spec, seq_spec, seq_spec,
                  pl.BlockSpec(s0t.shape, lambda c: (0, 0, 0, 0))],
        out_specs=[seq_spec, pl.BlockSpec(st_shape, lambda c: (0, 0, 0, 0))],
        out_shape=[jax.ShapeDtypeStruct((s, l, w), BF16), jax.ShapeDtypeStruct(st_shape, F32)],
        scratch_shapes=[pltpu.VMEM(st_shape, F32)],
        compiler_params=_params(("arbitrary",)),
        name="hgrn",
    )(q, k, iv, lf, s0t)


def _merge_branches(x, oa, ob, sg, gates, hn, wa, wb, wo):
    hd = HG_HEAD_DIM
    d = x.shape[1]
    heads = []
    for h in range(HG_HEADS):
        sl = slice(h * hd, (h + 1) * hd)
        heads.append(_rms(ob[:, sl].astype(F32), hn[:, sl]) * sg[:, sl].astype(F32))
    obn = jnp.concatenate(heads, axis=-1).astype(BF16)
    ma = jnp.dot(oa, wa, preferred_element_type=F32)
    mb = jnp.dot(obn, wb, preferred_element_type=F32)
    merged = gates[:, :d].astype(F32) * ma + gates[:, d:].astype(F32) * mb
    return x + jnp.dot(merged.astype(BF16), wo, preferred_element_type=F32)


def _post_kernel(x_ref, oa_ref, ob_ref, sg_ref, gates_ref, hn_ref, wa_ref, wb_ref, wo_ref,
                 n_ref, wg_ref, wu_ref, wd_ref, fn_ref, y_ref, acc_ref):
    x = jnp.concatenate(
        [_merge_branches(x_ref[r, :], oa_ref[r, :], ob_ref[r, :], sg_ref[r, :], gates_ref[r, :],
                         hn_ref[...], wa_ref[...], wb_ref[...], wo_ref[...])
         for r in _row_parts(x_ref.shape[0])], axis=0)
    x = _half_ffn(x, n_ref[...], wg_ref, wu_ref, wd_ref, acc_ref)
    y_ref[...] = _rms(x, fn_ref[...])


def _post_cast_kernel(x_ref, oa_ref, ob_ref, sg_ref, gates_ref, hn_ref, wa_ref, wb_ref, wo_ref,
                      n_ref, wg_ref, wu_ref, wd_ref, fn_ref,
                      y_ref, wab_ref, wbb_ref, wob_ref, wgb_ref, wub_ref, wdb_ref,
                      x2_ref, h_ref, acc_ref):
    c = pl.program_id(0)

    @pl.when(c == 0)
    def _():
        wa, wb, wo = (r[...].astype(BF16) for r in (wa_ref, wb_ref, wo_ref))
        for ref, wv in zip((wab_ref, wbb_ref, wob_ref), (wa, wb, wo)):
            ref[...] = wv
        x2 = _merge_branches(x_ref[...], oa_ref[...], ob_ref[...], sg_ref[...], gates_ref[...],
                             hn_ref[...], wa, wb, wo)
        x2_ref[...] = x2
        h_ref[...] = _rms(x2, n_ref[...]).astype(BF16)
        acc_ref[...] = jnp.zeros_like(acc_ref)

    @pl.when(c > 0)
    def _():
        acc_ref[...] += _cast_ffn_chunk(h_ref[...], (wg_ref, wu_ref, wd_ref),
                                        (wgb_ref, wub_ref, wdb_ref))

    @pl.when(c == pl.num_programs(0) - 1)
    def _():
        y_ref[...] = _rms(x2_ref[...] + 0.5 * acc_ref[...], fn_ref[...])


def _post_cast_call(x, oa, ob, sg, gates, hn, wa, wb, wo, norm, wg, wu, wd, fnorm):
    n, d = oa.shape[0], x.shape[1]
    dff = wg.shape[1]
    assert dff % FF_CHUNK == 0
    chunk_specs = _ff_chunk_specs(wg, wd, lambda c: jnp.maximum(c - 1, 0))
    whole = lambda a: pl.BlockSpec(a.shape, lambda c: (0,) * a.ndim)
    top = lambda a: pl.BlockSpec((n, a.shape[1]), lambda c: (0, 0))
    return pl.pallas_call(
        _post_cast_kernel,
        grid=(dff // FF_CHUNK + 1,),
        in_specs=[top(a) for a in (x, oa, ob, sg, gates)]
                 + [_const_spec(a.shape) for a in (hn, wa, wb, wo, norm)] + chunk_specs
                 + [_const_spec(fnorm.shape)],
        out_specs=[top(x)] + [whole(a) for a in (wa, wb, wo)] + chunk_specs,
        out_shape=[jax.ShapeDtypeStruct((n, d), F32)]
                  + [jax.ShapeDtypeStruct(a.shape, BF16) for a in (wa, wb, wo, wg, wu, wd)],
        scratch_shapes=[pltpu.VMEM((n, d), F32), pltpu.VMEM((n, d), BF16),
                        pltpu.VMEM((n, d), F32)],
        compiler_params=_params(("arbitrary",)),
        name="post_cast",
    )(x, oa, ob, sg, gates, hn, wa, wb, wo, norm, wg, wu, wd, fnorm)


def _post_call(x, oa, ob, sg, gates, hn, wa, wb, wo, norm, wg, wu, wd, fnorm, tm):
    n, d = x.shape
    consts = [hn, wa, wb, wo, norm, wg, wu, wd, fnorm]
    return pl.pallas_call(
        _post_kernel,
        grid=(n // tm,),
        in_specs=[_rows_spec(tm, d), _rows_spec(tm, oa.shape[1]), _rows_spec(tm, ob.shape[1]),
                  _rows_spec(tm, sg.shape[1]), _rows_spec(tm, gates.shape[1])]
                 + [_const_spec(a.shape) for a in consts],
        out_specs=_rows_spec(tm, d),
        out_shape=jax.ShapeDtypeStruct((n, d), F32),
        scratch_shapes=[pltpu.VMEM((tm, d), F32)],
        compiler_params=_params(("parallel",)),
        name="post",
    )(x, oa, ob, sg, gates, *consts)


def _row_tile(n, target):
    tm = min(n, target)
    assert n % tm == 0 and tm % 8 == 0
    return tm


def kernel(x_prompt, x_sample, cache_sb_k, cache_sb_v, state_hgrn, meta_tokens, ffn1_norm, ffn1_w_gate, ffn1_w_up, ffn1_w_down, mix_norm, w_in, b_gate, hg_lb_logits, hg_out_norm, w_branch_a, w_branch_b, w_out, ffn2_norm, ffn2_w_gate, ffn2_w_up, ffn2_w_down, final_norm):
    depth = w_in.shape[0]
    assert depth == 1, "single-layer step"
    nb, seq, d = x_prompt.shape
    ns, t_new, _ = x_sample.shape
    n_meta = meta_tokens.shape[0]
    row = lambda a: a.reshape(1, -1)

    mix_norms = (row(mix_norm[0]), row(b_gate[0]), hg_lb_logits)

    n_frames = nb * seq
    tm_p = _row_tile(seq, 512)
    n_run = ns * t_new
    x1_s, *f1 = _ffn_cast_call(x_sample.reshape(n_run, d), meta_tokens.astype(F32),
                               row(ffn1_norm[0]), ffn1_w_gate[0], ffn1_w_up[0], ffn1_w_down[0])
    (q_s, kb_s, vb_s, hq_s, hk_s, hi_s, hlf_s, sg_s, gates_s, k_s, v_s, meta_kt, meta_vt,
     meta_kb, meta_vb, w_in_b) = _mix_small_call(x1_s, mix_norms[0], w_in[0], *mix_norms[1:],
                                                 n_run, n_meta)
    run = lambda a: a[:n_run]
    meta = lambda a: a[n_run:]
    token_minor = lambda a: jnp.transpose(a, (0, 1, 3, 4, 2))
    token_major = lambda a: jnp.transpose(a, (0, 1, 4, 2, 3))
    seq3 = lambda a, s: a.reshape(s, -1, a.shape[-1])
    rows = lambda a: a.reshape(-1, a.shape[-1])
    hg = lambda sel, s, arrs: [seq3(sel(a), s) for a in arrs]

    oa_s = _sbd_call(q_s, kb_s, vb_s, token_minor(cache_sb_k), token_minor(cache_sb_v), t_new)
    ob_s, st_s = _hg_call(*hg(run, ns, (hq_s, hk_s, hi_s, hlf_s)), state_hgrn[0].astype(F32),
                          t_new)
    post_norms = (row(ffn2_norm[0]), row(final_norm))
    y_s, wa, wb, wo, *f2 = _post_cast_call(
        x1_s, oa_s, rows(ob_s), sg_s, gates_s, row(hg_out_norm[0]),
        w_branch_a[0], w_branch_b[0], w_out[0], post_norms[0],
        ffn2_w_gate[0], ffn2_w_up[0], ffn2_w_down[0], post_norms[1])

    zero_state = jnp.zeros((1, HG_HEADS, HG_HEAD_DIM, HG_HEAD_DIM), F32)
    _, st_meta = _hg_call(*hg(meta, 1, (hq_s, hk_s, hi_s, hlf_s)), zero_state, n_meta)
    x1_p = _ffn_call(x_prompt.reshape(n_frames, d), row(ffn1_norm[0]), *f1,
                     _row_tile(n_frames, FFN_TILE))
    (q_p, kb_p, vb_p, hq_p, hk_p, hi_p, hlf_p, sg_p, gates_p, kt_full, vt_full) = (
        _mix_frames_call(x1_p.reshape(nb, seq, d), mix_norms[0], w_in_b, *mix_norms[1:],
                         meta_kt, meta_vt, tm_p, n_meta))
    oa_p = _sb_call(q_p, kb_p, vb_p, meta_kb[None], meta_vb[None], n_meta, KEY_BLOCK)
    ob_p, st_p = _hg_call(hq_p, hk_p, hi_p, hlf_p, st_meta, HG_CHUNK, HG_CHUNKS_PER_STEP)
    y_p = _post_call(x1_p, rows(oa_p), rows(ob_p), rows(sg_p), rows(gates_p),
                     row(hg_out_norm[0]), wa, wb, wo, post_norms[0], *f2, post_norms[1], tm_p)

    heads = lambda a: a.reshape(1, ns, t_new, SB_HEADS, SB_HEAD_DIM)
    return (y_p.reshape(nb, seq, d), y_s.reshape(ns, t_new, d),
            token_major(kt_full), token_major(vt_full), st_p[None],
            heads(run(k_s)), heads(run(v_s)), st_s[None])
```
